```python
import jax, jax.numpy as jnp
from jax import lax
import numpy as np

D_MODEL = 1024
BATCH = 16
SEQ = 256
DEPTH = 2
DEC_BATCH = 4
DEC_SEQ = 1024
PAST_LEN = 512

GRID_W = 64
N_BRANCH = 4
BR_W = D_MODEL // 4
CHUNK = 128
A_GROUPS = 4
A_GD = BR_W // A_GROUPS
RW_HD = 64
RW_HEADS = BR_W // RW_HD
DECAY_RANK = 64
ICL_RANK = 64
ATT_HD = 64
ATT_HEADS = BR_W // ATT_HD
ATT_KV_HEADS = 2
ATT_GROUP = ATT_HEADS // ATT_KV_HEADS
WINDOW = 128
ROPE_BASE = 10000.0
POOL_SIZES = (2, 4, 8, 16)
POOL_GD = BR_W // len(POOL_SIZES)
NORM_EPS = 1e-6
GN_EPS = 64e-5
NEG_INF = -1e30
IN_SIZES = (BR_W, BR_W, BR_W, BR_W, BR_W, DECAY_RANK, ICL_RANK, ATT_HEADS * ATT_HD, ATT_KV_HEADS * ATT_HD, ATT_KV_HEADS * ATT_HD, BR_W, N_BRANCH * BR_W, N_BRANCH * D_MODEL)
IN_W = sum(IN_SIZES)

kernel_name = 'hybrid_flow_gated_branches_step'


def rmsnorm(x, g):
    xf = x.astype(jnp.float32)
    y = xf * lax.rsqrt(jnp.mean(xf * xf, axis=-1, keepdims=True) + NORM_EPS)
    return (y * g.astype(jnp.float32)).astype(x.dtype)


def split_in(p):
    cuts = [int(i) for i in np.cumsum(IN_SIZES)[:-1]]
    return jnp.split(p, cuts, axis=-1)


def chunk_mix(u, v, w_s, b_s):
    B, L, _ = u.shape
    nc = L // CHUNK
    vc = v.reshape(B, nc, CHUNK, A_GROUPS, A_GD)
    sv = jnp.einsum('gij,bcjgd->bcigd', w_s, vc) + b_s.T[None, None, :, :, None]
    return u * sv.reshape(B, L, BR_W)


def wkv_scan(s0, r, w, k, v, kk, a, reverse):
    xs = tuple(t.transpose(1, 0, 2, 3) for t in (r, w, k, v, kk, a))

    def step(S, inp):
        r_t, w_t, k_t, v_t, kk_t, a_t = inp
        sa = jnp.einsum('bhvk,bhk->bhv', S, kk_t)
        S = S * w_t[:, :, None, :] - sa[..., None] * (kk_t * a_t)[:, :, None, :] + v_t[..., None] * k_t[:, :, None, :]
        return S, jnp.einsum('bhvk,bhk->bhv', S, r_t)

    s_fin, o = lax.scan(step, s0.astype(jnp.float32), xs, reverse=reverse)
    return s_fin, o.transpose(1, 0, 2, 3)


def rwkv7_mix(r, k, v, wd, ad, s_init, lp):
    B, L, _ = r.shape
    heads = lambda t: t.astype(jnp.float32).reshape(B, L, RW_HEADS, RW_HD)
    r_h, k_h, v_h = heads(r), heads(k), heads(v)
    kk = heads(k * lp['rw_k_k'])
    kk = kk * lax.rsqrt(jnp.sum(kk * kk, axis=-1, keepdims=True) + 1e-12)
    k_a = lp['rw_k_a'].astype(jnp.float32).reshape(RW_HEADS, RW_HD)
    r_k = lp['rw_r_k'].astype(jnp.float32)
    wd_t = jnp.tanh(wd.astype(jnp.float32))
    adf = ad.astype(jnp.float32)
    o_sum = jnp.zeros_like(v_h)
    bonus = jnp.zeros_like(v_h)
    finals = []
    for d, rev in ((0, False), (1, True)):
        w_log = -jax.nn.softplus(-(lp['rw_w0'][d] + wd_t @ lp['rw_w_up'][d])) - 0.5
        decay = heads(jnp.exp(-jnp.exp(w_log)))
        a = heads(jax.nn.sigmoid(lp['rw_a0'][d] + adf @ lp['rw_a_up'][d]))
        k_d = k_h * (1.0 + (a - 1.0) * k_a)
        s_fin, o = wkv_scan(s_init[:, d], r_h, decay, k_d, v_h, kk, a, rev)
        o_sum = o_sum + o
        bonus = bonus + jnp.sum(r_h * k_d * r_k, axis=-1, keepdims=True) * v_h
        finals.append(s_fin)
    mu = jnp.mean(o_sum, axis=-1, keepdims=True)
    var = jnp.mean(jnp.square(o_sum - mu), axis=-1, keepdims=True)
    on = ((o_sum - mu) * lax.rsqrt(var + GN_EPS)).reshape(B, L, BR_W)
    y = on * lp['rw_ln_g'].astype(jnp.float32) + lp['rw_ln_b'].astype(jnp.float32) + bonus.reshape(B, L, BR_W)
    return y.astype(r.dtype), jnp.stack(finals, axis=1)


def sink_probs(s, sink):
    sk = sink.astype(jnp.float32).reshape(ATT_KV_HEADS, ATT_GROUP, 1)
    m = jnp.maximum(jnp.max(s, axis=-1), sk)
    p = jnp.exp(s - m[..., None])
    den = jnp.sum(p, axis=-1) + jnp.exp(sk - m)
    return p / den[..., None]


def rope_1d(x, pos):
    d = x.shape[-1]
    inv = ROPE_BASE ** (-jnp.arange(0, d, 2, dtype=jnp.float32) / d)
    ang = pos[:, None] * inv[None, :]
    cos, sin = jnp.cos(ang)[None, :, None, :], jnp.sin(ang)[None, :, None, :]
    x1, x2 = x[..., : d // 2], x[..., d // 2:]
    return jnp.concatenate([x1 * cos - x2 * sin, x2 * cos + x1 * sin], axis=-1)


def axial_rope(x):
    L = x.shape[1]
    rows = L // GRID_W
    row = jnp.repeat(jnp.arange(rows, dtype=jnp.float32), GRID_W)
    col = jnp.tile(jnp.arange(GRID_W, dtype=jnp.float32), rows)
    xf = x.astype(jnp.float32)
    half = x.shape[-1] // 2
    return jnp.concatenate([rope_1d(xf[..., :half], row), rope_1d(xf[..., half:], col)], axis=-1)


def ctx_attention(q, k, v, sink):
    B, L, _ = q.shape
    nb = L // CHUNK
    scale = ATT_HD ** -0.5
    qb = q.astype(jnp.float32).reshape(B, nb, CHUNK, ATT_KV_HEADS, ATT_GROUP, ATT_HD).transpose(1, 0, 2, 3, 4, 5)
    kf = k.astype(jnp.float32).reshape(B, L, ATT_KV_HEADS, ATT_HD)
    vf = v.astype(jnp.float32).reshape(B, L, ATT_KV_HEADS, ATT_HD)

    def blk(qi):
        s = jnp.einsum('bqhgd,bkhd->bhgqk', qi, kf) * scale
        return jnp.einsum('bhgqk,bkhd->bqhgd', sink_probs(s, sink), vf)

    o = lax.map(blk, qb)
    return o.transpose(1, 0, 2, 3, 4, 5).reshape(B, L, ATT_HEADS * ATT_HD).astype(q.dtype)


def latent_attention(q, k, v, ck, cv, sink):
    B, L, _ = q.shape
    nb = L // CHUNK
    scale = ATT_HD ** -0.5
    qh = axial_rope(q.reshape(B, L, ATT_HEADS, ATT_HD)).reshape(B, nb, CHUNK, ATT_KV_HEADS, ATT_GROUP, ATT_HD)
    kh = axial_rope(k.reshape(B, L, ATT_KV_HEADS, ATT_HD))
    vh = v.astype(jnp.float32).reshape(B, L, ATT_KV_HEADS, ATT_HD)

    def bands(t):
        tp = jnp.pad(t, ((0, 0), (CHUNK, CHUNK), (0, 0), (0, 0))).reshape(B, nb + 2, CHUNK, ATT_KV_HEADS, ATT_HD)
        return jnp.concatenate([tp[:, :-2], tp[:, 1:-1], tp[:, 2:]], axis=2)

    kb, vb = bands(kh), bands(vh)
    blk = jnp.arange(nb)[:, None] * CHUNK
    qpos = blk + jnp.arange(CHUNK)[None, :]
    kpos = blk - CHUNK + jnp.arange(3 * CHUNK)[None, :]
    valid = (jnp.abs(qpos[:, :, None] - kpos[:, None, :]) <= WINDOW) & (kpos[:, None, :] >= 0) & (kpos[:, None, :] < L)
    ckf, cvf = ck.astype(jnp.float32), cv.astype(jnp.float32)
    s_loc = jnp.einsum('bnqhgd,bnkhd->bnhgqk', qh, kb) * scale
    s_loc = jnp.where(valid[None, :, None, None], s_loc, NEG_INF)
    s_ctx = jnp.einsum('bnqhgd,bkhd->bnhgqk', qh, ckf) * scale
    p = sink_probs(jnp.concatenate([s_loc, s_ctx], axis=-1), sink)
    nl = 3 * CHUNK
    o = jnp.einsum('bnhgqk,bnkhd->bnqhgd', p[..., :nl], vb) + jnp.einsum('bnhgqk,bkhd->bnqhgd', p[..., nl:], cvf)
    return o.reshape(B, L, ATT_HEADS * ATT_HD).astype(q.dtype)


def pool_mix(p, pool_w, pool_scale):
    B, L, _ = p.shape
    pg = p.astype(jnp.float32).reshape(B, L, len(POOL_SIZES), POOL_GD)
    t = jnp.arange(L)
    outs = []
    for g, w in enumerate(POOL_SIZES):
        xg = pg[:, :, g]
        cs = jnp.concatenate([jnp.zeros((B, 1, POOL_GD), jnp.float32), jnp.cumsum(xg, axis=1)], axis=1)
        lo = jnp.clip(t - w // 2, 0, L)
        hi = jnp.clip(t - w // 2 + w, 0, L)
        mean = (cs[:, hi] - cs[:, lo]) / (hi - lo).astype(jnp.float32)[None, :, None]
        outs.append(mean - xg)
    d = jnp.stack(outs, axis=2)
    y = jnp.einsum('blgc,gcd->blgd', d, pool_w.astype(jnp.float32)).reshape(B, L, BR_W) * pool_scale.astype(jnp.float32)
    return y.astype(p.dtype)


def trunk_layer(x, cond, lp, ctx=None):
    B, L, _ = x.shape
    mod = jax.nn.silu(cond) @ lp['w_mod'] + lp['b_mod']
    shift, scale, gate = jnp.split(mod[:, None, :], 3, axis=-1)
    h = rmsnorm(x, lp['g_norm']) * (1 + scale) + shift
    (a_u, a_v, b_r, b_k, b_v, b_wd, b_ad, c_q, c_k, c_v, d_p, z, mg) = split_in(h @ lp['w_in'])
    y_a = chunk_mix(a_u, a_v, lp['w_s'], lp['b_s'])
    if ctx is None:
        s0 = jnp.zeros((B, 2, RW_HEADS, RW_HD, RW_HD), jnp.float32)
        y_c = ctx_attention(c_q, c_k, c_v, lp['att_sink'])
    else:
        ck, cv, s0 = ctx
        y_c = latent_attention(c_q, c_k, c_v, ck, cv, lp['att_sink'])
    y_b, s_fin = rwkv7_mix(b_r, b_k, b_v, b_wd, b_ad, s0, lp)
    y_d = pool_mix(d_p, lp['pool_w'], lp['pool_scale'])
    ys = jnp.stack([y_a, y_b, y_c, y_d], axis=2) * jax.nn.silu(z).reshape(B, L, N_BRANCH, BR_W)
    up = jnp.einsum('blnc,ncd->blnd', ys, lp['w_up'])
    merged = jnp.sum(jax.nn.sigmoid(mg.reshape(B, L, N_BRANCH, D_MODEL)) * up, axis=2)
    x = x + gate * (merged @ lp['w_o'])
    ctx_k = c_k.reshape(B, L, ATT_KV_HEADS, ATT_HD)
    ctx_v = c_v.reshape(B, L, ATT_KV_HEADS, ATT_HD)
    return x, ctx_k, ctx_v, s_fin.astype(x.dtype)


def setup_inputs(seed: int = 0) -> dict:
    key = jax.random.key(seed)
    ks = jax.random.split(key, 32)

    def nrm(k, shape, scale):
        return scale * jax.random.normal(k, shape, jnp.float32)

    D = D_MODEL
    NL = DEPTH
    return {
        'x_prompt': nrm(ks[0], (BATCH, SEQ, D), 1.0),
        'x_sample': nrm(ks[1], (DEC_BATCH, DEC_SEQ, D), 1.0),
        'cache_k': nrm(ks[2], (DEC_BATCH, NL, PAST_LEN, ATT_KV_HEADS, ATT_HD), 1.0),
        'cache_v': nrm(ks[3], (DEC_BATCH, NL, PAST_LEN, ATT_KV_HEADS, ATT_HD), 1.0),
        'state_rwkv': nrm(ks[4], (DEC_BATCH, NL, 2, RW_HEADS, RW_HD, RW_HD), 0.3),
        'c': nrm(ks[5], (DEC_BATCH, D), 1.0),
        'c_ctx': nrm(ks[6], (D,), 1.0),
        'w_mod': nrm(ks[7], (NL, D, 3 * D), 0.5 * D ** -0.5),
        'b_mod': nrm(ks[8], (NL, 3 * D), 0.01),
        'g_norm': 1.0 + nrm(ks[9], (NL, D), 0.05),
        'w_in': nrm(ks[10], (NL, D, IN_W), D ** -0.5),
        'w_s': nrm(ks[11], (NL, A_GROUPS, CHUNK, CHUNK), 0.5 * CHUNK ** -0.5),
        'b_s': 1.0 + nrm(ks[12], (NL, A_GROUPS, CHUNK), 0.02),
        'rw_w0': nrm(ks[13], (NL, 2, BR_W), 1.0),
        'rw_w_up': nrm(ks[14], (NL, 2, DECAY_RANK, BR_W), 0.5 * DECAY_RANK ** -0.5),
        'rw_a0': nrm(ks[15], (NL, 2, BR_W), 0.5),
        'rw_a_up': nrm(ks[16], (NL, 2, ICL_RANK, BR_W), 0.5 * ICL_RANK ** -0.5),
        'rw_k_k': 0.85 + nrm(ks[17], (NL, BR_W), 0.05),
        'rw_k_a': 1.0 + nrm(ks[18], (NL, BR_W), 0.05),
        'rw_r_k': nrm(ks[19], (NL, RW_HEADS, RW_HD), 0.1),
        'rw_ln_g': 1.0 + nrm(ks[20], (NL, BR_W), 0.05),
        'rw_ln_b': nrm(ks[21], (NL, BR_W), 0.01),
        'att_sink': nrm(ks[22], (NL, ATT_HEADS), 0.5),
        'pool_w': nrm(ks[23], (NL, len(POOL_SIZES), POOL_GD, POOL_GD), POOL_GD ** -0.5),
        'pool_scale': 1.0 + nrm(ks[24], (NL, BR_W), 0.05),
        'w_up': nrm(ks[25], (NL, N_BRANCH, BR_W, D), BR_W ** -0.5),
        'w_o': nrm(ks[26], (NL, D, D), D ** -0.5),
        'g_final': 1.0 + nrm(ks[27], (D,), 0.05),
    }


def reference(x_prompt, x_sample, cache_k, cache_v, state_rwkv, c, c_ctx, w_mod, b_mod, g_norm, w_in, w_s, b_s, rw_w0, rw_w_up, rw_a0, rw_a_up, rw_k_k, rw_k_a, rw_r_k, rw_ln_g, rw_ln_b, att_sink, pool_w, pool_scale, w_up, w_o, g_final):
    xp, xs = x_prompt, x_sample
    cond_ctx = jnp.broadcast_to(c_ctx[None, :], (xp.shape[0], D_MODEL))
    ks, vs, ss = [], [], []
    for l in range(DEPTH):
        lp = {
            'w_mod': w_mod[l], 'b_mod': b_mod[l], 'g_norm': g_norm[l], 'w_in': w_in[l],
            'w_s': w_s[l], 'b_s': b_s[l],
            'rw_w0': rw_w0[l], 'rw_w_up': rw_w_up[l], 'rw_a0': rw_a0[l], 'rw_a_up': rw_a_up[l],
            'rw_k_k': rw_k_k[l], 'rw_k_a': rw_k_a[l], 'rw_r_k': rw_r_k[l], 'rw_ln_g': rw_ln_g[l], 'rw_ln_b': rw_ln_b[l],
            'att_sink': att_sink[l], 'pool_w': pool_w[l], 'pool_scale': pool_scale[l],
            'w_up': w_up[l], 'w_o': w_o[l],
        }
        xp, k_l, v_l, s_l = trunk_layer(xp, cond_ctx, lp)
        ks.append(k_l)
        vs.append(v_l)
        ss.append(s_l)
        xs, _, _, _ = trunk_layer(xs, c, lp, (cache_k[:, l], cache_v[:, l], state_rwkv[:, l]))
    y_prompt = rmsnorm(xp, g_final)
    y_sample = rmsnorm(xs, g_final)
    new_k = jnp.stack(ks, axis=1)
    new_v = jnp.stack(vs, axis=1)
    new_state = jnp.stack(ss, axis=1)
    return (y_prompt, y_sample, new_k, new_v, new_state)
```

```python
import functools

import numpy as np
import jax
import jax.numpy as jnp
from jax import lax
from jax.experimental import pallas as pl
from jax.experimental.pallas import tpu as pltpu

D_MODEL = 1024
BATCH = 16
SEQ = 256
DEPTH = 2
DEC_BATCH = 4
DEC_SEQ = 1024
PAST_LEN = 512
GRID_W = 64
N_BRANCH = 4
BR_W = D_MODEL // 4
CHUNK = 128
A_GROUPS = 4
A_GD = BR_W // A_GROUPS
RW_HD = 64
RW_HEADS = BR_W // RW_HD
DECAY_RANK = 64
ICL_RANK = 64
ATT_HD = 64
ATT_HEADS = BR_W // ATT_HD
ATT_KV_HEADS = 2
ATT_GROUP = ATT_HEADS // ATT_KV_HEADS
WINDOW = 128
ROPE_BASE = 10000.0
POOL_SIZES = (2, 4, 8, 16)
POOL_GD = BR_W // len(POOL_SIZES)
NORM_EPS = 1e-6
GN_EPS = 64e-5
NEG_INF = -1e30

N_CTX = BATCH * SEQ
N_LAT = DEC_BATCH * DEC_SEQ
N_TOK = N_CTX + N_LAT
IN_SMALL = 2176
IN_GATES = N_BRANCH * BR_W + N_BRANCH * D_MODEL
B_COLS = 3 * BR_W + DECAY_RANK + ICL_RANK

F32 = jnp.float32
BF16 = jnp.bfloat16
HIGHEST = lax.Precision.HIGHEST

V7X_VMEM_LIMIT = 56 * 1024 * 1024
LANES = 128

TM_IN = 512
TM_MERGE = 256
SCAN_TC = 32


def _cparams(sem):
    return pltpu.CompilerParams(dimension_semantics=sem, vmem_limit_bytes=V7X_VMEM_LIMIT)


def _mod_row(i, tm):
    per = DEC_SEQ // tm
    return jnp.maximum(i // per - (N_CTX // DEC_SEQ - 1), 0)


def _norm_mod(x, g, m):
    ms = jnp.mean(x * x, axis=-1, keepdims=True)
    y = x * lax.rsqrt(ms + NORM_EPS) * g
    shift = m[:, :D_MODEL]
    scale = m[:, D_MODEL:2 * D_MODEL]
    return y * (1.0 + scale) + shift


def _sigmoid(x):
    return 1.0 / (1.0 + jnp.exp(-x))


def _mod_kernel(c_ref, w_ref, b_ref, o_ref):
    cnd = c_ref[...]
    s = cnd * _sigmoid(cnd)
    o_ref[...] = jnp.dot(s.astype(BF16), w_ref[...].astype(BF16), preferred_element_type=F32) + b_ref[...]


def _modulation(cond, w_mod, b_mod):
    nt = 3 * D_MODEL // 1024
    return pl.pallas_call(
        _mod_kernel,
        grid=(DEPTH, nt),
        in_specs=[
            pl.BlockSpec((8, D_MODEL), lambda l, j: (0, 0)),
            pl.BlockSpec((None, D_MODEL, 1024), lambda l, j: (l, 0, j)),
            pl.BlockSpec((None, 1, 1024), lambda l, j: (l, 0, j)),
        ],
        out_specs=pl.BlockSpec((None, 8, 1024), lambda l, j: (l, 0, j)),
        out_shape=jax.ShapeDtypeStruct((DEPTH, 8, 3 * D_MODEL), F32),
        compiler_params=_cparams(("arbitrary", "arbitrary")),
        name="modulation",
    )(cond, w_mod, b_mod.reshape(DEPTH, 1, 3 * D_MODEL))


def _inproj_kernel(x_ref, mod_ref, g_ref, w_ref, a_ref, b_ref, q_ref, k_ref, v_ref, d_ref):
    i = pl.program_id(0)
    m = mod_ref[pl.ds(_mod_row(i, TM_IN), 1), :]
    h = _norm_mod(x_ref[...], g_ref[...], m)
    p = jnp.dot(h.astype(BF16), w_ref[...], preferred_element_type=F32)
    a_ref[...] = p[:, 0:512]
    b_ref[...] = p[:, 512:1408]
    q_ref[...] = p[:, 1408:1664]
    k_ref[...] = p[:, 1664:1792]
    v_ref[...] = p[:, 1792:1920]
    d_ref[...] = p[:, 1920:2176]


def _inproj(x, mod_l, g, w_small):
    widths = (512, B_COLS, 256, 128, 128, 256)
    return pl.pallas_call(
        _inproj_kernel,
        grid=(N_TOK // TM_IN,),
        in_specs=[
            pl.BlockSpec((TM_IN, D_MODEL), lambda i: (i, 0)),
            pl.BlockSpec((8, 3 * D_MODEL), lambda i: (0, 0)),
            pl.BlockSpec((1, D_MODEL), lambda i: (0, 0)),
            pl.BlockSpec((D_MODEL, IN_SMALL), lambda i: (0, 0)),
        ],
        out_specs=[pl.BlockSpec((TM_IN, w), lambda i: (i, 0)) for w in widths],
        out_shape=[jax.ShapeDtypeStruct((N_TOK, w), F32) for w in widths],
        compiler_params=_cparams(("arbitrary",)),
        name="in_proj",
    )(x, mod_l, g.reshape(1, D_MODEL), w_small)


def _softplus(x):
    return jnp.maximum(x, 0.0) + jnp.log1p(jnp.exp(-jnp.abs(x)))


def _rwkv_pre_kernel(b_ref, w0_ref, wup_ref, a0_ref, aup_ref, kkw_ref, ka_ref, rk_ref, ones_ref,
                     w0o, w1o, kka0o, kka1o, kd0o, kd1o, kko, bonuso):
    bm = b_ref[...]
    r = bm[:, 0:256]
    k = bm[:, 256:512]
    v = bm[:, 512:768]
    wd_t = jnp.tanh(bm[:, 768:832])
    ad = bm[:, 832:896]
    ones = ones_ref[...]
    kk = k * kkw_ref[...]
    ss = jnp.dot(kk * kk, ones, precision=HIGHEST, preferred_element_type=F32)
    kkn = kk * lax.rsqrt(ss + 1e-12)
    kko[...] = kkn
    bonus = jnp.zeros_like(v)
    for d, (wo, kkao, kdo) in enumerate(((w0o, kka0o, kd0o), (w1o, kka1o, kd1o))):
        pre = w0_ref[pl.ds(d, 1), :] + jnp.dot(wd_t, wup_ref[d], precision=HIGHEST, preferred_element_type=F32)
        w_log = -_softplus(-pre) - 0.5
        wo[...] = jnp.exp(-jnp.exp(w_log))
        a = _sigmoid(a0_ref[pl.ds(d, 1), :] + jnp.dot(ad, aup_ref[d], precision=HIGHEST, preferred_element_type=F32))
        k_d = k * (1.0 + (a - 1.0) * ka_ref[...])
        kdo[...] = k_d
        kkao[...] = kkn * a
        bonus = bonus + jnp.dot(r * k_d * rk_ref[...], ones, precision=HIGHEST, preferred_element_type=F32) * v
    bonuso[...] = bonus


def _rwkv_pre(bm, w0, wup, a0, aup, k_k, k_a, r_k, ones_bd):
    full2 = lambda s: pl.BlockSpec(s, lambda i: (0, 0))
    return pl.pallas_call(
        _rwkv_pre_kernel,
        grid=(N_TOK // TM_IN,),
        in_specs=[
            pl.BlockSpec((TM_IN, B_COLS), lambda i: (i, 0)),
            full2((2, BR_W)),
            pl.BlockSpec((2, DECAY_RANK, BR_W), lambda i: (0, 0, 0)),
            full2((2, BR_W)),
            pl.BlockSpec((2, ICL_RANK, BR_W), lambda i: (0, 0, 0)),
            full2((1, BR_W)), full2((1, BR_W)), full2((1, BR_W)),
            full2((BR_W, BR_W)),
        ],
        out_specs=[pl.BlockSpec((TM_IN, BR_W), lambda i: (i, 0))] * 8,
        out_shape=[jax.ShapeDtypeStruct((N_TOK, BR_W), F32)] * 8,
        compiler_params=_cparams(("arbitrary",)),
        name="rwkv_pre",
    )(bm, w0, wup, a0, aup, k_k.reshape(1, BR_W), k_a.reshape(1, BR_W), r_k.reshape(1, BR_W), ones_bd)


def _scan_kernel(*refs, has_init):
    if has_init:
        s0_ref, w_ref, kk_ref, kka_ref, kd_ref, r_ref, v_ref, o_ref, st_ref = refs
    else:
        w_ref, kk_ref, kka_ref, kd_ref, r_ref, v_ref, o_ref, st_ref = refs
    nk = st_ref.shape[0]
    tc = w_ref.shape[0]
    tile = st_ref.shape[1:]

    @pl.when(pl.program_id(0) == 0)
    def _():
        if has_init:
            st_ref[...] = s0_ref[...]
        else:
            st_ref[...] = jnp.zeros(st_ref.shape, F32)

    sa0 = jnp.zeros(tile, F32)
    for k in range(nk):
        sa0 = sa0 + st_ref[k] * kk_ref[0, pl.ds(k, 1), :]

    def step(t, sa):
        tn = jnp.minimum(t + 1, tc - 1)
        vt = v_ref[t]
        o = jnp.zeros(tile, F32)
        san = jnp.zeros(tile, F32)
        for k in range(nk):
            new = (st_ref[k] * w_ref[t, pl.ds(k, 1), :] - kka_ref[t, pl.ds(k, 1), :] * sa
                   + kd_ref[t, pl.ds(k, 1), :] * vt)
            st_ref[k] = new
            o = o + new * r_ref[t, pl.ds(k, 1), :]
            san = san + new * kk_ref[tn, pl.ds(k, 1), :]
        o_ref[t] = o
        return san

    lax.fori_loop(0, tc, step, sa0)


def _scan(vecs, v, s0, name):
    t_len, nk, _ = vecs[0].shape
    vh = v.shape[1]
    kspec = pl.BlockSpec((SCAN_TC, nk, LANES), lambda i: (i, 0, 0))
    vspec = pl.BlockSpec((SCAN_TC, vh, LANES), lambda i: (i, 0, 0))
    sspec = pl.BlockSpec((nk, vh, LANES), lambda i: (0, 0, 0))
    has_init = s0 is not None
    args = ((s0,) if has_init else ()) + tuple(vecs) + (v,)
    in_specs = ([sspec] if has_init else []) + [kspec] * 5 + [vspec]
    return pl.pallas_call(
        functools.partial(_scan_kernel, has_init=has_init),
        grid=(t_len // SCAN_TC,),
        in_specs=in_specs,
        out_specs=[vspec, sspec],
        out_shape=[jax.ShapeDtypeStruct((t_len, vh, LANES), F32),
                   jax.ShapeDtypeStruct((nk, vh, LANES), F32)],
        compiler_params=_cparams(("arbitrary",)),
        name=name,
    )(*args)


def _ctx_lanes(xf, xb):
    f = xf.reshape(BATCH, SEQ, RW_HEADS, RW_HD).transpose(1, 3, 0, 2).reshape(SEQ, RW_HD, 64)
    b = xb.reshape(BATCH, SEQ, RW_HEADS, RW_HD)[:, ::-1].transpose(1, 3, 0, 2).reshape(SEQ, RW_HD, 64)
    return jnp.concatenate([f, b], axis=-1)


def _ctx_unlanes(o):
    f = o[:, :, :64].reshape(SEQ, RW_HD, BATCH, RW_HEADS).transpose(2, 0, 3, 1).reshape(N_CTX, BR_W)
    b = o[::-1, :, 64:].reshape(SEQ, RW_HD, BATCH, RW_HEADS).transpose(2, 0, 3, 1).reshape(N_CTX, BR_W)
    return f, b


LAT_VS = 4
LAT_VH = RW_HD // LAT_VS


def _lat_lanes(xf, xb):
    f = xf.reshape(DEC_BATCH, DEC_SEQ, RW_HEADS, RW_HD).transpose(1, 3, 0, 2).reshape(DEC_SEQ, RW_HD, 16)
    b = xb.reshape(DEC_BATCH, DEC_SEQ, RW_HEADS, RW_HD)[:, ::-1].transpose(1, 3, 0, 2).reshape(DEC_SEQ, RW_HD, 16)
    return jnp.tile(jnp.concatenate([f, b], axis=-1), (1, 1, LAT_VS))


def _lat_values(v):
    v5 = v.reshape(DEC_BATCH, DEC_SEQ, RW_HEADS, LAT_VS, LAT_VH)
    f = v5.transpose(1, 4, 3, 0, 2)
    b = v5[:, ::-1].transpose(1, 4, 3, 0, 2)
    return jnp.stack([f, b], axis=3).reshape(DEC_SEQ, LAT_VH, LANES)


def _lat_unlanes(o):
    o6 = o.reshape(DEC_SEQ, LAT_VH, LAT_VS, 2, DEC_BATCH, RW_HEADS)
    f = o6[:, :, :, 0].transpose(3, 0, 4, 2, 1).reshape(N_LAT, BR_W)
    b = o6[::-1, :, :, 1].transpose(3, 0, 4, 2, 1).reshape(N_LAT, BR_W)
    return f, b


def _lat_state(s):
    s6 = s.reshape(DEC_BATCH, 2, RW_HEADS, LAT_VS, LAT_VH, RW_HD)
    return s6.transpose(5, 4, 3, 1, 0, 2).reshape(RW_HD, LAT_VH, LANES)


def _sink_softmax_pv(s, sk, vb):
    m = jnp.maximum(jnp.max(s, axis=-1, keepdims=True), sk)
    p = jnp.exp(s - m)
    den = jnp.sum(p, axis=-1, keepdims=True) + jnp.exp(sk - m)
    return jnp.dot(p.astype(BF16), vb, preferred_element_type=F32) / den


def _att_ctx_kernel(sink_ref, q_ref, k_ref, v_ref, o_ref):
    scale = ATT_HD ** -0.5
    q = q_ref[...]
    kb = k_ref[...].astype(BF16)
    vb = v_ref[...].astype(BF16)
    outs = []
    for hd in range(ATT_HEADS):
        kvh = hd // ATT_GROUP
        qh = q[:, hd * ATT_HD:(hd + 1) * ATT_HD].astype(BF16)
        kh = kb[:, kvh * ATT_HD:(kvh + 1) * ATT_HD]
        s = lax.dot_general(qh, kh, (((1,), (1,)), ((), ())), preferred_element_type=F32) * scale
        outs.append(_sink_softmax_pv(s, sink_ref[hd], vb[:, kvh * ATT_HD:(kvh + 1) * ATT_HD]))
    o_ref[...] = jnp.concatenate(outs, axis=1)


def _att_ctx(sink, cq, ck, cv):
    kvw = ATT_KV_HEADS * ATT_HD
    return pl.pallas_call(
        _att_ctx_kernel,
        grid=(BATCH,),
        in_specs=[
            pl.BlockSpec(memory_space=pltpu.SMEM),
            pl.BlockSpec((SEQ, BR_W), lambda b: (b, 0)),
            pl.BlockSpec((SEQ, kvw), lambda b: (b, 0)),
            pl.BlockSpec((SEQ, kvw), lambda b: (b, 0)),
        ],
        out_specs=pl.BlockSpec((SEQ, BR_W), lambda b: (b, 0)),
        out_shape=jax.ShapeDtypeStruct((N_CTX, BR_W), F32),
        compiler_params=_cparams(("arbitrary",)),
        name="att_ctx",
    )(sink, cq, ck, cv)


def _rope(x, cos, sin_signed):
    w = x.shape[1]
    lane = lax.broadcasted_iota(jnp.int32, (1, w), 1)
    first = (lane % 32) < 16
    swapped = jnp.where(first, pltpu.roll(x, w - 16, 1), pltpu.roll(x, 16, 1))
    return x * cos + swapped * sin_signed


def _att_lat_kernel(sink_ref, q_ref, k_ref, v_ref, ck_ref, cv_ref, cq_ref, sq_ref, ckk_ref, skk_ref,
                    o_ref, qs_ref, ks_ref, vs_ref):
    scale = ATT_HD ** -0.5
    nb = DEC_SEQ // CHUNK
    qs_ref[...] = _rope(q_ref[...], cq_ref[...], sq_ref[...]).astype(BF16)
    ks_ref[...] = _rope(k_ref[...], ckk_ref[...], skk_ref[...]).astype(BF16)
    vs_ref[...] = v_ref[...].astype(BF16)
    ckb = ck_ref[...].astype(BF16)
    cvb = cv_ref[...].astype(BF16)
    for n in range(nb):
        lo = max(n - 1, 0) * CHUNK
        hi = min(n + 2, nb) * CHUNK
        nloc = hi - lo
        ncol = nloc + PAST_LEN
        col = lax.broadcasted_iota(jnp.int32, (CHUNK, ncol), 1)
        row = lax.broadcasted_iota(jnp.int32, (CHUNK, ncol), 0)
        dist = jnp.abs((n * CHUNK + row) - (lo + col))
        valid = (col >= nloc) | (dist <= WINDOW)
        outs = []
        for hd in range(ATT_HEADS):
            kvh = hd // ATT_GROUP
            hs = slice(hd * ATT_HD, (hd + 1) * ATT_HD)
            ks = slice(kvh * ATT_HD, (kvh + 1) * ATT_HD)
            qh = qs_ref[n * CHUNK:(n + 1) * CHUNK, hs]
            kall = jnp.concatenate([ks_ref[lo:hi, ks], ckb[:, ks]], axis=0)
            vall = jnp.concatenate([vs_ref[lo:hi, ks], cvb[:, ks]], axis=0)
            s = lax.dot_general(qh, kall, (((1,), (1,)), ((), ())), preferred_element_type=F32) * scale
            s = jnp.where(valid, s, NEG_INF)
            outs.append(_sink_softmax_pv(s, sink_ref[hd], vall))
        o_ref[n * CHUNK:(n + 1) * CHUNK, :] = jnp.concatenate(outs, axis=1)


def _att_lat(sink, cq, ck, cv, cache_k, cache_v, layer, tables):
    kvw = ATT_KV_HEADS * ATT_HD
    off = N_CTX // DEC_SEQ
    cosq, sinq, cosk, sink_k = tables
    seq = lambda w: pl.BlockSpec((DEC_SEQ, w), lambda b: (b + off, 0))
    cache = pl.BlockSpec((None, None, PAST_LEN, kvw), lambda b: (b, layer, 0, 0))
    tab = lambda w: pl.BlockSpec((DEC_SEQ, w), lambda b: (0, 0))
    return pl.pallas_call(
        _att_lat_kernel,
        grid=(DEC_BATCH,),
        in_specs=[pl.BlockSpec(memory_space=pltpu.SMEM), seq(BR_W), seq(kvw), seq(kvw), cache, cache,
                  tab(BR_W), tab(BR_W), tab(kvw), tab(kvw)],
        out_specs=pl.BlockSpec((DEC_SEQ, BR_W), lambda b: (b, 0)),
        out_shape=jax.ShapeDtypeStruct((N_LAT, BR_W), F32),
        scratch_shapes=[pltpu.VMEM((DEC_SEQ, BR_W), BF16), pltpu.VMEM((DEC_SEQ, kvw), BF16),
                        pltpu.VMEM((DEC_SEQ, kvw), BF16)],
        compiler_params=_cparams(("arbitrary",)),
        name="att_lat",
    )(sink, cq, ck, cv, cache_k, cache_v, cosq, sinq, cosk, sink_k)


def _rope_tables():
    pos = np.arange(DEC_SEQ)
    lane = np.arange(ATT_HD)
    p = np.where(lane[None, :] < 32, (pos // GRID_W)[:, None], (pos % GRID_W)[:, None]).astype(np.float32)
    inv = (ROPE_BASE ** (-jnp.arange(0, 32, 2, dtype=F32) / 32))[lane % 16]
    ang = jnp.asarray(p) * inv[None, :]
    sign = jnp.asarray(np.where((lane % 32) < 16, -1.0, 1.0).astype(np.float32))
    cos, sin = jnp.cos(ang), jnp.sin(ang) * sign[None, :]
    return (jnp.tile(cos, (1, ATT_HEADS)), jnp.tile(sin, (1, ATT_HEADS)),
            jnp.tile(cos, (1, ATT_KV_HEADS)), jnp.tile(sin, (1, ATT_KV_HEADS)))


POOL_PAD = 8


def _pool_kernel(x_ref, w_ref, sc_ref, o_ref, pad_ref):
    n = x_ref.shape[0]
    x = x_ref[...]
    pad_ref[0:POOL_PAD, :] = jnp.zeros((POOL_PAD, BR_W), F32)
    pad_ref[POOL_PAD:POOL_PAD + n, :] = x
    pad_ref[POOL_PAD + n:2 * POOL_PAD + n, :] = jnp.zeros((POOL_PAD, BR_W), F32)
    grp = lax.broadcasted_iota(jnp.int32, (1, BR_W), 1) // POOL_GD
    half = jnp.left_shift(1, grp)
    acc = jnp.zeros((n, BR_W), F32)
    for j in range(-POOL_PAD, POOL_PAD):
        inside = (j >= -half) & (j < half)
        acc = acc + jnp.where(inside, pad_ref[POOL_PAD + j:POOL_PAD + j + n, :], 0.0)
    t = lax.broadcasted_iota(jnp.int32, (n, BR_W), 0)
    cnt = jnp.minimum(t + half, n) - jnp.maximum(t - half, 0)
    d = acc / cnt.astype(F32) - x
    o_ref[...] = jnp.dot(d.astype(BF16), w_ref[...], preferred_element_type=F32) * sc_ref[...]


def _pool(dp, w_bd, scale, n_seq, seq_len, row_off, name):
    off = row_off // seq_len
    return pl.pallas_call(
        _pool_kernel,
        grid=(n_seq,),
        in_specs=[pl.BlockSpec((seq_len, BR_W), lambda b: (b + off, 0)),
                  pl.BlockSpec((BR_W, BR_W), lambda b: (0, 0)),
                  pl.BlockSpec((1, BR_W), lambda b: (0, 0))],
        out_specs=pl.BlockSpec((seq_len, BR_W), lambda b: (b, 0)),
        out_shape=jax.ShapeDtypeStruct((n_seq * seq_len, BR_W), F32),
        scratch_shapes=[pltpu.VMEM((seq_len + 2 * POOL_PAD, BR_W), F32)],
        compiler_params=_cparams(("arbitrary",)),
        name=name,
    )(dp, w_bd, scale.reshape(1, BR_W))


def _merge_kernel(x_ref, mod_ref, g_ref, a_ref, of_ref, ob_ref, bonus_ref, yc_ref, yd_ref,
                  wz_ref, wmg_ref, ws_ref, bs_ref, lng_ref, lnb_ref, ones_ref, wup_ref, wo_ref, gf_ref,
                  o_ref, *, final):
    i = pl.program_id(0)
    m = mod_ref[pl.ds(_mod_row(i, TM_MERGE), 1), :]
    x = x_ref[...]
    hb = _norm_mod(x, g_ref[...], m).astype(BF16)

    a = a_ref[...]
    a_u, a_v = a[:, :BR_W], a[:, BR_W:]
    grp = lax.broadcasted_iota(jnp.int32, (1, BR_W), 1) // A_GD
    svs = []
    for c in range(TM_MERGE // CHUNK):
        vc = a_v[c * CHUNK:(c + 1) * CHUNK, :]
        sv = bs_ref[...]
        for g in range(A_GROUPS):
            sv = sv + jnp.dot(ws_ref[g], jnp.where(grp == g, vc, 0.0).astype(BF16), preferred_element_type=F32)
        svs.append(sv)
    y_a = a_u * jnp.concatenate(svs, axis=0)

    ones = ones_ref[...]
    osum = of_ref[...] + ob_ref[...]
    mu = jnp.dot(osum, ones, precision=HIGHEST, preferred_element_type=F32) * (1.0 / RW_HD)
    dev = osum - mu
    var = jnp.dot(dev * dev, ones, precision=HIGHEST, preferred_element_type=F32) * (1.0 / RW_HD)
    y_b = dev * lax.rsqrt(var + GN_EPS) * lng_ref[...] + lnb_ref[...] + bonus_ref[...]

    merged = jnp.zeros((TM_MERGE, D_MODEL), F32)
    for n, y in enumerate((y_a, y_b, yc_ref[...], yd_ref[...])):
        z = jnp.dot(hb, wz_ref[:, n * BR_W:(n + 1) * BR_W], preferred_element_type=F32)
        ys = y * (z * _sigmoid(z))
        up = jnp.dot(ys.astype(BF16), wup_ref[n], preferred_element_type=F32)
        mg = jnp.dot(hb, wmg_ref[:, n * D_MODEL:(n + 1) * D_MODEL], preferred_element_type=F32)
        merged = merged + _sigmoid(mg) * up
    gate = m[:, 2 * D_MODEL:]
    out = x + gate * jnp.dot(merged.astype(BF16), wo_ref[...], preferred_element_type=F32)
    if final:
        ms = jnp.mean(out * out, axis=-1, keepdims=True)
        out = out * lax.rsqrt(ms + NORM_EPS) * gf_ref[...]
    o_ref[...] = out


def _merge(x, mod_l, g, a, o_f, o_b, bonus, y_c, y_d, wz, wmg, ws, bs_tile, ln_g, ln_b, ones_bd, wup, wo,
           g_final, final):
    row = lambda w: pl.BlockSpec((TM_MERGE, w), lambda i: (i, 0))
    full2 = lambda s: pl.BlockSpec(s, lambda i: (0, 0))
    full3 = lambda s: pl.BlockSpec(s, lambda i: (0, 0, 0))
    return pl.pallas_call(
        functools.partial(_merge_kernel, final=final),
        grid=(N_TOK // TM_MERGE,),
        in_specs=[row(D_MODEL), full2((8, 3 * D_MODEL)), full2((1, D_MODEL)), row(2 * BR_W),
                  row(BR_W), row(BR_W), row(BR_W), row(BR_W), row(BR_W),
                  full2((D_MODEL, N_BRANCH * BR_W)), full2((D_MODEL, N_BRANCH * D_MODEL)),
                  full3((A_GROUPS, CHUNK, CHUNK)), full2((CHUNK, BR_W)),
                  full2((1, BR_W)), full2((1, BR_W)), full2((BR_W, BR_W)),
                  full3((N_BRANCH, BR_W, D_MODEL)), full2((D_MODEL, D_MODEL)), full2((1, D_MODEL))],
        out_specs=row(D_MODEL),
        out_shape=jax.ShapeDtypeStruct((N_TOK, D_MODEL), F32),
        compiler_params=_cparams(("arbitrary",)),
        name="merge_final" if final else "merge",
    )(x, mod_l, g.reshape(1, D_MODEL), a, o_f, o_b, bonus, y_c, y_d, wz, wmg, ws, bs_tile,
      ln_g.reshape(1, BR_W), ln_b.reshape(1, BR_W), ones_bd, wup, wo, g_final.reshape(1, D_MODEL))


def _block_diag(blocks):
    n, r, c = blocks.shape
    eye = jnp.eye(n, dtype=blocks.dtype)
    return (eye[:, None, :, None] * blocks[:, :, None, :]).reshape(n * r, n * c)


def kernel(x_prompt, x_sample, cache_k, cache_v, state_rwkv, c, c_ctx, w_mod, b_mod, g_norm, w_in, w_s, b_s,
           rw_w0, rw_w_up, rw_a0, rw_a_up, rw_k_k, rw_k_a, rw_r_k, rw_ln_g, rw_ln_b, att_sink, pool_w,
           pool_scale, w_up, w_o, g_final):
    x = jnp.concatenate([x_prompt.reshape(N_CTX, D_MODEL), x_sample.reshape(N_LAT, D_MODEL)], axis=0)
    cond = jnp.concatenate([c_ctx[None, :], c, jnp.zeros((8 - 1 - DEC_BATCH, D_MODEL), F32)], axis=0)
    mod = _modulation(cond, w_mod, b_mod)

    w_in_b = w_in.astype(BF16)
    w_s_b = w_s.astype(BF16)
    w_up_b = w_up.astype(BF16)
    w_o_b = w_o.astype(BF16)
    ones_bd = _block_diag(jnp.ones((RW_HEADS, RW_HD, RW_HD), F32))
    tables = _rope_tables()
    kvw = ATT_KV_HEADS * ATT_HD
    cache_k4 = cache_k.reshape(DEC_BATCH, DEPTH, PAST_LEN, kvw)
    cache_v4 = cache_v.reshape(DEC_BATCH, DEPTH, PAST_LEN, kvw)

    new_k, new_v, new_s = [], [], []
    for l in range(DEPTH):
        a, bm, cq, ck, cv, dp = _inproj(x, mod[l], g_norm[l], w_in_b[l, :, :IN_SMALL])
        new_k.append(ck[:N_CTX].reshape(BATCH, SEQ, ATT_KV_HEADS, ATT_HD))
        new_v.append(cv[:N_CTX].reshape(BATCH, SEQ, ATT_KV_HEADS, ATT_HD))

        w0, w1, kka0, kka1, kd0, kd1, kk, bonus = _rwkv_pre(
            bm, rw_w0[l], rw_w_up[l], rw_a0[l], rw_a_up[l], rw_k_k[l], rw_k_a[l], rw_r_k[l], ones_bd)
        r_all, v_all = bm[:, 0:BR_W], bm[:, 2 * BR_W:3 * BR_W]
        cs = slice(0, N_CTX)
        ls = slice(N_CTX, N_TOK)
        o_ctx, s_ctx = _scan(
            [_ctx_lanes(w0[cs], w1[cs]), _ctx_lanes(kk[cs], kk[cs]), _ctx_lanes(kka0[cs], kka1[cs]),
             _ctx_lanes(kd0[cs], kd1[cs]), _ctx_lanes(r_all[cs], r_all[cs])],
            _ctx_lanes(v_all[cs], v_all[cs]), None, "scan_ctx")
        o_lat, _ = _scan(
            [_lat_lanes(w0[ls], w1[ls]), _lat_lanes(kk[ls], kk[ls]), _lat_lanes(kka0[ls], kka1[ls]),
             _lat_lanes(kd0[ls], kd1[ls]), _lat_lanes(r_all[ls], r_all[ls])],
            _lat_values(v_all[ls]), _lat_state(state_rwkv[:, l]), "scan_lat")
        of_c, ob_c = _ctx_unlanes(o_ctx)
        of_l, ob_l = _lat_unlanes(o_lat)
        o_f = jnp.concatenate([of_c, of_l], axis=0)
        o_b = jnp.concatenate([ob_c, ob_l], axis=0)
        new_s.append(s_ctx.reshape(RW_HD, RW_HD, 2, BATCH, RW_HEADS).transpose(3, 2, 4, 1, 0))

        y_c = jnp.concatenate([
            _att_ctx(att_sink[l], cq, ck, cv),
            _att_lat(att_sink[l], cq, ck, cv, cache_k4, cache_v4, l, tables)], axis=0)

        pw = _block_diag(pool_w[l]).astype(BF16)
        y_d = jnp.concatenate([
            _pool(dp, pw, pool_scale[l], BATCH, SEQ, 0, "pool_ctx"),
            _pool(dp, pw, pool_scale[l], DEC_BATCH, DEC_SEQ, N_CTX, "pool_lat")], axis=0)

        bs_tile = jnp.repeat(b_s[l].T, A_GD, axis=1)
        x = _merge(x, mod[l], g_norm[l], a, o_f, o_b, bonus, y_c, y_d,
                   w_in_b[l, :, IN_SMALL:IN_SMALL + N_BRANCH * BR_W], w_in_b[l, :, IN_SMALL + N_BRANCH * BR_W:],
                   w_s_b[l], bs_tile, rw_ln_g[l], rw_ln_b[l], ones_bd, w_up_b[l], w_o_b[l], g_final,
                   final=(l == DEPTH - 1))

    y_prompt = x[:N_CTX].reshape(BATCH, SEQ, D_MODEL)
    y_sample = x[N_CTX:].reshape(DEC_BATCH, DEC_SEQ, D_MODEL)
    return (y_prompt, y_sample, jnp.stack(new_k, axis=1), jnp.stack(new_v, axis=1), jnp.stack(new_s, axis=1))
```

```python
import functools

import numpy as np
import jax
import jax.numpy as jnp
from jax import lax
from jax.experimental import pallas as pl
from jax.experimental.pallas import tpu as pltpu

D_MODEL = 1024
BATCH = 16
SEQ = 256
DEPTH = 2
DEC_BATCH = 4
DEC_SEQ = 1024
PAST_LEN = 512
GRID_W = 64
N_BRANCH = 4
BR_W = D_MODEL // 4
CHUNK = 128
A_GROUPS = 4
A_GD = BR_W // A_GROUPS
RW_HD = 64
RW_HEADS = BR_W // RW_HD
DECAY_RANK = 64
ICL_RANK = 64
ATT_HD = 64
ATT_HEADS = BR_W // ATT_HD
ATT_KV_HEADS = 2
ATT_GROUP = ATT_HEADS // ATT_KV_HEADS
WINDOW = 128
ROPE_BASE = 10000.0
POOL_SIZES = (2, 4, 8, 16)
POOL_GD = BR_W // len(POOL_SIZES)
NORM_EPS = 1e-6
GN_EPS = 64e-5
NEG_INF = -1e30

N_CTX = BATCH * SEQ
N_LAT = DEC_BATCH * DEC_SEQ
N_TOK = N_CTX + N_LAT
IN_SMALL = 2176
IN_GATES = N_BRANCH * BR_W + N_BRANCH * D_MODEL
B_COLS = 3 * BR_W + DECAY_RANK + ICL_RANK

F32 = jnp.float32
BF16 = jnp.bfloat16
HIGHEST = lax.Precision.HIGHEST

V7X_VMEM_LIMIT = 56 * 1024 * 1024
LANES = 128

TM_IN = 512
TM_MERGE = 256
SCAN_TC = 32


def _cparams(sem):
    return pltpu.CompilerParams(dimension_semantics=sem, vmem_limit_bytes=V7X_VMEM_LIMIT)


def _mod_row(i, tm):
    per = DEC_SEQ // tm
    return jnp.maximum(i // per - (N_CTX // DEC_SEQ - 1), 0)


def _norm_mod(x, g, m):
    ms = jnp.mean(x * x, axis=-1, keepdims=True)
    y = x * lax.rsqrt(ms + NORM_EPS) * g
    shift = m[:, :D_MODEL]
    scale = m[:, D_MODEL:2 * D_MODEL]
    return y * (1.0 + scale) + shift


def _sigmoid(x):
    return 1.0 / (1.0 + jnp.exp(-x))


def _mod_kernel(c_ref, w_ref, b_ref, o_ref):
    cnd = c_ref[...]
    s = cnd * _sigmoid(cnd)
    o_ref[...] = jnp.dot(s.astype(BF16), w_ref[...].astype(BF16), preferred_element_type=F32) + b_ref[...]


def _modulation(cond, w_mod, b_mod):
    nt = 3 * D_MODEL // 1024
    return pl.pallas_call(
        _mod_kernel,
        grid=(DEPTH, nt),
        in_specs=[
            pl.BlockSpec((8, D_MODEL), lambda l, j: (0, 0)),
            pl.BlockSpec((None, D_MODEL, 1024), lambda l, j: (l, 0, j)),
            pl.BlockSpec((None, 1, 1024), lambda l, j: (l, 0, j)),
        ],
        out_specs=pl.BlockSpec((None, 8, 1024), lambda l, j: (l, 0, j)),
        out_shape=jax.ShapeDtypeStruct((DEPTH, 8, 3 * D_MODEL), F32),
        compiler_params=_cparams(("arbitrary", "arbitrary")),
        name="modulation",
    )(cond, w_mod, b_mod.reshape(DEPTH, 1, 3 * D_MODEL))


def _inproj_kernel(x_ref, mod_ref, g_ref, w_ref, a_ref, b_ref, q_ref, k_ref, v_ref, d_ref):
    i = pl.program_id(0)
    m = mod_ref[pl.ds(_mod_row(i, TM_IN), 1), :]
    h = _norm_mod(x_ref[...], g_ref[...], m)
    p = jnp.dot(h.astype(BF16), w_ref[...], preferred_element_type=F32)
    a_ref[...] = p[:, 0:512]
    b_ref[...] = p[:, 512:1408]
    q_ref[...] = p[:, 1408:1664]
    k_ref[...] = p[:, 1664:1792]
    v_ref[...] = p[:, 1792:1920]
    d_ref[...] = p[:, 1920:2176]


def _inproj(x, mod_l, g, w_small):
    widths = (512, B_COLS, 256, 128, 128, 256)
    return pl.pallas_call(
        _inproj_kernel,
        grid=(N_TOK // TM_IN,),
        in_specs=[
            pl.BlockSpec((TM_IN, D_MODEL), lambda i: (i, 0)),
            pl.BlockSpec((8, 3 * D_MODEL), lambda i: (0, 0)),
            pl.BlockSpec((1, D_MODEL), lambda i: (0, 0)),
            pl.BlockSpec((D_MODEL, IN_SMALL), lambda i: (0, 0)),
        ],
        out_specs=[pl.BlockSpec((TM_IN, w), lambda i: (i, 0)) for w in widths],
        out_shape=[jax.ShapeDtypeStruct((N_TOK, w), F32) for w in widths],
        compiler_params=_cparams(("arbitrary",)),
        name="in_proj",
    )(x, mod_l, g.reshape(1, D_MODEL), w_small)


def _softplus(x):
    return jnp.maximum(x, 0.0) + jnp.log1p(jnp.exp(-jnp.abs(x)))


def _rwkv_pre_kernel(b_ref, w0_ref, wup_ref, a0_ref, aup_ref, kkw_ref, ka_ref, rk_ref, ones_ref,
                     w0o, w1o, kka0o, kka1o, kd0o, kd1o, kko, bonuso):
    bm = b_ref[...]
    r = bm[:, 0:256]
    k = bm[:, 256:512]
    v = bm[:, 512:768]
    wd_t = jnp.tanh(bm[:, 768:832])
    ad = bm[:, 832:896]
    ones = ones_ref[...]
    kk = k * kkw_ref[...]
    ss = jnp.dot(kk * kk, ones, precision=HIGHEST, preferred_element_type=F32)
    kkn = kk * lax.rsqrt(ss + 1e-12)
    kko[...] = kkn
    bonus = jnp.zeros_like(v)
    for d, (wo, kkao, kdo) in enumerate(((w0o, kka0o, kd0o), (w1o, kka1o, kd1o))):
        pre = w0_ref[pl.ds(d, 1), :] + jnp.dot(wd_t, wup_ref[d], precision=HIGHEST, preferred_element_type=F32)
        w_log = -_softplus(-pre) - 0.5
        wo[...] = jnp.exp(-jnp.exp(w_log))
        a = _sigmoid(a0_ref[pl.ds(d, 1), :] + jnp.dot(ad, aup_ref[d], precision=HIGHEST, preferred_element_type=F32))
        k_d = k * (1.0 + (a - 1.0) * ka_ref[...])
        kdo[...] = k_d
        kkao[...] = kkn * a
        bonus = bonus + jnp.dot(r * k_d * rk_ref[...], ones, precision=HIGHEST, preferred_element_type=F32) * v
    bonuso[...] = bonus


def _rwkv_pre(bm, w0, wup, a0, aup, k_k, k_a, r_k, ones_bd):
    full2 = lambda s: pl.BlockSpec(s, lambda i: (0, 0))
    return pl.pallas_call(
        _rwkv_pre_kernel,
        grid=(N_TOK // TM_IN,),
        in_specs=[
            pl.BlockSpec((TM_IN, B_COLS), lambda i: (i, 0)),
            full2((2, BR_W)),
            pl.BlockSpec((2, DECAY_RANK, BR_W), lambda i: (0, 0, 0)),
            full2((2, BR_W)),
            pl.BlockSpec((2, ICL_RANK, BR_W), lambda i: (0, 0, 0)),
            full2((1, BR_W)), full2((1, BR_W)), full2((1, BR_W)),
            full2((BR_W, BR_W)),
        ],
        out_specs=[pl.BlockSpec((TM_IN, BR_W), lambda i: (i, 0))] * 8,
        out_shape=[jax.ShapeDtypeStruct((N_TOK, BR_W), F32)] * 8,
        compiler_params=_cparams(("arbitrary",)),
        name="rwkv_pre",
    )(bm, w0, wup, a0, aup, k_k.reshape(1, BR_W), k_a.reshape(1, BR_W), r_k.reshape(1, BR_W), ones_bd)


def _to_lanes(z0, z1, rep):
    lo = lax.broadcasted_iota(jnp.int32, (1, LANES), 1) < RW_HD
    parts = []
    for hp in range(RW_HEADS // 2):
        a = z0[:, hp * LANES:(hp + 1) * LANES]
        b = z1[:, hp * LANES:(hp + 1) * LANES]
        parts.append(jnp.where(lo, a, pltpu.roll(b, RW_HD, 1)))
        parts.append(jnp.where(lo, pltpu.roll(a, RW_HD, 1), b))
    return jnp.concatenate(parts * rep, axis=0).T


def _scan_kernel(*refs, nb, row0, has_init):
    if has_init:
        s0_ref, refs = refs[0], refs[1:]
    (w0_ref, w1_ref, kka0_ref, kka1_ref, kd0_ref, kd1_ref, kkf_ref, kkb_ref, bmf_ref, bmb_ref,
     of_ref, ob_ref, st_ref, w_s, kk_s, kka_s, kd_s, r_s, v_s, o_s) = refs
    nk, vh, _ = st_ref.shape
    tc = w_s.shape[0]
    rep = RW_HD // vh
    nrec = LANES // rep
    lane = lax.broadcasted_iota(jnp.int32, (1, LANES), 1)
    grp = lane // nrec
    lo = lane < RW_HD

    def rows(f_ref, b_ref, j, width, col):
        f = f_ref[row0:row0 + nb, j * width + col:j * width + col + BR_W]
        jb = tc - 1 - j
        b = b_ref[row0:row0 + nb, jb * width + col:jb * width + col + BR_W]
        return jnp.concatenate([f, b], axis=0)

    sources = ((w0_ref, w1_ref, BR_W, 0, w_s), (kkf_ref, kkb_ref, BR_W, 0, kk_s),
               (kka0_ref, kka1_ref, BR_W, 0, kka_s), (kd0_ref, kd1_ref, BR_W, 0, kd_s),
               (bmf_ref, bmb_ref, B_COLS, 0, r_s))
    for j in range(0, tc, 2):
        for f_ref, b_ref, width, col, dst in sources:
            gt = _to_lanes(rows(f_ref, b_ref, j, width, col), rows(f_ref, b_ref, j + 1, width, col), rep)
            dst[j] = gt[0:RW_HD]
            dst[j + 1] = gt[RW_HD:2 * RW_HD]
        gt = _to_lanes(rows(bmf_ref, bmb_ref, j, B_COLS, 2 * BR_W),
                       rows(bmf_ref, bmb_ref, j + 1, B_COLS, 2 * BR_W), rep)
        for s in range(2):
            blk = gt[s * RW_HD:(s + 1) * RW_HD]
            if rep > 1:
                blk = sum(jnp.where(grp == q, blk[q * vh:(q + 1) * vh], 0.0) for q in range(rep))
            v_s[j + s] = blk

    @pl.when(pl.program_id(0) == 0)
    def _():
        if has_init:
            st_ref[...] = s0_ref[...]
        else:
            st_ref[...] = jnp.zeros(st_ref.shape, F32)

    sa0 = jnp.zeros((vh, LANES), F32)
    for k in range(nk):
        sa0 = sa0 + st_ref[k] * kk_s[0, pl.ds(k, 1), :]

    def step(t, sa):
        tn = jnp.minimum(t + 1, tc - 1)
        vt = v_s[t]
        o = jnp.zeros((vh, LANES), F32)
        san = jnp.zeros((vh, LANES), F32)
        for k in range(nk):
            new = (st_ref[k] * w_s[t, pl.ds(k, 1), :] - kka_s[t, pl.ds(k, 1), :] * sa
                   + kd_s[t, pl.ds(k, 1), :] * vt)
            st_ref[k] = new
            o = o + new * r_s[t, pl.ds(k, 1), :]
            san = san + new * kk_s[tn, pl.ds(k, 1), :]
        o_s[t] = o
        return san

    lax.fori_loop(0, tc, step, sa0)

    for j in range(0, tc, 2):
        tiles = []
        for s in range(2):
            o = o_s[j + s]
            if rep == 1:
                tiles.append(o)
            else:
                tiles.extend(jnp.where(grp == q, o, 0.0) for q in range(rep))
        mt = jnp.concatenate(tiles, axis=0).T
        sm = mt[0:nrec]
        for q in range(1, rep):
            sm = sm + mt[q * nrec:(q + 1) * nrec]
        rs = pltpu.roll(sm, RW_HD, 1)
        for hp in range(RW_HEADS // 2):
            e0, e1 = (2 * hp) * 2 * nb, (2 * hp + 1) * 2 * nb
            cur = jnp.where(lo, sm[e0:e0 + 2 * nb], rs[e1:e1 + 2 * nb])
            nxt = jnp.where(lo, rs[e0:e0 + 2 * nb], sm[e1:e1 + 2 * nb])
            c0 = hp * LANES
            of_ref[:, j * BR_W + c0:j * BR_W + c0 + LANES] = cur[0:nb]
            of_ref[:, (j + 1) * BR_W + c0:(j + 1) * BR_W + c0 + LANES] = nxt[0:nb]
            jb = tc - 1 - j
            ob_ref[:, jb * BR_W + c0:jb * BR_W + c0 + LANES] = cur[nb:2 * nb]
            ob_ref[:, (jb - 1) * BR_W + c0:(jb - 1) * BR_W + c0 + LANES] = nxt[nb:2 * nb]


def _scan(pre, bm, s0, *, nb, t_len, name):
    w0, w1, kka0, kka1, kd0, kd1, kk = (p.reshape(N_TOK // t_len, t_len * BR_W) for p in pre)
    bm2 = bm.reshape(N_TOK // t_len, t_len * B_COLS)
    nrow = max(nb, 8)
    row0 = (N_CTX // t_len) % nrow if nb < 8 else 0
    row_blk = 0 if nb >= 8 else (N_CTX // t_len) // nrow
    n_t = t_len // SCAN_TC
    vh = RW_HD * nb * 2 * RW_HEADS // LANES
    fwd = lambda w: pl.BlockSpec((nrow, SCAN_TC * w), lambda i: (row_blk, i))
    bwd = lambda w: pl.BlockSpec((nrow, SCAN_TC * w), lambda i: (row_blk, n_t - 1 - i))
    sspec = pl.BlockSpec((RW_HD, vh, LANES), lambda i: (0, 0, 0))
    has_init = s0 is not None
    args = ((s0,) if has_init else ()) + (w0, w1, kka0, kka1, kd0, kd1, kk, kk, bm2, bm2)
    in_specs = ([sspec] if has_init else []) + [fwd(BR_W), bwd(BR_W)] * 4 + [fwd(B_COLS), bwd(B_COLS)]
    kbuf = pltpu.VMEM((SCAN_TC, RW_HD, LANES), F32)
    vbuf = pltpu.VMEM((SCAN_TC, vh, LANES), F32)
    return pl.pallas_call(
        functools.partial(_scan_kernel, nb=nb, row0=row0, has_init=has_init),
        grid=(n_t,),
        in_specs=in_specs,
        out_specs=[pl.BlockSpec((nb, SCAN_TC * BR_W), lambda i: (0, i)),
                   pl.BlockSpec((nb, SCAN_TC * BR_W), lambda i: (0, n_t - 1 - i)),
                   sspec],
        out_shape=[jax.ShapeDtypeStruct((nb, t_len * BR_W), F32),
                   jax.ShapeDtypeStruct((nb, t_len * BR_W), F32),
                   jax.ShapeDtypeStruct((RW_HD, vh, LANES), F32)],
        scratch_shapes=[kbuf] * 5 + [vbuf] * 2,
        compiler_params=_cparams(("arbitrary",)),
        name=name,
    )(*args)


LAT_VS = 4
LAT_VH = RW_HD // LAT_VS


def _lat_state(s):
    s6 = s.reshape(DEC_BATCH, 2, RW_HEADS, LAT_VS, LAT_VH, RW_HD)
    return s6.transpose(5, 4, 3, 2, 1, 0).reshape(RW_HD, LAT_VH, LANES)


def _sink_softmax_pv(s, sk, vb):
    m = jnp.maximum(jnp.max(s, axis=-1, keepdims=True), sk)
    p = jnp.exp(s - m)
    den = jnp.sum(p, axis=-1, keepdims=True) + jnp.exp(sk - m)
    return jnp.dot(p.astype(BF16), vb, preferred_element_type=F32) / den


def _att_ctx_kernel(sink_ref, q_ref, k_ref, v_ref, o_ref):
    scale = ATT_HD ** -0.5
    q = q_ref[...]
    kb = k_ref[...].astype(BF16)
    vb = v_ref[...].astype(BF16)
    outs = []
    for hd in range(ATT_HEADS):
        kvh = hd // ATT_GROUP
        qh = q[:, hd * ATT_HD:(hd + 1) * ATT_HD].astype(BF16)
        kh = kb[:, kvh * ATT_HD:(kvh + 1) * ATT_HD]
        s = lax.dot_general(qh, kh, (((1,), (1,)), ((), ())), preferred_element_type=F32) * scale
        outs.append(_sink_softmax_pv(s, sink_ref[hd], vb[:, kvh * ATT_HD:(kvh + 1) * ATT_HD]))
    o_ref[...] = jnp.concatenate(outs, axis=1)


def _att_ctx(sink, cq, ck, cv):
    kvw = ATT_KV_HEADS * ATT_HD
    return pl.pallas_call(
        _att_ctx_kernel,
        grid=(BATCH,),
        in_specs=[
            pl.BlockSpec(memory_space=pltpu.SMEM),
            pl.BlockSpec((SEQ, BR_W), lambda b: (b, 0)),
            pl.BlockSpec((SEQ, kvw), lambda b: (b, 0)),
            pl.BlockSpec((SEQ, kvw), lambda b: (b, 0)),
        ],
        out_specs=pl.BlockSpec((SEQ, BR_W), lambda b: (b, 0)),
        out_shape=jax.ShapeDtypeStruct((N_CTX, BR_W), F32),
        compiler_params=_cparams(("arbitrary",)),
        name="att_ctx",
    )(sink, cq, ck, cv)


def _rope(x, cos, sin_signed):
    w = x.shape[1]
    lane = lax.broadcasted_iota(jnp.int32, (1, w), 1)
    first = (lane % 32) < 16
    swapped = jnp.where(first, pltpu.roll(x, w - 16, 1), pltpu.roll(x, 16, 1))
    return x * cos + swapped * sin_signed


def _att_lat_kernel(sink_ref, q_ref, k_ref, v_ref, ck_ref, cv_ref, cq_ref, sq_ref, ckk_ref, skk_ref,
                    o_ref, qs_ref, ks_ref, vs_ref):
    scale = ATT_HD ** -0.5
    nb = DEC_SEQ // CHUNK
    qs_ref[...] = _rope(q_ref[...], cq_ref[...], sq_ref[...]).astype(BF16)
    ks_ref[...] = _rope(k_ref[...], ckk_ref[...], skk_ref[...]).astype(BF16)
    vs_ref[...] = v_ref[...].astype(BF16)
    ckb = ck_ref[...].astype(BF16)
    cvb = cv_ref[...].astype(BF16)
    for n in range(nb):
        lo = max(n - 1, 0) * CHUNK
        hi = min(n + 2, nb) * CHUNK
        nloc = hi - lo
        ncol = nloc + PAST_LEN
        col = lax.broadcasted_iota(jnp.int32, (CHUNK, ncol), 1)
        row = lax.broadcasted_iota(jnp.int32, (CHUNK, ncol), 0)
        dist = jnp.abs((n * CHUNK + row) - (lo + col))
        valid = (col >= nloc) | (dist <= WINDOW)
        outs = []
        for hd in range(ATT_HEADS):
            kvh = hd // ATT_GROUP
            hs = slice(hd * ATT_HD, (hd + 1) * ATT_HD)
            ks = slice(kvh * ATT_HD, (kvh + 1) * ATT_HD)
            qh = qs_ref[n * CHUNK:(n + 1) * CHUNK, hs]
            kall = jnp.concatenate([ks_ref[lo:hi, ks], ckb[:, ks]], axis=0)
            vall = jnp.concatenate([vs_ref[lo:hi, ks], cvb[:, ks]], axis=0)
            s = lax.dot_general(qh, kall, (((1,), (1,)), ((), ())), preferred_element_type=F32) * scale
            s = jnp.where(valid, s, NEG_INF)
            outs.append(_sink_softmax_pv(s, sink_ref[hd], vall))
        o_ref[n * CHUNK:(n + 1) * CHUNK, :] = jnp.concatenate(outs, axis=1)


def _att_lat(sink, cq, ck, cv, cache_k, cache_v, layer, tables):
    kvw = ATT_KV_HEADS * ATT_HD
    off = N_CTX // DEC_SEQ
    cosq, sinq, cosk, sink_k = tables
    seq = lambda w: pl.BlockSpec((DEC_SEQ, w), lambda b: (b + off, 0))
    cache = pl.BlockSpec((None, None, PAST_LEN, kvw), lambda b: (b, layer, 0, 0))
    tab = lambda w: pl.BlockSpec((DEC_SEQ, w), lambda b: (0, 0))
    return pl.pallas_call(
        _att_lat_kernel,
        grid=(DEC_BATCH,),
        in_specs=[pl.BlockSpec(memory_space=pltpu.SMEM), seq(BR_W), seq(kvw), seq(kvw), cache, cache,
                  tab(BR_W), tab(BR_W), tab(kvw), tab(kvw)],
        out_specs=pl.BlockSpec((DEC_SEQ, BR_W), lambda b: (b, 0)),
        out_shape=jax.ShapeDtypeStruct((N_LAT, BR_W), F32),
        scratch_shapes=[pltpu.VMEM((DEC_SEQ, BR_W), BF16), pltpu.VMEM((DEC_SEQ, kvw), BF16),
                        pltpu.VMEM((DEC_SEQ, kvw), BF16)],
        compiler_params=_cparams(("arbitrary",)),
        name="att_lat",
    )(sink, cq, ck, cv, cache_k, cache_v, cosq, sinq, cosk, sink_k)


def _rope_tables():
    pos = np.arange(DEC_SEQ)
    lane = np.arange(ATT_HD)
    p = np.where(lane[None, :] < 32, (pos // GRID_W)[:, None], (pos % GRID_W)[:, None]).astype(np.float32)
    inv = (ROPE_BASE ** (-jnp.arange(0, 32, 2, dtype=F32) / 32))[lane % 16]
    ang = jnp.asarray(p) * inv[None, :]
    sign = jnp.asarray(np.where((lane % 32) < 16, -1.0, 1.0).astype(np.float32))
    cos, sin = jnp.cos(ang), jnp.sin(ang) * sign[None, :]
    return (jnp.tile(cos, (1, ATT_HEADS)), jnp.tile(sin, (1, ATT_HEADS)),
            jnp.tile(cos, (1, ATT_KV_HEADS)), jnp.tile(sin, (1, ATT_KV_HEADS)))


POOL_PAD = 8


def _pool_kernel(x_ref, w_ref, sc_ref, o_ref, pad_ref):
    n = x_ref.shape[0]
    x = x_ref[...]
    pad_ref[0:POOL_PAD, :] = jnp.zeros((POOL_PAD, BR_W), F32)
    pad_ref[POOL_PAD:POOL_PAD + n, :] = x
    pad_ref[POOL_PAD + n:2 * POOL_PAD + n, :] = jnp.zeros((POOL_PAD, BR_W), F32)
    grp = lax.broadcasted_iota(jnp.int32, (1, BR_W), 1) // POOL_GD
    half = jnp.left_shift(1, grp)
    acc = jnp.zeros((n, BR_W), F32)
    for j in range(-POOL_PAD, POOL_PAD):
        inside = (j >= -half) & (j < half)
        acc = acc + jnp.where(inside, pad_ref[POOL_PAD + j:POOL_PAD + j + n, :], 0.0)
    t = lax.broadcasted_iota(jnp.int32, (n, BR_W), 0)
    cnt = jnp.minimum(t + half, n) - jnp.maximum(t - half, 0)
    d = acc / cnt.astype(F32) - x
    o_ref[...] = jnp.dot(d.astype(BF16), w_ref[...], preferred_element_type=F32) * sc_ref[...]


def _pool(dp, w_bd, scale, n_seq, seq_len, row_off, name):
    off = row_off // seq_len
    return pl.pallas_call(
        _pool_kernel,
        grid=(n_seq,),
        in_specs=[pl.BlockSpec((seq_len, BR_W), lambda b: (b + off, 0)),
                  pl.BlockSpec((BR_W, BR_W), lambda b: (0, 0)),
                  pl.BlockSpec((1, BR_W), lambda b: (0, 0))],
        out_specs=pl.BlockSpec((seq_len, BR_W), lambda b: (b, 0)),
        out_shape=jax.ShapeDtypeStruct((n_seq * seq_len, BR_W), F32),
        scratch_shapes=[pltpu.VMEM((seq_len + 2 * POOL_PAD, BR_W), F32)],
        compiler_params=_cparams(("arbitrary",)),
        name=name,
    )(dp, w_bd, scale.reshape(1, BR_W))


def _merge_kernel(x_ref, mod_ref, g_ref, a_ref, bonus_ref, ofc_ref, ofl_ref, obc_ref, obl_ref,
                  ycc_ref, ycl_ref, ydc_ref, ydl_ref,
                  wz_ref, wmg_ref, ws_ref, bs_ref, lng_ref, lnb_ref, ones_ref, wup_ref, wo_ref, gf_ref,
                  o_ref, *, final):
    i = pl.program_id(0)
    is_ctx = i < N_CTX // TM_MERGE
    pick = lambda c_ref, l_ref: jnp.where(is_ctx, c_ref[...], l_ref[...])
    m = mod_ref[pl.ds(_mod_row(i, TM_MERGE), 1), :]
    x = x_ref[...]
    hb = _norm_mod(x, g_ref[...], m).astype(BF16)

    a = a_ref[...]
    a_u, a_v = a[:, :BR_W], a[:, BR_W:]
    grp = lax.broadcasted_iota(jnp.int32, (1, BR_W), 1) // A_GD
    svs = []
    for c in range(TM_MERGE // CHUNK):
        vc = a_v[c * CHUNK:(c + 1) * CHUNK, :]
        sv = bs_ref[...]
        for g in range(A_GROUPS):
            sv = sv + jnp.dot(ws_ref[g], jnp.where(grp == g, vc, 0.0).astype(BF16), preferred_element_type=F32)
        svs.append(sv)
    y_a = a_u * jnp.concatenate(svs, axis=0)

    ones = ones_ref[...]
    osum = pick(ofc_ref, ofl_ref) + pick(obc_ref, obl_ref)
    mu = jnp.dot(osum, ones, precision=HIGHEST, preferred_element_type=F32) * (1.0 / RW_HD)
    dev = osum - mu
    var = jnp.dot(dev * dev, ones, precision=HIGHEST, preferred_element_type=F32) * (1.0 / RW_HD)
    y_b = dev * lax.rsqrt(var + GN_EPS) * lng_ref[...] + lnb_ref[...] + bonus_ref[...]

    merged = jnp.zeros((TM_MERGE, D_MODEL), F32)
    for n, y in enumerate((y_a, y_b, pick(ycc_ref, ycl_ref), pick(ydc_ref, ydl_ref))):
        z = jnp.dot(hb, wz_ref[:, n * BR_W:(n + 1) * BR_W], preferred_element_type=F32)
        ys = y * (z * _sigmoid(z))
        up = jnp.dot(ys.astype(BF16), wup_ref[n], preferred_element_type=F32)
        mg = jnp.dot(hb, wmg_ref[:, n * D_MODEL:(n + 1) * D_MODEL], preferred_element_type=F32)
        merged = merged + _sigmoid(mg) * up
    gate = m[:, 2 * D_MODEL:]
    out = x + gate * jnp.dot(merged.astype(BF16), wo_ref[...], preferred_element_type=F32)
    if final:
        ms = jnp.mean(out * out, axis=-1, keepdims=True)
        out = out * lax.rsqrt(ms + NORM_EPS) * gf_ref[...]
    o_ref[...] = out


def _merge(x, mod_l, g, a, bonus, o_f, o_b, y_c, y_d, wz, wmg, ws, bs_tile, ln_g, ln_b, ones_bd, wup, wo,
           g_final, final):
    n_c = N_CTX // TM_MERGE
    row = lambda w: pl.BlockSpec((TM_MERGE, w), lambda i: (i, 0))
    pair = [pl.BlockSpec((TM_MERGE, BR_W), lambda i: (jnp.minimum(i, n_c - 1), 0)),
            pl.BlockSpec((TM_MERGE, BR_W), lambda i: (jnp.maximum(i - n_c, 0), 0))]
    full2 = lambda s: pl.BlockSpec(s, lambda i: (0, 0))
    full3 = lambda s: pl.BlockSpec(s, lambda i: (0, 0, 0))
    return pl.pallas_call(
        functools.partial(_merge_kernel, final=final),
        grid=(N_TOK // TM_MERGE,),
        in_specs=[row(D_MODEL), full2((8, 3 * D_MODEL)), full2((1, D_MODEL)), row(2 * BR_W),
                  row(BR_W)] + pair * 4 + [
                  full2((D_MODEL, N_BRANCH * BR_W)), full2((D_MODEL, N_BRANCH * D_MODEL)),
                  full3((A_GROUPS, CHUNK, CHUNK)), full2((CHUNK, BR_W)),
                  full2((1, BR_W)), full2((1, BR_W)), full2((BR_W, BR_W)),
                  full3((N_BRANCH, BR_W, D_MODEL)), full2((D_MODEL, D_MODEL)), full2((1, D_MODEL))],
        out_specs=row(D_MODEL),
        out_shape=jax.ShapeDtypeStruct((N_TOK, D_MODEL), F32),
        compiler_params=_cparams(("arbitrary",)),
        name="merge_final" if final else "merge",
    )(x, mod_l, g.reshape(1, D_MODEL), a, bonus, *o_f, *o_b, *y_c, *y_d, wz, wmg, ws, bs_tile,
      ln_g.reshape(1, BR_W), ln_b.reshape(1, BR_W), ones_bd, wup, wo, g_final.reshape(1, D_MODEL))


def _block_diag(blocks):
    n, r, c = blocks.shape
    eye = jnp.eye(n, dtype=blocks.dtype)
    return (eye[:, None, :, None] * blocks[:, :, None, :]).reshape(n * r, n * c)


def kernel(x_prompt, x_sample, cache_k, cache_v, state_rwkv, c, c_ctx, w_mod, b_mod, g_norm, w_in, w_s, b_s,
           rw_w0, rw_w_up, rw_a0, rw_a_up, rw_k_k, rw_k_a, rw_r_k, rw_ln_g, rw_ln_b, att_sink, pool_w,
           pool_scale, w_up, w_o, g_final):
    x = jnp.concatenate([x_prompt.reshape(N_CTX, D_MODEL), x_sample.reshape(N_LAT, D_MODEL)], axis=0)
    cond = jnp.concatenate([c_ctx[None, :], c, jnp.zeros((8 - 1 - DEC_BATCH, D_MODEL), F32)], axis=0)
    mod = _modulation(cond, w_mod, b_mod)

    w_in_b = w_in.astype(BF16)
    w_s_b = w_s.astype(BF16)
    w_up_b = w_up.astype(BF16)
    w_o_b = w_o.astype(BF16)
    ones_bd = _block_diag(jnp.ones((RW_HEADS, RW_HD, RW_HD), F32))
    tables = _rope_tables()
    kvw = ATT_KV_HEADS * ATT_HD
    cache_k4 = cache_k.reshape(DEC_BATCH, DEPTH, PAST_LEN, kvw)
    cache_v4 = cache_v.reshape(DEC_BATCH, DEPTH, PAST_LEN, kvw)

    new_k, new_v, new_s = [], [], []
    for l in range(DEPTH):
        a, bm, cq, ck, cv, dp = _inproj(x, mod[l], g_norm[l], w_in_b[l, :, :IN_SMALL])
        new_k.append(ck[:N_CTX].reshape(BATCH, SEQ, ATT_KV_HEADS, ATT_HD))
        new_v.append(cv[:N_CTX].reshape(BATCH, SEQ, ATT_KV_HEADS, ATT_HD))

        pre = _rwkv_pre(bm, rw_w0[l], rw_w_up[l], rw_a0[l], rw_a_up[l], rw_k_k[l], rw_k_a[l], rw_r_k[l], ones_bd)
        bonus = pre[7]
        of_c, ob_c, s_ctx = _scan(pre[:7], bm, None, nb=BATCH, t_len=SEQ, name="scan_ctx")
        of_l, ob_l, _ = _scan(pre[:7], bm, _lat_state(state_rwkv[:, l]), nb=DEC_BATCH, t_len=DEC_SEQ,
                              name="scan_lat")
        o_f = (of_c.reshape(N_CTX, BR_W), of_l.reshape(N_LAT, BR_W))
        o_b = (ob_c.reshape(N_CTX, BR_W), ob_l.reshape(N_LAT, BR_W))
        new_s.append(s_ctx.reshape(RW_HD, RW_HD, RW_HEADS, 2, BATCH).transpose(4, 3, 2, 1, 0))

        y_c = (_att_ctx(att_sink[l], cq, ck, cv),
               _att_lat(att_sink[l], cq, ck, cv, cache_k4, cache_v4, l, tables))

        pw = _block_diag(pool_w[l]).astype(BF16)
        y_d = (_pool(dp, pw, pool_scale[l], BATCH, SEQ, 0, "pool_ctx"),
               _pool(dp, pw, pool_scale[l], DEC_BATCH, DEC_SEQ, N_CTX, "pool_lat"))

        bs_tile = jnp.repeat(b_s[l].T, A_GD, axis=1)
        x = _merge(x, mod[l], g_norm[l], a, bonus, o_f, o_b, y_c, y_d,
                   w_in_b[l, :, IN_SMALL:IN_SMALL + N_BRANCH * BR_W], w_in_b[l, :, IN_SMALL + N_BRANCH * BR_W:],
                   w_s_b[l], bs_tile, rw_ln_g[l], rw_ln_b[l], ones_bd, w_up_b[l], w_o_b[l], g_final,
                   final=(l == DEPTH - 1))

    y_prompt = x[:N_CTX].reshape(BATCH, SEQ, D_MODEL)
    y_sample = x[N_CTX:].reshape(DEC_BATCH, DEC_SEQ, D_MODEL)
    return (y_prompt, y_sample, jnp.stack(new_k, axis=1), jnp.stack(new_v, axis=1), jnp.stack(new_s, axis=1))
```

```python
import functools

import numpy as np
import jax
import jax.numpy as jnp
from jax import lax
from jax.experimental import pallas as pl
from jax.experimental.pallas import tpu as pltpu

D_MODEL = 1024
BATCH = 16
SEQ = 256
DEPTH = 2
DEC_BATCH = 4
DEC_SEQ = 1024
PAST_LEN = 512
GRID_W = 64
N_BRANCH = 4
BR_W = D_MODEL // 4
CHUNK = 128
A_GROUPS = 4
A_GD = BR_W // A_GROUPS
RW_HD = 64
RW_HEADS = BR_W // RW_HD
DECAY_RANK = 64
ICL_RANK = 64
ATT_HD = 64
ATT_HEADS = BR_W // ATT_HD
ATT_KV_HEADS = 2
ATT_GROUP = ATT_HEADS // ATT_KV_HEADS
WINDOW = 128
ROPE_BASE = 10000.0
POOL_SIZES = (2, 4, 8, 16)
POOL_GD = BR_W // len(POOL_SIZES)
NORM_EPS = 1e-6
GN_EPS = 64e-5
NEG_INF = -1e30

N_CTX = BATCH * SEQ
N_LAT = DEC_BATCH * DEC_SEQ
N_TOK = N_CTX + N_LAT
IN_SMALL = 2176
IN_GATES = N_BRANCH * BR_W + N_BRANCH * D_MODEL
B_COLS = 3 * BR_W + DECAY_RANK + ICL_RANK

F32 = jnp.float32
BF16 = jnp.bfloat16
HIGHEST = lax.Precision.HIGHEST

V7X_VMEM_LIMIT = 56 * 1024 * 1024
LANES = 128

TM_IN = 512
TM_MERGE = 256
SCAN_TC = 32


def _cparams(sem):
    return pltpu.CompilerParams(dimension_semantics=sem, vmem_limit_bytes=V7X_VMEM_LIMIT)


def _mod_row(i, tm):
    per = DEC_SEQ // tm
    return jnp.maximum(i // per - (N_CTX // DEC_SEQ - 1), 0)


def _norm_mod(x, g, m):
    ms = jnp.mean(x * x, axis=-1, keepdims=True)
    y = x * lax.rsqrt(ms + NORM_EPS) * g
    shift = m[:, :D_MODEL]
    scale = m[:, D_MODEL:2 * D_MODEL]
    return y * (1.0 + scale) + shift


def _sigmoid(x):
    return 1.0 / (1.0 + jnp.exp(-x))


def _mod_kernel(c_ref, w_ref, b_ref, o_ref):
    cnd = c_ref[...]
    s = cnd * _sigmoid(cnd)
    o_ref[...] = jnp.dot(s.astype(BF16), w_ref[...].astype(BF16), preferred_element_type=F32) + b_ref[...]


def _modulation(cond, w_mod, b_mod):
    nt = 3 * D_MODEL // 1024
    return pl.pallas_call(
        _mod_kernel,
        grid=(DEPTH, nt),
        in_specs=[
            pl.BlockSpec((8, D_MODEL), lambda l, j: (0, 0)),
            pl.BlockSpec((None, D_MODEL, 1024), lambda l, j: (l, 0, j)),
            pl.BlockSpec((None, 1, 1024), lambda l, j: (l, 0, j)),
        ],
        out_specs=pl.BlockSpec((None, 8, 1024), lambda l, j: (l, 0, j)),
        out_shape=jax.ShapeDtypeStruct((DEPTH, 8, 3 * D_MODEL), F32),
        compiler_params=_cparams(("arbitrary", "arbitrary")),
        name="modulation",
    )(cond, w_mod, b_mod.reshape(DEPTH, 1, 3 * D_MODEL))


def _inproj_kernel(x_ref, mod_ref, g_ref, w_ref, a_ref, b_ref, q_ref, k_ref, v_ref, d_ref):
    i = pl.program_id(0)
    m = mod_ref[pl.ds(_mod_row(i, TM_IN), 1), :]
    h = _norm_mod(x_ref[...], g_ref[...], m)
    p = jnp.dot(h.astype(BF16), w_ref[...], preferred_element_type=F32)
    a_ref[...] = p[:, 0:512]
    b_ref[...] = p[:, 512:1408]
    q_ref[...] = p[:, 1408:1664]
    k_ref[...] = p[:, 1664:1792]
    v_ref[...] = p[:, 1792:1920]
    d_ref[...] = p[:, 1920:2176]


def _inproj(x, mod_l, g, w_small):
    widths = (512, B_COLS, 256, 128, 128, 256)
    return pl.pallas_call(
        _inproj_kernel,
        grid=(N_TOK // TM_IN,),
        in_specs=[
            pl.BlockSpec((TM_IN, D_MODEL), lambda i: (i, 0)),
            pl.BlockSpec((8, 3 * D_MODEL), lambda i: (0, 0)),
            pl.BlockSpec((1, D_MODEL), lambda i: (0, 0)),
            pl.BlockSpec((D_MODEL, IN_SMALL), lambda i: (0, 0)),
        ],
        out_specs=[pl.BlockSpec((TM_IN, w), lambda i: (i, 0)) for w in widths],
        out_shape=[jax.ShapeDtypeStruct((N_TOK, w), F32) for w in widths],
        compiler_params=_cparams(("arbitrary",)),
        name="in_proj",
    )(x, mod_l, g.reshape(1, D_MODEL), w_small)


def _softplus(x):
    return jnp.maximum(x, 0.0) + jnp.log1p(jnp.exp(-jnp.abs(x)))


def _rwkv_pre_kernel(b_ref, w0_ref, wup_ref, a0_ref, aup_ref, kkw_ref, ka_ref, rk_ref, ones_ref,
                     w0o, w1o, kka0o, kka1o, kd0o, kd1o, kko, bonuso):
    bm = b_ref[...]
    r = bm[:, 0:256]
    k = bm[:, 256:512]
    v = bm[:, 512:768]
    wd_t = jnp.tanh(bm[:, 768:832])
    ad = bm[:, 832:896]
    ones = ones_ref[...]
    kk = k * kkw_ref[...]
    ss = jnp.dot(kk * kk, ones, precision=HIGHEST, preferred_element_type=F32)
    kkn = kk * lax.rsqrt(ss + 1e-12)
    kko[...] = kkn
    bonus = jnp.zeros_like(v)
    for d, (wo, kkao, kdo) in enumerate(((w0o, kka0o, kd0o), (w1o, kka1o, kd1o))):
        pre = w0_ref[pl.ds(d, 1), :] + jnp.dot(wd_t, wup_ref[d], precision=HIGHEST, preferred_element_type=F32)
        w_log = -_softplus(-pre) - 0.5
        wo[...] = jnp.exp(-jnp.exp(w_log))
        a = _sigmoid(a0_ref[pl.ds(d, 1), :] + jnp.dot(ad, aup_ref[d], precision=HIGHEST, preferred_element_type=F32))
        k_d = k * (1.0 + (a - 1.0) * ka_ref[...])
        kdo[...] = k_d
        kkao[...] = kkn * a
        bonus = bonus + jnp.dot(r * k_d * rk_ref[...], ones, precision=HIGHEST, preferred_element_type=F32) * v
    bonuso[...] = bonus


def _rwkv_pre(bm, w0, wup, a0, aup, k_k, k_a, r_k, ones_bd):
    full2 = lambda s: pl.BlockSpec(s, lambda i: (0, 0))
    return pl.pallas_call(
        _rwkv_pre_kernel,
        grid=(N_TOK // TM_IN,),
        in_specs=[
            pl.BlockSpec((TM_IN, B_COLS), lambda i: (i, 0)),
            full2((2, BR_W)),
            pl.BlockSpec((2, DECAY_RANK, BR_W), lambda i: (0, 0, 0)),
            full2((2, BR_W)),
            pl.BlockSpec((2, ICL_RANK, BR_W), lambda i: (0, 0, 0)),
            full2((1, BR_W)), full2((1, BR_W)), full2((1, BR_W)),
            full2((BR_W, BR_W)),
        ],
        out_specs=[pl.BlockSpec((TM_IN, BR_W), lambda i: (i, 0))] * 8,
        out_shape=[jax.ShapeDtypeStruct((N_TOK, BR_W), F32)] * 8,
        compiler_params=_cparams(("arbitrary",)),
        name="rwkv_pre",
    )(bm, w0, wup, a0, aup, k_k.reshape(1, BR_W), k_a.reshape(1, BR_W), r_k.reshape(1, BR_W), ones_bd)


def _to_lanes(z0, z1, rep):
    lo = lax.broadcasted_iota(jnp.int32, (1, LANES), 1) < RW_HD
    parts = []
    for hp in range(RW_HEADS // 2):
        a = z0[:, hp * LANES:(hp + 1) * LANES]
        b = z1[:, hp * LANES:(hp + 1) * LANES]
        parts.append(jnp.where(lo, a, pltpu.roll(b, RW_HD, 1)))
        parts.append(jnp.where(lo, pltpu.roll(a, RW_HD, 1), b))
    return jnp.concatenate(parts * rep, axis=0).T


def _scan_kernel(*refs, nb, has_init):
    if has_init:
        s0_ref, refs = refs[0], refs[1:]
    (w0_ref, w1_ref, kka0_ref, kka1_ref, kd0_ref, kd1_ref, kkf_ref, kkb_ref, rf_ref, rb_ref, vf_ref, vb_ref,
     of_ref, ob_ref, st_ref, w_s, kk_s, kka_s, kd_s, r_s, v_s, o_s) = refs
    nk, vh, _ = st_ref.shape
    tc = w_s.shape[0]
    rep = RW_HD // vh
    nrec = LANES // rep
    lane = lax.broadcasted_iota(jnp.int32, (1, LANES), 1)
    grp = lane // nrec
    lo = lane < RW_HD

    def rows(f_ref, b_ref, j):
        return jnp.concatenate([f_ref[:, j, :], b_ref[:, tc - 1 - j, :]], axis=0)

    sources = ((w0_ref, w1_ref, w_s), (kkf_ref, kkb_ref, kk_s), (kka0_ref, kka1_ref, kka_s),
               (kd0_ref, kd1_ref, kd_s), (rf_ref, rb_ref, r_s))
    for j in range(0, tc, 2):
        for f_ref, b_ref, dst in sources:
            gt = _to_lanes(rows(f_ref, b_ref, j), rows(f_ref, b_ref, j + 1), rep)
            dst[j] = gt[0:RW_HD]
            dst[j + 1] = gt[RW_HD:2 * RW_HD]
        gt = _to_lanes(rows(vf_ref, vb_ref, j), rows(vf_ref, vb_ref, j + 1), rep)
        for s in range(2):
            blk = gt[s * RW_HD:(s + 1) * RW_HD]
            if rep > 1:
                blk = sum(jnp.where(grp == q, blk[q * vh:(q + 1) * vh], 0.0) for q in range(rep))
            v_s[j + s] = blk

    @pl.when(pl.program_id(0) == 0)
    def _():
        if has_init:
            st_ref[...] = s0_ref[...]
        else:
            st_ref[...] = jnp.zeros(st_ref.shape, F32)

    sa0 = jnp.zeros((vh, LANES), F32)
    for k in range(nk):
        sa0 = sa0 + st_ref[k] * kk_s[0, pl.ds(k, 1), :]

    def step(t, sa):
        tn = jnp.minimum(t + 1, tc - 1)
        vt = v_s[t]
        o = jnp.zeros((vh, LANES), F32)
        san = jnp.zeros((vh, LANES), F32)
        for k in range(nk):
            new = (st_ref[k] * w_s[t, pl.ds(k, 1), :] - kka_s[t, pl.ds(k, 1), :] * sa
                   + kd_s[t, pl.ds(k, 1), :] * vt)
            st_ref[k] = new
            o = o + new * r_s[t, pl.ds(k, 1), :]
            san = san + new * kk_s[tn, pl.ds(k, 1), :]
        o_s[t] = o
        return san

    lax.fori_loop(0, tc, step, sa0)

    for j in range(0, tc, 2):
        tiles = []
        for s in range(2):
            o = o_s[j + s]
            if rep == 1:
                tiles.append(o)
            else:
                tiles.extend(jnp.where(grp == q, o, 0.0) for q in range(rep))
        mt = jnp.concatenate(tiles, axis=0).T
        sm = mt[0:nrec]
        for q in range(1, rep):
            sm = sm + mt[q * nrec:(q + 1) * nrec]
        rs = pltpu.roll(sm, RW_HD, 1)
        for hp in range(RW_HEADS // 2):
            e0, e1 = (2 * hp) * 2 * nb, (2 * hp + 1) * 2 * nb
            cur = jnp.where(lo, sm[e0:e0 + 2 * nb], rs[e1:e1 + 2 * nb])
            nxt = jnp.where(lo, rs[e0:e0 + 2 * nb], sm[e1:e1 + 2 * nb])
            cs = slice(hp * LANES, (hp + 1) * LANES)
            of_ref[:, j, cs] = cur[0:nb]
            of_ref[:, j + 1, cs] = nxt[0:nb]
            ob_ref[:, tc - 1 - j, cs] = cur[nb:2 * nb]
            ob_ref[:, tc - 2 - j, cs] = nxt[nb:2 * nb]


def _scan(pre, bm, s0, *, nb, t_len, name):
    n_seq = N_TOK // t_len
    w0, w1, kka0, kka1, kd0, kd1, kk = (p.reshape(n_seq, t_len, BR_W) for p in pre)
    bm3 = bm.reshape(n_seq, t_len, B_COLS)
    seq_blk = (N_CTX // t_len) // nb if t_len == DEC_SEQ else 0
    n_t = t_len // SCAN_TC
    vh = RW_HD * nb * 2 * RW_HEADS // LANES
    fwd = lambda c: pl.BlockSpec((nb, SCAN_TC, BR_W), lambda i: (seq_blk, i, c))
    bwd = lambda c: pl.BlockSpec((nb, SCAN_TC, BR_W), lambda i: (seq_blk, n_t - 1 - i, c))
    sspec = pl.BlockSpec((RW_HD, vh, LANES), lambda i: (0, 0, 0))
    has_init = s0 is not None
    args = ((s0,) if has_init else ()) + (w0, w1, kka0, kka1, kd0, kd1, kk, kk, bm3, bm3, bm3, bm3)
    in_specs = ([sspec] if has_init else []) + [fwd(0), bwd(0)] * 5 + [fwd(2), bwd(2)]
    kbuf = pltpu.VMEM((SCAN_TC, RW_HD, LANES), F32)
    vbuf = pltpu.VMEM((SCAN_TC, vh, LANES), F32)
    return pl.pallas_call(
        functools.partial(_scan_kernel, nb=nb, has_init=has_init),
        grid=(n_t,),
        in_specs=in_specs,
        out_specs=[pl.BlockSpec((nb, SCAN_TC, BR_W), lambda i: (0, i, 0)),
                   pl.BlockSpec((nb, SCAN_TC, BR_W), lambda i: (0, n_t - 1 - i, 0)),
                   sspec],
        out_shape=[jax.ShapeDtypeStruct((nb, t_len, BR_W), F32),
                   jax.ShapeDtypeStruct((nb, t_len, BR_W), F32),
                   jax.ShapeDtypeStruct((RW_HD, vh, LANES), F32)],
        scratch_shapes=[kbuf] * 5 + [vbuf] * 2,
        compiler_params=_cparams(("arbitrary",)),
        name=name,
    )(*args)


LAT_VS = 4
LAT_VH = RW_HD // LAT_VS


def _lat_state(s):
    s6 = s.reshape(DEC_BATCH, 2, RW_HEADS, LAT_VS, LAT_VH, RW_HD)
    return s6.transpose(5, 4, 3, 2, 1, 0).reshape(RW_HD, LAT_VH, LANES)


def _sink_softmax_pv(s, sk, vb):
    m = jnp.maximum(jnp.max(s, axis=-1, keepdims=True), sk)
    p = jnp.exp(s - m)
    den = jnp.sum(p, axis=-1, keepdims=True) + jnp.exp(sk - m)
    return jnp.dot(p.astype(BF16), vb, preferred_element_type=F32) / den


def _att_ctx_kernel(sink_ref, q_ref, k_ref, v_ref, o_ref):
    scale = ATT_HD ** -0.5
    q = q_ref[...]
    kb = k_ref[...].astype(BF16)
    vb = v_ref[...].astype(BF16)
    outs = []
    for hd in range(ATT_HEADS):
        kvh = hd // ATT_GROUP
        qh = q[:, hd * ATT_HD:(hd + 1) * ATT_HD].astype(BF16)
        kh = kb[:, kvh * ATT_HD:(kvh + 1) * ATT_HD]
        s = lax.dot_general(qh, kh, (((1,), (1,)), ((), ())), preferred_element_type=F32) * scale
        outs.append(_sink_softmax_pv(s, sink_ref[hd], vb[:, kvh * ATT_HD:(kvh + 1) * ATT_HD]))
    o_ref[...] = jnp.concatenate(outs, axis=1)


def _att_ctx(sink, cq, ck, cv):
    kvw = ATT_KV_HEADS * ATT_HD
    return pl.pallas_call(
        _att_ctx_kernel,
        grid=(BATCH,),
        in_specs=[
            pl.BlockSpec(memory_space=pltpu.SMEM),
            pl.BlockSpec((SEQ, BR_W), lambda b: (b, 0)),
            pl.BlockSpec((SEQ, kvw), lambda b: (b, 0)),
            pl.BlockSpec((SEQ, kvw), lambda b: (b, 0)),
        ],
        out_specs=pl.BlockSpec((SEQ, BR_W), lambda b: (b, 0)),
        out_shape=jax.ShapeDtypeStruct((N_CTX, BR_W), F32),
        compiler_params=_cparams(("arbitrary",)),
        name="att_ctx",
    )(sink, cq, ck, cv)


def _rope(x, cos, sin_signed):
    w = x.shape[1]
    lane = lax.broadcasted_iota(jnp.int32, (1, w), 1)
    first = (lane % 32) < 16
    swapped = jnp.where(first, pltpu.roll(x, w - 16, 1), pltpu.roll(x, 16, 1))
    return x * cos + swapped * sin_signed


def _att_lat_kernel(sink_ref, q_ref, k_ref, v_ref, ck_ref, cv_ref, cq_ref, sq_ref, ckk_ref, skk_ref,
                    o_ref, qs_ref, ks_ref, vs_ref):
    scale = ATT_HD ** -0.5
    nb = DEC_SEQ // CHUNK
    qs_ref[...] = _rope(q_ref[...], cq_ref[...], sq_ref[...]).astype(BF16)
    ks_ref[...] = _rope(k_ref[...], ckk_ref[...], skk_ref[...]).astype(BF16)
    vs_ref[...] = v_ref[...].astype(BF16)
    ckb = ck_ref[...].astype(BF16)
    cvb = cv_ref[...].astype(BF16)
    for n in range(nb):
        lo = max(n - 1, 0) * CHUNK
        hi = min(n + 2, nb) * CHUNK
        nloc = hi - lo
        ncol = nloc + PAST_LEN
        col = lax.broadcasted_iota(jnp.int32, (CHUNK, ncol), 1)
        row = lax.broadcasted_iota(jnp.int32, (CHUNK, ncol), 0)
        dist = jnp.abs((n * CHUNK + row) - (lo + col))
        valid = (col >= nloc) | (dist <= WINDOW)
        outs = []
        for hd in range(ATT_HEADS):
            kvh = hd // ATT_GROUP
            hs = slice(hd * ATT_HD, (hd + 1) * ATT_HD)
            ks = slice(kvh * ATT_HD, (kvh + 1) * ATT_HD)
            qh = qs_ref[n * CHUNK:(n + 1) * CHUNK, hs]
            kall = jnp.concatenate([ks_ref[lo:hi, ks], ckb[:, ks]], axis=0)
            vall = jnp.concatenate([vs_ref[lo:hi, ks], cvb[:, ks]], axis=0)
            s = lax.dot_general(qh, kall, (((1,), (1,)), ((), ())), preferred_element_type=F32) * scale
            s = jnp.where(valid, s, NEG_INF)
            outs.append(_sink_softmax_pv(s, sink_ref[hd], vall))
        o_ref[n * CHUNK:(n + 1) * CHUNK, :] = jnp.concatenate(outs, axis=1)


def _att_lat(sink, cq, ck, cv, cache_k, cache_v, layer, tables):
    kvw = ATT_KV_HEADS * ATT_HD
    off = N_CTX // DEC_SEQ
    cosq, sinq, cosk, sink_k = tables
    seq = lambda w: pl.BlockSpec((DEC_SEQ, w), lambda b: (b + off, 0))
    cache = pl.BlockSpec((None, None, PAST_LEN, kvw), lambda b: (b, layer, 0, 0))
    tab = lambda w: pl.BlockSpec((DEC_SEQ, w), lambda b: (0, 0))
    return pl.pallas_call(
        _att_lat_kernel,
        grid=(DEC_BATCH,),
        in_specs=[pl.BlockSpec(memory_space=pltpu.SMEM), seq(BR_W), seq(kvw), seq(kvw), cache, cache,
                  tab(BR_W), tab(BR_W), tab(kvw), tab(kvw)],
        out_specs=pl.BlockSpec((DEC_SEQ, BR_W), lambda b: (b, 0)),
        out_shape=jax.ShapeDtypeStruct((N_LAT, BR_W), F32),
        scratch_shapes=[pltpu.VMEM((DEC_SEQ, BR_W), BF16), pltpu.VMEM((DEC_SEQ, kvw), BF16),
                        pltpu.VMEM((DEC_SEQ, kvw), BF16)],
        compiler_params=_cparams(("arbitrary",)),
        name="att_lat",
    )(sink, cq, ck, cv, cache_k, cache_v, cosq, sinq, cosk, sink_k)


def _rope_tables():
    pos = np.arange(DEC_SEQ)
    lane = np.arange(ATT_HD)
    p = np.where(lane[None, :] < 32, (pos // GRID_W)[:, None], (pos % GRID_W)[:, None]).astype(np.float32)
    inv = (ROPE_BASE ** (-jnp.arange(0, 32, 2, dtype=F32) / 32))[lane % 16]
    ang = jnp.asarray(p) * inv[None, :]
    sign = jnp.asarray(np.where((lane % 32) < 16, -1.0, 1.0).astype(np.float32))
    cos, sin = jnp.cos(ang), jnp.sin(ang) * sign[None, :]
    return (jnp.tile(cos, (1, ATT_HEADS)), jnp.tile(sin, (1, ATT_HEADS)),
            jnp.tile(cos, (1, ATT_KV_HEADS)), jnp.tile(sin, (1, ATT_KV_HEADS)))


POOL_PAD = 8


def _pool_kernel(x_ref, w_ref, sc_ref, o_ref, pad_ref):
    n = x_ref.shape[0]
    x = x_ref[...]
    pad_ref[0:POOL_PAD, :] = jnp.zeros((POOL_PAD, BR_W), F32)
    pad_ref[POOL_PAD:POOL_PAD + n, :] = x
    pad_ref[POOL_PAD + n:2 * POOL_PAD + n, :] = jnp.zeros((POOL_PAD, BR_W), F32)
    grp = lax.broadcasted_iota(jnp.int32, (1, BR_W), 1) // POOL_GD
    half = jnp.left_shift(1, grp)
    acc = jnp.zeros((n, BR_W), F32)
    for j in range(-POOL_PAD, POOL_PAD):
        inside = (j >= -half) & (j < half)
        acc = acc + jnp.where(inside, pad_ref[POOL_PAD + j:POOL_PAD + j + n, :], 0.0)
    t = lax.broadcasted_iota(jnp.int32, (n, BR_W), 0)
    cnt = jnp.minimum(t + half, n) - jnp.maximum(t - half, 0)
    d = acc / cnt.astype(F32) - x
    o_ref[...] = jnp.dot(d.astype(BF16), w_ref[...], preferred_element_type=F32) * sc_ref[...]


def _pool(dp, w_bd, scale, n_seq, seq_len, row_off, name):
    off = row_off // seq_len
    return pl.pallas_call(
        _pool_kernel,
        grid=(n_seq,),
        in_specs=[pl.BlockSpec((seq_len, BR_W), lambda b: (b + off, 0)),
                  pl.BlockSpec((BR_W, BR_W), lambda b: (0, 0)),
                  pl.BlockSpec((1, BR_W), lambda b: (0, 0))],
        out_specs=pl.BlockSpec((seq_len, BR_W), lambda b: (b, 0)),
        out_shape=jax.ShapeDtypeStruct((n_seq * seq_len, BR_W), F32),
        scratch_shapes=[pltpu.VMEM((seq_len + 2 * POOL_PAD, BR_W), F32)],
        compiler_params=_cparams(("arbitrary",)),
        name=name,
    )(dp, w_bd, scale.reshape(1, BR_W))


def _merge_kernel(x_ref, mod_ref, g_ref, a_ref, bonus_ref, ofc_ref, ofl_ref, obc_ref, obl_ref,
                  ycc_ref, ycl_ref, ydc_ref, ydl_ref,
                  wz_ref, wmg_ref, ws_ref, bs_ref, lng_ref, lnb_ref, ones_ref, wup_ref, wo_ref, gf_ref,
                  o_ref, *, final):
    i = pl.program_id(0)
    is_ctx = i < N_CTX // TM_MERGE
    pick = lambda c_ref, l_ref: jnp.where(is_ctx, c_ref[...], l_ref[...])
    m = mod_ref[pl.ds(_mod_row(i, TM_MERGE), 1), :]
    x = x_ref[...]
    hb = _norm_mod(x, g_ref[...], m).astype(BF16)

    a = a_ref[...]
    a_u, a_v = a[:, :BR_W], a[:, BR_W:]
    grp = lax.broadcasted_iota(jnp.int32, (1, BR_W), 1) // A_GD
    svs = []
    for c in range(TM_MERGE // CHUNK):
        vc = a_v[c * CHUNK:(c + 1) * CHUNK, :]
        sv = bs_ref[...]
        for g in range(A_GROUPS):
            sv = sv + jnp.dot(ws_ref[g], jnp.where(grp == g, vc, 0.0).astype(BF16), preferred_element_type=F32)
        svs.append(sv)
    y_a = a_u * jnp.concatenate(svs, axis=0)

    ones = ones_ref[...]
    osum = pick(ofc_ref, ofl_ref) + pick(obc_ref, obl_ref)
    mu = jnp.dot(osum, ones, precision=HIGHEST, preferred_element_type=F32) * (1.0 / RW_HD)
    dev = osum - mu
    var = jnp.dot(dev * dev, ones, precision=HIGHEST, preferred_element_type=F32) * (1.0 / RW_HD)
    y_b = dev * lax.rsqrt(var + GN_EPS) * lng_ref[...] + lnb_ref[...] + bonus_ref[...]

    merged = jnp.zeros((TM_MERGE, D_MODEL), F32)
    for n, y in enumerate((y_a, y_b, pick(ycc_ref, ycl_ref), pick(ydc_ref, ydl_ref))):
        z = jnp.dot(hb, wz_ref[:, n * BR_W:(n + 1) * BR_W], preferred_element_type=F32)
        ys = y * (z * _sigmoid(z))
        up = jnp.dot(ys.astype(BF16), wup_ref[n], preferred_element_type=F32)
        mg = jnp.dot(hb, wmg_ref[:, n * D_MODEL:(n + 1) * D_MODEL], preferred_element_type=F32)
        merged = merged + _sigmoid(mg) * up
    gate = m[:, 2 * D_MODEL:]
    out = x + gate * jnp.dot(merged.astype(BF16), wo_ref[...], preferred_element_type=F32)
    if final:
        ms = jnp.mean(out * out, axis=-1, keepdims=True)
        out = out * lax.rsqrt(ms + NORM_EPS) * gf_ref[...]
    o_ref[...] = out


def _merge(x, mod_l, g, a, bonus, o_f, o_b, y_c, y_d, wz, wmg, ws, bs_tile, ln_g, ln_b, ones_bd, wup, wo,
           g_final, final):
    n_c = N_CTX // TM_MERGE
    row = lambda w: pl.BlockSpec((TM_MERGE, w), lambda i: (i, 0))
    pair = [pl.BlockSpec((TM_MERGE, BR_W), lambda i: (jnp.minimum(i, n_c - 1), 0)),
            pl.BlockSpec((TM_MERGE, BR_W), lambda i: (jnp.maximum(i - n_c, 0), 0))]
    full2 = lambda s: pl.BlockSpec(s, lambda i: (0, 0))
    full3 = lambda s: pl.BlockSpec(s, lambda i: (0, 0, 0))
    return pl.pallas_call(
        functools.partial(_merge_kernel, final=final),
        grid=(N_TOK // TM_MERGE,),
        in_specs=[row(D_MODEL), full2((8, 3 * D_MODEL)), full2((1, D_MODEL)), row(2 * BR_W),
                  row(BR_W)] + pair * 4 + [
                  full2((D_MODEL, N_BRANCH * BR_W)), full2((D_MODEL, N_BRANCH * D_MODEL)),
                  full3((A_GROUPS, CHUNK, CHUNK)), full2((CHUNK, BR_W)),
                  full2((1, BR_W)), full2((1, BR_W)), full2((BR_W, BR_W)),
                  full3((N_BRANCH, BR_W, D_MODEL)), full2((D_MODEL, D_MODEL)), full2((1, D_MODEL))],
        out_specs=row(D_MODEL),
        out_shape=jax.ShapeDtypeStruct((N_TOK, D_MODEL), F32),
        compiler_params=_cparams(("arbitrary",)),
        name="merge_final" if final else "merge",
    )(x, mod_l, g.reshape(1, D_MODEL), a, bonus, *o_f, *o_b, *y_c, *y_d, wz, wmg, ws, bs_tile,
      ln_g.reshape(1, BR_W), ln_b.reshape(1, BR_W), ones_bd, wup, wo, g_final.reshape(1, D_MODEL))


def _block_diag(blocks):
    n, r, c = blocks.shape
    eye = jnp.eye(n, dtype=blocks.dtype)
    return (eye[:, None, :, None] * blocks[:, :, None, :]).reshape(n * r, n * c)


def kernel(x_prompt, x_sample, cache_k, cache_v, state_rwkv, c, c_ctx, w_mod, b_mod, g_norm, w_in, w_s, b_s,
           rw_w0, rw_w_up, rw_a0, rw_a_up, rw_k_k, rw_k_a, rw_r_k, rw_ln_g, rw_ln_b, att_sink, pool_w,
           pool_scale, w_up, w_o, g_final):
    x = jnp.concatenate([x_prompt.reshape(N_CTX, D_MODEL), x_sample.reshape(N_LAT, D_MODEL)], axis=0)
    cond = jnp.concatenate([c_ctx[None, :], c, jnp.zeros((8 - 1 - DEC_BATCH, D_MODEL), F32)], axis=0)
    mod = _modulation(cond, w_mod, b_mod)

    w_in_b = w_in.astype(BF16)
    w_s_b = w_s.astype(BF16)
    w_up_b = w_up.astype(BF16)
    w_o_b = w_o.astype(BF16)
    ones_bd = _block_diag(jnp.ones((RW_HEADS, RW_HD, RW_HD), F32))
    tables = _rope_tables()
    kvw = ATT_KV_HEADS * ATT_HD
    cache_k4 = cache_k.reshape(DEC_BATCH, DEPTH, PAST_LEN, kvw)
    cache_v4 = cache_v.reshape(DEC_BATCH, DEPTH, PAST_LEN, kvw)

    new_k, new_v, new_s = [], [], []
    for l in range(DEPTH):
        a, bm, cq, ck, cv, dp = _inproj(x, mod[l], g_norm[l], w_in_b[l, :, :IN_SMALL])
        new_k.append(ck[:N_CTX].reshape(BATCH, SEQ, ATT_KV_HEADS, ATT_HD))
        new_v.append(cv[:N_CTX].reshape(BATCH, SEQ, ATT_KV_HEADS, ATT_HD))

        pre = _rwkv_pre(bm, rw_w0[l], rw_w_up[l], rw_a0[l], rw_a_up[l], rw_k_k[l], rw_k_a[l], rw_r_k[l], ones_bd)
        bonus = pre[7]
        of_c, ob_c, s_ctx = _scan(pre[:7], bm, None, nb=BATCH, t_len=SEQ, name="scan_ctx")
        of_l, ob_l, _ = _scan(pre[:7], bm, _lat_state(state_rwkv[:, l]), nb=DEC_BATCH, t_len=DEC_SEQ,
                              name="scan_lat")
        o_f = (of_c.reshape(N_CTX, BR_W), of_l.reshape(N_LAT, BR_W))
        o_b = (ob_c.reshape(N_CTX, BR_W), ob_l.reshape(N_LAT, BR_W))
        new_s.append(s_ctx.reshape(RW_HD, RW_HD, RW_HEADS, 2, BATCH).transpose(4, 3, 2, 1, 0))

        y_c = (_att_ctx(att_sink[l], cq, ck, cv),
               _att_lat(att_sink[l], cq, ck, cv, cache_k4, cache_v4, l, tables))

        pw = _block_diag(pool_w[l]).astype(BF16)
        y_d = (_pool(dp, pw, pool_scale[l], BATCH, SEQ, 0, "pool_ctx"),
               _pool(dp, pw, pool_scale[l], DEC_BATCH, DEC_SEQ, N_CTX, "pool_lat"))

        bs_tile = jnp.repeat(b_s[l].T, A_GD, axis=1)
        x = _merge(x, mod[l], g_norm[l], a, bonus, o_f, o_b, y_c, y_d,
                   w_in_b[l, :, IN_SMALL:IN_SMALL + N_BRANCH * BR_W], w_in_b[l, :, IN_SMALL + N_BRANCH * BR_W:],
                   w_s_b[l], bs_tile, rw_ln_g[l], rw_ln_b[l], ones_bd, w_up_b[l], w_o_b[l], g_final,
                   final=(l == DEPTH - 1))

    y_prompt = x[:N_CTX].reshape(BATCH, SEQ, D_MODEL)
    y_sample = x[N_CTX:].reshape(DEC_BATCH, DEC_SEQ, D_MODEL)
    return (y_prompt, y_sample, jnp.stack(new_k, axis=1), jnp.stack(new_v, axis=1), jnp.stack(new_s, axis=1))
```

```python
import functools

import numpy as np
import jax
import jax.numpy as jnp
from jax import lax
from jax.experimental import pallas as pl
from jax.experimental.pallas import tpu as pltpu

D_MODEL = 1024
BATCH = 16
SEQ = 256
DEPTH = 2
DEC_BATCH = 4
DEC_SEQ = 1024
PAST_LEN = 512
GRID_W = 64
N_BRANCH = 4
BR_W = D_MODEL // 4
CHUNK = 128
A_GROUPS = 4
A_GD = BR_W // A_GROUPS
RW_HD = 64
RW_HEADS = BR_W // RW_HD
DECAY_RANK = 64
ICL_RANK = 64
ATT_HD = 64
ATT_HEADS = BR_W // ATT_HD
ATT_KV_HEADS = 2
ATT_GROUP = ATT_HEADS // ATT_KV_HEADS
WINDOW = 128
ROPE_BASE = 10000.0
POOL_SIZES = (2, 4, 8, 16)
POOL_GD = BR_W // len(POOL_SIZES)
NORM_EPS = 1e-6
GN_EPS = 64e-5
NEG_INF = -1e30

N_CTX = BATCH * SEQ
N_LAT = DEC_BATCH * DEC_SEQ
N_GRP = N_CTX
assert N_LAT == N_GRP
IN_SMALL = 2176
IN_GATES = N_BRANCH * BR_W + N_BRANCH * D_MODEL
B_COLS = 3 * BR_W + DECAY_RANK + ICL_RANK

F32 = jnp.float32
BF16 = jnp.bfloat16
HIGHEST = lax.Precision.HIGHEST

V7X_VMEM_LIMIT = 56 * 1024 * 1024
LANES = 128

TM_IN = 512
TM_MERGE = 256
SCAN_TC = 32


def _cparams(sem):
    return pltpu.CompilerParams(dimension_semantics=sem, vmem_limit_bytes=V7X_VMEM_LIMIT)


def _mod_row(i, tm, lat):
    return 1 + i // (DEC_SEQ // tm) if lat else 0


def _norm_mod(x, g, m):
    ms = jnp.mean(x * x, axis=-1, keepdims=True)
    y = x * lax.rsqrt(ms + NORM_EPS) * g
    shift = m[:, :D_MODEL]
    scale = m[:, D_MODEL:2 * D_MODEL]
    return y * (1.0 + scale) + shift


def _sigmoid(x):
    return 1.0 / (1.0 + jnp.exp(-x))


def _mod_kernel(c_ref, w_ref, b_ref, o_ref):
    cnd = c_ref[...]
    s = cnd * _sigmoid(cnd)
    o_ref[...] = jnp.dot(s.astype(BF16), w_ref[...].astype(BF16), preferred_element_type=F32) + b_ref[...]


def _modulation(cond, w_mod, b_mod):
    nt = 3 * D_MODEL // 1024
    return pl.pallas_call(
        _mod_kernel,
        grid=(DEPTH, nt),
        in_specs=[
            pl.BlockSpec((8, D_MODEL), lambda l, j: (0, 0)),
            pl.BlockSpec((None, D_MODEL, 1024), lambda l, j: (l, 0, j)),
            pl.BlockSpec((None, 1, 1024), lambda l, j: (l, 0, j)),
        ],
        out_specs=pl.BlockSpec((None, 8, 1024), lambda l, j: (l, 0, j)),
        out_shape=jax.ShapeDtypeStruct((DEPTH, 8, 3 * D_MODEL), F32),
        compiler_params=_cparams(("arbitrary", "arbitrary")),
        name="modulation",
    )(cond, w_mod, b_mod.reshape(DEPTH, 1, 3 * D_MODEL))


def _inproj_kernel(x_ref, mod_ref, g_ref, w_ref, a_ref, b_ref, q_ref, k_ref, v_ref, d_ref, *, lat):
    i = pl.program_id(0)
    m = mod_ref[pl.ds(_mod_row(i, TM_IN, lat), 1), :]
    h = _norm_mod(x_ref[...], g_ref[...], m)
    p = jnp.dot(h.astype(BF16), w_ref[...], preferred_element_type=F32)
    a_ref[...] = p[:, 0:512]
    b_ref[...] = p[:, 512:1408]
    q_ref[...] = p[:, 1408:1664]
    k_ref[...] = p[:, 1664:1792]
    v_ref[...] = p[:, 1792:1920]
    d_ref[...] = p[:, 1920:2176]


def _inproj(x, mod_l, g, w_small, lat):
    widths = (512, B_COLS, 256, 128, 128, 256)
    return pl.pallas_call(
        functools.partial(_inproj_kernel, lat=lat),
        grid=(N_GRP // TM_IN,),
        in_specs=[
            pl.BlockSpec((TM_IN, D_MODEL), lambda i: (i, 0)),
            pl.BlockSpec((8, 3 * D_MODEL), lambda i: (0, 0)),
            pl.BlockSpec((1, D_MODEL), lambda i: (0, 0)),
            pl.BlockSpec((D_MODEL, IN_SMALL), lambda i: (0, 0)),
        ],
        out_specs=[pl.BlockSpec((TM_IN, w), lambda i: (i, 0)) for w in widths],
        out_shape=[jax.ShapeDtypeStruct((N_GRP, w), F32) for w in widths],
        compiler_params=_cparams(("arbitrary",)),
        name="in_proj_lat" if lat else "in_proj_ctx",
    )(x, mod_l, g.reshape(1, D_MODEL), w_small)


def _softplus(x):
    return jnp.maximum(x, 0.0) + jnp.log1p(jnp.exp(-jnp.abs(x)))


def _rwkv_pre_kernel(b_ref, w0_ref, wup_ref, a0_ref, aup_ref, kkw_ref, ka_ref, rk_ref, ones_ref,
                     w0o, w1o, kka0o, kka1o, kd0o, kd1o, kko, bonuso):
    bm = b_ref[...]
    r = bm[:, 0:256]
    k = bm[:, 256:512]
    v = bm[:, 512:768]
    wd_t = jnp.tanh(bm[:, 768:832])
    ad = bm[:, 832:896]
    ones = ones_ref[...]
    kk = k * kkw_ref[...]
    ss = jnp.dot(kk * kk, ones, precision=HIGHEST, preferred_element_type=F32)
    kkn = kk * lax.rsqrt(ss + 1e-12)
    kko[...] = kkn
    bonus = jnp.zeros_like(v)
    for d, (wo, kkao, kdo) in enumerate(((w0o, kka0o, kd0o), (w1o, kka1o, kd1o))):
        pre = w0_ref[pl.ds(d, 1), :] + jnp.dot(wd_t, wup_ref[d], precision=HIGHEST, preferred_element_type=F32)
        w_log = -_softplus(-pre) - 0.5
        wo[...] = jnp.exp(-jnp.exp(w_log))
        a = _sigmoid(a0_ref[pl.ds(d, 1), :] + jnp.dot(ad, aup_ref[d], precision=HIGHEST, preferred_element_type=F32))
        k_d = k * (1.0 + (a - 1.0) * ka_ref[...])
        kdo[...] = k_d
        kkao[...] = kkn * a
        bonus = bonus + jnp.dot(r * k_d * rk_ref[...], ones, precision=HIGHEST, preferred_element_type=F32) * v
    bonuso[...] = bonus


def _rwkv_pre(bm, w0, wup, a0, aup, k_k, k_a, r_k, ones_bd, name):
    full2 = lambda s: pl.BlockSpec(s, lambda i: (0, 0))
    return pl.pallas_call(
        _rwkv_pre_kernel,
        grid=(N_GRP // TM_IN,),
        in_specs=[
            pl.BlockSpec((TM_IN, B_COLS), lambda i: (i, 0)),
            full2((2, BR_W)),
            pl.BlockSpec((2, DECAY_RANK, BR_W), lambda i: (0, 0, 0)),
            full2((2, BR_W)),
            pl.BlockSpec((2, ICL_RANK, BR_W), lambda i: (0, 0, 0)),
            full2((1, BR_W)), full2((1, BR_W)), full2((1, BR_W)),
            full2((BR_W, BR_W)),
        ],
        out_specs=[pl.BlockSpec((TM_IN, BR_W), lambda i: (i, 0))] * 8,
        out_shape=[jax.ShapeDtypeStruct((N_GRP, BR_W), F32)] * 8,
        compiler_params=_cparams(("arbitrary",)),
        name=name,
    )(bm, w0, wup, a0, aup, k_k.reshape(1, BR_W), k_a.reshape(1, BR_W), r_k.reshape(1, BR_W), ones_bd)


def _to_lanes(z0, z1, rep):
    lo = lax.broadcasted_iota(jnp.int32, (1, LANES), 1) < RW_HD
    parts = []
    for hp in range(RW_HEADS // 2):
        a = z0[:, hp * LANES:(hp + 1) * LANES]
        b = z1[:, hp * LANES:(hp + 1) * LANES]
        parts.append(jnp.where(lo, a, pltpu.roll(b, RW_HD, 1)))
        parts.append(jnp.where(lo, pltpu.roll(a, RW_HD, 1), b))
    return jnp.concatenate(parts * rep, axis=0).T


def _scan_kernel(*refs, nb, has_init):
    if has_init:
        s0_ref, refs = refs[0], refs[1:]
    (w0_ref, w1_ref, kka0_ref, kka1_ref, kd0_ref, kd1_ref, kkf_ref, kkb_ref, rf_ref, rb_ref, vf_ref, vb_ref,
     of_ref, ob_ref, st_ref, w_s, kk_s, kka_s, kd_s, r_s, v_s, o_s) = refs
    nk, vh, _ = st_ref.shape
    tc = w_s.shape[0]
    rep = RW_HD // vh
    nrec = LANES // rep
    lane = lax.broadcasted_iota(jnp.int32, (1, LANES), 1)
    grp = lane // nrec
    lo = lane < RW_HD

    def rows(f_ref, b_ref, j):
        return jnp.concatenate([f_ref[:, j, :], b_ref[:, tc - 1 - j, :]], axis=0)

    sources = ((w0_ref, w1_ref, w_s), (kkf_ref, kkb_ref, kk_s), (kka0_ref, kka1_ref, kka_s),
               (kd0_ref, kd1_ref, kd_s), (rf_ref, rb_ref, r_s))
    for j in range(0, tc, 2):
        for f_ref, b_ref, dst in sources:
            gt = _to_lanes(rows(f_ref, b_ref, j), rows(f_ref, b_ref, j + 1), rep)
            dst[j] = gt[0:RW_HD]
            dst[j + 1] = gt[RW_HD:2 * RW_HD]
        gt = _to_lanes(rows(vf_ref, vb_ref, j), rows(vf_ref, vb_ref, j + 1), rep)
        for s in range(2):
            blk = gt[s * RW_HD:(s + 1) * RW_HD]
            if rep > 1:
                blk = sum(jnp.where(grp == q, blk[q * vh:(q + 1) * vh], 0.0) for q in range(rep))
            v_s[j + s] = blk

    @pl.when(pl.program_id(0) == 0)
    def _():
        if has_init:
            st_ref[...] = s0_ref[...]
        else:
            st_ref[...] = jnp.zeros(st_ref.shape, F32)

    sa0 = jnp.zeros((vh, LANES), F32)
    for k in range(nk):
        sa0 = sa0 + st_ref[k] * kk_s[0, pl.ds(k, 1), :]

    def step(t, sa):
        tn = jnp.minimum(t + 1, tc - 1)
        vt = v_s[t]
        o = jnp.zeros((vh, LANES), F32)
        san = jnp.zeros((vh, LANES), F32)
        for k in range(nk):
            new = (st_ref[k] * w_s[t, pl.ds(k, 1), :] - kka_s[t, pl.ds(k, 1), :] * sa
                   + kd_s[t, pl.ds(k, 1), :] * vt)
            st_ref[k] = new
            o = o + new * r_s[t, pl.ds(k, 1), :]
            san = san + new * kk_s[tn, pl.ds(k, 1), :]
        o_s[t] = o
        return san

    lax.fori_loop(0, tc, step, sa0)

    for j in range(0, tc, 2):
        tiles = []
        for s in range(2):
            o = o_s[j + s]
            if rep == 1:
                tiles.append(o)
            else:
                tiles.extend(jnp.where(grp == q, o, 0.0) for q in range(rep))
        mt = jnp.concatenate(tiles, axis=0).T
        sm = mt[0:nrec]
        for q in range(1, rep):
            sm = sm + mt[q * nrec:(q + 1) * nrec]
        rs = pltpu.roll(sm, RW_HD, 1)
        for hp in range(RW_HEADS // 2):
            e0, e1 = (2 * hp) * 2 * nb, (2 * hp + 1) * 2 * nb
            cur = jnp.where(lo, sm[e0:e0 + 2 * nb], rs[e1:e1 + 2 * nb])
            nxt = jnp.where(lo, rs[e0:e0 + 2 * nb], sm[e1:e1 + 2 * nb])
            cs = slice(hp * LANES, (hp + 1) * LANES)
            of_ref[:, j, cs] = cur[0:nb]
            of_ref[:, j + 1, cs] = nxt[0:nb]
            ob_ref[:, tc - 1 - j, cs] = cur[nb:2 * nb]
            ob_ref[:, tc - 2 - j, cs] = nxt[nb:2 * nb]


def _scan(pre, bm, s0, *, nb, t_len, name):
    w0, w1, kka0, kka1, kd0, kd1, kk = (p.reshape(nb, t_len, BR_W) for p in pre)
    bm3 = bm.reshape(nb, t_len, B_COLS)
    n_t = t_len // SCAN_TC
    vh = RW_HD * nb * 2 * RW_HEADS // LANES
    fwd = lambda c: pl.BlockSpec((nb, SCAN_TC, BR_W), lambda i: (0, i, c))
    bwd = lambda c: pl.BlockSpec((nb, SCAN_TC, BR_W), lambda i: (0, n_t - 1 - i, c))
    sspec = pl.BlockSpec((RW_HD, vh, LANES), lambda i: (0, 0, 0))
    has_init = s0 is not None
    args = ((s0,) if has_init else ()) + (w0, w1, kka0, kka1, kd0, kd1, kk, kk, bm3, bm3, bm3, bm3)
    in_specs = ([sspec] if has_init else []) + [fwd(0), bwd(0)] * 5 + [fwd(2), bwd(2)]
    kbuf = pltpu.VMEM((SCAN_TC, RW_HD, LANES), F32)
    vbuf = pltpu.VMEM((SCAN_TC, vh, LANES), F32)
    return pl.pallas_call(
        functools.partial(_scan_kernel, nb=nb, has_init=has_init),
        grid=(n_t,),
        in_specs=in_specs,
        out_specs=[pl.BlockSpec((nb, SCAN_TC, BR_W), lambda i: (0, i, 0)),
                   pl.BlockSpec((nb, SCAN_TC, BR_W), lambda i: (0, n_t - 1 - i, 0)),
                   sspec],
        out_shape=[jax.ShapeDtypeStruct((nb, t_len, BR_W), F32),
                   jax.ShapeDtypeStruct((nb, t_len, BR_W), F32),
                   jax.ShapeDtypeStruct((RW_HD, vh, LANES), F32)],
        scratch_shapes=[kbuf] * 5 + [vbuf] * 2,
        compiler_params=_cparams(("arbitrary",)),
        name=name,
    )(*args)


LAT_VS = 4
LAT_VH = RW_HD // LAT_VS


def _lat_state(s):
    s6 = s.reshape(DEC_BATCH, 2, RW_HEADS, LAT_VS, LAT_VH, RW_HD)
    return s6.transpose(5, 4, 3, 2, 1, 0).reshape(RW_HD, LAT_VH, LANES)


def _sink_softmax_pv(s, sk, vb):
    m = jnp.maximum(jnp.max(s, axis=-1, keepdims=True), sk)
    p = jnp.exp(s - m)
    den = jnp.sum(p, axis=-1, keepdims=True) + jnp.exp(sk - m)
    return jnp.dot(p.astype(BF16), vb, preferred_element_type=F32) / den


def _att_ctx_kernel(sink_ref, q_ref, k_ref, v_ref, o_ref):
    scale = ATT_HD ** -0.5
    q = q_ref[...]
    kb = k_ref[...].astype(BF16)
    vb = v_ref[...].astype(BF16)
    outs = []
    for hd in range(ATT_HEADS):
        kvh = hd // ATT_GROUP
        qh = q[:, hd * ATT_HD:(hd + 1) * ATT_HD].astype(BF16)
        kh = kb[:, kvh * ATT_HD:(kvh + 1) * ATT_HD]
        s = lax.dot_general(qh, kh, (((1,), (1,)), ((), ())), preferred_element_type=F32) * scale
        outs.append(_sink_softmax_pv(s, sink_ref[hd], vb[:, kvh * ATT_HD:(kvh + 1) * ATT_HD]))
    o_ref[...] = jnp.concatenate(outs, axis=1)


def _att_ctx(sink, cq, ck, cv):
    kvw = ATT_KV_HEADS * ATT_HD
    return pl.pallas_call(
        _att_ctx_kernel,
        grid=(BATCH,),
        in_specs=[
            pl.BlockSpec(memory_space=pltpu.SMEM),
            pl.BlockSpec((SEQ, BR_W), lambda b: (b, 0)),
            pl.BlockSpec((SEQ, kvw), lambda b: (b, 0)),
            pl.BlockSpec((SEQ, kvw), lambda b: (b, 0)),
        ],
        out_specs=pl.BlockSpec((SEQ, BR_W), lambda b: (b, 0)),
        out_shape=jax.ShapeDtypeStruct((N_CTX, BR_W), F32),
        compiler_params=_cparams(("arbitrary",)),
        name="att_ctx",
    )(sink, cq, ck, cv)


def _rope(x, cos, sin_signed):
    w = x.shape[1]
    lane = lax.broadcasted_iota(jnp.int32, (1, w), 1)
    first = (lane % 32) < 16
    swapped = jnp.where(first, pltpu.roll(x, w - 16, 1), pltpu.roll(x, 16, 1))
    return x * cos + swapped * sin_signed


def _att_lat_kernel(sink_ref, q_ref, k_ref, v_ref, ck_ref, cv_ref, cq_ref, sq_ref, ckk_ref, skk_ref,
                    o_ref, qs_ref, ks_ref, vs_ref):
    scale = ATT_HD ** -0.5
    nb = DEC_SEQ // CHUNK
    qs_ref[...] = _rope(q_ref[...], cq_ref[...], sq_ref[...]).astype(BF16)
    ks_ref[...] = _rope(k_ref[...], ckk_ref[...], skk_ref[...]).astype(BF16)
    vs_ref[...] = v_ref[...].astype(BF16)
    ckb = ck_ref[...].astype(BF16)
    cvb = cv_ref[...].astype(BF16)
    for n in range(nb):
        lo = max(n - 1, 0) * CHUNK
        hi = min(n + 2, nb) * CHUNK
        nloc = hi - lo
        ncol = nloc + PAST_LEN
        col = lax.broadcasted_iota(jnp.int32, (CHUNK, ncol), 1)
        row = lax.broadcasted_iota(jnp.int32, (CHUNK, ncol), 0)
        dist = jnp.abs((n * CHUNK + row) - (lo + col))
        valid = (col >= nloc) | (dist <= WINDOW)
        outs = []
        for hd in range(ATT_HEADS):
            kvh = hd // ATT_GROUP
            hs = slice(hd * ATT_HD, (hd + 1) * ATT_HD)
            ks = slice(kvh * ATT_HD, (kvh + 1) * ATT_HD)
            qh = qs_ref[n * CHUNK:(n + 1) * CHUNK, hs]
            kall = jnp.concatenate([ks_ref[lo:hi, ks], ckb[:, ks]], axis=0)
            vall = jnp.concatenate([vs_ref[lo:hi, ks], cvb[:, ks]], axis=0)
            s = lax.dot_general(qh, kall, (((1,), (1,)), ((), ())), preferred_element_type=F32) * scale
            s = jnp.where(valid, s, NEG_INF)
            outs.append(_sink_softmax_pv(s, sink_ref[hd], vall))
        o_ref[n * CHUNK:(n + 1) * CHUNK, :] = jnp.concatenate(outs, axis=1)


def _att_lat(sink, cq, ck, cv, cache_k, cache_v, layer, tables):
    kvw = ATT_KV_HEADS * ATT_HD
    cosq, sinq, cosk, sink_k = tables
    seq = lambda w: pl.BlockSpec((DEC_SEQ, w), lambda b: (b, 0))
    cache = pl.BlockSpec((None, None, PAST_LEN, kvw), lambda b: (b, layer, 0, 0))
    tab = lambda w: pl.BlockSpec((DEC_SEQ, w), lambda b: (0, 0))
    return pl.pallas_call(
        _att_lat_kernel,
        grid=(DEC_BATCH,),
        in_specs=[pl.BlockSpec(memory_space=pltpu.SMEM), seq(BR_W), seq(kvw), seq(kvw), cache, cache,
                  tab(BR_W), tab(BR_W), tab(kvw), tab(kvw)],
        out_specs=pl.BlockSpec((DEC_SEQ, BR_W), lambda b: (b, 0)),
        out_shape=jax.ShapeDtypeStruct((N_LAT, BR_W), F32),
        scratch_shapes=[pltpu.VMEM((DEC_SEQ, BR_W), BF16), pltpu.VMEM((DEC_SEQ, kvw), BF16),
                        pltpu.VMEM((DEC_SEQ, kvw), BF16)],
        compiler_params=_cparams(("arbitrary",)),
        name="att_lat",
    )(sink, cq, ck, cv, cache_k, cache_v, cosq, sinq, cosk, sink_k)


def _rope_tables():
    pos = np.arange(DEC_SEQ)
    lane = np.arange(ATT_HD)
    p = np.where(lane[None, :] < 32, (pos // GRID_W)[:, None], (pos % GRID_W)[:, None]).astype(np.float32)
    inv = (ROPE_BASE ** (-jnp.arange(0, 32, 2, dtype=F32) / 32))[lane % 16]
    ang = jnp.asarray(p) * inv[None, :]
    sign = jnp.asarray(np.where((lane % 32) < 16, -1.0, 1.0).astype(np.float32))
    cos, sin = jnp.cos(ang), jnp.sin(ang) * sign[None, :]
    return (jnp.tile(cos, (1, ATT_HEADS)), jnp.tile(sin, (1, ATT_HEADS)),
            jnp.tile(cos, (1, ATT_KV_HEADS)), jnp.tile(sin, (1, ATT_KV_HEADS)))


POOL_PAD = 8


def _pool_kernel(x_ref, w_ref, sc_ref, o_ref, pad_ref):
    n = x_ref.shape[0]
    x = x_ref[...]
    pad_ref[0:POOL_PAD, :] = jnp.zeros((POOL_PAD, BR_W), F32)
    pad_ref[POOL_PAD:POOL_PAD + n, :] = x
    pad_ref[POOL_PAD + n:2 * POOL_PAD + n, :] = jnp.zeros((POOL_PAD, BR_W), F32)
    grp = lax.broadcasted_iota(jnp.int32, (1, BR_W), 1) // POOL_GD
    half = jnp.left_shift(1, grp)
    acc = jnp.zeros((n, BR_W), F32)
    for j in range(-POOL_PAD, POOL_PAD):
        inside = (j >= -half) & (j < half)
        acc = acc + jnp.where(inside, pad_ref[POOL_PAD + j:POOL_PAD + j + n, :], 0.0)
    t = lax.broadcasted_iota(jnp.int32, (n, BR_W), 0)
    cnt = jnp.minimum(t + half, n) - jnp.maximum(t - half, 0)
    d = acc / cnt.astype(F32) - x
    o_ref[...] = jnp.dot(d.astype(BF16), w_ref[...], preferred_element_type=F32) * sc_ref[...]


def _pool(dp, w_bd, scale, n_seq, seq_len, name):
    return pl.pallas_call(
        _pool_kernel,
        grid=(n_seq,),
        in_specs=[pl.BlockSpec((seq_len, BR_W), lambda b: (b, 0)),
                  pl.BlockSpec((BR_W, BR_W), lambda b: (0, 0)),
                  pl.BlockSpec((1, BR_W), lambda b: (0, 0))],
        out_specs=pl.BlockSpec((seq_len, BR_W), lambda b: (b, 0)),
        out_shape=jax.ShapeDtypeStruct((n_seq * seq_len, BR_W), F32),
        scratch_shapes=[pltpu.VMEM((seq_len + 2 * POOL_PAD, BR_W), F32)],
        compiler_params=_cparams(("arbitrary",)),
        name=name,
    )(dp, w_bd, scale.reshape(1, BR_W))


def _merge_kernel(x_ref, mod_ref, g_ref, a_ref, bonus_ref, of_ref, ob_ref, yc_ref, yd_ref,
                  wz_ref, wmg_ref, ws_ref, bs_ref, lng_ref, lnb_ref, ones_ref, wup_ref, wo_ref, gf_ref,
                  o_ref, *, lat, final):
    i = pl.program_id(0)
    m = mod_ref[pl.ds(_mod_row(i, TM_MERGE, lat), 1), :]
    x = x_ref[...]
    hb = _norm_mod(x, g_ref[...], m).astype(BF16)

    a = a_ref[...]
    a_u, a_v = a[:, :BR_W], a[:, BR_W:]
    grp = lax.broadcasted_iota(jnp.int32, (1, BR_W), 1) // A_GD
    svs = []
    for c in range(TM_MERGE // CHUNK):
        vc = a_v[c * CHUNK:(c + 1) * CHUNK, :]
        sv = bs_ref[...]
        for g in range(A_GROUPS):
            sv = sv + jnp.dot(ws_ref[g], jnp.where(grp == g, vc, 0.0).astype(BF16), preferred_element_type=F32)
        svs.append(sv)
    y_a = a_u * jnp.concatenate(svs, axis=0)

    ones = ones_ref[...]
    osum = of_ref[...] + ob_ref[...]
    mu = jnp.dot(osum, ones, precision=HIGHEST, preferred_element_type=F32) * (1.0 / RW_HD)
    dev = osum - mu
    var = jnp.dot(dev * dev, ones, precision=HIGHEST, preferred_element_type=F32) * (1.0 / RW_HD)
    y_b = dev * lax.rsqrt(var + GN_EPS) * lng_ref[...] + lnb_ref[...] + bonus_ref[...]

    merged = jnp.zeros((TM_MERGE, D_MODEL), F32)
    for n, y in enumerate((y_a, y_b, yc_ref[...], yd_ref[...])):
        z = jnp.dot(hb, wz_ref[:, n * BR_W:(n + 1) * BR_W], preferred_element_type=F32)
        ys = y * (z * _sigmoid(z))
        up = jnp.dot(ys.astype(BF16), wup_ref[n], preferred_element_type=F32)
        mg = jnp.dot(hb, wmg_ref[:, n * D_MODEL:(n + 1) * D_MODEL], preferred_element_type=F32)
        merged = merged + _sigmoid(mg) * up
    gate = m[:, 2 * D_MODEL:]
    out = x + gate * jnp.dot(merged.astype(BF16), wo_ref[...], preferred_element_type=F32)
    if final:
        ms = jnp.mean(out * out, axis=-1, keepdims=True)
        out = out * lax.rsqrt(ms + NORM_EPS) * gf_ref[...]
    o_ref[...] = out


def _merge(x, mod_l, g, a, bonus, o_f, o_b, y_c, y_d, wz, wmg, ws, bs_tile, ln_g, ln_b, ones_bd, wup, wo,
           g_final, lat, final):
    row = lambda w: pl.BlockSpec((TM_MERGE, w), lambda i: (i, 0))
    full2 = lambda s: pl.BlockSpec(s, lambda i: (0, 0))
    full3 = lambda s: pl.BlockSpec(s, lambda i: (0, 0, 0))
    return pl.pallas_call(
        functools.partial(_merge_kernel, lat=lat, final=final),
        grid=(N_GRP // TM_MERGE,),
        in_specs=[row(D_MODEL), full2((8, 3 * D_MODEL)), full2((1, D_MODEL)), row(2 * BR_W),
                  row(BR_W), row(BR_W), row(BR_W), row(BR_W), row(BR_W),
                  full2((D_MODEL, N_BRANCH * BR_W)), full2((D_MODEL, N_BRANCH * D_MODEL)),
                  full3((A_GROUPS, CHUNK, CHUNK)), full2((CHUNK, BR_W)),
                  full2((1, BR_W)), full2((1, BR_W)), full2((BR_W, BR_W)),
                  full3((N_BRANCH, BR_W, D_MODEL)), full2((D_MODEL, D_MODEL)), full2((1, D_MODEL))],
        out_specs=row(D_MODEL),
        out_shape=jax.ShapeDtypeStruct((N_GRP, D_MODEL), F32),
        compiler_params=_cparams(("arbitrary",)),
        name=("merge_lat" if lat else "merge_ctx") + ("_final" if final else ""),
    )(x, mod_l, g.reshape(1, D_MODEL), a, bonus, o_f, o_b, y_c, y_d, wz, wmg, ws, bs_tile,
      ln_g.reshape(1, BR_W), ln_b.reshape(1, BR_W), ones_bd, wup, wo, g_final.reshape(1, D_MODEL))


def _block_diag(blocks):
    n, r, c = blocks.shape
    eye = jnp.eye(n, dtype=blocks.dtype)
    return (eye[:, None, :, None] * blocks[:, :, None, :]).reshape(n * r, n * c)


def kernel(x_prompt, x_sample, cache_k, cache_v, state_rwkv, c, c_ctx, w_mod, b_mod, g_norm, w_in, w_s, b_s,
           rw_w0, rw_w_up, rw_a0, rw_a_up, rw_k_k, rw_k_a, rw_r_k, rw_ln_g, rw_ln_b, att_sink, pool_w,
           pool_scale, w_up, w_o, g_final):
    xs = [x_prompt.reshape(N_CTX, D_MODEL), x_sample.reshape(N_LAT, D_MODEL)]
    cond = jnp.concatenate([c_ctx[None, :], c, jnp.zeros((8 - 1 - DEC_BATCH, D_MODEL), F32)], axis=0)
    mod = _modulation(cond, w_mod, b_mod)

    ones_bd = _block_diag(jnp.ones((RW_HEADS, RW_HD, RW_HD), F32))
    tables = _rope_tables()
    kvw = ATT_KV_HEADS * ATT_HD
    cache_k4 = cache_k.reshape(DEC_BATCH, DEPTH, PAST_LEN, kvw)
    cache_v4 = cache_v.reshape(DEC_BATCH, DEPTH, PAST_LEN, kvw)
    z0, z1 = IN_SMALL, IN_SMALL + N_BRANCH * BR_W

    new_k, new_v, new_s = [], [], []
    for l in range(DEPTH):
        w_small = w_in[l, :, :z0].astype(BF16)
        w_z = w_in[l, :, z0:z1].astype(BF16)
        w_mg = w_in[l, :, z1:].astype(BF16)
        w_s_b, w_up_b, w_o_b = w_s[l].astype(BF16), w_up[l].astype(BF16), w_o[l].astype(BF16)
        pw = _block_diag(pool_w[l]).astype(BF16)
        bs_tile = jnp.repeat(b_s[l].T, A_GD, axis=1)
        final = l == DEPTH - 1
        for lat in (False, True):
            tag = "lat" if lat else "ctx"
            x = xs[lat]
            a, bm, cq, ck, cv, dp = _inproj(x, mod[l], g_norm[l], w_small, lat)

            pre = _rwkv_pre(bm, rw_w0[l], rw_w_up[l], rw_a0[l], rw_a_up[l], rw_k_k[l], rw_k_a[l], rw_r_k[l],
                            ones_bd, "rwkv_pre_" + tag)
            if lat:
                o_f, o_b, _ = _scan(pre[:7], bm, _lat_state(state_rwkv[:, l]), nb=DEC_BATCH, t_len=DEC_SEQ,
                                    name="scan_lat")
                y_c = _att_lat(att_sink[l], cq, ck, cv, cache_k4, cache_v4, l, tables)
                y_d = _pool(dp, pw, pool_scale[l], DEC_BATCH, DEC_SEQ, "pool_lat")
            else:
                o_f, o_b, s_fin = _scan(pre[:7], bm, None, nb=BATCH, t_len=SEQ, name="scan_ctx")
                y_c = _att_ctx(att_sink[l], cq, ck, cv)
                y_d = _pool(dp, pw, pool_scale[l], BATCH, SEQ, "pool_ctx")
                new_k.append(ck.reshape(BATCH, SEQ, ATT_KV_HEADS, ATT_HD))
                new_v.append(cv.reshape(BATCH, SEQ, ATT_KV_HEADS, ATT_HD))
                new_s.append(s_fin.reshape(RW_HD, RW_HD, RW_HEADS, 2, BATCH).transpose(4, 3, 2, 1, 0))

            xs[lat] = _merge(x, mod[l], g_norm[l], a, pre[7], o_f.reshape(N_GRP, BR_W), o_b.reshape(N_GRP, BR_W),
                             y_c, y_d, w_z, w_mg, w_s_b, bs_tile, rw_ln_g[l], rw_ln_b[l], ones_bd, w_up_b, w_o_b,
                             g_final, lat, final)

    y_prompt = xs[0].reshape(BATCH, SEQ, D_MODEL)
    y_sample = xs[1].reshape(DEC_BATCH, DEC_SEQ, D_MODEL)
    return (y_prompt, y_sample, jnp.stack(new_k, axis=1), jnp.stack(new_v, axis=1), jnp.stack(new_s, axis=1))
```

```python
import functools

import numpy as np
import jax
import jax.numpy as jnp
from jax import lax
from jax.experimental import pallas as pl
from jax.experimental.pallas import tpu as pltpu

D_MODEL = 1024
BATCH = 16
SEQ = 256
DEPTH = 2
DEC_BATCH = 4
DEC_SEQ = 1024
PAST_LEN = 512
GRID_W = 64
N_BRANCH = 4
BR_W = D_MODEL // 4
CHUNK = 128
A_GROUPS = 4
A_GD = BR_W // A_GROUPS
RW_HD = 64
RW_HEADS = BR_W // RW_HD
DECAY_RANK = 64
ICL_RANK = 64
ATT_HD = 64
ATT_HEADS = BR_W // ATT_HD
ATT_KV_HEADS = 2
ATT_GROUP = ATT_HEADS // ATT_KV_HEADS
WINDOW = 128
ROPE_BASE = 10000.0
POOL_SIZES = (2, 4, 8, 16)
POOL_GD = BR_W // len(POOL_SIZES)
NORM_EPS = 1e-6
GN_EPS = 64e-5
NEG_INF = -1e30

N_CTX = BATCH * SEQ
N_LAT = DEC_BATCH * DEC_SEQ
N_GRP = N_CTX
assert N_LAT == N_GRP
IN_SMALL = 2176
IN_GATES = N_BRANCH * BR_W + N_BRANCH * D_MODEL
B_COLS = 3 * BR_W + DECAY_RANK + ICL_RANK

F32 = jnp.float32
BF16 = jnp.bfloat16

V7X_VMEM_LIMIT = 56 * 1024 * 1024
LANES = 128

TM_IN = 512
TM_MERGE = 512
SCAN_TC = 32


def _cparams(sem):
    return pltpu.CompilerParams(dimension_semantics=sem, vmem_limit_bytes=V7X_VMEM_LIMIT)


def _mod_row(i, tm, lat):
    return 1 + i // (DEC_SEQ // tm) if lat else 0


def _norm_mod(x, g, m):
    ms = jnp.mean(x * x, axis=-1, keepdims=True)
    y = x * lax.rsqrt(ms + NORM_EPS) * g
    shift = m[:, :D_MODEL]
    scale = m[:, D_MODEL:2 * D_MODEL]
    return y * (1.0 + scale) + shift


def _sigmoid(x):
    return 1.0 / (1.0 + jnp.exp(-x))


def _split(x):
    hi = x.astype(BF16)
    return hi, (x - hi.astype(F32)).astype(BF16)


def _seg_sum(x, ones_bd):
    hi, lo = _split(x)
    return (jnp.dot(hi, ones_bd, preferred_element_type=F32)
            + jnp.dot(lo, ones_bd, preferred_element_type=F32))


def _dot_split(x, w):
    xh, xl = _split(x)
    wh, wl = _split(w)
    lhs = jnp.concatenate([xh, xl, xh], axis=1)
    rhs = jnp.concatenate([wh, wh, wl], axis=0)
    return jnp.dot(lhs, rhs, preferred_element_type=F32)


def _mod_kernel(c_ref, w_ref, b_ref, o_ref):
    cnd = c_ref[...]
    s = cnd * _sigmoid(cnd)
    o_ref[...] = jnp.dot(s.astype(BF16), w_ref[...].astype(BF16), preferred_element_type=F32) + b_ref[...]


def _modulation(cond, w_mod, b_mod):
    nt = 3 * D_MODEL // 1024
    return pl.pallas_call(
        _mod_kernel,
        grid=(DEPTH, nt),
        in_specs=[
            pl.BlockSpec((8, D_MODEL), lambda l, j: (0, 0)),
            pl.BlockSpec((None, D_MODEL, 1024), lambda l, j: (l, 0, j)),
            pl.BlockSpec((None, 1, 1024), lambda l, j: (l, 0, j)),
        ],
        out_specs=pl.BlockSpec((None, 8, 1024), lambda l, j: (l, 0, j)),
        out_shape=jax.ShapeDtypeStruct((DEPTH, 8, 3 * D_MODEL), F32),
        compiler_params=_cparams(("arbitrary", "arbitrary")),
        name="modulation",
    )(cond, w_mod, b_mod.reshape(DEPTH, 1, 3 * D_MODEL))


def _inproj_kernel(x_ref, mod_ref, g_ref, w_ref, a_ref, b_ref, q_ref, k_ref, v_ref, d_ref, *, lat):
    i = pl.program_id(0)
    m = mod_ref[pl.ds(_mod_row(i, TM_IN, lat), 1), :]
    h = _norm_mod(x_ref[...], g_ref[...], m)
    p = jnp.dot(h.astype(BF16), w_ref[...], preferred_element_type=F32)
    a_ref[...] = p[:, 0:512]
    b_ref[...] = p[:, 512:1408]
    q_ref[...] = p[:, 1408:1664]
    k_ref[...] = p[:, 1664:1792]
    v_ref[...] = p[:, 1792:1920]
    d_ref[...] = p[:, 1920:2176]


def _inproj(x, mod_l, g, w_small, lat):
    widths = (512, B_COLS, 256, 128, 128, 256)
    return pl.pallas_call(
        functools.partial(_inproj_kernel, lat=lat),
        grid=(N_GRP // TM_IN,),
        in_specs=[
            pl.BlockSpec((TM_IN, D_MODEL), lambda i: (i, 0)),
            pl.BlockSpec((8, 3 * D_MODEL), lambda i: (0, 0)),
            pl.BlockSpec((1, D_MODEL), lambda i: (0, 0)),
            pl.BlockSpec((D_MODEL, IN_SMALL), lambda i: (0, 0)),
        ],
        out_specs=[pl.BlockSpec((TM_IN, w), lambda i: (i, 0)) for w in widths],
        out_shape=[jax.ShapeDtypeStruct((N_GRP, w), F32) for w in widths],
        compiler_params=_cparams(("arbitrary",)),
        name="in_proj_lat" if lat else "in_proj_ctx",
    )(x, mod_l, g.reshape(1, D_MODEL), w_small)


def _softplus(x):
    return jnp.maximum(x, 0.0) + jnp.log1p(jnp.exp(-jnp.abs(x)))


def _rwkv_pre_kernel(b_ref, w0_ref, wup_ref, a0_ref, aup_ref, kkw_ref, ka_ref, rk_ref, ones_ref,
                     w0o, w1o, kka0o, kka1o, kd0o, kd1o, kko, bonuso):
    bm = b_ref[...]
    r = bm[:, 0:256]
    k = bm[:, 256:512]
    v = bm[:, 512:768]
    wd_t = jnp.tanh(bm[:, 768:832])
    ad = bm[:, 832:896]
    ones = ones_ref[...]
    kk = k * kkw_ref[...]
    ss = _seg_sum(kk * kk, ones)
    kkn = kk * lax.rsqrt(ss + 1e-12)
    kko[...] = kkn
    bonus = jnp.zeros_like(v)
    for d, (wo, kkao, kdo) in enumerate(((w0o, kka0o, kd0o), (w1o, kka1o, kd1o))):
        pre = w0_ref[pl.ds(d, 1), :] + _dot_split(wd_t, wup_ref[d])
        w_log = -_softplus(-pre) - 0.5
        wo[...] = jnp.exp(-jnp.exp(w_log))
        a = _sigmoid(a0_ref[pl.ds(d, 1), :] + _dot_split(ad, aup_ref[d]))
        k_d = k * (1.0 + (a - 1.0) * ka_ref[...])
        kdo[...] = k_d
        kkao[...] = kkn * a
        bonus = bonus + _seg_sum(r * k_d * rk_ref[...], ones) * v
    bonuso[...] = bonus


def _rwkv_pre(bm, w0, wup, a0, aup, k_k, k_a, r_k, ones_bd, name):
    full2 = lambda s: pl.BlockSpec(s, lambda i: (0, 0))
    return pl.pallas_call(
        _rwkv_pre_kernel,
        grid=(N_GRP // TM_IN,),
        in_specs=[
            pl.BlockSpec((TM_IN, B_COLS), lambda i: (i, 0)),
            full2((2, BR_W)),
            pl.BlockSpec((2, DECAY_RANK, BR_W), lambda i: (0, 0, 0)),
            full2((2, BR_W)),
            pl.BlockSpec((2, ICL_RANK, BR_W), lambda i: (0, 0, 0)),
            full2((1, BR_W)), full2((1, BR_W)), full2((1, BR_W)),
            full2((BR_W, BR_W)),
        ],
        out_specs=[pl.BlockSpec((TM_IN, BR_W), lambda i: (i, 0))] * 8,
        out_shape=[jax.ShapeDtypeStruct((N_GRP, BR_W), F32)] * 8,
        compiler_params=_cparams(("arbitrary",)),
        name=name,
    )(bm, w0, wup, a0, aup, k_k.reshape(1, BR_W), k_a.reshape(1, BR_W), r_k.reshape(1, BR_W), ones_bd)


def _to_lanes(z0, z1, rep):
    lo = lax.broadcasted_iota(jnp.int32, (1, LANES), 1) < RW_HD
    parts = []
    for hp in range(RW_HEADS // 2):
        a = z0[:, hp * LANES:(hp + 1) * LANES]
        b = z1[:, hp * LANES:(hp + 1) * LANES]
        parts.append(jnp.where(lo, a, pltpu.roll(b, RW_HD, 1)))
        parts.append(jnp.where(lo, pltpu.roll(a, RW_HD, 1), b))
    return jnp.concatenate(parts * rep, axis=0).T


def _scan_kernel(*refs, nb, has_init):
    if has_init:
        s0_ref, refs = refs[0], refs[1:]
    (w0_ref, w1_ref, kka0_ref, kka1_ref, kd0_ref, kd1_ref, kkf_ref, kkb_ref, rf_ref, rb_ref, vf_ref, vb_ref,
     of_ref, ob_ref, st_ref, w_s, kk_s, kka_s, kd_s, r_s, v_s, o_s) = refs
    nk, vh, _ = st_ref.shape
    tc = w_s.shape[0]
    rep = RW_HD // vh
    nrec = LANES // rep
    lane = lax.broadcasted_iota(jnp.int32, (1, LANES), 1)
    grp = lane // nrec
    lo = lane < RW_HD

    def rows(f_ref, b_ref, j):
        return jnp.concatenate([f_ref[:, j, :], b_ref[:, tc - 1 - j, :]], axis=0)

    sources = ((w0_ref, w1_ref, w_s), (kkf_ref, kkb_ref, kk_s), (kka0_ref, kka1_ref, kka_s),
               (kd0_ref, kd1_ref, kd_s), (rf_ref, rb_ref, r_s))
    for j in range(0, tc, 2):
        for f_ref, b_ref, dst in sources:
            gt = _to_lanes(rows(f_ref, b_ref, j), rows(f_ref, b_ref, j + 1), rep)
            dst[j] = gt[0:RW_HD]
            dst[j + 1] = gt[RW_HD:2 * RW_HD]
        gt = _to_lanes(rows(vf_ref, vb_ref, j), rows(vf_ref, vb_ref, j + 1), rep)
        for s in range(2):
            blk = gt[s * RW_HD:(s + 1) * RW_HD]
            if rep > 1:
                blk = sum(jnp.where(grp == q, blk[q * vh:(q + 1) * vh], 0.0) for q in range(rep))
            v_s[j + s] = blk

    @pl.when(pl.program_id(0) == 0)
    def _():
        if has_init:
            st_ref[...] = s0_ref[...]
        else:
            st_ref[...] = jnp.zeros(st_ref.shape, F32)

    sa0 = jnp.zeros((vh, LANES), F32)
    for k in range(nk):
        sa0 = sa0 + st_ref[k] * kk_s[0, pl.ds(k, 1), :]

    def step(t, sa):
        tn = jnp.minimum(t + 1, tc - 1)
        vt = v_s[t]
        o = jnp.zeros((vh, LANES), F32)
        san = jnp.zeros((vh, LANES), F32)
        for k in range(nk):
            new = (st_ref[k] * w_s[t, pl.ds(k, 1), :] - kka_s[t, pl.ds(k, 1), :] * sa
                   + kd_s[t, pl.ds(k, 1), :] * vt)
            st_ref[k] = new
            o = o + new * r_s[t, pl.ds(k, 1), :]
            san = san + new * kk_s[tn, pl.ds(k, 1), :]
        o_s[t] = o
        return san

    lax.fori_loop(0, tc, step, sa0)

    for j in range(0, tc, 2):
        tiles = []
        for s in range(2):
            o = o_s[j + s]
            if rep == 1:
                tiles.append(o)
            else:
                tiles.extend(jnp.where(grp == q, o, 0.0) for q in range(rep))
        mt = jnp.concatenate(tiles, axis=0).T
        sm = mt[0:nrec]
        for q in range(1, rep):
            sm = sm + mt[q * nrec:(q + 1) * nrec]
        rs = pltpu.roll(sm, RW_HD, 1)
        for hp in range(RW_HEADS // 2):
            e0, e1 = (2 * hp) * 2 * nb, (2 * hp + 1) * 2 * nb
            cur = jnp.where(lo, sm[e0:e0 + 2 * nb], rs[e1:e1 + 2 * nb])
            nxt = jnp.where(lo, rs[e0:e0 + 2 * nb], sm[e1:e1 + 2 * nb])
            cs = slice(hp * LANES, (hp + 1) * LANES)
            of_ref[:, j, cs] = cur[0:nb]
            of_ref[:, j + 1, cs] = nxt[0:nb]
            ob_ref[:, tc - 1 - j, cs] = cur[nb:2 * nb]
            ob_ref[:, tc - 2 - j, cs] = nxt[nb:2 * nb]


def _scan(pre, bm, s0, *, nb, t_len, name):
    w0, w1, kka0, kka1, kd0, kd1, kk = (p.reshape(nb, t_len, BR_W) for p in pre)
    bm3 = bm.reshape(nb, t_len, B_COLS)
    n_t = t_len // SCAN_TC
    vh = RW_HD * nb * 2 * RW_HEADS // LANES
    fwd = lambda c: pl.BlockSpec((nb, SCAN_TC, BR_W), lambda i: (0, i, c))
    bwd = lambda c: pl.BlockSpec((nb, SCAN_TC, BR_W), lambda i: (0, n_t - 1 - i, c))
    sspec = pl.BlockSpec((RW_HD, vh, LANES), lambda i: (0, 0, 0))
    has_init = s0 is not None
    args = ((s0,) if has_init else ()) + (w0, w1, kka0, kka1, kd0, kd1, kk, kk, bm3, bm3, bm3, bm3)
    in_specs = ([sspec] if has_init else []) + [fwd(0), bwd(0)] * 5 + [fwd(2), bwd(2)]
    kbuf = pltpu.VMEM((SCAN_TC, RW_HD, LANES), F32)
    vbuf = pltpu.VMEM((SCAN_TC, vh, LANES), F32)
    return pl.pallas_call(
        functools.partial(_scan_kernel, nb=nb, has_init=has_init),
        grid=(n_t,),
        in_specs=in_specs,
        out_specs=[pl.BlockSpec((nb, SCAN_TC, BR_W), lambda i: (0, i, 0)),
                   pl.BlockSpec((nb, SCAN_TC, BR_W), lambda i: (0, n_t - 1 - i, 0)),
                   sspec],
        out_shape=[jax.ShapeDtypeStruct((nb, t_len, BR_W), F32),
                   jax.ShapeDtypeStruct((nb, t_len, BR_W), F32),
                   jax.ShapeDtypeStruct((RW_HD, vh, LANES), F32)],
        scratch_shapes=[kbuf] * 5 + [vbuf] * 2,
        compiler_params=_cparams(("arbitrary",)),
        name=name,
    )(*args)


LAT_VS = 4
LAT_VH = RW_HD // LAT_VS


def _lat_state(s):
    s6 = s.reshape(DEC_BATCH, 2, RW_HEADS, LAT_VS, LAT_VH, RW_HD)
    return s6.transpose(5, 4, 3, 2, 1, 0).reshape(RW_HD, LAT_VH, LANES)


def _sink_softmax_pv(s, sk, vb):
    m = jnp.maximum(jnp.max(s, axis=-1, keepdims=True), sk)
    p = jnp.exp(s - m)
    den = jnp.sum(p, axis=-1, keepdims=True) + jnp.exp(sk - m)
    return jnp.dot(p.astype(BF16), vb, preferred_element_type=F32) / den


def _att_ctx_kernel(sink_ref, q_ref, k_ref, v_ref, o_ref):
    scale = ATT_HD ** -0.5
    q = q_ref[...]
    kb = k_ref[...].astype(BF16)
    vb = v_ref[...].astype(BF16)
    outs = []
    for hd in range(ATT_HEADS):
        kvh = hd // ATT_GROUP
        qh = q[:, hd * ATT_HD:(hd + 1) * ATT_HD].astype(BF16)
        kh = kb[:, kvh * ATT_HD:(kvh + 1) * ATT_HD]
        s = lax.dot_general(qh, kh, (((1,), (1,)), ((), ())), preferred_element_type=F32) * scale
        outs.append(_sink_softmax_pv(s, sink_ref[hd], vb[:, kvh * ATT_HD:(kvh + 1) * ATT_HD]))
    o_ref[...] = jnp.concatenate(outs, axis=1)


def _att_ctx(sink, cq, ck, cv):
    kvw = ATT_KV_HEADS * ATT_HD
    return pl.pallas_call(
        _att_ctx_kernel,
        grid=(BATCH,),
        in_specs=[
            pl.BlockSpec(memory_space=pltpu.SMEM),
            pl.BlockSpec((SEQ, BR_W), lambda b: (b, 0)),
            pl.BlockSpec((SEQ, kvw), lambda b: (b, 0)),
            pl.BlockSpec((SEQ, kvw), lambda b: (b, 0)),
        ],
        out_specs=pl.BlockSpec((SEQ, BR_W), lambda b: (b, 0)),
        out_shape=jax.ShapeDtypeStruct((N_CTX, BR_W), F32),
        compiler_params=_cparams(("arbitrary",)),
        name="att_ctx",
    )(sink, cq, ck, cv)


def _rope(x, cos, sin_signed):
    w = x.shape[1]
    lane = lax.broadcasted_iota(jnp.int32, (1, w), 1)
    first = (lane % 32) < 16
    swapped = jnp.where(first, pltpu.roll(x, w - 16, 1), pltpu.roll(x, 16, 1))
    return x * cos + swapped * sin_signed


def _att_lat_kernel(sink_ref, q_ref, k_ref, v_ref, ck_ref, cv_ref, cq_ref, sq_ref, ckk_ref, skk_ref,
                    o_ref, qs_ref, ks_ref, vs_ref):
    scale = ATT_HD ** -0.5
    nb = DEC_SEQ // CHUNK
    qs_ref[...] = _rope(q_ref[...], cq_ref[...], sq_ref[...]).astype(BF16)
    ks_ref[...] = _rope(k_ref[...], ckk_ref[...], skk_ref[...]).astype(BF16)
    vs_ref[...] = v_ref[...].astype(BF16)
    ckb = ck_ref[...].astype(BF16)
    cvb = cv_ref[...].astype(BF16)
    for n in range(nb):
        lo = max(n - 1, 0) * CHUNK
        hi = min(n + 2, nb) * CHUNK
        nloc = hi - lo
        ncol = nloc + PAST_LEN
        col = lax.broadcasted_iota(jnp.int32, (CHUNK, ncol), 1)
        row = lax.broadcasted_iota(jnp.int32, (CHUNK, ncol), 0)
        dist = jnp.abs((n * CHUNK + row) - (lo + col))
        valid = (col >= nloc) | (dist <= WINDOW)
        outs = []
        for hd in range(ATT_HEADS):
            kvh = hd // ATT_GROUP
            hs = slice(hd * ATT_HD, (hd + 1) * ATT_HD)
            ks = slice(kvh * ATT_HD, (kvh + 1) * ATT_HD)
            qh = qs_ref[n * CHUNK:(n + 1) * CHUNK, hs]
            kall = jnp.concatenate([ks_ref[lo:hi, ks], ckb[:, ks]], axis=0)
            vall = jnp.concatenate([vs_ref[lo:hi, ks], cvb[:, ks]], axis=0)
            s = lax.dot_general(qh, kall, (((1,), (1,)), ((), ())), preferred_element_type=F32) * scale
            s = jnp.where(valid, s, NEG_INF)
            outs.append(_sink_softmax_pv(s, sink_ref[hd], vall))
        o_ref[n * CHUNK:(n + 1) * CHUNK, :] = jnp.concatenate(outs, axis=1)


def _att_lat(sink, cq, ck, cv, cache_k, cache_v, layer, tables):
    kvw = ATT_KV_HEADS * ATT_HD
    cosq, sinq, cosk, sink_k = tables
    seq = lambda w: pl.BlockSpec((DEC_SEQ, w), lambda b: (b, 0))
    cache = pl.BlockSpec((None, None, PAST_LEN, kvw), lambda b: (b, layer, 0, 0))
    tab = lambda w: pl.BlockSpec((DEC_SEQ, w), lambda b: (0, 0))
    return pl.pallas_call(
        _att_lat_kernel,
        grid=(DEC_BATCH,),
        in_specs=[pl.BlockSpec(memory_space=pltpu.SMEM), seq(BR_W), seq(kvw), seq(kvw), cache, cache,
                  tab(BR_W), tab(BR_W), tab(kvw), tab(kvw)],
        out_specs=pl.BlockSpec((DEC_SEQ, BR_W), lambda b: (b, 0)),
        out_shape=jax.ShapeDtypeStruct((N_LAT, BR_W), F32),
        scratch_shapes=[pltpu.VMEM((DEC_SEQ, BR_W), BF16), pltpu.VMEM((DEC_SEQ, kvw), BF16),
                        pltpu.VMEM((DEC_SEQ, kvw), BF16)],
        compiler_params=_cparams(("arbitrary",)),
        name="att_lat",
    )(sink, cq, ck, cv, cache_k, cache_v, cosq, sinq, cosk, sink_k)


def _rope_tables():
    pos = np.arange(DEC_SEQ)
    lane = np.arange(ATT_HD)
    p = np.where(lane[None, :] < 32, (pos // GRID_W)[:, None], (pos % GRID_W)[:, None]).astype(np.float32)
    inv = (ROPE_BASE ** (-jnp.arange(0, 32, 2, dtype=F32) / 32))[lane % 16]
    ang = jnp.asarray(p) * inv[None, :]
    sign = jnp.asarray(np.where((lane % 32) < 16, -1.0, 1.0).astype(np.float32))
    cos, sin = jnp.cos(ang), jnp.sin(ang) * sign[None, :]
    return (jnp.tile(cos, (1, ATT_HEADS)), jnp.tile(sin, (1, ATT_HEADS)),
            jnp.tile(cos, (1, ATT_KV_HEADS)), jnp.tile(sin, (1, ATT_KV_HEADS)))


POOL_PAD = 8


def _pool_kernel(x_ref, w_ref, sc_ref, o_ref, a_ref, b_ref):
    n = x_ref.shape[0]
    ext = n + POOL_PAD
    x = x_ref[...]
    zeros = jnp.zeros((POOL_PAD, BR_W), F32)
    a_ref[0:POOL_PAD, :] = zeros
    b_ref[0:POOL_PAD, :] = zeros
    a_ref[POOL_PAD:POOL_PAD + n, :] = x
    a_ref[POOL_PAD + n:POOL_PAD + ext, :] = zeros
    grp = lax.broadcasted_iota(jnp.int32, (1, BR_W), 1) // POOL_GD
    acc = jnp.zeros((n, BR_W), F32)
    src_ref, dst_ref = a_ref, b_ref
    for m in range(len(POOL_SIZES)):
        back = 1 << m
        dst_ref[POOL_PAD:POOL_PAD + ext, :] = (src_ref[POOL_PAD:POOL_PAD + ext, :]
                                               + src_ref[POOL_PAD - back:POOL_PAD - back + ext, :])
        off = POOL_PAD + back - 1
        acc = jnp.where(grp == m, dst_ref[off:off + n, :], acc)
        src_ref, dst_ref = dst_ref, src_ref
    half = jnp.left_shift(1, grp)
    t = lax.broadcasted_iota(jnp.int32, (n, BR_W), 0)
    cnt = jnp.minimum(t + half, n) - jnp.maximum(t - half, 0)
    d = acc / cnt.astype(F32) - x
    o_ref[...] = jnp.dot(d.astype(BF16), w_ref[...], preferred_element_type=F32) * sc_ref[...]


def _pool(dp, w_bd, scale, n_seq, seq_len, name):
    return pl.pallas_call(
        _pool_kernel,
        grid=(n_seq,),
        in_specs=[pl.BlockSpec((seq_len, BR_W), lambda b: (b, 0)),
                  pl.BlockSpec((BR_W, BR_W), lambda b: (0, 0)),
                  pl.BlockSpec((1, BR_W), lambda b: (0, 0))],
        out_specs=pl.BlockSpec((seq_len, BR_W), lambda b: (b, 0)),
        out_shape=jax.ShapeDtypeStruct((n_seq * seq_len, BR_W), F32),
        scratch_shapes=[pltpu.VMEM((seq_len + 2 * POOL_PAD, BR_W), F32)] * 2,
        compiler_params=_cparams(("arbitrary",)),
        name=name,
    )(dp, w_bd, scale.reshape(1, BR_W))


def _merge_kernel(x_ref, mod_ref, g_ref, a_ref, bonus_ref, of_ref, ob_ref, yc_ref, yd_ref,
                  wz_ref, wmg_ref, ws_ref, bs_ref, lng_ref, lnb_ref, ones_ref, wup_ref, wo_ref, gf_ref,
                  o_ref, *, lat, final):
    i = pl.program_id(0)
    m = mod_ref[pl.ds(_mod_row(i, TM_MERGE, lat), 1), :]
    x = x_ref[...]
    hb = _norm_mod(x, g_ref[...], m).astype(BF16)

    a = a_ref[...]
    a_u, a_v = a[:, :BR_W], a[:, BR_W:]
    grp = lax.broadcasted_iota(jnp.int32, (1, BR_W), 1) // A_GD
    svs = []
    for c in range(TM_MERGE // CHUNK):
        vc = a_v[c * CHUNK:(c + 1) * CHUNK, :]
        sv = bs_ref[...]
        for g in range(A_GROUPS):
            sv = sv + jnp.dot(ws_ref[g], jnp.where(grp == g, vc, 0.0).astype(BF16), preferred_element_type=F32)
        svs.append(sv)
    y_a = a_u * jnp.concatenate(svs, axis=0)

    ones = ones_ref[...]
    osum = of_ref[...] + ob_ref[...]
    mu = _seg_sum(osum, ones) * (1.0 / RW_HD)
    dev = osum - mu
    var = _seg_sum(dev * dev, ones) * (1.0 / RW_HD)
    y_b = dev * lax.rsqrt(var + GN_EPS) * lng_ref[...] + lnb_ref[...] + bonus_ref[...]

    merged = jnp.zeros((TM_MERGE, D_MODEL), F32)
    for n, y in enumerate((y_a, y_b, yc_ref[...], yd_ref[...])):
        z = jnp.dot(hb, wz_ref[:, n * BR_W:(n + 1) * BR_W], preferred_element_type=F32)
        ys = y * (z * _sigmoid(z))
        up = jnp.dot(ys.astype(BF16), wup_ref[n], preferred_element_type=F32)
        mg = jnp.dot(hb, wmg_ref[:, n * D_MODEL:(n + 1) * D_MODEL], preferred_element_type=F32)
        merged = merged + _sigmoid(mg) * up
    gate = m[:, 2 * D_MODEL:]
    out = x + gate * jnp.dot(merged.astype(BF16), wo_ref[...], preferred_element_type=F32)
    if final:
        ms = jnp.mean(out * out, axis=-1, keepdims=True)
        out = out * lax.rsqrt(ms + NORM_EPS) * gf_ref[...]
    o_ref[...] = out


def _merge(x, mod_l, g, a, bonus, o_f, o_b, y_c, y_d, wz, wmg, ws, bs_tile, ln_g, ln_b, ones_bd, wup, wo,
           g_final, lat, final):
    row = lambda w: pl.BlockSpec((TM_MERGE, w), lambda i: (i, 0))
    full2 = lambda s: pl.BlockSpec(s, lambda i: (0, 0))
    full3 = lambda s: pl.BlockSpec(s, lambda i: (0, 0, 0))
    return pl.pallas_call(
        functools.partial(_merge_kernel, lat=lat, final=final),
        grid=(N_GRP // TM_MERGE,),
        in_specs=[row(D_MODEL), full2((8, 3 * D_MODEL)), full2((1, D_MODEL)), row(2 * BR_W),
                  row(BR_W), row(BR_W), row(BR_W), row(BR_W), row(BR_W),
                  full2((D_MODEL, N_BRANCH * BR_W)), full2((D_MODEL, N_BRANCH * D_MODEL)),
                  full3((A_GROUPS, CHUNK, CHUNK)), full2((CHUNK, BR_W)),
                  full2((1, BR_W)), full2((1, BR_W)), full2((BR_W, BR_W)),
                  full3((N_BRANCH, BR_W, D_MODEL)), full2((D_MODEL, D_MODEL)), full2((1, D_MODEL))],
        out_specs=row(D_MODEL),
        out_shape=jax.ShapeDtypeStruct((N_GRP, D_MODEL), F32),
        compiler_params=_cparams(("arbitrary",)),
        name=("merge_lat" if lat else "merge_ctx") + ("_final" if final else ""),
    )(x, mod_l, g.reshape(1, D_MODEL), a, bonus, o_f, o_b, y_c, y_d, wz, wmg, ws, bs_tile,
      ln_g.reshape(1, BR_W), ln_b.reshape(1, BR_W), ones_bd, wup, wo, g_final.reshape(1, D_MODEL))


def _block_diag(blocks):
    n, r, c = blocks.shape
    eye = jnp.eye(n, dtype=blocks.dtype)
    return (eye[:, None, :, None] * blocks[:, :, None, :]).reshape(n * r, n * c)


def kernel(x_prompt, x_sample, cache_k, cache_v, state_rwkv, c, c_ctx, w_mod, b_mod, g_norm, w_in, w_s, b_s,
           rw_w0, rw_w_up, rw_a0, rw_a_up, rw_k_k, rw_k_a, rw_r_k, rw_ln_g, rw_ln_b, att_sink, pool_w,
           pool_scale, w_up, w_o, g_final):
    xs = [x_prompt.reshape(N_CTX, D_MODEL), x_sample.reshape(N_LAT, D_MODEL)]
    cond = jnp.concatenate([c_ctx[None, :], c, jnp.zeros((8 - 1 - DEC_BATCH, D_MODEL), F32)], axis=0)
    mod = _modulation(cond, w_mod, b_mod)

    ones_bd = _block_diag(jnp.ones((RW_HEADS, RW_HD, RW_HD), BF16))
    tables = _rope_tables()
    kvw = ATT_KV_HEADS * ATT_HD
    cache_k4 = cache_k.reshape(DEC_BATCH, DEPTH, PAST_LEN, kvw)
    cache_v4 = cache_v.reshape(DEC_BATCH, DEPTH, PAST_LEN, kvw)
    z0, z1 = IN_SMALL, IN_SMALL + N_BRANCH * BR_W

    new_k, new_v, new_s = [], [], []
    for l in range(DEPTH):
        w_small = w_in[l, :, :z0].astype(BF16)
        w_z = w_in[l, :, z0:z1].astype(BF16)
        w_mg = w_in[l, :, z1:].astype(BF16)
        w_s_b, w_up_b, w_o_b = w_s[l].astype(BF16), w_up[l].astype(BF16), w_o[l].astype(BF16)
        pw = _block_diag(pool_w[l]).astype(BF16)
        bs_tile = jnp.repeat(b_s[l].T, A_GD, axis=1)
        final = l == DEPTH - 1
        for lat in (False, True):
            tag = "lat" if lat else "ctx"
            x = xs[lat]
            a, bm, cq, ck, cv, dp = _inproj(x, mod[l], g_norm[l], w_small, lat)

            pre = _rwkv_pre(bm, rw_w0[l], rw_w_up[l], rw_a0[l], rw_a_up[l], rw_k_k[l], rw_k_a[l], rw_r_k[l],
                            ones_bd, "rwkv_pre_" + tag)
            if lat:
                o_f, o_b, _ = _scan(pre[:7], bm, _lat_state(state_rwkv[:, l]), nb=DEC_BATCH, t_len=DEC_SEQ,
                                    name="scan_lat")
                y_c = _att_lat(att_sink[l], cq, ck, cv, cache_k4, cache_v4, l, tables)
                y_d = _pool(dp, pw, pool_scale[l], DEC_BATCH, DEC_SEQ, "pool_lat")
            else:
                o_f, o_b, s_fin = _scan(pre[:7], bm, None, nb=BATCH, t_len=SEQ, name="scan_ctx")
                y_c = _att_ctx(att_sink[l], cq, ck, cv)
                y_d = _pool(dp, pw, pool_scale[l], BATCH, SEQ, "pool_ctx")
                new_k.append(ck.reshape(BATCH, SEQ, ATT_KV_HEADS, ATT_HD))
                new_v.append(cv.reshape(BATCH, SEQ, ATT_KV_HEADS, ATT_HD))
                new_s.append(s_fin.reshape(RW_HD, RW_HD, RW_HEADS, 2, BATCH).transpose(4, 3, 2, 1, 0))

            xs[lat] = _merge(x, mod[l], g_norm[l], a, pre[7], o_f.reshape(N_GRP, BR_W), o_b.reshape(N_GRP, BR_W),
                             y_c, y_d, w_z, w_mg, w_s_b, bs_tile, rw_ln_g[l], rw_ln_b[l], ones_bd, w_up_b, w_o_b,
                             g_final, lat, final)

    y_prompt = xs[0].reshape(BATCH, SEQ, D_MODEL)
    y_sample = xs[1].reshape(DEC_BATCH, DEC_SEQ, D_MODEL)
    return (y_prompt, y_sample, jnp.stack(new_k, axis=1), jnp.stack(new_v, axis=1), jnp.stack(new_s, axis=1))
```

```python
import functools

import numpy as np
import jax
import jax.numpy as jnp
from jax import lax
from jax.experimental import pallas as pl
from jax.experimental.pallas import tpu as pltpu

D_MODEL = 1024
BATCH = 16
SEQ = 256
DEPTH = 2
DEC_BATCH = 4
DEC_SEQ = 1024
PAST_LEN = 512
GRID_W = 64
N_BRANCH = 4
BR_W = D_MODEL // 4
CHUNK = 128
A_GROUPS = 4
A_GD = BR_W // A_GROUPS
RW_HD = 64
RW_HEADS = BR_W // RW_HD
DECAY_RANK = 64
ICL_RANK = 64
ATT_HD = 64
ATT_HEADS = BR_W // ATT_HD
ATT_KV_HEADS = 2
ATT_GROUP = ATT_HEADS // ATT_KV_HEADS
WINDOW = 128
ROPE_BASE = 10000.0
POOL_SIZES = (2, 4, 8, 16)
POOL_GD = BR_W // len(POOL_SIZES)
NORM_EPS = 1e-6
GN_EPS = 64e-5
NEG_INF = -1e30

N_CTX = BATCH * SEQ
N_LAT = DEC_BATCH * DEC_SEQ
N_GRP = N_CTX
assert N_LAT == N_GRP
IN_SMALL = 2176
IN_GATES = N_BRANCH * BR_W + N_BRANCH * D_MODEL
B_COLS = 3 * BR_W + DECAY_RANK + ICL_RANK

F32 = jnp.float32
BF16 = jnp.bfloat16

V7X_VMEM_LIMIT = 56 * 1024 * 1024
LANES = 128

TM_IN = 512
TM_MERGE = 512
SCAN_TC = 32


def _cparams(sem):
    return pltpu.CompilerParams(dimension_semantics=sem, vmem_limit_bytes=V7X_VMEM_LIMIT)


def _mod_row(i, tm, lat):
    return 1 + i // (DEC_SEQ // tm) if lat else 0


def _norm_mod(x, g, m):
    ms = jnp.mean(x * x, axis=-1, keepdims=True)
    y = x * lax.rsqrt(ms + NORM_EPS) * g
    shift = m[:, :D_MODEL]
    scale = m[:, D_MODEL:2 * D_MODEL]
    return y * (1.0 + scale) + shift


def _sigmoid(x):
    return 1.0 / (1.0 + jnp.exp(-x))


def _split(x):
    hi = x.astype(BF16)
    return hi, (x - hi.astype(F32)).astype(BF16)


def _seg_sum(x, ones_bd):
    hi, lo = _split(x)
    return (jnp.dot(hi, ones_bd, preferred_element_type=F32)
            + jnp.dot(lo, ones_bd, preferred_element_type=F32))


def _dot_split(x, w):
    xh, xl = _split(x)
    wh, wl = _split(w)
    lhs = jnp.concatenate([xh, xl, xh], axis=1)
    rhs = jnp.concatenate([wh, wh, wl], axis=0)
    return jnp.dot(lhs, rhs, preferred_element_type=F32)


def _mod_kernel(c_ref, w_ref, b_ref, o_ref):
    cnd = c_ref[...]
    s = cnd * _sigmoid(cnd)
    o_ref[...] = jnp.dot(s.astype(BF16), w_ref[...].astype(BF16), preferred_element_type=F32) + b_ref[...]


def _modulation(cond, w_mod, b_mod):
    nt = 3 * D_MODEL // 1024
    return pl.pallas_call(
        _mod_kernel,
        grid=(DEPTH, nt),
        in_specs=[
            pl.BlockSpec((8, D_MODEL), lambda l, j: (0, 0)),
            pl.BlockSpec((None, D_MODEL, 1024), lambda l, j: (l, 0, j)),
            pl.BlockSpec((None, 1, 1024), lambda l, j: (l, 0, j)),
        ],
        out_specs=pl.BlockSpec((None, 8, 1024), lambda l, j: (l, 0, j)),
        out_shape=jax.ShapeDtypeStruct((DEPTH, 8, 3 * D_MODEL), F32),
        compiler_params=_cparams(("arbitrary", "arbitrary")),
        name="modulation",
    )(cond, w_mod, b_mod.reshape(DEPTH, 1, 3 * D_MODEL))


def _softplus(x):
    return jnp.maximum(x, 0.0) + jnp.log1p(jnp.exp(-jnp.abs(x)))


def _inproj_kernel(x_ref, mod_ref, g_ref, w_ref, w0_ref, wup_ref, a0_ref, aup_ref, kkw_ref, ka_ref, rk_ref, ones_ref,
                   a_ref, q_ref, ck_ref, cv_ref, d_ref, r_ref, v_ref,
                   w0o, w1o, kka0o, kka1o, kd0o, kd1o, kko, bonuso, *, lat):
    i = pl.program_id(0)
    m = mod_ref[pl.ds(_mod_row(i, TM_IN, lat), 1), :]
    h = _norm_mod(x_ref[...], g_ref[...], m)
    p = jnp.dot(h.astype(BF16), w_ref[0], preferred_element_type=F32)
    a_ref[...] = p[:, 0:512]
    q_ref[...] = p[:, 1408:1664]
    ck_ref[...] = p[:, 1664:1792]
    cv_ref[...] = p[:, 1792:1920]
    d_ref[...] = p[:, 1920:2176]

    r = p[:, 512:768]
    k = p[:, 768:1024]
    v = p[:, 1024:1280]
    wd_t = jnp.tanh(p[:, 1280:1344])
    ad = p[:, 1344:1408]
    r_ref[...] = r
    v_ref[...] = v
    ones = ones_ref[...]
    kk = k * kkw_ref[...]
    ss = _seg_sum(kk * kk, ones)
    kkn = kk * lax.rsqrt(ss + 1e-12)
    kko[...] = kkn
    bonus = jnp.zeros_like(v)
    for d, (wo, kkao, kdo) in enumerate(((w0o, kka0o, kd0o), (w1o, kka1o, kd1o))):
        pre = w0_ref[pl.ds(d, 1), :] + _dot_split(wd_t, wup_ref[d])
        w_log = -_softplus(-pre) - 0.5
        wo[...] = jnp.exp(-jnp.exp(w_log))
        a = _sigmoid(a0_ref[pl.ds(d, 1), :] + _dot_split(ad, aup_ref[d]))
        k_d = k * (1.0 + (a - 1.0) * ka_ref[...])
        kdo[...] = k_d
        kkao[...] = kkn * a
        bonus = bonus + _seg_sum(r * k_d * rk_ref[...], ones) * v
    bonuso[...] = bonus


def _inproj(x, mod_l, g, w_in_b, layer, w0, wup, a0, aup, k_k, k_a, r_k, ones_bd, lat):
    kvw = ATT_KV_HEADS * ATT_HD
    widths = (2 * BR_W, BR_W, kvw, kvw, BR_W) + (BR_W,) * 10
    full2 = lambda s: pl.BlockSpec(s, lambda i: (0, 0))
    return pl.pallas_call(
        functools.partial(_inproj_kernel, lat=lat),
        grid=(N_GRP // TM_IN,),
        in_specs=[
            pl.BlockSpec((TM_IN, D_MODEL), lambda i: (i, 0)),
            full2((8, 3 * D_MODEL)),
            full2((1, D_MODEL)),
            pl.BlockSpec((pl.Element(1), pl.Element(D_MODEL), pl.Element(IN_SMALL)), lambda i: (layer, 0, 0)),
            full2((2, BR_W)),
            pl.BlockSpec((2, DECAY_RANK, BR_W), lambda i: (0, 0, 0)),
            full2((2, BR_W)),
            pl.BlockSpec((2, ICL_RANK, BR_W), lambda i: (0, 0, 0)),
            full2((1, BR_W)), full2((1, BR_W)), full2((1, BR_W)),
            full2((BR_W, BR_W)),
        ],
        out_specs=[pl.BlockSpec((TM_IN, w), lambda i: (i, 0)) for w in widths],
        out_shape=[jax.ShapeDtypeStruct((N_GRP, w), F32) for w in widths],
        compiler_params=_cparams(("arbitrary",)),
        name="in_proj_lat" if lat else "in_proj_ctx",
    )(x, mod_l, g.reshape(1, D_MODEL), w_in_b, w0, wup, a0, aup,
      k_k.reshape(1, BR_W), k_a.reshape(1, BR_W), r_k.reshape(1, BR_W), ones_bd)


def _to_lanes(z0, z1, rep):
    lo = lax.broadcasted_iota(jnp.int32, (1, LANES), 1) < RW_HD
    parts = []
    for hp in range(RW_HEADS // 2):
        a = z0[:, hp * LANES:(hp + 1) * LANES]
        b = z1[:, hp * LANES:(hp + 1) * LANES]
        parts.append(jnp.where(lo, a, pltpu.roll(b, RW_HD, 1)))
        parts.append(jnp.where(lo, pltpu.roll(a, RW_HD, 1), b))
    return jnp.concatenate(parts * rep, axis=0).T


def _scan_kernel(*refs, nb, has_init):
    if has_init:
        s0_ref, refs = refs[0], refs[1:]
    (w0_ref, w1_ref, kka0_ref, kka1_ref, kd0_ref, kd1_ref, kkf_ref, kkb_ref, rf_ref, rb_ref, vf_ref, vb_ref,
     of_ref, ob_ref, st_ref, w_s, kk_s, kka_s, kd_s, r_s, v_s, o_s) = refs
    nk, vh, _ = st_ref.shape
    tc = w_s.shape[0]
    rep = RW_HD // vh
    nrec = LANES // rep
    lane = lax.broadcasted_iota(jnp.int32, (1, LANES), 1)
    grp = lane // nrec
    lo = lane < RW_HD

    def rows(f_ref, b_ref, j):
        return jnp.concatenate([f_ref[:, j, :], b_ref[:, tc - 1 - j, :]], axis=0)

    sources = ((w0_ref, w1_ref, w_s), (kkf_ref, kkb_ref, kk_s), (kka0_ref, kka1_ref, kka_s),
               (kd0_ref, kd1_ref, kd_s), (rf_ref, rb_ref, r_s))
    for j in range(0, tc, 2):
        for f_ref, b_ref, dst in sources:
            gt = _to_lanes(rows(f_ref, b_ref, j), rows(f_ref, b_ref, j + 1), rep)
            dst[j] = gt[0:RW_HD]
            dst[j + 1] = gt[RW_HD:2 * RW_HD]
        gt = _to_lanes(rows(vf_ref, vb_ref, j), rows(vf_ref, vb_ref, j + 1), rep)
        for s in range(2):
            blk = gt[s * RW_HD:(s + 1) * RW_HD]
            if rep > 1:
                blk = sum(jnp.where(grp == q, blk[q * vh:(q + 1) * vh], 0.0) for q in range(rep))
            v_s[j + s] = blk

    @pl.when(pl.program_id(0) == 0)
    def _():
        if has_init:
            st_ref[...] = s0_ref[...]
        else:
            st_ref[...] = jnp.zeros(st_ref.shape, F32)

    sa0 = jnp.zeros((vh, LANES), F32)
    for k in range(nk):
        sa0 = sa0 + st_ref[k] * kk_s[0, pl.ds(k, 1), :]

    def step(t, sa):
        tn = jnp.minimum(t + 1, tc - 1)
        vt = v_s[t]
        o = jnp.zeros((vh, LANES), F32)
        san = jnp.zeros((vh, LANES), F32)
        for k in range(nk):
            new = (st_ref[k] * w_s[t, pl.ds(k, 1), :] - kka_s[t, pl.ds(k, 1), :] * sa
                   + kd_s[t, pl.ds(k, 1), :] * vt)
            st_ref[k] = new
            o = o + new * r_s[t, pl.ds(k, 1), :]
            san = san + new * kk_s[tn, pl.ds(k, 1), :]
        o_s[t] = o
        return san

    lax.fori_loop(0, tc, step, sa0)

    for j in range(0, tc, 2):
        tiles = []
        for s in range(2):
            o = o_s[j + s]
            if rep == 1:
                tiles.append(o)
            else:
                tiles.extend(jnp.where(grp == q, o, 0.0) for q in range(rep))
        mt = jnp.concatenate(tiles, axis=0).T
        sm = mt[0:nrec]
        for q in range(1, rep):
            sm = sm + mt[q * nrec:(q + 1) * nrec]
        rs = pltpu.roll(sm, RW_HD, 1)
        for hp in range(RW_HEADS // 2):
            e0, e1 = (2 * hp) * 2 * nb, (2 * hp + 1) * 2 * nb
            cur = jnp.where(lo, sm[e0:e0 + 2 * nb], rs[e1:e1 + 2 * nb])
            nxt = jnp.where(lo, rs[e0:e0 + 2 * nb], sm[e1:e1 + 2 * nb])
            cs = slice(hp * LANES, (hp + 1) * LANES)
            of_ref[:, j, cs] = cur[0:nb]
            of_ref[:, j + 1, cs] = nxt[0:nb]
            ob_ref[:, tc - 1 - j, cs] = cur[nb:2 * nb]
            ob_ref[:, tc - 2 - j, cs] = nxt[nb:2 * nb]


def _scan(pre, s0, *, nb, t_len, name):
    w0, w1, kka0, kka1, kd0, kd1, kk, r, v = (p.reshape(nb, t_len, BR_W) for p in pre)
    n_t = t_len // SCAN_TC
    vh = RW_HD * nb * 2 * RW_HEADS // LANES
    fwd = pl.BlockSpec((nb, SCAN_TC, BR_W), lambda i: (0, i, 0))
    bwd = pl.BlockSpec((nb, SCAN_TC, BR_W), lambda i: (0, n_t - 1 - i, 0))
    sspec = pl.BlockSpec((RW_HD, vh, LANES), lambda i: (0, 0, 0))
    has_init = s0 is not None
    args = ((s0,) if has_init else ()) + (w0, w1, kka0, kka1, kd0, kd1, kk, kk, r, r, v, v)
    in_specs = ([sspec] if has_init else []) + [fwd, bwd] * 6
    kbuf = pltpu.VMEM((SCAN_TC, RW_HD, LANES), F32)
    vbuf = pltpu.VMEM((SCAN_TC, vh, LANES), F32)
    return pl.pallas_call(
        functools.partial(_scan_kernel, nb=nb, has_init=has_init),
        grid=(n_t,),
        in_specs=in_specs,
        out_specs=[pl.BlockSpec((nb, SCAN_TC, BR_W), lambda i: (0, i, 0)),
                   pl.BlockSpec((nb, SCAN_TC, BR_W), lambda i: (0, n_t - 1 - i, 0)),
                   sspec],
        out_shape=[jax.ShapeDtypeStruct((nb, t_len, BR_W), F32),
                   jax.ShapeDtypeStruct((nb, t_len, BR_W), F32),
                   jax.ShapeDtypeStruct((RW_HD, vh, LANES), F32)],
        scratch_shapes=[kbuf] * 5 + [vbuf] * 2,
        compiler_params=_cparams(("arbitrary",)),
        name=name,
    )(*args)


LAT_VS = 4
LAT_VH = RW_HD // LAT_VS


def _lat_state(s):
    s6 = s.reshape(DEC_BATCH, 2, RW_HEADS, LAT_VS, LAT_VH, RW_HD)
    return s6.transpose(5, 4, 3, 2, 1, 0).reshape(RW_HD, LAT_VH, LANES)


def _sink_softmax_pv(s, sk, vb):
    m = jnp.maximum(jnp.max(s, axis=-1, keepdims=True), sk)
    p = jnp.exp(s - m)
    den = jnp.sum(p, axis=-1, keepdims=True) + jnp.exp(sk - m)
    return jnp.dot(p.astype(BF16), vb, preferred_element_type=F32) / den


def _att_ctx_kernel(sink_ref, q_ref, k_ref, v_ref, o_ref):
    scale = ATT_HD ** -0.5
    q = q_ref[...]
    kb = k_ref[...].astype(BF16)
    vb = v_ref[...].astype(BF16)
    outs = []
    for hd in range(ATT_HEADS):
        kvh = hd // ATT_GROUP
        qh = q[:, hd * ATT_HD:(hd + 1) * ATT_HD].astype(BF16)
        kh = kb[:, kvh * ATT_HD:(kvh + 1) * ATT_HD]
        s = lax.dot_general(qh, kh, (((1,), (1,)), ((), ())), preferred_element_type=F32) * scale
        outs.append(_sink_softmax_pv(s, sink_ref[hd], vb[:, kvh * ATT_HD:(kvh + 1) * ATT_HD]))
    o_ref[...] = jnp.concatenate(outs, axis=1)


def _att_ctx(sink, cq, ck, cv):
    kvw = ATT_KV_HEADS * ATT_HD
    return pl.pallas_call(
        _att_ctx_kernel,
        grid=(BATCH,),
        in_specs=[
            pl.BlockSpec(memory_space=pltpu.SMEM),
            pl.BlockSpec((SEQ, BR_W), lambda b: (b, 0)),
            pl.BlockSpec((SEQ, kvw), lambda b: (b, 0)),
            pl.BlockSpec((SEQ, kvw), lambda b: (b, 0)),
        ],
        out_specs=pl.BlockSpec((SEQ, BR_W), lambda b: (b, 0)),
        out_shape=jax.ShapeDtypeStruct((N_CTX, BR_W), F32),
        compiler_params=_cparams(("arbitrary",)),
        name="att_ctx",
    )(sink, cq, ck, cv)


def _rope(x, cos, sin_signed):
    w = x.shape[1]
    lane = lax.broadcasted_iota(jnp.int32, (1, w), 1)
    first = (lane % 32) < 16
    swapped = jnp.where(first, pltpu.roll(x, w - 16, 1), pltpu.roll(x, 16, 1))
    return x * cos + swapped * sin_signed


def _att_lat_kernel(sink_ref, q_ref, k_ref, v_ref, ck_ref, cv_ref, cq_ref, sq_ref, ckk_ref, skk_ref,
                    o_ref, qs_ref, ks_ref, vs_ref):
    scale = ATT_HD ** -0.5
    nb = DEC_SEQ // CHUNK
    qs_ref[...] = _rope(q_ref[...], cq_ref[...], sq_ref[...]).astype(BF16)
    ks_ref[...] = _rope(k_ref[...], ckk_ref[...], skk_ref[...]).astype(BF16)
    vs_ref[...] = v_ref[...].astype(BF16)
    ckb = ck_ref[...].astype(BF16)
    cvb = cv_ref[...].astype(BF16)
    for n in range(nb):
        lo = max(n - 1, 0) * CHUNK
        hi = min(n + 2, nb) * CHUNK
        nloc = hi - lo
        ncol = nloc + PAST_LEN
        col = lax.broadcasted_iota(jnp.int32, (CHUNK, ncol), 1)
        row = lax.broadcasted_iota(jnp.int32, (CHUNK, ncol), 0)
        dist = jnp.abs((n * CHUNK + row) - (lo + col))
        valid = (col >= nloc) | (dist <= WINDOW)
        outs = []
        for hd in range(ATT_HEADS):
            kvh = hd // ATT_GROUP
            hs = slice(hd * ATT_HD, (hd + 1) * ATT_HD)
            ks = slice(kvh * ATT_HD, (kvh + 1) * ATT_HD)
            qh = qs_ref[n * CHUNK:(n + 1) * CHUNK, hs]
            kall = jnp.concatenate([ks_ref[lo:hi, ks], ckb[:, ks]], axis=0)
            vall = jnp.concatenate([vs_ref[lo:hi, ks], cvb[:, ks]], axis=0)
            s = lax.dot_general(qh, kall, (((1,), (1,)), ((), ())), preferred_element_type=F32) * scale
            s = jnp.where(valid, s, NEG_INF)
            outs.append(_sink_softmax_pv(s, sink_ref[hd], vall))
        o_ref[n * CHUNK:(n + 1) * CHUNK, :] = jnp.concatenate(outs, axis=1)


def _att_lat(sink, cq, ck, cv, cache_k, cache_v, layer, tables):
    kvw = ATT_KV_HEADS * ATT_HD
    cosq, sinq, cosk, sink_k = tables
    seq = lambda w: pl.BlockSpec((DEC_SEQ, w), lambda b: (b, 0))
    cache = pl.BlockSpec((None, None, PAST_LEN, kvw), lambda b: (b, layer, 0, 0))
    tab = lambda w: pl.BlockSpec((DEC_SEQ, w), lambda b: (0, 0))
    return pl.pallas_call(
        _att_lat_kernel,
        grid=(DEC_BATCH,),
        in_specs=[pl.BlockSpec(memory_space=pltpu.SMEM), seq(BR_W), seq(kvw), seq(kvw), cache, cache,
                  tab(BR_W), tab(BR_W), tab(kvw), tab(kvw)],
        out_specs=pl.BlockSpec((DEC_SEQ, BR_W), lambda b: (b, 0)),
        out_shape=jax.ShapeDtypeStruct((N_LAT, BR_W), F32),
        scratch_shapes=[pltpu.VMEM((DEC_SEQ, BR_W), BF16), pltpu.VMEM((DEC_SEQ, kvw), BF16),
                        pltpu.VMEM((DEC_SEQ, kvw), BF16)],
        compiler_params=_cparams(("arbitrary",)),
        name="att_lat",
    )(sink, cq, ck, cv, cache_k, cache_v, cosq, sinq, cosk, sink_k)


def _rope_tables():
    pos = np.arange(DEC_SEQ)
    lane = np.arange(ATT_HD)
    p = np.where(lane[None, :] < 32, (pos // GRID_W)[:, None], (pos % GRID_W)[:, None]).astype(np.float32)
    inv = (ROPE_BASE ** (-jnp.arange(0, 32, 2, dtype=F32) / 32))[lane % 16]
    ang = jnp.asarray(p) * inv[None, :]
    sign = jnp.asarray(np.where((lane % 32) < 16, -1.0, 1.0).astype(np.float32))
    cos, sin = jnp.cos(ang), jnp.sin(ang) * sign[None, :]
    return (jnp.tile(cos, (1, ATT_HEADS)), jnp.tile(sin, (1, ATT_HEADS)),
            jnp.tile(cos, (1, ATT_KV_HEADS)), jnp.tile(sin, (1, ATT_KV_HEADS)))


POOL_PAD = 8


def _pool_kernel(x_ref, w_ref, sc_ref, o_ref, a_ref, b_ref):
    n = x_ref.shape[0]
    ext = n + POOL_PAD
    x = x_ref[...]
    zeros = jnp.zeros((POOL_PAD, BR_W), F32)
    a_ref[0:POOL_PAD, :] = zeros
    b_ref[0:POOL_PAD, :] = zeros
    a_ref[POOL_PAD:POOL_PAD + n, :] = x
    a_ref[POOL_PAD + n:POOL_PAD + ext, :] = zeros
    grp = lax.broadcasted_iota(jnp.int32, (1, BR_W), 1) // POOL_GD
    acc = jnp.zeros((n, BR_W), F32)
    src_ref, dst_ref = a_ref, b_ref
    for m in range(len(POOL_SIZES)):
        back = 1 << m
        dst_ref[POOL_PAD:POOL_PAD + ext, :] = (src_ref[POOL_PAD:POOL_PAD + ext, :]
                                               + src_ref[POOL_PAD - back:POOL_PAD - back + ext, :])
        off = POOL_PAD + back - 1
        acc = jnp.where(grp == m, dst_ref[off:off + n, :], acc)
        src_ref, dst_ref = dst_ref, src_ref
    half = jnp.left_shift(1, grp)
    t = lax.broadcasted_iota(jnp.int32, (n, BR_W), 0)
    cnt = jnp.minimum(t + half, n) - jnp.maximum(t - half, 0)
    d = acc / cnt.astype(F32) - x
    o_ref[...] = jnp.dot(d.astype(BF16), w_ref[...], preferred_element_type=F32) * sc_ref[...]


def _pool(dp, w_bd, scale, n_seq, seq_len, name):
    return pl.pallas_call(
        _pool_kernel,
        grid=(n_seq,),
        in_specs=[pl.BlockSpec((seq_len, BR_W), lambda b: (b, 0)),
                  pl.BlockSpec((BR_W, BR_W), lambda b: (0, 0)),
                  pl.BlockSpec((1, BR_W), lambda b: (0, 0))],
        out_specs=pl.BlockSpec((seq_len, BR_W), lambda b: (b, 0)),
        out_shape=jax.ShapeDtypeStruct((n_seq * seq_len, BR_W), F32),
        scratch_shapes=[pltpu.VMEM((seq_len + 2 * POOL_PAD, BR_W), F32)] * 2,
        compiler_params=_cparams(("arbitrary",)),
        name=name,
    )(dp, w_bd, scale.reshape(1, BR_W))


def _merge_kernel(x_ref, mod_ref, g_ref, a_ref, bonus_ref, of_ref, ob_ref, yc_ref, yd_ref,
                  wz_ref, wmg_ref, ws_ref, bs_ref, lng_ref, lnb_ref, ones_ref, wup_ref, wo_ref, gf_ref,
                  o_ref, *, lat, final):
    i = pl.program_id(0)
    m = mod_ref[pl.ds(_mod_row(i, TM_MERGE, lat), 1), :]
    x = x_ref[...]
    hb = _norm_mod(x, g_ref[...], m).astype(BF16)

    a = a_ref[...]
    a_u, a_v = a[:, :BR_W], a[:, BR_W:]
    grp = lax.broadcasted_iota(jnp.int32, (1, BR_W), 1) // A_GD
    svs = []
    for c in range(TM_MERGE // CHUNK):
        vc = a_v[c * CHUNK:(c + 1) * CHUNK, :]
        sv = bs_ref[...]
        for g in range(A_GROUPS):
            sv = sv + jnp.dot(ws_ref[g], jnp.where(grp == g, vc, 0.0).astype(BF16), preferred_element_type=F32)
        svs.append(sv)
    y_a = a_u * jnp.concatenate(svs, axis=0)

    ones = ones_ref[...]
    osum = of_ref[...] + ob_ref[...]
    mu = _seg_sum(osum, ones) * (1.0 / RW_HD)
    dev = osum - mu
    var = _seg_sum(dev * dev, ones) * (1.0 / RW_HD)
    y_b = dev * lax.rsqrt(var + GN_EPS) * lng_ref[...] + lnb_ref[...] + bonus_ref[...]

    merged = jnp.zeros((TM_MERGE, D_MODEL), F32)
    for n, y in enumerate((y_a, y_b, yc_ref[...], yd_ref[...])):
        z = jnp.dot(hb, wz_ref[0, :, n * BR_W:(n + 1) * BR_W], preferred_element_type=F32)
        ys = y * (z * _sigmoid(z))
        up = jnp.dot(ys.astype(BF16), wup_ref[n], preferred_element_type=F32)
        mg = jnp.dot(hb, wmg_ref[0, :, n * D_MODEL:(n + 1) * D_MODEL], preferred_element_type=F32)
        merged = merged + _sigmoid(mg) * up
    gate = m[:, 2 * D_MODEL:]
    out = x + gate * jnp.dot(merged.astype(BF16), wo_ref[...], preferred_element_type=F32)
    if final:
        ms = jnp.mean(out * out, axis=-1, keepdims=True)
        out = out * lax.rsqrt(ms + NORM_EPS) * gf_ref[...]
    o_ref[...] = out


def _merge(x, mod_l, g, a, bonus, o_f, o_b, y_c, y_d, w_in_b, layer, ws, bs_tile, ln_g, ln_b, ones_bd, wup, wo,
           g_final, lat, final):
    row = lambda w: pl.BlockSpec((TM_MERGE, w), lambda i: (i, 0))
    full2 = lambda s: pl.BlockSpec(s, lambda i: (0, 0))
    z0, z1 = IN_SMALL, IN_SMALL + N_BRANCH * BR_W
    cols = lambda c0, w: pl.BlockSpec((pl.Element(1), pl.Element(D_MODEL), pl.Element(w)), lambda i: (layer, 0, c0))
    return pl.pallas_call(
        functools.partial(_merge_kernel, lat=lat, final=final),
        grid=(N_GRP // TM_MERGE,),
        in_specs=[row(D_MODEL), full2((8, 3 * D_MODEL)), full2((1, D_MODEL)), row(2 * BR_W),
                  row(BR_W), row(BR_W), row(BR_W), row(BR_W), row(BR_W),
                  cols(z0, N_BRANCH * BR_W), cols(z1, N_BRANCH * D_MODEL),
                  pl.BlockSpec((None, A_GROUPS, CHUNK, CHUNK), lambda i: (layer, 0, 0, 0)), full2((CHUNK, BR_W)),
                  full2((1, BR_W)), full2((1, BR_W)), full2((BR_W, BR_W)),
                  pl.BlockSpec((None, N_BRANCH, BR_W, D_MODEL), lambda i: (layer, 0, 0, 0)),
                  pl.BlockSpec((None, D_MODEL, D_MODEL), lambda i: (layer, 0, 0)), full2((1, D_MODEL))],
        out_specs=row(D_MODEL),
        out_shape=jax.ShapeDtypeStruct((N_GRP, D_MODEL), F32),
        compiler_params=_cparams(("arbitrary",)),
        name=("merge_lat" if lat else "merge_ctx") + ("_final" if final else ""),
    )(x, mod_l, g.reshape(1, D_MODEL), a, bonus, o_f, o_b, y_c, y_d, w_in_b, w_in_b, ws, bs_tile,
      ln_g.reshape(1, BR_W), ln_b.reshape(1, BR_W), ones_bd, wup, wo, g_final.reshape(1, D_MODEL))


def _block_diag(blocks):
    n, r, c = blocks.shape
    eye = jnp.eye(n, dtype=blocks.dtype)
    return (eye[:, None, :, None] * blocks[:, :, None, :]).reshape(n * r, n * c)


def kernel(x_prompt, x_sample, cache_k, cache_v, state_rwkv, c, c_ctx, w_mod, b_mod, g_norm, w_in, w_s, b_s,
           rw_w0, rw_w_up, rw_a0, rw_a_up, rw_k_k, rw_k_a, rw_r_k, rw_ln_g, rw_ln_b, att_sink, pool_w,
           pool_scale, w_up, w_o, g_final):
    xs = [x_prompt.reshape(N_CTX, D_MODEL), x_sample.reshape(N_LAT, D_MODEL)]
    cond = jnp.concatenate([c_ctx[None, :], c, jnp.zeros((8 - 1 - DEC_BATCH, D_MODEL), F32)], axis=0)
    mod = _modulation(cond, w_mod, b_mod)

    ones_bd = _block_diag(jnp.ones((RW_HEADS, RW_HD, RW_HD), BF16))
    tables = _rope_tables()
    kvw = ATT_KV_HEADS * ATT_HD
    cache_k4 = cache_k.reshape(DEC_BATCH, DEPTH, PAST_LEN, kvw)
    cache_v4 = cache_v.reshape(DEC_BATCH, DEPTH, PAST_LEN, kvw)
    w_in_b, w_s_b, w_up_b, w_o_b = (w.astype(BF16) for w in (w_in, w_s, w_up, w_o))

    new_k, new_v, new_s = [], [], []
    for l in range(DEPTH):
        pw = _block_diag(pool_w[l]).astype(BF16)
        bs_tile = jnp.repeat(b_s[l].T, A_GD, axis=1)
        final = l == DEPTH - 1
        for lat in (False, True):
            x = xs[lat]
            a, cq, ck, cv, dp, *pre, bonus = _inproj(
                x, mod[l], g_norm[l], w_in_b, l, rw_w0[l], rw_w_up[l], rw_a0[l], rw_a_up[l], rw_k_k[l], rw_k_a[l],
                rw_r_k[l], ones_bd, lat)
            r, v = pre[0], pre[1]
            pre = pre[2:] + [r, v]
            if lat:
                o_f, o_b, _ = _scan(pre, _lat_state(state_rwkv[:, l]), nb=DEC_BATCH, t_len=DEC_SEQ, name="scan_lat")
                y_c = _att_lat(att_sink[l], cq, ck, cv, cache_k4, cache_v4, l, tables)
                y_d = _pool(dp, pw, pool_scale[l], DEC_BATCH, DEC_SEQ, "pool_lat")
            else:
                o_f, o_b, s_fin = _scan(pre, None, nb=BATCH, t_len=SEQ, name="scan_ctx")
                y_c = _att_ctx(att_sink[l], cq, ck, cv)
                y_d = _pool(dp, pw, pool_scale[l], BATCH, SEQ, "pool_ctx")
                new_k.append(ck.reshape(BATCH, SEQ, ATT_KV_HEADS, ATT_HD))
                new_v.append(cv.reshape(BATCH, SEQ, ATT_KV_HEADS, ATT_HD))
                new_s.append(s_fin.reshape(RW_HD, RW_HD, RW_HEADS, 2, BATCH).transpose(4, 3, 2, 1, 0))

            xs[lat] = _merge(x, mod[l], g_norm[l], a, bonus, o_f.reshape(N_GRP, BR_W), o_b.reshape(N_GRP, BR_W),
                             y_c, y_d, w_in_b, l, w_s_b, bs_tile, rw_ln_g[l], rw_ln_b[l], ones_bd, w_up_b, w_o_b,
                             g_final, lat, final)

    y_prompt = xs[0].reshape(BATCH, SEQ, D_MODEL)
    y_sample = xs[1].reshape(DEC_BATCH, DEC_SEQ, D_MODEL)
    return (y_prompt, y_sample, jnp.stack(new_k, axis=1), jnp.stack(new_v, axis=1), jnp.stack(new_s, axis=1))
```

```python
import functools

import numpy as np
import jax
import jax.numpy as jnp
from jax import lax
from jax.experimental import pallas as pl
from jax.experimental.pallas import tpu as pltpu

D_MODEL = 1024
BATCH = 16
SEQ = 256
DEPTH = 2
DEC_BATCH = 4
DEC_SEQ = 1024
PAST_LEN = 512
GRID_W = 64
N_BRANCH = 4
BR_W = D_MODEL // 4
CHUNK = 128
A_GROUPS = 4
A_GD = BR_W // A_GROUPS
RW_HD = 64
RW_HEADS = BR_W // RW_HD
DECAY_RANK = 64
ICL_RANK = 64
ATT_HD = 64
ATT_HEADS = BR_W // ATT_HD
ATT_KV_HEADS = 2
ATT_GROUP = ATT_HEADS // ATT_KV_HEADS
WINDOW = 128
ROPE_BASE = 10000.0
POOL_SIZES = (2, 4, 8, 16)
POOL_GD = BR_W // len(POOL_SIZES)
NORM_EPS = 1e-6
GN_EPS = 64e-5
NEG_INF = -1e30

N_CTX = BATCH * SEQ
N_LAT = DEC_BATCH * DEC_SEQ
N_GRP = N_CTX
assert N_LAT == N_GRP
IN_SMALL = 2176

F32 = jnp.float32
BF16 = jnp.bfloat16

V7X_VMEM_LIMIT = 56 * 1024 * 1024
LANES = 128

TM_IN = 512
TM_MERGE = 512
SCAN_TC = 32


def _cparams(sem):
    return pltpu.CompilerParams(dimension_semantics=sem, vmem_limit_bytes=V7X_VMEM_LIMIT)


def _mod_row(i, tm, lat):
    return 1 + i // (DEC_SEQ // tm) if lat else 0


def _norm_mod(x, g, m):
    ms = jnp.mean(x * x, axis=-1, keepdims=True)
    y = x * lax.rsqrt(ms + NORM_EPS) * g
    shift = m[:, :D_MODEL]
    scale = m[:, D_MODEL:2 * D_MODEL]
    return y * (1.0 + scale) + shift


def _sigmoid(x):
    return 1.0 / (1.0 + jnp.exp(-x))


def _split(x):
    hi = x.astype(BF16)
    return hi, (x - hi.astype(F32)).astype(BF16)


def _seg_sum(x, ones_bd):
    hi, lo = _split(x)
    return (jnp.dot(hi, ones_bd, preferred_element_type=F32)
            + jnp.dot(lo, ones_bd, preferred_element_type=F32))


def _dot_split(x, w):
    xh, xl = _split(x)
    wh, wl = _split(w)
    lhs = jnp.concatenate([xh, xl, xh], axis=1)
    rhs = jnp.concatenate([wh, wh, wl], axis=0)
    return jnp.dot(lhs, rhs, preferred_element_type=F32)


def _mod_kernel(c_ref, w_ref, b_ref, o_ref):
    cnd = c_ref[...]
    s = cnd * _sigmoid(cnd)
    o_ref[...] = jnp.dot(s.astype(BF16), w_ref[...].astype(BF16), preferred_element_type=F32) + b_ref[...]


def _modulation(cond, w_mod, b_mod):
    nt = 3 * D_MODEL // 1024
    return pl.pallas_call(
        _mod_kernel,
        grid=(DEPTH, nt),
        in_specs=[
            pl.BlockSpec((8, D_MODEL), lambda l, j: (0, 0)),
            pl.BlockSpec((None, D_MODEL, 1024), lambda l, j: (l, 0, j)),
            pl.BlockSpec((None, 1, 1024), lambda l, j: (l, 0, j)),
        ],
        out_specs=pl.BlockSpec((None, 8, 1024), lambda l, j: (l, 0, j)),
        out_shape=jax.ShapeDtypeStruct((DEPTH, 8, 3 * D_MODEL), F32),
        compiler_params=_cparams(("arbitrary", "arbitrary")),
        name="modulation",
    )(cond, w_mod, b_mod.reshape(DEPTH, 1, 3 * D_MODEL))


def _softplus(x):
    return jnp.maximum(x, 0.0) + jnp.log1p(jnp.exp(-jnp.abs(x)))


def _inproj_kernel(x_ref, mod_ref, g_ref, w_ref, w0_ref, wup_ref, a0_ref, aup_ref, kkw_ref, ka_ref, rk_ref, ones_ref,
                   a_ref, q_ref, ck_ref, cv_ref, d_ref, r_ref, v_ref,
                   w0o, w1o, kka0o, kka1o, kd0o, kd1o, kko, bonuso, *, lat):
    i = pl.program_id(0)
    m = mod_ref[pl.ds(_mod_row(i, TM_IN, lat), 1), :]
    h = _norm_mod(x_ref[...], g_ref[...], m)
    p = jnp.dot(h.astype(BF16), w_ref[0], preferred_element_type=F32)
    a_ref[...] = p[:, 0:512]
    q_ref[...] = p[:, 1408:1664]
    ck_ref[...] = p[:, 1664:1792]
    cv_ref[...] = p[:, 1792:1920]
    d_ref[...] = p[:, 1920:2176]

    r = p[:, 512:768]
    k = p[:, 768:1024]
    v = p[:, 1024:1280]
    wd_t = jnp.tanh(p[:, 1280:1344])
    ad = p[:, 1344:1408]
    r_ref[...] = r
    v_ref[...] = v
    ones = ones_ref[...]
    kk = k * kkw_ref[...]
    ss = _seg_sum(kk * kk, ones)
    kkn = kk * lax.rsqrt(ss + 1e-12)
    kko[...] = kkn
    bonus = jnp.zeros_like(v)
    for d, (wo, kkao, kdo) in enumerate(((w0o, kka0o, kd0o), (w1o, kka1o, kd1o))):
        pre = w0_ref[pl.ds(d, 1), :] + _dot_split(wd_t, wup_ref[d])
        w_log = -_softplus(-pre) - 0.5
        wo[...] = jnp.exp(-jnp.exp(w_log))
        a = _sigmoid(a0_ref[pl.ds(d, 1), :] + _dot_split(ad, aup_ref[d]))
        k_d = k * (1.0 + (a - 1.0) * ka_ref[...])
        kdo[...] = k_d
        kkao[...] = kkn * a
        bonus = bonus + _seg_sum(r * k_d * rk_ref[...], ones) * v
    bonuso[...] = bonus


def _inproj(x, mod_l, g, w_in_b, layer, w0, wup, a0, aup, k_k, k_a, r_k, ones_bd, lat):
    kvw = ATT_KV_HEADS * ATT_HD
    widths = (2 * BR_W, BR_W, kvw, kvw, BR_W) + (BR_W,) * 10
    full2 = lambda s: pl.BlockSpec(s, lambda i: (0, 0))
    return pl.pallas_call(
        functools.partial(_inproj_kernel, lat=lat),
        grid=(N_GRP // TM_IN,),
        in_specs=[
            pl.BlockSpec((TM_IN, D_MODEL), lambda i: (i, 0)),
            full2((8, 3 * D_MODEL)),
            full2((1, D_MODEL)),
            pl.BlockSpec((pl.Element(1), pl.Element(D_MODEL), pl.Element(IN_SMALL)), lambda i: (layer, 0, 0)),
            full2((2, BR_W)),
            pl.BlockSpec((2, DECAY_RANK, BR_W), lambda i: (0, 0, 0)),
            full2((2, BR_W)),
            pl.BlockSpec((2, ICL_RANK, BR_W), lambda i: (0, 0, 0)),
            full2((1, BR_W)), full2((1, BR_W)), full2((1, BR_W)),
            full2((BR_W, BR_W)),
        ],
        out_specs=[pl.BlockSpec((TM_IN, w), lambda i: (i, 0)) for w in widths],
        out_shape=[jax.ShapeDtypeStruct((N_GRP, w), F32) for w in widths],
        compiler_params=_cparams(("arbitrary",)),
        name="in_proj_lat" if lat else "in_proj_ctx",
    )(x, mod_l, g.reshape(1, D_MODEL), w_in_b, w0, wup, a0, aup,
      k_k.reshape(1, BR_W), k_a.reshape(1, BR_W), r_k.reshape(1, BR_W), ones_bd)


def _to_lanes(z0, z1, rep):
    lo = lax.broadcasted_iota(jnp.int32, (1, LANES), 1) < RW_HD
    parts = []
    for hp in range(RW_HEADS // 2):
        a = z0[:, hp * LANES:(hp + 1) * LANES]
        b = z1[:, hp * LANES:(hp + 1) * LANES]
        parts.append(jnp.where(lo, a, pltpu.roll(b, RW_HD, 1)))
        parts.append(jnp.where(lo, pltpu.roll(a, RW_HD, 1), b))
    return jnp.concatenate(parts * rep, axis=0).T


def _scan_kernel(*refs, nb, has_init):
    if has_init:
        s0_ref, refs = refs[0], refs[1:]
    (w0_ref, w1_ref, kka0_ref, kka1_ref, kd0_ref, kd1_ref, kkf_ref, kkb_ref, rf_ref, rb_ref, vf_ref, vb_ref,
     of_ref, ob_ref, st_ref, w_s, kk_s, kka_s, kd_s, r_s, v_s, o_s) = refs
    nk, vh, _ = st_ref.shape
    tc = w_s.shape[0]
    rep = RW_HD // vh
    nrec = LANES // rep
    lane = lax.broadcasted_iota(jnp.int32, (1, LANES), 1)
    grp = lane // nrec
    lo = lane < RW_HD

    def rows(f_ref, b_ref, j):
        return jnp.concatenate([f_ref[:, j, :], b_ref[:, tc - 1 - j, :]], axis=0)

    sources = ((w0_ref, w1_ref, w_s), (kkf_ref, kkb_ref, kk_s), (kka0_ref, kka1_ref, kka_s),
               (kd0_ref, kd1_ref, kd_s), (rf_ref, rb_ref, r_s))
    for j in range(0, tc, 2):
        for f_ref, b_ref, dst in sources:
            gt = _to_lanes(rows(f_ref, b_ref, j), rows(f_ref, b_ref, j + 1), rep)
            dst[j] = gt[0:RW_HD]
            dst[j + 1] = gt[RW_HD:2 * RW_HD]
        gt = _to_lanes(rows(vf_ref, vb_ref, j), rows(vf_ref, vb_ref, j + 1), rep)
        for s in range(2):
            blk = gt[s * RW_HD:(s + 1) * RW_HD]
            if rep > 1:
                blk = sum(jnp.where(grp == q, blk[q * vh:(q + 1) * vh], 0.0) for q in range(rep))
            v_s[j + s] = blk

    @pl.when(pl.program_id(0) == 0)
    def _():
        if has_init:
            st_ref[...] = s0_ref[...]
        else:
            st_ref[...] = jnp.zeros(st_ref.shape, F32)

    sa0 = jnp.zeros((vh, LANES), F32)
    for k in range(nk):
        sa0 = sa0 + st_ref[k] * kk_s[0, pl.ds(k, 1), :]

    def step(t, sa):
        tn = jnp.minimum(t + 1, tc - 1)
        vt = v_s[t]
        o = jnp.zeros((vh, LANES), F32)
        san = jnp.zeros((vh, LANES), F32)
        for k in range(nk):
            new = (st_ref[k] * w_s[t, pl.ds(k, 1), :] - kka_s[t, pl.ds(k, 1), :] * sa
                   + kd_s[t, pl.ds(k, 1), :] * vt)
            st_ref[k] = new
            o = o + new * r_s[t, pl.ds(k, 1), :]
            san = san + new * kk_s[tn, pl.ds(k, 1), :]
        o_s[t] = o
        return san

    lax.fori_loop(0, tc, step, sa0)

    for j in range(0, tc, 2):
        tiles = []
        for s in range(2):
            o = o_s[j + s]
            if rep == 1:
                tiles.append(o)
            else:
                tiles.extend(jnp.where(grp == q, o, 0.0) for q in range(rep))
        mt = jnp.concatenate(tiles, axis=0).T
        sm = mt[0:nrec]
        for q in range(1, rep):
            sm = sm + mt[q * nrec:(q + 1) * nrec]
        rs = pltpu.roll(sm, RW_HD, 1)
        for hp in range(RW_HEADS // 2):
            e0, e1 = (2 * hp) * 2 * nb, (2 * hp + 1) * 2 * nb
            cur = jnp.where(lo, sm[e0:e0 + 2 * nb], rs[e1:e1 + 2 * nb])
            nxt = jnp.where(lo, rs[e0:e0 + 2 * nb], sm[e1:e1 + 2 * nb])
            cs = slice(hp * LANES, (hp + 1) * LANES)
            of_ref[:, j, cs] = cur[0:nb]
            of_ref[:, j + 1, cs] = nxt[0:nb]
            ob_ref[:, tc - 1 - j, cs] = cur[nb:2 * nb]
            ob_ref[:, tc - 2 - j, cs] = nxt[nb:2 * nb]


def _scan(pre, s0, *, nb, t_len, name):
    w0, w1, kka0, kka1, kd0, kd1, kk, r, v = (p.reshape(nb, t_len, BR_W) for p in pre)
    n_t = t_len // SCAN_TC
    vh = RW_HD * nb * 2 * RW_HEADS // LANES
    fwd = pl.BlockSpec((nb, SCAN_TC, BR_W), lambda i: (0, i, 0))
    bwd = pl.BlockSpec((nb, SCAN_TC, BR_W), lambda i: (0, n_t - 1 - i, 0))
    sspec = pl.BlockSpec((RW_HD, vh, LANES), lambda i: (0, 0, 0))
    has_init = s0 is not None
    args = ((s0,) if has_init else ()) + (w0, w1, kka0, kka1, kd0, kd1, kk, kk, r, r, v, v)
    in_specs = ([sspec] if has_init else []) + [fwd, bwd] * 6
    kbuf = pltpu.VMEM((SCAN_TC, RW_HD, LANES), F32)
    vbuf = pltpu.VMEM((SCAN_TC, vh, LANES), F32)
    return pl.pallas_call(
        functools.partial(_scan_kernel, nb=nb, has_init=has_init),
        grid=(n_t,),
        in_specs=in_specs,
        out_specs=[pl.BlockSpec((nb, SCAN_TC, BR_W), lambda i: (0, i, 0)),
                   pl.BlockSpec((nb, SCAN_TC, BR_W), lambda i: (0, n_t - 1 - i, 0)),
                   sspec],
        out_shape=[jax.ShapeDtypeStruct((nb, t_len, BR_W), F32),
                   jax.ShapeDtypeStruct((nb, t_len, BR_W), F32),
                   jax.ShapeDtypeStruct((RW_HD, vh, LANES), F32)],
        scratch_shapes=[kbuf] * 5 + [vbuf] * 2,
        compiler_params=_cparams(("arbitrary",)),
        name=name,
    )(*args)


LAT_VS = 4
LAT_VH = RW_HD // LAT_VS


def _lat_state(s):
    s6 = s.reshape(DEC_BATCH, 2, RW_HEADS, LAT_VS, LAT_VH, RW_HD)
    return s6.transpose(5, 4, 3, 2, 1, 0).reshape(RW_HD, LAT_VH, LANES)


def _sink_softmax_pv(s, sk, vb):
    m = jnp.maximum(jnp.max(s, axis=-1, keepdims=True), sk)
    p = jnp.exp(s - m)
    den = jnp.sum(p, axis=-1, keepdims=True) + jnp.exp(sk - m)
    return jnp.dot(p.astype(BF16), vb, preferred_element_type=F32) / den


def _att_ctx_kernel(sink_ref, q_ref, k_ref, v_ref, o_ref):
    scale = ATT_HD ** -0.5
    q = q_ref[...]
    kb = k_ref[...].astype(BF16)
    vb = v_ref[...].astype(BF16)
    first = lax.broadcasted_iota(jnp.int32, (ATT_GROUP * SEQ, 1), 0) < SEQ
    outs = []
    for kvh in range(ATT_KV_HEADS):
        ks = slice(kvh * ATT_HD, (kvh + 1) * ATT_HD)
        q2 = jnp.concatenate([q[:, hd * ATT_HD:(hd + 1) * ATT_HD]
                              for hd in (ATT_GROUP * kvh, ATT_GROUP * kvh + 1)], axis=0).astype(BF16)
        s = lax.dot_general(q2, kb[:, ks], (((1,), (1,)), ((), ())), preferred_element_type=F32) * scale
        sk = jnp.where(first, sink_ref[ATT_GROUP * kvh], sink_ref[ATT_GROUP * kvh + 1])
        o2 = _sink_softmax_pv(s, sk, vb[:, ks])
        outs += [o2[:SEQ], o2[SEQ:]]
    o_ref[...] = jnp.concatenate(outs, axis=1)


def _att_ctx(sink, cq, ck, cv):
    kvw = ATT_KV_HEADS * ATT_HD
    return pl.pallas_call(
        _att_ctx_kernel,
        grid=(BATCH,),
        in_specs=[
            pl.BlockSpec(memory_space=pltpu.SMEM),
            pl.BlockSpec((SEQ, BR_W), lambda b: (b, 0)),
            pl.BlockSpec((SEQ, kvw), lambda b: (b, 0)),
            pl.BlockSpec((SEQ, kvw), lambda b: (b, 0)),
        ],
        out_specs=pl.BlockSpec((SEQ, BR_W), lambda b: (b, 0)),
        out_shape=jax.ShapeDtypeStruct((N_CTX, BR_W), F32),
        compiler_params=_cparams(("arbitrary",)),
        name="att_ctx",
    )(sink, cq, ck, cv)


def _rope(x, cos, sin_signed):
    w = x.shape[1]
    lane = lax.broadcasted_iota(jnp.int32, (1, w), 1)
    first = (lane % 32) < 16
    swapped = jnp.where(first, pltpu.roll(x, w - 16, 1), pltpu.roll(x, 16, 1))
    return x * cos + swapped * sin_signed


def _att_lat_kernel(sink_ref, q_ref, k_ref, v_ref, ck_ref, cv_ref, cq_ref, sq_ref, ckk_ref, skk_ref,
                    o_ref, qs_ref, ks_ref, vs_ref):
    scale = ATT_HD ** -0.5
    nb = DEC_SEQ // CHUNK
    qs_ref[...] = _rope(q_ref[...], cq_ref[...], sq_ref[...]).astype(BF16)
    ks_ref[...] = _rope(k_ref[...], ckk_ref[...], skk_ref[...]).astype(BF16)
    vs_ref[...] = v_ref[...].astype(BF16)
    ckb = ck_ref[...].astype(BF16)
    cvb = cv_ref[...].astype(BF16)
    for n in range(nb):
        lo = max(n - 1, 0) * CHUNK
        hi = min(n + 2, nb) * CHUNK
        nloc = hi - lo
        ncol = nloc + PAST_LEN
        col = lax.broadcasted_iota(jnp.int32, (CHUNK, ncol), 1)
        row = lax.broadcasted_iota(jnp.int32, (CHUNK, ncol), 0)
        dist = jnp.abs((n * CHUNK + row) - (lo + col))
        valid = (col >= nloc) | (dist <= WINDOW)
        valid2 = jnp.concatenate([valid] * ATT_GROUP, axis=0)
        first = lax.broadcasted_iota(jnp.int32, (ATT_GROUP * CHUNK, 1), 0) < CHUNK
        outs = []
        for kvh in range(ATT_KV_HEADS):
            ks = slice(kvh * ATT_HD, (kvh + 1) * ATT_HD)
            q2 = jnp.concatenate([qs_ref[n * CHUNK:(n + 1) * CHUNK, hd * ATT_HD:(hd + 1) * ATT_HD]
                                  for hd in (ATT_GROUP * kvh, ATT_GROUP * kvh + 1)], axis=0)
            kall = jnp.concatenate([ks_ref[lo:hi, ks], ckb[:, ks]], axis=0)
            vall = jnp.concatenate([vs_ref[lo:hi, ks], cvb[:, ks]], axis=0)
            s = lax.dot_general(q2, kall, (((1,), (1,)), ((), ())), preferred_element_type=F32) * scale
            s = jnp.where(valid2, s, NEG_INF)
            sk = jnp.where(first, sink_ref[ATT_GROUP * kvh], sink_ref[ATT_GROUP * kvh + 1])
            o2 = _sink_softmax_pv(s, sk, vall)
            outs += [o2[:CHUNK], o2[CHUNK:]]
        o_ref[n * CHUNK:(n + 1) * CHUNK, :] = jnp.concatenate(outs, axis=1)


def _att_lat(sink, cq, ck, cv, cache_k, cache_v, layer, tables):
    kvw = ATT_KV_HEADS * ATT_HD
    cosq, sinq, cosk, sink_k = tables
    seq = lambda w: pl.BlockSpec((DEC_SEQ, w), lambda b: (b, 0))
    cache = pl.BlockSpec((None, None, PAST_LEN, kvw), lambda b: (b, layer, 0, 0))
    tab = lambda w: pl.BlockSpec((DEC_SEQ, w), lambda b: (0, 0))
    return pl.pallas_call(
        _att_lat_kernel,
        grid=(DEC_BATCH,),
        in_specs=[pl.BlockSpec(memory_space=pltpu.SMEM), seq(BR_W), seq(kvw), seq(kvw), cache, cache,
                  tab(BR_W), tab(BR_W), tab(kvw), tab(kvw)],
        out_specs=pl.BlockSpec((DEC_SEQ, BR_W), lambda b: (b, 0)),
        out_shape=jax.ShapeDtypeStruct((N_LAT, BR_W), F32),
        scratch_shapes=[pltpu.VMEM((DEC_SEQ, BR_W), BF16), pltpu.VMEM((DEC_SEQ, kvw), BF16),
                        pltpu.VMEM((DEC_SEQ, kvw), BF16)],
        compiler_params=_cparams(("arbitrary",)),
        name="att_lat",
    )(sink, cq, ck, cv, cache_k, cache_v, cosq, sinq, cosk, sink_k)


def _rope_tables():
    pos = np.arange(DEC_SEQ)
    lane = np.arange(ATT_HD)
    p = np.where(lane[None, :] < 32, (pos // GRID_W)[:, None], (pos % GRID_W)[:, None]).astype(np.float32)
    inv = (ROPE_BASE ** (-jnp.arange(0, 32, 2, dtype=F32) / 32))[lane % 16]
    ang = jnp.asarray(p) * inv[None, :]
    sign = jnp.asarray(np.where((lane % 32) < 16, -1.0, 1.0).astype(np.float32))
    cos, sin = jnp.cos(ang), jnp.sin(ang) * sign[None, :]
    return (jnp.tile(cos, (1, ATT_HEADS)), jnp.tile(sin, (1, ATT_HEADS)),
            jnp.tile(cos, (1, ATT_KV_HEADS)), jnp.tile(sin, (1, ATT_KV_HEADS)))


POOL_PAD = 8


def _pool_kernel(x_ref, w_ref, sc_ref, ic_ref, o_ref, a_ref, b_ref):
    n = x_ref.shape[0]
    ext = n + POOL_PAD
    x = x_ref[...]
    zeros = jnp.zeros((POOL_PAD, BR_W), F32)
    a_ref[0:POOL_PAD, :] = zeros
    b_ref[0:POOL_PAD, :] = zeros
    a_ref[POOL_PAD:POOL_PAD + n, :] = x
    a_ref[POOL_PAD + n:POOL_PAD + ext, :] = zeros
    grp = lax.broadcasted_iota(jnp.int32, (1, BR_W), 1) // POOL_GD
    acc = jnp.zeros((n, BR_W), F32)
    src_ref, dst_ref = a_ref, b_ref
    for m in range(len(POOL_SIZES)):
        back = 1 << m
        dst_ref[POOL_PAD:POOL_PAD + ext, :] = (src_ref[POOL_PAD:POOL_PAD + ext, :]
                                               + src_ref[POOL_PAD - back:POOL_PAD - back + ext, :])
        off = POOL_PAD + back - 1
        acc = jnp.where(grp == m, dst_ref[off:off + n, :], acc)
        src_ref, dst_ref = dst_ref, src_ref
    d = acc * ic_ref[...] - x
    o_ref[...] = jnp.dot(d.astype(BF16), w_ref[...], preferred_element_type=F32) * sc_ref[...]


def _pool_inv_count(seq_len):
    t = np.arange(seq_len)[:, None]
    half = (np.asarray(POOL_SIZES) // 2)[np.arange(BR_W) // POOL_GD][None, :]
    cnt = np.minimum(t + half, seq_len) - np.maximum(t - half, 0)
    return jnp.asarray(1.0 / cnt, F32)


def _pool(dp, w_bd, scale, n_seq, seq_len, name):
    return pl.pallas_call(
        _pool_kernel,
        grid=(n_seq,),
        in_specs=[pl.BlockSpec((seq_len, BR_W), lambda b: (b, 0)),
                  pl.BlockSpec((BR_W, BR_W), lambda b: (0, 0)),
                  pl.BlockSpec((1, BR_W), lambda b: (0, 0)),
                  pl.BlockSpec((seq_len, BR_W), lambda b: (0, 0))],
        out_specs=pl.BlockSpec((seq_len, BR_W), lambda b: (b, 0)),
        out_shape=jax.ShapeDtypeStruct((n_seq * seq_len, BR_W), F32),
        scratch_shapes=[pltpu.VMEM((seq_len + 2 * POOL_PAD, BR_W), F32)] * 2,
        compiler_params=_cparams(("arbitrary",)),
        name=name,
    )(dp, w_bd, scale.reshape(1, BR_W), _pool_inv_count(seq_len))


def _merge_kernel(x_ref, mod_ref, g_ref, a_ref, bonus_ref, of_ref, ob_ref, yc_ref, yd_ref,
                  wz_ref, wmg_ref, ws_ref, bs_ref, lng_ref, lnb_ref, ones_ref, wup_ref, wo_ref, gf_ref,
                  o_ref, *, lat, final):
    i = pl.program_id(0)
    m = mod_ref[pl.ds(_mod_row(i, TM_MERGE, lat), 1), :]
    x = x_ref[...]
    hb = _norm_mod(x, g_ref[...], m).astype(BF16)

    a = a_ref[...]
    a_u, a_v = a[:, :BR_W], a[:, BR_W:]
    grp = lax.broadcasted_iota(jnp.int32, (1, BR_W), 1) // A_GD
    svs = []
    for c in range(TM_MERGE // CHUNK):
        vc = a_v[c * CHUNK:(c + 1) * CHUNK, :]
        sv = bs_ref[...]
        for g in range(A_GROUPS):
            sv = sv + jnp.dot(ws_ref[g], jnp.where(grp == g, vc, 0.0).astype(BF16), preferred_element_type=F32)
        svs.append(sv)
    y_a = a_u * jnp.concatenate(svs, axis=0)

    ones = ones_ref[...]
    osum = of_ref[...] + ob_ref[...]
    mu = _seg_sum(osum, ones) * (1.0 / RW_HD)
    dev = osum - mu
    var = _seg_sum(dev * dev, ones) * (1.0 / RW_HD)
    y_b = dev * lax.rsqrt(var + GN_EPS) * lng_ref[...] + lnb_ref[...] + bonus_ref[...]

    merged = jnp.zeros((TM_MERGE, D_MODEL), F32)
    for n, y in enumerate((y_a, y_b, yc_ref[...], yd_ref[...])):
        z = jnp.dot(hb, wz_ref[0, :, n * BR_W:(n + 1) * BR_W], preferred_element_type=F32)
        ys = y * (z * _sigmoid(z))
        up = jnp.dot(ys.astype(BF16), wup_ref[n], preferred_element_type=F32)
        mg = jnp.dot(hb, wmg_ref[0, :, n * D_MODEL:(n + 1) * D_MODEL], preferred_element_type=F32)
        merged = merged + _sigmoid(mg) * up
    gate = m[:, 2 * D_MODEL:]
    out = x + gate * jnp.dot(merged.astype(BF16), wo_ref[...], preferred_element_type=F32)
    if final:
        ms = jnp.mean(out * out, axis=-1, keepdims=True)
        out = out * lax.rsqrt(ms + NORM_EPS) * gf_ref[...]
    o_ref[...] = out


def _merge(x, mod_l, g, a, bonus, o_f, o_b, y_c, y_d, w_in_b, layer, ws, bs_tile, ln_g, ln_b, ones_bd, wup, wo,
           g_final, lat, final):
    row = lambda w: pl.BlockSpec((TM_MERGE, w), lambda i: (i, 0))
    full2 = lambda s: pl.BlockSpec(s, lambda i: (0, 0))
    z0, z1 = IN_SMALL, IN_SMALL + N_BRANCH * BR_W
    cols = lambda c0, w: pl.BlockSpec((pl.Element(1), pl.Element(D_MODEL), pl.Element(w)), lambda i: (layer, 0, c0))
    return pl.pallas_call(
        functools.partial(_merge_kernel, lat=lat, final=final),
        grid=(N_GRP // TM_MERGE,),
        in_specs=[row(D_MODEL), full2((8, 3 * D_MODEL)), full2((1, D_MODEL)), row(2 * BR_W),
                  row(BR_W), row(BR_W), row(BR_W), row(BR_W), row(BR_W),
                  cols(z0, N_BRANCH * BR_W), cols(z1, N_BRANCH * D_MODEL),
                  pl.BlockSpec((None, A_GROUPS, CHUNK, CHUNK), lambda i: (layer, 0, 0, 0)), full2((CHUNK, BR_W)),
                  full2((1, BR_W)), full2((1, BR_W)), full2((BR_W, BR_W)),
                  pl.BlockSpec((None, N_BRANCH, BR_W, D_MODEL), lambda i: (layer, 0, 0, 0)),
                  pl.BlockSpec((None, D_MODEL, D_MODEL), lambda i: (layer, 0, 0)), full2((1, D_MODEL))],
        out_specs=row(D_MODEL),
        out_shape=jax.ShapeDtypeStruct((N_GRP, D_MODEL), F32),
        compiler_params=_cparams(("arbitrary",)),
        name=("merge_lat" if lat else "merge_ctx") + ("_final" if final else ""),
    )(x, mod_l, g.reshape(1, D_MODEL), a, bonus, o_f, o_b, y_c, y_d, w_in_b, w_in_b, ws, bs_tile,
      ln_g.reshape(1, BR_W), ln_b.reshape(1, BR_W), ones_bd, wup, wo, g_final.reshape(1, D_MODEL))


def _block_diag(blocks):
    n, r, c = blocks.shape
    eye = jnp.eye(n, dtype=blocks.dtype)
    return (eye[:, None, :, None] * blocks[:, :, None, :]).reshape(n * r, n * c)


def kernel(x_prompt, x_sample, cache_k, cache_v, state_rwkv, c, c_ctx, w_mod, b_mod, g_norm, w_in, w_s, b_s,
           rw_w0, rw_w_up, rw_a0, rw_a_up, rw_k_k, rw_k_a, rw_r_k, rw_ln_g, rw_ln_b, att_sink, pool_w,
           pool_scale, w_up, w_o, g_final):
    xs = [x_prompt.reshape(N_CTX, D_MODEL), x_sample.reshape(N_LAT, D_MODEL)]
    cond = jnp.concatenate([c_ctx[None, :], c, jnp.zeros((8 - 1 - DEC_BATCH, D_MODEL), F32)], axis=0)
    mod = _modulation(cond, w_mod, b_mod)

    ones_bd = _block_diag(jnp.ones((RW_HEADS, RW_HD, RW_HD), BF16))
    tables = _rope_tables()
    kvw = ATT_KV_HEADS * ATT_HD
    cache_k4 = cache_k.reshape(DEC_BATCH, DEPTH, PAST_LEN, kvw)
    cache_v4 = cache_v.reshape(DEC_BATCH, DEPTH, PAST_LEN, kvw)
    w_in_b, w_s_b, w_up_b, w_o_b = (w.astype(BF16) for w in (w_in, w_s, w_up, w_o))

    new_k, new_v, new_s = [], [], []
    for l in range(DEPTH):
        pw = _block_diag(pool_w[l]).astype(BF16)
        bs_tile = jnp.repeat(b_s[l].T, A_GD, axis=1)
        final = l == DEPTH - 1
        for lat in (False, True):
            x = xs[lat]
            a, cq, ck, cv, dp, *pre, bonus = _inproj(
                x, mod[l], g_norm[l], w_in_b, l, rw_w0[l], rw_w_up[l], rw_a0[l], rw_a_up[l], rw_k_k[l], rw_k_a[l],
                rw_r_k[l], ones_bd, lat)
            r, v = pre[0], pre[1]
            pre = pre[2:] + [r, v]
            if lat:
                o_f, o_b, _ = _scan(pre, _lat_state(state_rwkv[:, l]), nb=DEC_BATCH, t_len=DEC_SEQ, name="scan_lat")
                y_c = _att_lat(att_sink[l], cq, ck, cv, cache_k4, cache_v4, l, tables)
                y_d = _pool(dp, pw, pool_scale[l], DEC_BATCH, DEC_SEQ, "pool_lat")
            else:
                o_f, o_b, s_fin = _scan(pre, None, nb=BATCH, t_len=SEQ, name="scan_ctx")
                y_c = _att_ctx(att_sink[l], cq, ck, cv)
                y_d = _pool(dp, pw, pool_scale[l], BATCH, SEQ, "pool_ctx")
                new_k.append(ck.reshape(BATCH, SEQ, ATT_KV_HEADS, ATT_HD))
                new_v.append(cv.reshape(BATCH, SEQ, ATT_KV_HEADS, ATT_HD))
                new_s.append(s_fin.reshape(RW_HD, RW_HD, RW_HEADS, 2, BATCH).transpose(4, 3, 2, 1, 0))

            xs[lat] = _merge(x, mod[l], g_norm[l], a, bonus, o_f.reshape(N_GRP, BR_W), o_b.reshape(N_GRP, BR_W),
                             y_c, y_d, w_in_b, l, w_s_b, bs_tile, rw_ln_g[l], rw_ln_b[l], ones_bd, w_up_b, w_o_b,
                             g_final, lat, final)

    y_prompt = xs[0].reshape(BATCH, SEQ, D_MODEL)
    y_sample = xs[1].reshape(DEC_BATCH, DEC_SEQ, D_MODEL)
    return (y_prompt, y_sample, jnp.stack(new_k, axis=1), jnp.stack(new_v, axis=1), jnp.stack(new_s, axis=1))
```

```python
import functools

import numpy as np
import jax
import jax.numpy as jnp
from jax import lax
from jax.experimental import pallas as pl
from jax.experimental.pallas import tpu as pltpu

D_MODEL = 1024
BATCH = 16
SEQ = 256
DEPTH = 2
DEC_BATCH = 4
DEC_SEQ = 1024
PAST_LEN = 512
GRID_W = 64
N_BRANCH = 4
BR_W = D_MODEL // 4
CHUNK = 128
A_GROUPS = 4
A_GD = BR_W // A_GROUPS
RW_HD = 64
RW_HEADS = BR_W // RW_HD
DECAY_RANK = 64
ICL_RANK = 64
ATT_HD = 64
ATT_HEADS = BR_W // ATT_HD
ATT_KV_HEADS = 2
ATT_GROUP = ATT_HEADS // ATT_KV_HEADS
WINDOW = 128
ROPE_BASE = 10000.0
POOL_SIZES = (2, 4, 8, 16)
POOL_GD = BR_W // len(POOL_SIZES)
NORM_EPS = 1e-6
GN_EPS = 64e-5
NEG_INF = -1e30

N_CTX = BATCH * SEQ
N_LAT = DEC_BATCH * DEC_SEQ
N_GRP = N_CTX
assert N_LAT == N_GRP
IN_SMALL = 2176

F32 = jnp.float32
BF16 = jnp.bfloat16

V7X_VMEM_LIMIT = 56 * 1024 * 1024
LANES = 128

TM_IN = 512
TM_MERGE = 512
SCAN_TC = 32


def _cparams(sem):
    return pltpu.CompilerParams(dimension_semantics=sem, vmem_limit_bytes=V7X_VMEM_LIMIT)


def _mod_row(i, tm, lat):
    return 1 + i // (DEC_SEQ // tm) if lat else 0


def _norm_mod(x, g, m):
    ms = jnp.mean(x * x, axis=-1, keepdims=True)
    y = x * lax.rsqrt(ms + NORM_EPS) * g
    shift = m[:, :D_MODEL]
    scale = m[:, D_MODEL:2 * D_MODEL]
    return y * (1.0 + scale) + shift


def _sigmoid(x):
    return 1.0 / (1.0 + jnp.exp(-x))


def _split(x):
    hi = x.astype(BF16)
    return hi, (x - hi.astype(F32)).astype(BF16)


def _seg_sum(x, ones_bd):
    hi, lo = _split(x)
    return (jnp.dot(hi, ones_bd, preferred_element_type=F32)
            + jnp.dot(lo, ones_bd, preferred_element_type=F32))


def _dot_split(x, w):
    xh, xl = _split(x)
    wh, wl = _split(w)
    lhs = jnp.concatenate([xh, xl, xh], axis=1)
    rhs = jnp.concatenate([wh, wh, wl], axis=0)
    return jnp.dot(lhs, rhs, preferred_element_type=F32)


def _mod_kernel(c_ref, w_ref, b_ref, o_ref):
    cnd = c_ref[...]
    s = cnd * _sigmoid(cnd)
    o_ref[...] = jnp.dot(s.astype(BF16), w_ref[...].astype(BF16), preferred_element_type=F32) + b_ref[...]


def _modulation(cond, w_mod, b_mod):
    nt = 3 * D_MODEL // 1024
    return pl.pallas_call(
        _mod_kernel,
        grid=(DEPTH, nt),
        in_specs=[
            pl.BlockSpec((8, D_MODEL), lambda l, j: (0, 0)),
            pl.BlockSpec((None, D_MODEL, 1024), lambda l, j: (l, 0, j)),
            pl.BlockSpec((None, 1, 1024), lambda l, j: (l, 0, j)),
        ],
        out_specs=pl.BlockSpec((None, 8, 1024), lambda l, j: (l, 0, j)),
        out_shape=jax.ShapeDtypeStruct((DEPTH, 8, 3 * D_MODEL), F32),
        compiler_params=_cparams(("arbitrary", "arbitrary")),
        name="modulation",
    )(cond, w_mod, b_mod.reshape(DEPTH, 1, 3 * D_MODEL))


def _softplus(x):
    return jnp.maximum(x, 0.0) + jnp.log1p(jnp.exp(-jnp.abs(x)))


def _inproj_kernel(x_ref, mod_ref, g_ref, w_ref, w0_ref, wup_ref, a0_ref, aup_ref, kkw_ref, ka_ref, rk_ref, ones_ref,
                   a_ref, q_ref, ck_ref, cv_ref, d_ref, r_ref, v_ref,
                   w0o, w1o, kka0o, kka1o, kd0o, kd1o, kko, bonuso, *, lat):
    i = pl.program_id(0)
    m = mod_ref[pl.ds(_mod_row(i, TM_IN, lat), 1), :]
    h = _norm_mod(x_ref[...], g_ref[...], m)
    p = jnp.dot(h.astype(BF16), w_ref[0], preferred_element_type=F32)
    a_ref[...] = p[:, 0:512]
    q_ref[...] = p[:, 1408:1664]
    ck_ref[...] = p[:, 1664:1792]
    cv_ref[...] = p[:, 1792:1920]
    d_ref[...] = p[:, 1920:2176]

    r = p[:, 512:768]
    k = p[:, 768:1024]
    v = p[:, 1024:1280]
    wd_t = jnp.tanh(p[:, 1280:1344])
    ad = p[:, 1344:1408]
    r_ref[...] = r
    v_ref[...] = v
    ones = ones_ref[...]
    kk = k * kkw_ref[...]
    ss = _seg_sum(kk * kk, ones)
    kkn = kk * lax.rsqrt(ss + 1e-12)
    kko[...] = kkn
    bonus = jnp.zeros_like(v)
    for d, (wo, kkao, kdo) in enumerate(((w0o, kka0o, kd0o), (w1o, kka1o, kd1o))):
        pre = w0_ref[pl.ds(d, 1), :] + _dot_split(wd_t, wup_ref[d])
        w_log = -_softplus(-pre) - 0.5
        wo[...] = jnp.exp(-jnp.exp(w_log))
        a = _sigmoid(a0_ref[pl.ds(d, 1), :] + _dot_split(ad, aup_ref[d]))
        k_d = k * (1.0 + (a - 1.0) * ka_ref[...])
        kdo[...] = k_d
        kkao[...] = kkn * a
        bonus = bonus + _seg_sum(r * k_d * rk_ref[...], ones) * v
    bonuso[...] = bonus


def _inproj(x, mod_l, g, w_in_b, layer, w0, wup, a0, aup, k_k, k_a, r_k, ones_bd, lat):
    kvw = ATT_KV_HEADS * ATT_HD
    widths = (2 * BR_W, BR_W, kvw, kvw, BR_W) + (BR_W,) * 10
    full2 = lambda s: pl.BlockSpec(s, lambda i: (0, 0))
    return pl.pallas_call(
        functools.partial(_inproj_kernel, lat=lat),
        grid=(N_GRP // TM_IN,),
        in_specs=[
            pl.BlockSpec((TM_IN, D_MODEL), lambda i: (i, 0)),
            full2((8, 3 * D_MODEL)),
            full2((1, D_MODEL)),
            pl.BlockSpec((pl.Element(1), pl.Element(D_MODEL), pl.Element(IN_SMALL)), lambda i: (layer, 0, 0)),
            full2((2, BR_W)),
            pl.BlockSpec((2, DECAY_RANK, BR_W), lambda i: (0, 0, 0)),
            full2((2, BR_W)),
            pl.BlockSpec((2, ICL_RANK, BR_W), lambda i: (0, 0, 0)),
            full2((1, BR_W)), full2((1, BR_W)), full2((1, BR_W)),
            full2((BR_W, BR_W)),
        ],
        out_specs=[pl.BlockSpec((TM_IN, w), lambda i: (i, 0)) for w in widths],
        out_shape=[jax.ShapeDtypeStruct((N_GRP, w), F32) for w in widths],
        compiler_params=_cparams(("arbitrary",)),
        name="in_proj_lat" if lat else "in_proj_ctx",
    )(x, mod_l, g.reshape(1, D_MODEL), w_in_b, w0, wup, a0, aup,
      k_k.reshape(1, BR_W), k_a.reshape(1, BR_W), r_k.reshape(1, BR_W), ones_bd)


def _to_lanes(z0, z1, rep):
    lo = lax.broadcasted_iota(jnp.int32, (1, LANES), 1) < RW_HD
    parts = []
    for hp in range(RW_HEADS // 2):
        a = z0[:, hp * LANES:(hp + 1) * LANES]
        b = z1[:, hp * LANES:(hp + 1) * LANES]
        parts.append(jnp.where(lo, a, pltpu.roll(b, RW_HD, 1)))
        parts.append(jnp.where(lo, pltpu.roll(a, RW_HD, 1), b))
    return jnp.concatenate(parts * rep, axis=0).T


def _scan_kernel(*refs, nb, has_init):
    if has_init:
        s0_ref, refs = refs[0], refs[1:]
    (w0_ref, w1_ref, kka0_ref, kka1_ref, kd0_ref, kd1_ref, kkf_ref, kkb_ref, rf_ref, rb_ref, vf_ref, vb_ref,
     of_ref, ob_ref, st_ref, w_s, kk_s, kka_s, kd_s, r_s, v_s, o_s) = refs
    nk, vh, _ = st_ref.shape
    tc = w_s.shape[0]
    rep = RW_HD // vh
    nrec = LANES // rep
    lane = lax.broadcasted_iota(jnp.int32, (1, LANES), 1)
    grp = lane // nrec
    lo = lane < RW_HD

    def rows(f_ref, b_ref, j):
        return jnp.concatenate([f_ref[:, j, :], b_ref[:, tc - 1 - j, :]], axis=0)

    sources = ((w0_ref, w1_ref, w_s), (kkf_ref, kkb_ref, kk_s), (kka0_ref, kka1_ref, kka_s),
               (kd0_ref, kd1_ref, kd_s), (rf_ref, rb_ref, r_s))
    for j in range(0, tc, 2):
        for f_ref, b_ref, dst in sources:
            gt = _to_lanes(rows(f_ref, b_ref, j), rows(f_ref, b_ref, j + 1), rep)
            dst[j] = gt[0:RW_HD]
            dst[j + 1] = gt[RW_HD:2 * RW_HD]
        gt = _to_lanes(rows(vf_ref, vb_ref, j), rows(vf_ref, vb_ref, j + 1), rep)
        for s in range(2):
            blk = gt[s * RW_HD:(s + 1) * RW_HD]
            if rep > 1:
                blk = sum(jnp.where(grp == q, blk[q * vh:(q + 1) * vh], 0.0) for q in range(rep))
            v_s[j + s] = blk

    p = jnp.ones((nk, LANES), F32)
    for t in range(tc):
        kk_s[t] = kk_s[t] * p
        p = p * w_s[t]
        inv = 1.0 / p
        kka_s[t] = kka_s[t] * inv
        kd_s[t] = kd_s[t] * inv
        r_s[t] = r_s[t] * p
    w_s[0] = p

    @pl.when(pl.program_id(0) == 0)
    def _():
        if has_init:
            st_ref[...] = s0_ref[...]
        else:
            st_ref[...] = jnp.zeros(st_ref.shape, F32)

    sa0 = jnp.zeros((vh, LANES), F32)
    for k in range(nk):
        sa0 = sa0 + st_ref[k] * kk_s[0, pl.ds(k, 1), :]

    def step(t, sa):
        tn = jnp.minimum(t + 1, tc - 1)
        vt = v_s[t]
        o = jnp.zeros((vh, LANES), F32)
        san = jnp.zeros((vh, LANES), F32)
        for k in range(nk):
            new = st_ref[k] - kka_s[t, pl.ds(k, 1), :] * sa + kd_s[t, pl.ds(k, 1), :] * vt
            st_ref[k] = new
            o = o + new * r_s[t, pl.ds(k, 1), :]
            san = san + new * kk_s[tn, pl.ds(k, 1), :]
        o_s[t] = o
        return san

    lax.fori_loop(0, tc, step, sa0)
    for k in range(nk):
        st_ref[k] = st_ref[k] * w_s[0, pl.ds(k, 1), :]

    for j in range(0, tc, 2):
        tiles = []
        for s in range(2):
            o = o_s[j + s]
            if rep == 1:
                tiles.append(o)
            else:
                tiles.extend(jnp.where(grp == q, o, 0.0) for q in range(rep))
        mt = jnp.concatenate(tiles, axis=0).T
        sm = mt[0:nrec]
        for q in range(1, rep):
            sm = sm + mt[q * nrec:(q + 1) * nrec]
        rs = pltpu.roll(sm, RW_HD, 1)
        for hp in range(RW_HEADS // 2):
            e0, e1 = (2 * hp) * 2 * nb, (2 * hp + 1) * 2 * nb
            cur = jnp.where(lo, sm[e0:e0 + 2 * nb], rs[e1:e1 + 2 * nb])
            nxt = jnp.where(lo, rs[e0:e0 + 2 * nb], sm[e1:e1 + 2 * nb])
            cs = slice(hp * LANES, (hp + 1) * LANES)
            of_ref[:, j, cs] = cur[0:nb]
            of_ref[:, j + 1, cs] = nxt[0:nb]
            ob_ref[:, tc - 1 - j, cs] = cur[nb:2 * nb]
            ob_ref[:, tc - 2 - j, cs] = nxt[nb:2 * nb]


def _scan(pre, s0, *, nb, t_len, name):
    w0, w1, kka0, kka1, kd0, kd1, kk, r, v = (p.reshape(nb, t_len, BR_W) for p in pre)
    n_t = t_len // SCAN_TC
    vh = RW_HD * nb * 2 * RW_HEADS // LANES
    fwd = pl.BlockSpec((nb, SCAN_TC, BR_W), lambda i: (0, i, 0))
    bwd = pl.BlockSpec((nb, SCAN_TC, BR_W), lambda i: (0, n_t - 1 - i, 0))
    sspec = pl.BlockSpec((RW_HD, vh, LANES), lambda i: (0, 0, 0))
    has_init = s0 is not None
    args = ((s0,) if has_init else ()) + (w0, w1, kka0, kka1, kd0, kd1, kk, kk, r, r, v, v)
    in_specs = ([sspec] if has_init else []) + [fwd, bwd] * 6
    kbuf = pltpu.VMEM((SCAN_TC, RW_HD, LANES), F32)
    vbuf = pltpu.VMEM((SCAN_TC, vh, LANES), F32)
    return pl.pallas_call(
        functools.partial(_scan_kernel, nb=nb, has_init=has_init),
        grid=(n_t,),
        in_specs=in_specs,
        out_specs=[pl.BlockSpec((nb, SCAN_TC, BR_W), lambda i: (0, i, 0)),
                   pl.BlockSpec((nb, SCAN_TC, BR_W), lambda i: (0, n_t - 1 - i, 0)),
                   sspec],
        out_shape=[jax.ShapeDtypeStruct((nb, t_len, BR_W), F32),
                   jax.ShapeDtypeStruct((nb, t_len, BR_W), F32),
                   jax.ShapeDtypeStruct((RW_HD, vh, LANES), F32)],
        scratch_shapes=[kbuf] * 5 + [vbuf] * 2,
        compiler_params=_cparams(("arbitrary",)),
        name=name,
    )(*args)


LAT_VS = 4
LAT_VH = RW_HD // LAT_VS


def _lat_state(s):
    s6 = s.reshape(DEC_BATCH, 2, RW_HEADS, LAT_VS, LAT_VH, RW_HD)
    return s6.transpose(5, 4, 3, 2, 1, 0).reshape(RW_HD, LAT_VH, LANES)


def _sink_softmax_pv(s, sk, vb):
    m = jnp.maximum(jnp.max(s, axis=-1, keepdims=True), sk)
    p = jnp.exp(s - m)
    den = jnp.sum(p, axis=-1, keepdims=True) + jnp.exp(sk - m)
    return jnp.dot(p.astype(BF16), vb, preferred_element_type=F32) / den


def _att_ctx_kernel(sink_ref, q_ref, k_ref, v_ref, o_ref):
    scale = ATT_HD ** -0.5
    q = q_ref[...]
    kb = k_ref[...].astype(BF16)
    vb = v_ref[...].astype(BF16)
    first = lax.broadcasted_iota(jnp.int32, (ATT_GROUP * SEQ, 1), 0) < SEQ
    outs = []
    for kvh in range(ATT_KV_HEADS):
        ks = slice(kvh * ATT_HD, (kvh + 1) * ATT_HD)
        q2 = jnp.concatenate([q[:, hd * ATT_HD:(hd + 1) * ATT_HD]
                              for hd in (ATT_GROUP * kvh, ATT_GROUP * kvh + 1)], axis=0).astype(BF16)
        s = lax.dot_general(q2, kb[:, ks], (((1,), (1,)), ((), ())), preferred_element_type=F32) * scale
        sk = jnp.where(first, sink_ref[ATT_GROUP * kvh], sink_ref[ATT_GROUP * kvh + 1])
        o2 = _sink_softmax_pv(s, sk, vb[:, ks])
        outs += [o2[:SEQ], o2[SEQ:]]
    o_ref[...] = jnp.concatenate(outs, axis=1)


def _att_ctx(sink, cq, ck, cv):
    kvw = ATT_KV_HEADS * ATT_HD
    return pl.pallas_call(
        _att_ctx_kernel,
        grid=(BATCH,),
        in_specs=[
            pl.BlockSpec(memory_space=pltpu.SMEM),
            pl.BlockSpec((SEQ, BR_W), lambda b: (b, 0)),
            pl.BlockSpec((SEQ, kvw), lambda b: (b, 0)),
            pl.BlockSpec((SEQ, kvw), lambda b: (b, 0)),
        ],
        out_specs=pl.BlockSpec((SEQ, BR_W), lambda b: (b, 0)),
        out_shape=jax.ShapeDtypeStruct((N_CTX, BR_W), F32),
        compiler_params=_cparams(("arbitrary",)),
        name="att_ctx",
    )(sink, cq, ck, cv)


def _rope(x, cos, sin_signed):
    w = x.shape[1]
    lane = lax.broadcasted_iota(jnp.int32, (1, w), 1)
    first = (lane % 32) < 16
    swapped = jnp.where(first, pltpu.roll(x, w - 16, 1), pltpu.roll(x, 16, 1))
    return x * cos + swapped * sin_signed


def _att_lat_kernel(sink_ref, q_ref, k_ref, v_ref, ck_ref, cv_ref, cq_ref, sq_ref, ckk_ref, skk_ref,
                    o_ref, qs_ref, ks_ref, vs_ref):
    scale = ATT_HD ** -0.5
    nb = DEC_SEQ // CHUNK
    qs_ref[...] = _rope(q_ref[...], cq_ref[...], sq_ref[...]).astype(BF16)
    ks_ref[...] = _rope(k_ref[...], ckk_ref[...], skk_ref[...]).astype(BF16)
    vs_ref[...] = v_ref[...].astype(BF16)
    ckb = ck_ref[...].astype(BF16)
    cvb = cv_ref[...].astype(BF16)
    for n in range(nb):
        lo = max(n - 1, 0) * CHUNK
        hi = min(n + 2, nb) * CHUNK
        nloc = hi - lo
        ncol = nloc + PAST_LEN
        col = lax.broadcasted_iota(jnp.int32, (CHUNK, ncol), 1)
        row = lax.broadcasted_iota(jnp.int32, (CHUNK, ncol), 0)
        dist = jnp.abs((n * CHUNK + row) - (lo + col))
        valid = (col >= nloc) | (dist <= WINDOW)
        valid2 = jnp.concatenate([valid] * ATT_GROUP, axis=0)
        first = lax.broadcasted_iota(jnp.int32, (ATT_GROUP * CHUNK, 1), 0) < CHUNK
        outs = []
        for kvh in range(ATT_KV_HEADS):
            ks = slice(kvh * ATT_HD, (kvh + 1) * ATT_HD)
            q2 = jnp.concatenate([qs_ref[n * CHUNK:(n + 1) * CHUNK, hd * ATT_HD:(hd + 1) * ATT_HD]
                                  for hd in (ATT_GROUP * kvh, ATT_GROUP * kvh + 1)], axis=0)
            kall = jnp.concatenate([ks_ref[lo:hi, ks], ckb[:, ks]], axis=0)
            vall = jnp.concatenate([vs_ref[lo:hi, ks], cvb[:, ks]], axis=0)
            s = lax.dot_general(q2, kall, (((1,), (1,)), ((), ())), preferred_element_type=F32) * scale
            s = jnp.where(valid2, s, NEG_INF)
            sk = jnp.where(first, sink_ref[ATT_GROUP * kvh], sink_ref[ATT_GROUP * kvh + 1])
            o2 = _sink_softmax_pv(s, sk, vall)
            outs += [o2[:CHUNK], o2[CHUNK:]]
        o_ref[n * CHUNK:(n + 1) * CHUNK, :] = jnp.concatenate(outs, axis=1)


def _att_lat(sink, cq, ck, cv, cache_k, cache_v, layer, tables):
    kvw = ATT_KV_HEADS * ATT_HD
    cosq, sinq, cosk, sink_k = tables
    seq = lambda w: pl.BlockSpec((DEC_SEQ, w), lambda b: (b, 0))
    cache = pl.BlockSpec((None, None, PAST_LEN, kvw), lambda b: (b, layer, 0, 0))
    tab = lambda w: pl.BlockSpec((DEC_SEQ, w), lambda b: (0, 0))
    return pl.pallas_call(
        _att_lat_kernel,
        grid=(DEC_BATCH,),
        in_specs=[pl.BlockSpec(memory_space=pltpu.SMEM), seq(BR_W), seq(kvw), seq(kvw), cache, cache,
                  tab(BR_W), tab(BR_W), tab(kvw), tab(kvw)],
        out_specs=pl.BlockSpec((DEC_SEQ, BR_W), lambda b: (b, 0)),
        out_shape=jax.ShapeDtypeStruct((N_LAT, BR_W), F32),
        scratch_shapes=[pltpu.VMEM((DEC_SEQ, BR_W), BF16), pltpu.VMEM((DEC_SEQ, kvw), BF16),
                        pltpu.VMEM((DEC_SEQ, kvw), BF16)],
        compiler_params=_cparams(("arbitrary",)),
        name="att_lat",
    )(sink, cq, ck, cv, cache_k, cache_v, cosq, sinq, cosk, sink_k)


def _rope_tables():
    pos = np.arange(DEC_SEQ)
    lane = np.arange(ATT_HD)
    p = np.where(lane[None, :] < 32, (pos // GRID_W)[:, None], (pos % GRID_W)[:, None]).astype(np.float32)
    inv = (ROPE_BASE ** (-jnp.arange(0, 32, 2, dtype=F32) / 32))[lane % 16]
    ang = jnp.asarray(p) * inv[None, :]
    sign = jnp.asarray(np.where((lane % 32) < 16, -1.0, 1.0).astype(np.float32))
    cos, sin = jnp.cos(ang), jnp.sin(ang) * sign[None, :]
    return (jnp.tile(cos, (1, ATT_HEADS)), jnp.tile(sin, (1, ATT_HEADS)),
            jnp.tile(cos, (1, ATT_KV_HEADS)), jnp.tile(sin, (1, ATT_KV_HEADS)))


POOL_PAD = 8


def _pool_kernel(x_ref, w_ref, sc_ref, ic_ref, o_ref, a_ref, b_ref):
    n = x_ref.shape[0]
    ext = n + POOL_PAD
    x = x_ref[...]
    zeros = jnp.zeros((POOL_PAD, BR_W), F32)
    a_ref[0:POOL_PAD, :] = zeros
    b_ref[0:POOL_PAD, :] = zeros
    a_ref[POOL_PAD:POOL_PAD + n, :] = x
    a_ref[POOL_PAD + n:POOL_PAD + ext, :] = zeros
    grp = lax.broadcasted_iota(jnp.int32, (1, BR_W), 1) // POOL_GD
    acc = jnp.zeros((n, BR_W), F32)
    src_ref, dst_ref = a_ref, b_ref
    for m in range(len(POOL_SIZES)):
        back = 1 << m
        dst_ref[POOL_PAD:POOL_PAD + ext, :] = (src_ref[POOL_PAD:POOL_PAD + ext, :]
                                               + src_ref[POOL_PAD - back:POOL_PAD - back + ext, :])
        off = POOL_PAD + back - 1
        acc = jnp.where(grp == m, dst_ref[off:off + n, :], acc)
        src_ref, dst_ref = dst_ref, src_ref
    d = acc * ic_ref[...] - x
    o_ref[...] = jnp.dot(d.astype(BF16), w_ref[...], preferred_element_type=F32) * sc_ref[...]


def _pool_inv_count(seq_len):
    t = np.arange(seq_len)[:, None]
    half = (np.asarray(POOL_SIZES) // 2)[np.arange(BR_W) // POOL_GD][None, :]
    cnt = np.minimum(t + half, seq_len) - np.maximum(t - half, 0)
    return jnp.asarray(1.0 / cnt, F32)


def _pool(dp, w_bd, scale, n_seq, seq_len, name):
    return pl.pallas_call(
        _pool_kernel,
        grid=(n_seq,),
        in_specs=[pl.BlockSpec((seq_len, BR_W), lambda b: (b, 0)),
                  pl.BlockSpec((BR_W, BR_W), lambda b: (0, 0)),
                  pl.BlockSpec((1, BR_W), lambda b: (0, 0)),
                  pl.BlockSpec((seq_len, BR_W), lambda b: (0, 0))],
        out_specs=pl.BlockSpec((seq_len, BR_W), lambda b: (b, 0)),
        out_shape=jax.ShapeDtypeStruct((n_seq * seq_len, BR_W), F32),
        scratch_shapes=[pltpu.VMEM((seq_len + 2 * POOL_PAD, BR_W), F32)] * 2,
        compiler_params=_cparams(("arbitrary",)),
        name=name,
    )(dp, w_bd, scale.reshape(1, BR_W), _pool_inv_count(seq_len))


def _merge_kernel(x_ref, mod_ref, g_ref, a_ref, bonus_ref, of_ref, ob_ref, yc_ref, yd_ref,
                  wz_ref, wmg_ref, ws_ref, bs_ref, lng_ref, lnb_ref, ones_ref, wup_ref, wo_ref, gf_ref,
                  o_ref, *, lat, final):
    i = pl.program_id(0)
    m = mod_ref[pl.ds(_mod_row(i, TM_MERGE, lat), 1), :]
    x = x_ref[...]
    hb = _norm_mod(x, g_ref[...], m).astype(BF16)

    a = a_ref[...]
    a_u, a_v = a[:, :BR_W], a[:, BR_W:]
    grp = lax.broadcasted_iota(jnp.int32, (1, BR_W), 1) // A_GD
    svs = []
    for c in range(TM_MERGE // CHUNK):
        vc = a_v[c * CHUNK:(c + 1) * CHUNK, :]
        sv = bs_ref[...]
        for g in range(A_GROUPS):
            sv = sv + jnp.dot(ws_ref[g], jnp.where(grp == g, vc, 0.0).astype(BF16), preferred_element_type=F32)
        svs.append(sv)
    y_a = a_u * jnp.concatenate(svs, axis=0)

    ones = ones_ref[...]
    osum = of_ref[...] + ob_ref[...]
    mu = _seg_sum(osum, ones) * (1.0 / RW_HD)
    dev = osum - mu
    var = _seg_sum(dev * dev, ones) * (1.0 / RW_HD)
    y_b = dev * lax.rsqrt(var + GN_EPS) * lng_ref[...] + lnb_ref[...] + bonus_ref[...]

    merged = jnp.zeros((TM_MERGE, D_MODEL), F32)
    for n, y in enumerate((y_a, y_b, yc_ref[...], yd_ref[...])):
        z = jnp.dot(hb, wz_ref[0, :, n * BR_W:(n + 1) * BR_W], preferred_element_type=F32)
        ys = y * (z * _sigmoid(z))
        up = jnp.dot(ys.astype(BF16), wup_ref[n], preferred_element_type=F32)
        mg = jnp.dot(hb, wmg_ref[0, :, n * D_MODEL:(n + 1) * D_MODEL], preferred_element_type=F32)
        merged = merged + _sigmoid(mg) * up
    gate = m[:, 2 * D_MODEL:]
    out = x + gate * jnp.dot(merged.astype(BF16), wo_ref[...], preferred_element_type=F32)
    if final:
        ms = jnp.mean(out * out, axis=-1, keepdims=True)
        out = out * lax.rsqrt(ms + NORM_EPS) * gf_ref[...]
    o_ref[...] = out


def _merge(x, mod_l, g, a, bonus, o_f, o_b, y_c, y_d, w_in_b, layer, ws, bs_tile, ln_g, ln_b, ones_bd, wup, wo,
           g_final, lat, final):
    row = lambda w: pl.BlockSpec((TM_MERGE, w), lambda i: (i, 0))
    full2 = lambda s: pl.BlockSpec(s, lambda i: (0, 0))
    z0, z1 = IN_SMALL, IN_SMALL + N_BRANCH * BR_W
    cols = lambda c0, w: pl.BlockSpec((pl.Element(1), pl.Element(D_MODEL), pl.Element(w)), lambda i: (layer, 0, c0))
    return pl.pallas_call(
        functools.partial(_merge_kernel, lat=lat, final=final),
        grid=(N_GRP // TM_MERGE,),
        in_specs=[row(D_MODEL), full2((8, 3 * D_MODEL)), full2((1, D_MODEL)), row(2 * BR_W),
                  row(BR_W), row(BR_W), row(BR_W), row(BR_W), row(BR_W),
                  cols(z0, N_BRANCH * BR_W), cols(z1, N_BRANCH * D_MODEL),
                  pl.BlockSpec((None, A_GROUPS, CHUNK, CHUNK), lambda i: (layer, 0, 0, 0)), full2((CHUNK, BR_W)),
                  full2((1, BR_W)), full2((1, BR_W)), full2((BR_W, BR_W)),
                  pl.BlockSpec((None, N_BRANCH, BR_W, D_MODEL), lambda i: (layer, 0, 0, 0)),
                  pl.BlockSpec((None, D_MODEL, D_MODEL), lambda i: (layer, 0, 0)), full2((1, D_MODEL))],
        out_specs=row(D_MODEL),
        out_shape=jax.ShapeDtypeStruct((N_GRP, D_MODEL), F32),
        compiler_params=_cparams(("arbitrary",)),
        name=("merge_lat" if lat else "merge_ctx") + ("_final" if final else ""),
    )(x, mod_l, g.reshape(1, D_MODEL), a, bonus, o_f, o_b, y_c, y_d, w_in_b, w_in_b, ws, bs_tile,
      ln_g.reshape(1, BR_W), ln_b.reshape(1, BR_W), ones_bd, wup, wo, g_final.reshape(1, D_MODEL))


def _block_diag(blocks):
    n, r, c = blocks.shape
    eye = jnp.eye(n, dtype=blocks.dtype)
    return (eye[:, None, :, None] * blocks[:, :, None, :]).reshape(n * r, n * c)


def kernel(x_prompt, x_sample, cache_k, cache_v, state_rwkv, c, c_ctx, w_mod, b_mod, g_norm, w_in, w_s, b_s,
           rw_w0, rw_w_up, rw_a0, rw_a_up, rw_k_k, rw_k_a, rw_r_k, rw_ln_g, rw_ln_b, att_sink, pool_w,
           pool_scale, w_up, w_o, g_final):
    xs = [x_prompt.reshape(N_CTX, D_MODEL), x_sample.reshape(N_LAT, D_MODEL)]
    cond = jnp.concatenate([c_ctx[None, :], c, jnp.zeros((8 - 1 - DEC_BATCH, D_MODEL), F32)], axis=0)
    mod = _modulation(cond, w_mod, b_mod)

    ones_bd = _block_diag(jnp.ones((RW_HEADS, RW_HD, RW_HD), BF16))
    tables = _rope_tables()
    kvw = ATT_KV_HEADS * ATT_HD
    cache_k4 = cache_k.reshape(DEC_BATCH, DEPTH, PAST_LEN, kvw)
    cache_v4 = cache_v.reshape(DEC_BATCH, DEPTH, PAST_LEN, kvw)
    w_in_b, w_s_b, w_up_b, w_o_b = (w.astype(BF16) for w in (w_in, w_s, w_up, w_o))

    new_k, new_v, new_s = [], [], []
    for l in range(DEPTH):
        pw = _block_diag(pool_w[l]).astype(BF16)
        bs_tile = jnp.repeat(b_s[l].T, A_GD, axis=1)
        final = l == DEPTH - 1
        for lat in (False, True):
            x = xs[lat]
            a, cq, ck, cv, dp, *pre, bonus = _inproj(
                x, mod[l], g_norm[l], w_in_b, l, rw_w0[l], rw_w_up[l], rw_a0[l], rw_a_up[l], rw_k_k[l], rw_k_a[l],
                rw_r_k[l], ones_bd, lat)
            r, v = pre[0], pre[1]
            pre = pre[2:] + [r, v]
            if lat:
                o_f, o_b, _ = _scan(pre, _lat_state(state_rwkv[:, l]), nb=DEC_BATCH, t_len=DEC_SEQ, name="scan_lat")
                y_c = _att_lat(att_sink[l], cq, ck, cv, cache_k4, cache_v4, l, tables)
                y_d = _pool(dp, pw, pool_scale[l], DEC_BATCH, DEC_SEQ, "pool_lat")
            else:
                o_f, o_b, s_fin = _scan(pre, None, nb=BATCH, t_len=SEQ, name="scan_ctx")
                y_c = _att_ctx(att_sink[l], cq, ck, cv)
                y_d = _pool(dp, pw, pool_scale[l], BATCH, SEQ, "pool_ctx")
                new_k.append(ck.reshape(BATCH, SEQ, ATT_KV_HEADS, ATT_HD))
                new_v.append(cv.reshape(BATCH, SEQ, ATT_KV_HEADS, ATT_HD))
                new_s.append(s_fin.reshape(RW_HD, RW_HD, RW_HEADS, 2, BATCH).transpose(4, 3, 2, 1, 0))

            xs[lat] = _merge(x, mod[l], g_norm[l], a, bonus, o_f.reshape(N_GRP, BR_W), o_b.reshape(N_GRP, BR_W),
                             y_c, y_d, w_in_b, l, w_s_b, bs_tile, rw_ln_g[l], rw_ln_b[l], ones_bd, w_up_b, w_o_b,
                             g_final, lat, final)

    y_prompt = xs[0].reshape(BATCH, SEQ, D_MODEL)
    y_sample = xs[1].reshape(DEC_BATCH, DEC_SEQ, D_MODEL)
    return (y_prompt, y_sample, jnp.stack(new_k, axis=1), jnp.stack(new_v, axis=1), jnp.stack(new_s, axis=1))
```

```python
import functools

import numpy as np
import jax
import jax.numpy as jnp
from jax import lax
from jax.experimental import pallas as pl
from jax.experimental.pallas import tpu as pltpu

D_MODEL = 1024
BATCH = 16
SEQ = 256
DEPTH = 2
DEC_BATCH = 4
DEC_SEQ = 1024
PAST_LEN = 512
GRID_W = 64
N_BRANCH = 4
BR_W = D_MODEL // 4
CHUNK = 128
A_GROUPS = 4
A_GD = BR_W // A_GROUPS
RW_HD = 64
RW_HEADS = BR_W // RW_HD
DECAY_RANK = 64
ICL_RANK = 64
ATT_HD = 64
ATT_HEADS = BR_W // ATT_HD
ATT_KV_HEADS = 2
ATT_GROUP = ATT_HEADS // ATT_KV_HEADS
WINDOW = 128
ROPE_BASE = 10000.0
POOL_SIZES = (2, 4, 8, 16)
POOL_GD = BR_W // len(POOL_SIZES)
NORM_EPS = 1e-6
GN_EPS = 64e-5
NEG_INF = -1e30

N_CTX = BATCH * SEQ
N_LAT = DEC_BATCH * DEC_SEQ
N_GRP = N_CTX
assert N_LAT == N_GRP
IN_SMALL = 2176

F32 = jnp.float32
BF16 = jnp.bfloat16

V7X_VMEM_LIMIT = 56 * 1024 * 1024
LANES = 128

TM_IN = 512
TM_MERGE = 512
SCAN_TC = 32


def _cparams(sem):
    return pltpu.CompilerParams(dimension_semantics=sem, vmem_limit_bytes=V7X_VMEM_LIMIT)


def _mod_row(i, tm, lat):
    return 1 + i // (DEC_SEQ // tm) if lat else 0


def _norm_mod(x, g, m):
    ms = jnp.mean(x * x, axis=-1, keepdims=True)
    y = x * lax.rsqrt(ms + NORM_EPS) * g
    shift = m[:, :D_MODEL]
    scale = m[:, D_MODEL:2 * D_MODEL]
    return y * (1.0 + scale) + shift


def _sigmoid(x):
    return 1.0 / (1.0 + jnp.exp(-x))


def _split(x):
    hi = x.astype(BF16)
    return hi, (x - hi.astype(F32)).astype(BF16)


def _seg_sum(x, ones_bd):
    hi, lo = _split(x)
    return (jnp.dot(hi, ones_bd, preferred_element_type=F32)
            + jnp.dot(lo, ones_bd, preferred_element_type=F32))


def _dot_split(x, w):
    xh, xl = _split(x)
    wh, wl = _split(w)
    lhs = jnp.concatenate([xh, xl, xh], axis=1)
    rhs = jnp.concatenate([wh, wh, wl], axis=0)
    return jnp.dot(lhs, rhs, preferred_element_type=F32)


def _mod_kernel(c_ref, w_ref, b_ref, o_ref):
    cnd = c_ref[...]
    s = cnd * _sigmoid(cnd)
    o_ref[...] = jnp.dot(s.astype(BF16), w_ref[...].astype(BF16), preferred_element_type=F32) + b_ref[...]


def _modulation(cond, w_mod, b_mod):
    nt = 3 * D_MODEL // 1024
    return pl.pallas_call(
        _mod_kernel,
        grid=(DEPTH, nt),
        in_specs=[
            pl.BlockSpec((8, D_MODEL), lambda l, j: (0, 0)),
            pl.BlockSpec((None, D_MODEL, 1024), lambda l, j: (l, 0, j)),
            pl.BlockSpec((None, 1, 1024), lambda l, j: (l, 0, j)),
        ],
        out_specs=pl.BlockSpec((None, 8, 1024), lambda l, j: (l, 0, j)),
        out_shape=jax.ShapeDtypeStruct((DEPTH, 8, 3 * D_MODEL), F32),
        compiler_params=_cparams(("arbitrary", "arbitrary")),
        name="modulation",
    )(cond, w_mod, b_mod.reshape(DEPTH, 1, 3 * D_MODEL))


def _softplus(x):
    return jnp.maximum(x, 0.0) + jnp.log1p(jnp.exp(-jnp.abs(x)))


def _inproj_kernel(x_ref, mod_ref, g_ref, w_ref, w0_ref, wup_ref, a0_ref, aup_ref, kkw_ref, ka_ref, rk_ref, ones_ref,
                   a_ref, q_ref, ck_ref, cv_ref, d_ref, r_ref, v_ref,
                   w0o, w1o, kka0o, kka1o, kd0o, kd1o, kko, bonuso, *, lat):
    i = pl.program_id(0)
    m = mod_ref[pl.ds(_mod_row(i, TM_IN, lat), 1), :]
    h = _norm_mod(x_ref[...], g_ref[...], m)
    p = jnp.dot(h.astype(BF16), w_ref[0], preferred_element_type=F32)
    a_ref[...] = p[:, 0:512]
    q_ref[...] = p[:, 1408:1664]
    ck_ref[...] = p[:, 1664:1792]
    cv_ref[...] = p[:, 1792:1920]
    d_ref[...] = p[:, 1920:2176]

    r = p[:, 512:768]
    k = p[:, 768:1024]
    v = p[:, 1024:1280]
    wd_t = jnp.tanh(p[:, 1280:1344])
    ad = p[:, 1344:1408]
    r_ref[...] = r
    v_ref[...] = v
    ones = ones_ref[...]
    kk = k * kkw_ref[...]
    ss = _seg_sum(kk * kk, ones)
    kkn = kk * lax.rsqrt(ss + 1e-12)
    kko[...] = kkn
    bonus = jnp.zeros_like(v)
    for d, (wo, kkao, kdo) in enumerate(((w0o, kka0o, kd0o), (w1o, kka1o, kd1o))):
        pre = w0_ref[pl.ds(d, 1), :] + _dot_split(wd_t, wup_ref[d])
        w_log = -_softplus(-pre) - 0.5
        wo[...] = jnp.exp(-jnp.exp(w_log))
        a = _sigmoid(a0_ref[pl.ds(d, 1), :] + _dot_split(ad, aup_ref[d]))
        k_d = k * (1.0 + (a - 1.0) * ka_ref[...])
        kdo[...] = k_d
        kkao[...] = kkn * a
        bonus = bonus + _seg_sum(r * k_d * rk_ref[...], ones) * v
    bonuso[...] = bonus


def _inproj(x, mod_l, g, w_in_b, layer, w0, wup, a0, aup, k_k, k_a, r_k, ones_bd, lat):
    kvw = ATT_KV_HEADS * ATT_HD
    widths = (2 * BR_W, BR_W, kvw, kvw, BR_W) + (BR_W,) * 10
    full2 = lambda s: pl.BlockSpec(s, lambda i: (0, 0))
    return pl.pallas_call(
        functools.partial(_inproj_kernel, lat=lat),
        grid=(N_GRP // TM_IN,),
        in_specs=[
            pl.BlockSpec((TM_IN, D_MODEL), lambda i: (i, 0)),
            full2((8, 3 * D_MODEL)),
            full2((1, D_MODEL)),
            pl.BlockSpec((pl.Element(1), pl.Element(D_MODEL), pl.Element(IN_SMALL)), lambda i: (layer, 0, 0)),
            full2((2, BR_W)),
            pl.BlockSpec((2, DECAY_RANK, BR_W), lambda i: (0, 0, 0)),
            full2((2, BR_W)),
            pl.BlockSpec((2, ICL_RANK, BR_W), lambda i: (0, 0, 0)),
            full2((1, BR_W)), full2((1, BR_W)), full2((1, BR_W)),
            full2((BR_W, BR_W)),
        ],
        out_specs=[pl.BlockSpec((TM_IN, w), lambda i: (i, 0)) for w in widths],
        out_shape=[jax.ShapeDtypeStruct((N_GRP, w), F32) for w in widths],
        compiler_params=_cparams(("arbitrary",)),
        name="in_proj_lat" if lat else "in_proj_ctx",
    )(x, mod_l, g.reshape(1, D_MODEL), w_in_b, w0, wup, a0, aup,
      k_k.reshape(1, BR_W), k_a.reshape(1, BR_W), r_k.reshape(1, BR_W), ones_bd)


def _to_lanes(z0, z1, rep):
    lo = lax.broadcasted_iota(jnp.int32, (1, LANES), 1) < RW_HD
    parts = []
    for hp in range(RW_HEADS // 2):
        a = z0[:, hp * LANES:(hp + 1) * LANES]
        b = z1[:, hp * LANES:(hp + 1) * LANES]
        parts.append(jnp.where(lo, a, pltpu.roll(b, RW_HD, 1)))
        parts.append(jnp.where(lo, pltpu.roll(a, RW_HD, 1), b))
    return jnp.concatenate(parts * rep, axis=0).T


def _scan_kernel(*refs, nb, has_init):
    if has_init:
        s0_ref, refs = refs[0], refs[1:]
    (w0_ref, w1_ref, kka0_ref, kka1_ref, kd0_ref, kd1_ref, kkf_ref, kkb_ref, rf_ref, rb_ref, vf_ref, vb_ref,
     of_ref, ob_ref, st_ref, w_s, kk_s, kka_s, kd_s, r_s, v_s, o_s) = refs
    nk, vh, _ = st_ref.shape
    tc = w_s.shape[0]
    rep = RW_HD // vh
    nrec = LANES // rep
    lane = lax.broadcasted_iota(jnp.int32, (1, LANES), 1)
    grp = lane // nrec
    lo = lane < RW_HD

    def rows(f_ref, b_ref, j):
        return jnp.concatenate([f_ref[:, j, :], b_ref[:, tc - 1 - j, :]], axis=0)

    sources = ((w0_ref, w1_ref, w_s), (kkf_ref, kkb_ref, kk_s), (kka0_ref, kka1_ref, kka_s),
               (kd0_ref, kd1_ref, kd_s), (rf_ref, rb_ref, r_s))
    for j in range(0, tc, 2):
        for f_ref, b_ref, dst in sources:
            gt = _to_lanes(rows(f_ref, b_ref, j), rows(f_ref, b_ref, j + 1), rep)
            dst[j] = gt[0:RW_HD]
            dst[j + 1] = gt[RW_HD:2 * RW_HD]
        gt = _to_lanes(rows(vf_ref, vb_ref, j), rows(vf_ref, vb_ref, j + 1), rep)
        for s in range(2):
            blk = gt[s * RW_HD:(s + 1) * RW_HD]
            if rep > 1:
                blk = sum(jnp.where(grp == q, blk[q * vh:(q + 1) * vh], 0.0) for q in range(rep))
            v_s[j + s] = blk

    p = jnp.ones((nk, LANES), F32)
    for t in range(tc):
        kk_s[t] = kk_s[t] * p
        p = p * w_s[t]
        inv = 1.0 / p
        kka_s[t] = kka_s[t] * inv
        kd_s[t] = kd_s[t] * inv
        r_s[t] = r_s[t] * p
    w_s[0] = p

    @pl.when(pl.program_id(0) == 0)
    def _():
        if has_init:
            st_ref[...] = s0_ref[...]
        else:
            st_ref[...] = jnp.zeros(st_ref.shape, F32)

    sa0 = jnp.zeros((vh, LANES), F32)
    for k in range(nk):
        sa0 = sa0 + st_ref[k] * kk_s[0, pl.ds(k, 1), :]

    def step(t, sa):
        tn = jnp.minimum(t + 1, tc - 1)
        vt = v_s[t]
        o = jnp.zeros((vh, LANES), F32)
        san = jnp.zeros((vh, LANES), F32)
        for k in range(nk):
            new = st_ref[k] - kka_s[t, pl.ds(k, 1), :] * sa + kd_s[t, pl.ds(k, 1), :] * vt
            st_ref[k] = new
            o = o + new * r_s[t, pl.ds(k, 1), :]
            san = san + new * kk_s[tn, pl.ds(k, 1), :]
        o_s[t] = o
        return san

    lax.fori_loop(0, tc, step, sa0)
    for k in range(nk):
        st_ref[k] = st_ref[k] * w_s[0, pl.ds(k, 1), :]

    for j in range(0, tc, 2):
        tiles = []
        for s in range(2):
            o = o_s[j + s]
            if rep == 1:
                tiles.append(o)
            else:
                tiles.extend(jnp.where(grp == q, o, 0.0) for q in range(rep))
        mt = jnp.concatenate(tiles, axis=0).T
        sm = mt[0:nrec]
        for q in range(1, rep):
            sm = sm + mt[q * nrec:(q + 1) * nrec]
        rs = pltpu.roll(sm, RW_HD, 1)
        for hp in range(RW_HEADS // 2):
            e0, e1 = (2 * hp) * 2 * nb, (2 * hp + 1) * 2 * nb
            cur = jnp.where(lo, sm[e0:e0 + 2 * nb], rs[e1:e1 + 2 * nb])
            nxt = jnp.where(lo, rs[e0:e0 + 2 * nb], sm[e1:e1 + 2 * nb])
            cs = slice(hp * LANES, (hp + 1) * LANES)
            of_ref[:, j, cs] = cur[0:nb]
            of_ref[:, j + 1, cs] = nxt[0:nb]
            ob_ref[:, tc - 1 - j, cs] = cur[nb:2 * nb]
            ob_ref[:, tc - 2 - j, cs] = nxt[nb:2 * nb]


def _scan(pre, s0, *, nb, t_len, name):
    w0, w1, kka0, kka1, kd0, kd1, kk, r, v = (p.reshape(nb, t_len, BR_W) for p in pre)
    n_t = t_len // SCAN_TC
    vh = RW_HD * nb * 2 * RW_HEADS // LANES
    fwd = pl.BlockSpec((nb, SCAN_TC, BR_W), lambda i: (0, i, 0))
    bwd = pl.BlockSpec((nb, SCAN_TC, BR_W), lambda i: (0, n_t - 1 - i, 0))
    sspec = pl.BlockSpec((RW_HD, vh, LANES), lambda i: (0, 0, 0))
    has_init = s0 is not None
    args = ((s0,) if has_init else ()) + (w0, w1, kka0, kka1, kd0, kd1, kk, kk, r, r, v, v)
    in_specs = ([sspec] if has_init else []) + [fwd, bwd] * 6
    kbuf = pltpu.VMEM((SCAN_TC, RW_HD, LANES), F32)
    vbuf = pltpu.VMEM((SCAN_TC, vh, LANES), F32)
    return pl.pallas_call(
        functools.partial(_scan_kernel, nb=nb, has_init=has_init),
        grid=(n_t,),
        in_specs=in_specs,
        out_specs=[pl.BlockSpec((nb, SCAN_TC, BR_W), lambda i: (0, i, 0)),
                   pl.BlockSpec((nb, SCAN_TC, BR_W), lambda i: (0, n_t - 1 - i, 0)),
                   sspec],
        out_shape=[jax.ShapeDtypeStruct((nb, t_len, BR_W), F32),
                   jax.ShapeDtypeStruct((nb, t_len, BR_W), F32),
                   jax.ShapeDtypeStruct((RW_HD, vh, LANES), F32)],
        scratch_shapes=[kbuf] * 5 + [vbuf] * 2,
        compiler_params=_cparams(("arbitrary",)),
        name=name,
    )(*args)


LAT_VS = 4
LAT_VH = RW_HD // LAT_VS


def _lat_state(s):
    s6 = s.reshape(DEC_BATCH, 2, RW_HEADS, LAT_VS, LAT_VH, RW_HD)
    return s6.transpose(5, 4, 3, 2, 1, 0).reshape(RW_HD, LAT_VH, LANES)


def _sink_softmax_pv(s, sk, vb):
    m = jnp.maximum(jnp.max(s, axis=-1, keepdims=True), sk)
    p = jnp.exp(s - m)
    den = jnp.sum(p, axis=-1, keepdims=True) + jnp.exp(sk - m)
    return jnp.dot(p.astype(BF16), vb, preferred_element_type=F32) / den


def _att_ctx_kernel(sink_ref, q_ref, k_ref, v_ref, dp_ref, pw_ref, psc_ref, pic_ref, o_ref, yd_ref, pa_ref, pb_ref):
    _pool_body(dp_ref, pw_ref, psc_ref, pic_ref, yd_ref, pa_ref, pb_ref)
    scale = ATT_HD ** -0.5
    q = q_ref[...]
    kb = k_ref[...].astype(BF16)
    vb = v_ref[...].astype(BF16)
    first = lax.broadcasted_iota(jnp.int32, (ATT_GROUP * SEQ, 1), 0) < SEQ
    outs = []
    for kvh in range(ATT_KV_HEADS):
        ks = slice(kvh * ATT_HD, (kvh + 1) * ATT_HD)
        q2 = jnp.concatenate([q[:, hd * ATT_HD:(hd + 1) * ATT_HD]
                              for hd in (ATT_GROUP * kvh, ATT_GROUP * kvh + 1)], axis=0).astype(BF16)
        s = lax.dot_general(q2, kb[:, ks], (((1,), (1,)), ((), ())), preferred_element_type=F32) * scale
        sk = jnp.where(first, sink_ref[ATT_GROUP * kvh], sink_ref[ATT_GROUP * kvh + 1])
        o2 = _sink_softmax_pv(s, sk, vb[:, ks])
        outs += [o2[:SEQ], o2[SEQ:]]
    o_ref[...] = jnp.concatenate(outs, axis=1)


def _att_ctx(sink, cq, ck, cv, dp, pw, pscale):
    kvw = ATT_KV_HEADS * ATT_HD
    pool_in, pool_scratch = _pool_specs(SEQ)
    return pl.pallas_call(
        _att_ctx_kernel,
        grid=(BATCH,),
        in_specs=[
            pl.BlockSpec(memory_space=pltpu.SMEM),
            pl.BlockSpec((SEQ, BR_W), lambda b: (b, 0)),
            pl.BlockSpec((SEQ, kvw), lambda b: (b, 0)),
            pl.BlockSpec((SEQ, kvw), lambda b: (b, 0)),
        ] + pool_in,
        out_specs=[pl.BlockSpec((SEQ, BR_W), lambda b: (b, 0))] * 2,
        out_shape=[jax.ShapeDtypeStruct((N_CTX, BR_W), F32)] * 2,
        scratch_shapes=pool_scratch,
        compiler_params=_cparams(("arbitrary",)),
        name="att_ctx",
    )(sink, cq, ck, cv, dp, pw, pscale.reshape(1, BR_W), _pool_inv_count(SEQ))


def _rope(x, cos, sin_signed):
    w = x.shape[1]
    lane = lax.broadcasted_iota(jnp.int32, (1, w), 1)
    first = (lane % 32) < 16
    swapped = jnp.where(first, pltpu.roll(x, w - 16, 1), pltpu.roll(x, 16, 1))
    return x * cos + swapped * sin_signed


def _att_lat_kernel(sink_ref, q_ref, k_ref, v_ref, ck_ref, cv_ref, cq_ref, sq_ref, ckk_ref, skk_ref,
                    dp_ref, pw_ref, psc_ref, pic_ref, o_ref, yd_ref, qs_ref, ks_ref, vs_ref, pa_ref, pb_ref):
    _pool_body(dp_ref, pw_ref, psc_ref, pic_ref, yd_ref, pa_ref, pb_ref)
    scale = ATT_HD ** -0.5
    nb = DEC_SEQ // CHUNK
    qs_ref[...] = _rope(q_ref[...], cq_ref[...], sq_ref[...]).astype(BF16)
    ks_ref[...] = _rope(k_ref[...], ckk_ref[...], skk_ref[...]).astype(BF16)
    vs_ref[...] = v_ref[...].astype(BF16)
    ckb = ck_ref[...].astype(BF16)
    cvb = cv_ref[...].astype(BF16)
    for n in range(nb):
        lo = max(n - 1, 0) * CHUNK
        hi = min(n + 2, nb) * CHUNK
        nloc = hi - lo
        ncol = nloc + PAST_LEN
        col = lax.broadcasted_iota(jnp.int32, (CHUNK, ncol), 1)
        row = lax.broadcasted_iota(jnp.int32, (CHUNK, ncol), 0)
        dist = jnp.abs((n * CHUNK + row) - (lo + col))
        valid = (col >= nloc) | (dist <= WINDOW)
        valid2 = jnp.concatenate([valid] * ATT_GROUP, axis=0)
        first = lax.broadcasted_iota(jnp.int32, (ATT_GROUP * CHUNK, 1), 0) < CHUNK
        outs = []
        for kvh in range(ATT_KV_HEADS):
            ks = slice(kvh * ATT_HD, (kvh + 1) * ATT_HD)
            q2 = jnp.concatenate([qs_ref[n * CHUNK:(n + 1) * CHUNK, hd * ATT_HD:(hd + 1) * ATT_HD]
                                  for hd in (ATT_GROUP * kvh, ATT_GROUP * kvh + 1)], axis=0)
            kall = jnp.concatenate([ks_ref[lo:hi, ks], ckb[:, ks]], axis=0)
            vall = jnp.concatenate([vs_ref[lo:hi, ks], cvb[:, ks]], axis=0)
            s = lax.dot_general(q2, kall, (((1,), (1,)), ((), ())), preferred_element_type=F32) * scale
            s = jnp.where(valid2, s, NEG_INF)
            sk = jnp.where(first, sink_ref[ATT_GROUP * kvh], sink_ref[ATT_GROUP * kvh + 1])
            o2 = _sink_softmax_pv(s, sk, vall)
            outs += [o2[:CHUNK], o2[CHUNK:]]
        o_ref[n * CHUNK:(n + 1) * CHUNK, :] = jnp.concatenate(outs, axis=1)


def _att_lat(sink, cq, ck, cv, cache_k, cache_v, layer, tables, dp, pw, pscale):
    kvw = ATT_KV_HEADS * ATT_HD
    pool_in, pool_scratch = _pool_specs(DEC_SEQ)
    cosq, sinq, cosk, sink_k = tables
    seq = lambda w: pl.BlockSpec((DEC_SEQ, w), lambda b: (b, 0))
    cache = pl.BlockSpec((None, None, PAST_LEN, kvw), lambda b: (b, layer, 0, 0))
    tab = lambda w: pl.BlockSpec((DEC_SEQ, w), lambda b: (0, 0))
    return pl.pallas_call(
        _att_lat_kernel,
        grid=(DEC_BATCH,),
        in_specs=[pl.BlockSpec(memory_space=pltpu.SMEM), seq(BR_W), seq(kvw), seq(kvw), cache, cache,
                  tab(BR_W), tab(BR_W), tab(kvw), tab(kvw)] + pool_in,
        out_specs=[pl.BlockSpec((DEC_SEQ, BR_W), lambda b: (b, 0))] * 2,
        out_shape=[jax.ShapeDtypeStruct((N_LAT, BR_W), F32)] * 2,
        scratch_shapes=[pltpu.VMEM((DEC_SEQ, BR_W), BF16), pltpu.VMEM((DEC_SEQ, kvw), BF16),
                        pltpu.VMEM((DEC_SEQ, kvw), BF16)] + pool_scratch,
        compiler_params=_cparams(("arbitrary",)),
        name="att_lat",
    )(sink, cq, ck, cv, cache_k, cache_v, cosq, sinq, cosk, sink_k, dp, pw, pscale.reshape(1, BR_W),
      _pool_inv_count(DEC_SEQ))


def _rope_tables():
    pos = np.arange(DEC_SEQ)
    lane = np.arange(ATT_HD)
    p = np.where(lane[None, :] < 32, (pos // GRID_W)[:, None], (pos % GRID_W)[:, None]).astype(np.float32)
    inv = (ROPE_BASE ** (-jnp.arange(0, 32, 2, dtype=F32) / 32))[lane % 16]
    ang = jnp.asarray(p) * inv[None, :]
    sign = jnp.asarray(np.where((lane % 32) < 16, -1.0, 1.0).astype(np.float32))
    cos, sin = jnp.cos(ang), jnp.sin(ang) * sign[None, :]
    return (jnp.tile(cos, (1, ATT_HEADS)), jnp.tile(sin, (1, ATT_HEADS)),
            jnp.tile(cos, (1, ATT_KV_HEADS)), jnp.tile(sin, (1, ATT_KV_HEADS)))


POOL_PAD = 8


def _pool_body(x_ref, w_ref, sc_ref, ic_ref, o_ref, a_ref, b_ref):
    n = x_ref.shape[0]
    ext = n + POOL_PAD
    x = x_ref[...]
    zeros = jnp.zeros((POOL_PAD, BR_W), F32)
    a_ref[0:POOL_PAD, :] = zeros
    b_ref[0:POOL_PAD, :] = zeros
    a_ref[POOL_PAD:POOL_PAD + n, :] = x
    a_ref[POOL_PAD + n:POOL_PAD + ext, :] = zeros
    grp = lax.broadcasted_iota(jnp.int32, (1, BR_W), 1) // POOL_GD
    acc = jnp.zeros((n, BR_W), F32)
    src_ref, dst_ref = a_ref, b_ref
    for m in range(len(POOL_SIZES)):
        back = 1 << m
        dst_ref[POOL_PAD:POOL_PAD + ext, :] = (src_ref[POOL_PAD:POOL_PAD + ext, :]
                                               + src_ref[POOL_PAD - back:POOL_PAD - back + ext, :])
        off = POOL_PAD + back - 1
        acc = jnp.where(grp == m, dst_ref[off:off + n, :], acc)
        src_ref, dst_ref = dst_ref, src_ref
    d = acc * ic_ref[...] - x
    o_ref[...] = jnp.dot(d.astype(BF16), w_ref[...], preferred_element_type=F32) * sc_ref[...]


def _pool_inv_count(seq_len):
    t = np.arange(seq_len)[:, None]
    half = (np.asarray(POOL_SIZES) // 2)[np.arange(BR_W) // POOL_GD][None, :]
    cnt = np.minimum(t + half, seq_len) - np.maximum(t - half, 0)
    return jnp.asarray(1.0 / cnt, F32)


def _pool_specs(seq_len):
    return ([pl.BlockSpec((seq_len, BR_W), lambda b: (b, 0)), pl.BlockSpec((BR_W, BR_W), lambda b: (0, 0)),
             pl.BlockSpec((1, BR_W), lambda b: (0, 0)), pl.BlockSpec((seq_len, BR_W), lambda b: (0, 0))],
            [pltpu.VMEM((seq_len + 2 * POOL_PAD, BR_W), F32)] * 2)


def _merge_kernel(x_ref, mod_ref, g_ref, a_ref, bonus_ref, of_ref, ob_ref, yc_ref, yd_ref,
                  wz_ref, wmg_ref, ws_ref, bs_ref, lng_ref, lnb_ref, ones_ref, wup_ref, wo_ref, gf_ref,
                  o_ref, *, lat, final):
    i = pl.program_id(0)
    m = mod_ref[pl.ds(_mod_row(i, TM_MERGE, lat), 1), :]
    x = x_ref[...]
    hb = _norm_mod(x, g_ref[...], m).astype(BF16)

    a = a_ref[...]
    a_u, a_v = a[:, :BR_W], a[:, BR_W:]
    grp = lax.broadcasted_iota(jnp.int32, (1, BR_W), 1) // A_GD
    svs = []
    for c in range(TM_MERGE // CHUNK):
        vc = a_v[c * CHUNK:(c + 1) * CHUNK, :]
        sv = bs_ref[...]
        for g in range(A_GROUPS):
            sv = sv + jnp.dot(ws_ref[g], jnp.where(grp == g, vc, 0.0).astype(BF16), preferred_element_type=F32)
        svs.append(sv)
    y_a = a_u * jnp.concatenate(svs, axis=0)

    ones = ones_ref[...]
    osum = of_ref[...] + ob_ref[...]
    mu = _seg_sum(osum, ones) * (1.0 / RW_HD)
    dev = osum - mu
    var = _seg_sum(dev * dev, ones) * (1.0 / RW_HD)
    y_b = dev * lax.rsqrt(var + GN_EPS) * lng_ref[...] + lnb_ref[...] + bonus_ref[...]

    merged = jnp.zeros((TM_MERGE, D_MODEL), F32)
    z_all = jnp.dot(hb, wz_ref[0], preferred_element_type=F32)
    for n, y in enumerate((y_a, y_b, yc_ref[...], yd_ref[...])):
        z = z_all[:, n * BR_W:(n + 1) * BR_W]
        ys = y * (z * _sigmoid(z))
        up = jnp.dot(ys.astype(BF16), wup_ref[n], preferred_element_type=F32)
        mg = jnp.dot(hb, wmg_ref[0, :, n * D_MODEL:(n + 1) * D_MODEL], preferred_element_type=F32)
        merged = merged + _sigmoid(mg) * up
    gate = m[:, 2 * D_MODEL:]
    out = x + gate * jnp.dot(merged.astype(BF16), wo_ref[...], preferred_element_type=F32)
    if final:
        ms = jnp.mean(out * out, axis=-1, keepdims=True)
        out = out * lax.rsqrt(ms + NORM_EPS) * gf_ref[...]
    o_ref[...] = out


def _merge(x, mod_l, g, a, bonus, o_f, o_b, y_c, y_d, w_in_b, layer, ws, bs_tile, ln_g, ln_b, ones_bd, wup, wo,
           g_final, lat, final):
    row = lambda w: pl.BlockSpec((TM_MERGE, w), lambda i: (i, 0))
    full2 = lambda s: pl.BlockSpec(s, lambda i: (0, 0))
    z0, z1 = IN_SMALL, IN_SMALL + N_BRANCH * BR_W
    cols = lambda c0, w: pl.BlockSpec((pl.Element(1), pl.Element(D_MODEL), pl.Element(w)), lambda i: (layer, 0, c0))
    return pl.pallas_call(
        functools.partial(_merge_kernel, lat=lat, final=final),
        grid=(N_GRP // TM_MERGE,),
        in_specs=[row(D_MODEL), full2((8, 3 * D_MODEL)), full2((1, D_MODEL)), row(2 * BR_W),
                  row(BR_W), row(BR_W), row(BR_W), row(BR_W), row(BR_W),
                  cols(z0, N_BRANCH * BR_W), cols(z1, N_BRANCH * D_MODEL),
                  pl.BlockSpec((None, A_GROUPS, CHUNK, CHUNK), lambda i: (layer, 0, 0, 0)), full2((CHUNK, BR_W)),
                  full2((1, BR_W)), full2((1, BR_W)), full2((BR_W, BR_W)),
                  pl.BlockSpec((None, N_BRANCH, BR_W, D_MODEL), lambda i: (layer, 0, 0, 0)),
                  pl.BlockSpec((None, D_MODEL, D_MODEL), lambda i: (layer, 0, 0)), full2((1, D_MODEL))],
        out_specs=row(D_MODEL),
        out_shape=jax.ShapeDtypeStruct((N_GRP, D_MODEL), F32),
        compiler_params=_cparams(("arbitrary",)),
        name=("merge_lat" if lat else "merge_ctx") + ("_final" if final else ""),
    )(x, mod_l, g.reshape(1, D_MODEL), a, bonus, o_f, o_b, y_c, y_d, w_in_b, w_in_b, ws, bs_tile,
      ln_g.reshape(1, BR_W), ln_b.reshape(1, BR_W), ones_bd, wup, wo, g_final.reshape(1, D_MODEL))


def _block_diag(blocks):
    n, r, c = blocks.shape
    eye = jnp.eye(n, dtype=blocks.dtype)
    return (eye[:, None, :, None] * blocks[:, :, None, :]).reshape(n * r, n * c)


def kernel(x_prompt, x_sample, cache_k, cache_v, state_rwkv, c, c_ctx, w_mod, b_mod, g_norm, w_in, w_s, b_s,
           rw_w0, rw_w_up, rw_a0, rw_a_up, rw_k_k, rw_k_a, rw_r_k, rw_ln_g, rw_ln_b, att_sink, pool_w,
           pool_scale, w_up, w_o, g_final):
    xs = [x_prompt.reshape(N_CTX, D_MODEL), x_sample.reshape(N_LAT, D_MODEL)]
    cond = jnp.concatenate([c_ctx[None, :], c, jnp.zeros((8 - 1 - DEC_BATCH, D_MODEL), F32)], axis=0)
    mod = _modulation(cond, w_mod, b_mod)

    ones_bd = _block_diag(jnp.ones((RW_HEADS, RW_HD, RW_HD), BF16))
    tables = _rope_tables()
    kvw = ATT_KV_HEADS * ATT_HD
    cache_k4 = cache_k.reshape(DEC_BATCH, DEPTH, PAST_LEN, kvw)
    cache_v4 = cache_v.reshape(DEC_BATCH, DEPTH, PAST_LEN, kvw)
    w_in_b, w_s_b, w_up_b, w_o_b = (w.astype(BF16) for w in (w_in, w_s, w_up, w_o))

    new_k, new_v, new_s = [], [], []
    for l in range(DEPTH):
        pw = _block_diag(pool_w[l]).astype(BF16)
        bs_tile = jnp.repeat(b_s[l].T, A_GD, axis=1)
        final = l == DEPTH - 1
        for lat in (False, True):
            x = xs[lat]
            a, cq, ck, cv, dp, *pre, bonus = _inproj(
                x, mod[l], g_norm[l], w_in_b, l, rw_w0[l], rw_w_up[l], rw_a0[l], rw_a_up[l], rw_k_k[l], rw_k_a[l],
                rw_r_k[l], ones_bd, lat)
            r, v = pre[0], pre[1]
            pre = pre[2:] + [r, v]
            if lat:
                o_f, o_b, _ = _scan(pre, _lat_state(state_rwkv[:, l]), nb=DEC_BATCH, t_len=DEC_SEQ, name="scan_lat")
                y_c, y_d = _att_lat(att_sink[l], cq, ck, cv, cache_k4, cache_v4, l, tables, dp, pw, pool_scale[l])
            else:
                o_f, o_b, s_fin = _scan(pre, None, nb=BATCH, t_len=SEQ, name="scan_ctx")
                y_c, y_d = _att_ctx(att_sink[l], cq, ck, cv, dp, pw, pool_scale[l])
                new_k.append(ck.reshape(BATCH, SEQ, ATT_KV_HEADS, ATT_HD))
                new_v.append(cv.reshape(BATCH, SEQ, ATT_KV_HEADS, ATT_HD))
                new_s.append(s_fin.reshape(RW_HD, RW_HD, RW_HEADS, 2, BATCH).transpose(4, 3, 2, 1, 0))

            xs[lat] = _merge(x, mod[l], g_norm[l], a, bonus, o_f.reshape(N_GRP, BR_W), o_b.reshape(N_GRP, BR_W),
                             y_c, y_d, w_in_b, l, w_s_b, bs_tile, rw_ln_g[l], rw_ln_b[l], ones_bd, w_up_b, w_o_b,
                             g_final, lat, final)

    y_prompt = xs[0].reshape(BATCH, SEQ, D_MODEL)
    y_sample = xs[1].reshape(DEC_BATCH, DEC_SEQ, D_MODEL)
    return (y_prompt, y_sample, jnp.stack(new_k, axis=1), jnp.stack(new_v, axis=1), jnp.stack(new_s, axis=1))
```

```python
import functools

import numpy as np
import jax
import jax.numpy as jnp
from jax import lax
from jax.experimental import pallas as pl
from jax.experimental.pallas import tpu as pltpu

D_MODEL = 1024
BATCH = 16
SEQ = 256
DEPTH = 2
DEC_BATCH = 4
DEC_SEQ = 1024
PAST_LEN = 512
GRID_W = 64
N_BRANCH = 4
BR_W = D_MODEL // 4
CHUNK = 128
A_GROUPS = 4
A_GD = BR_W // A_GROUPS
RW_HD = 64
RW_HEADS = BR_W // RW_HD
DECAY_RANK = 64
ICL_RANK = 64
ATT_HD = 64
ATT_HEADS = BR_W // ATT_HD
ATT_KV_HEADS = 2
ATT_GROUP = ATT_HEADS // ATT_KV_HEADS
WINDOW = 128
ROPE_BASE = 10000.0
POOL_SIZES = (2, 4, 8, 16)
POOL_GD = BR_W // len(POOL_SIZES)
NORM_EPS = 1e-6
GN_EPS = 64e-5
NEG_INF = -1e30

N_CTX = BATCH * SEQ
N_LAT = DEC_BATCH * DEC_SEQ
N_GRP = N_CTX
assert N_LAT == N_GRP
IN_SMALL = 2176

F32 = jnp.float32
BF16 = jnp.bfloat16

V7X_VMEM_LIMIT = 56 * 1024 * 1024
LANES = 128

TM_IN = 512
TM_MERGE = 512
SCAN_TC = 32


def _cparams(sem):
    return pltpu.CompilerParams(dimension_semantics=sem, vmem_limit_bytes=V7X_VMEM_LIMIT)


def _mod_row(i, tm, lat):
    return 1 + i // (DEC_SEQ // tm) if lat else 0


def _norm_mod(x, g, m):
    ms = jnp.mean(x * x, axis=-1, keepdims=True)
    y = x * lax.rsqrt(ms + NORM_EPS) * g
    shift = m[:, :D_MODEL]
    scale = m[:, D_MODEL:2 * D_MODEL]
    return y * (1.0 + scale) + shift


def _sigmoid(x):
    return 1.0 / (1.0 + jnp.exp(-x))


def _split(x):
    hi = x.astype(BF16)
    return hi, (x - hi.astype(F32)).astype(BF16)


def _seg_sum(x, ones_bd):
    hi, lo = _split(x)
    return (jnp.dot(hi, ones_bd, preferred_element_type=F32)
            + jnp.dot(lo, ones_bd, preferred_element_type=F32))


def _dot_split(x, w):
    xh, xl = _split(x)
    wh, wl = _split(w)
    lhs = jnp.concatenate([xh, xl, xh], axis=1)
    rhs = jnp.concatenate([wh, wh, wl], axis=0)
    return jnp.dot(lhs, rhs, preferred_element_type=F32)


def _mod_kernel(c_ref, w_ref, b_ref, o_ref):
    cnd = c_ref[...]
    s = cnd * _sigmoid(cnd)
    o_ref[...] = jnp.dot(s.astype(BF16), w_ref[...].astype(BF16), preferred_element_type=F32) + b_ref[...]


def _modulation(cond, w_mod, b_mod):
    nt = 3 * D_MODEL // 1024
    return pl.pallas_call(
        _mod_kernel,
        grid=(DEPTH, nt),
        in_specs=[
            pl.BlockSpec((8, D_MODEL), lambda l, j: (0, 0)),
            pl.BlockSpec((None, D_MODEL, 1024), lambda l, j: (l, 0, j)),
            pl.BlockSpec((None, 1, 1024), lambda l, j: (l, 0, j)),
        ],
        out_specs=pl.BlockSpec((None, 8, 1024), lambda l, j: (l, 0, j)),
        out_shape=jax.ShapeDtypeStruct((DEPTH, 8, 3 * D_MODEL), F32),
        compiler_params=_cparams(("arbitrary", "arbitrary")),
        name="modulation",
    )(cond, w_mod, b_mod.reshape(DEPTH, 1, 3 * D_MODEL))


def _softplus(x):
    return jnp.maximum(x, 0.0) + jnp.log(1.0 + jnp.exp(-jnp.abs(x)))


def _inproj_kernel(x_ref, mod_ref, g_ref, w_ref, w0_ref, wup_ref, a0_ref, aup_ref, kkw_ref, ka_ref, rk_ref, ones_ref,
                   a_ref, q_ref, ck_ref, cv_ref, d_ref, r_ref, v_ref,
                   w0o, w1o, kka0o, kka1o, kd0o, kd1o, kko, bonuso, *, lat):
    i = pl.program_id(0)
    m = mod_ref[pl.ds(_mod_row(i, TM_IN, lat), 1), :]
    h = _norm_mod(x_ref[...], g_ref[...], m)
    p = jnp.dot(h.astype(BF16), w_ref[0], preferred_element_type=F32)
    a_ref[...] = p[:, 0:512]
    q_ref[...] = p[:, 1408:1664]
    ck_ref[...] = p[:, 1664:1792]
    cv_ref[...] = p[:, 1792:1920]
    d_ref[...] = p[:, 1920:2176]

    r = p[:, 512:768]
    k = p[:, 768:1024]
    v = p[:, 1024:1280]
    wd_t = jnp.tanh(p[:, 1280:1344])
    ad = p[:, 1344:1408]
    r_ref[...] = r
    v_ref[...] = v
    ones = ones_ref[...]
    kk = k * kkw_ref[...]
    ss = _seg_sum(kk * kk, ones)
    kkn = kk * lax.rsqrt(ss + 1e-12)
    kko[...] = kkn
    bonus = jnp.zeros_like(v)
    for d, (wo, kkao, kdo) in enumerate(((w0o, kka0o, kd0o), (w1o, kka1o, kd1o))):
        pre = w0_ref[pl.ds(d, 1), :] + _dot_split(wd_t, wup_ref[d])
        w_log = -_softplus(-pre) - 0.5
        wo[...] = jnp.exp(-jnp.exp(w_log))
        a = _sigmoid(a0_ref[pl.ds(d, 1), :] + _dot_split(ad, aup_ref[d]))
        k_d = k * (1.0 + (a - 1.0) * ka_ref[...])
        kdo[...] = k_d
        kkao[...] = kkn * a
        bonus = bonus + _seg_sum(r * k_d * rk_ref[...], ones) * v
    bonuso[...] = bonus


def _inproj(x, mod_l, g, w_in_b, layer, w0, wup, a0, aup, k_k, k_a, r_k, ones_bd, lat):
    kvw = ATT_KV_HEADS * ATT_HD
    widths = (2 * BR_W, BR_W, kvw, kvw, BR_W) + (BR_W,) * 10
    full2 = lambda s: pl.BlockSpec(s, lambda i: (0, 0))
    return pl.pallas_call(
        functools.partial(_inproj_kernel, lat=lat),
        grid=(N_GRP // TM_IN,),
        in_specs=[
            pl.BlockSpec((TM_IN, D_MODEL), lambda i: (i, 0)),
            full2((8, 3 * D_MODEL)),
            full2((1, D_MODEL)),
            pl.BlockSpec((pl.Element(1), pl.Element(D_MODEL), pl.Element(IN_SMALL)), lambda i: (layer, 0, 0)),
            full2((2, BR_W)),
            pl.BlockSpec((2, DECAY_RANK, BR_W), lambda i: (0, 0, 0)),
            full2((2, BR_W)),
            pl.BlockSpec((2, ICL_RANK, BR_W), lambda i: (0, 0, 0)),
            full2((1, BR_W)), full2((1, BR_W)), full2((1, BR_W)),
            full2((BR_W, BR_W)),
        ],
        out_specs=[pl.BlockSpec((TM_IN, w), lambda i: (i, 0)) for w in widths],
        out_shape=[jax.ShapeDtypeStruct((N_GRP, w), F32) for w in widths],
        compiler_params=_cparams(("arbitrary",)),
        name="in_proj_lat" if lat else "in_proj_ctx",
    )(x, mod_l, g.reshape(1, D_MODEL), w_in_b, w0, wup, a0, aup,
      k_k.reshape(1, BR_W), k_a.reshape(1, BR_W), r_k.reshape(1, BR_W), ones_bd)


def _to_lanes(z0, z1, rep):
    lo = lax.broadcasted_iota(jnp.int32, (1, LANES), 1) < RW_HD
    parts = []
    for hp in range(RW_HEADS // 2):
        a = z0[:, hp * LANES:(hp + 1) * LANES]
        b = z1[:, hp * LANES:(hp + 1) * LANES]
        parts.append(jnp.where(lo, a, pltpu.roll(b, RW_HD, 1)))
        parts.append(jnp.where(lo, pltpu.roll(a, RW_HD, 1), b))
    return jnp.concatenate(parts * rep, axis=0).T


def _scan_kernel(*refs, nb, has_init):
    if has_init:
        s0_ref, refs = refs[0], refs[1:]
    (w0_ref, w1_ref, kka0_ref, kka1_ref, kd0_ref, kd1_ref, kkf_ref, kkb_ref, rf_ref, rb_ref, vf_ref, vb_ref,
     of_ref, ob_ref, st_ref, w_s, kk_s, kka_s, kd_s, r_s, v_s, o_s) = refs
    nk, vh, _ = st_ref.shape
    tc = w_s.shape[0]
    rep = RW_HD // vh
    nrec = LANES // rep
    lane = lax.broadcasted_iota(jnp.int32, (1, LANES), 1)
    grp = lane // nrec
    lo = lane < RW_HD

    def rows(f_ref, b_ref, j):
        return jnp.concatenate([f_ref[:, j, :], b_ref[:, tc - 1 - j, :]], axis=0)

    sources = ((w0_ref, w1_ref, w_s), (kkf_ref, kkb_ref, kk_s), (kka0_ref, kka1_ref, kka_s),
               (kd0_ref, kd1_ref, kd_s), (rf_ref, rb_ref, r_s))
    for j in range(0, tc, 2):
        for f_ref, b_ref, dst in sources:
            gt = _to_lanes(rows(f_ref, b_ref, j), rows(f_ref, b_ref, j + 1), rep)
            dst[j] = gt[0:RW_HD]
            dst[j + 1] = gt[RW_HD:2 * RW_HD]
        gt = _to_lanes(rows(vf_ref, vb_ref, j), rows(vf_ref, vb_ref, j + 1), rep)
        for s in range(2):
            blk = gt[s * RW_HD:(s + 1) * RW_HD]
            if rep > 1:
                blk = sum(jnp.where(grp == q, blk[q * vh:(q + 1) * vh], 0.0) for q in range(rep))
            v_s[j + s] = blk

    p = jnp.ones((nk, LANES), F32)
    for t in range(tc):
        kk_s[t] = kk_s[t] * p
        p = p * w_s[t]
        inv = 1.0 / p
        kka_s[t] = kka_s[t] * inv
        kd_s[t] = kd_s[t] * inv
        r_s[t] = r_s[t] * p
    w_s[0] = p

    @pl.when(pl.program_id(0) == 0)
    def _():
        if has_init:
            st_ref[...] = s0_ref[...]
        else:
            st_ref[...] = jnp.zeros(st_ref.shape, F32)

    sa0 = jnp.zeros((vh, LANES), F32)
    for k in range(nk):
        sa0 = sa0 + st_ref[k] * kk_s[0, pl.ds(k, 1), :]

    def step(t, sa):
        tn = jnp.minimum(t + 1, tc - 1)
        vt = v_s[t]
        o = jnp.zeros((vh, LANES), F32)
        san = jnp.zeros((vh, LANES), F32)
        for k in range(nk):
            new = st_ref[k] - kka_s[t, pl.ds(k, 1), :] * sa + kd_s[t, pl.ds(k, 1), :] * vt
            st_ref[k] = new
            o = o + new * r_s[t, pl.ds(k, 1), :]
            san = san + new * kk_s[tn, pl.ds(k, 1), :]
        o_s[t] = o
        return san

    lax.fori_loop(0, tc, step, sa0)
    for k in range(nk):
        st_ref[k] = st_ref[k] * w_s[0, pl.ds(k, 1), :]

    for j in range(0, tc, 2):
        tiles = []
        for s in range(2):
            o = o_s[j + s]
            if rep == 1:
                tiles.append(o)
            else:
                tiles.extend(jnp.where(grp == q, o, 0.0) for q in range(rep))
        mt = jnp.concatenate(tiles, axis=0).T
        sm = mt[0:nrec]
        for q in range(1, rep):
            sm = sm + mt[q * nrec:(q + 1) * nrec]
        rs = pltpu.roll(sm, RW_HD, 1)
        for hp in range(RW_HEADS // 2):
            e0, e1 = (2 * hp) * 2 * nb, (2 * hp + 1) * 2 * nb
            cur = jnp.where(lo, sm[e0:e0 + 2 * nb], rs[e1:e1 + 2 * nb])
            nxt = jnp.where(lo, rs[e0:e0 + 2 * nb], sm[e1:e1 + 2 * nb])
            cs = slice(hp * LANES, (hp + 1) * LANES)
            of_ref[:, j, cs] = cur[0:nb]
            of_ref[:, j + 1, cs] = nxt[0:nb]
            ob_ref[:, tc - 1 - j, cs] = cur[nb:2 * nb]
            ob_ref[:, tc - 2 - j, cs] = nxt[nb:2 * nb]


def _scan(pre, s0, *, nb, t_len, name):
    w0, w1, kka0, kka1, kd0, kd1, kk, r, v = (p.reshape(nb, t_len, BR_W) for p in pre)
    n_t = t_len // SCAN_TC
    vh = RW_HD * nb * 2 * RW_HEADS // LANES
    fwd = pl.BlockSpec((nb, SCAN_TC, BR_W), lambda i: (0, i, 0))
    bwd = pl.BlockSpec((nb, SCAN_TC, BR_W), lambda i: (0, n_t - 1 - i, 0))
    sspec = pl.BlockSpec((RW_HD, vh, LANES), lambda i: (0, 0, 0))
    has_init = s0 is not None
    args = ((s0,) if has_init else ()) + (w0, w1, kka0, kka1, kd0, kd1, kk, kk, r, r, v, v)
    in_specs = ([sspec] if has_init else []) + [fwd, bwd] * 6
    kbuf = pltpu.VMEM((SCAN_TC, RW_HD, LANES), F32)
    vbuf = pltpu.VMEM((SCAN_TC, vh, LANES), F32)
    return pl.pallas_call(
        functools.partial(_scan_kernel, nb=nb, has_init=has_init),
        grid=(n_t,),
        in_specs=in_specs,
        out_specs=[pl.BlockSpec((nb, SCAN_TC, BR_W), lambda i: (0, i, 0)),
                   pl.BlockSpec((nb, SCAN_TC, BR_W), lambda i: (0, n_t - 1 - i, 0)),
                   sspec],
        out_shape=[jax.ShapeDtypeStruct((nb, t_len, BR_W), F32),
                   jax.ShapeDtypeStruct((nb, t_len, BR_W), F32),
                   jax.ShapeDtypeStruct((RW_HD, vh, LANES), F32)],
        scratch_shapes=[kbuf] * 5 + [vbuf] * 2,
        compiler_params=_cparams(("arbitrary",)),
        name=name,
    )(*args)


LAT_VS = 4
LAT_VH = RW_HD // LAT_VS


def _lat_state(s):
    s6 = s.reshape(DEC_BATCH, 2, RW_HEADS, LAT_VS, LAT_VH, RW_HD)
    return s6.transpose(5, 4, 3, 2, 1, 0).reshape(RW_HD, LAT_VH, LANES)


def _sink_softmax_pv(ss, sks, vbs):
    ms = [jnp.maximum(jnp.max(s, axis=-1, keepdims=True), sk) for s, sk in zip(ss, sks)]
    ps = [jnp.exp(s - m) for s, m in zip(ss, ms)]
    dens = [jnp.sum(p, axis=-1, keepdims=True) + jnp.exp(sk - m) for p, sk, m in zip(ps, sks, ms)]
    return [jnp.dot(p.astype(BF16), vb, preferred_element_type=F32) / den for p, vb, den in zip(ps, vbs, dens)]


def _att_ctx_kernel(sink_ref, q_ref, k_ref, v_ref, dp_ref, pw_ref, psc_ref, pic_ref, o_ref, yd_ref, pa_ref, pb_ref):
    _pool_body(dp_ref, pw_ref, psc_ref, pic_ref, yd_ref, pa_ref, pb_ref)
    scale = ATT_HD ** -0.5
    q = q_ref[...]
    kb = k_ref[...].astype(BF16)
    vb = v_ref[...].astype(BF16)
    first = lax.broadcasted_iota(jnp.int32, (ATT_GROUP * SEQ, 1), 0) < SEQ
    ss, sks = [], []
    for kvh in range(ATT_KV_HEADS):
        ks = slice(kvh * ATT_HD, (kvh + 1) * ATT_HD)
        q2 = jnp.concatenate([q[:, hd * ATT_HD:(hd + 1) * ATT_HD]
                              for hd in (ATT_GROUP * kvh, ATT_GROUP * kvh + 1)], axis=0).astype(BF16)
        ss.append(lax.dot_general(q2, kb[:, ks], (((1,), (1,)), ((), ())), preferred_element_type=F32) * scale)
        sks.append(jnp.where(first, sink_ref[ATT_GROUP * kvh], sink_ref[ATT_GROUP * kvh + 1]))
    outs = []
    for o2 in _sink_softmax_pv(ss, sks, [vb[:, kvh * ATT_HD:(kvh + 1) * ATT_HD] for kvh in range(ATT_KV_HEADS)]):
        outs += [o2[:SEQ], o2[SEQ:]]
    o_ref[...] = jnp.concatenate(outs, axis=1)


def _att_ctx(sink, cq, ck, cv, dp, pw, pscale):
    kvw = ATT_KV_HEADS * ATT_HD
    pool_in, pool_scratch = _pool_specs(SEQ)
    return pl.pallas_call(
        _att_ctx_kernel,
        grid=(BATCH,),
        in_specs=[
            pl.BlockSpec(memory_space=pltpu.SMEM),
            pl.BlockSpec((SEQ, BR_W), lambda b: (b, 0)),
            pl.BlockSpec((SEQ, kvw), lambda b: (b, 0)),
            pl.BlockSpec((SEQ, kvw), lambda b: (b, 0)),
        ] + pool_in,
        out_specs=[pl.BlockSpec((SEQ, BR_W), lambda b: (b, 0))] * 2,
        out_shape=[jax.ShapeDtypeStruct((N_CTX, BR_W), F32)] * 2,
        scratch_shapes=pool_scratch,
        compiler_params=_cparams(("arbitrary",)),
        name="att_ctx",
    )(sink, cq, ck, cv, dp, pw, pscale.reshape(1, BR_W), _pool_inv_count(SEQ))


def _rope(x, cos, sin_signed):
    w = x.shape[1]
    lane = lax.broadcasted_iota(jnp.int32, (1, w), 1)
    first = (lane % 32) < 16
    swapped = jnp.where(first, pltpu.roll(x, w - 16, 1), pltpu.roll(x, 16, 1))
    return x * cos + swapped * sin_signed


def _att_lat_kernel(sink_ref, q_ref, k_ref, v_ref, ck_ref, cv_ref, cq_ref, sq_ref, ckk_ref, skk_ref,
                    dp_ref, pw_ref, psc_ref, pic_ref, o_ref, yd_ref, qs_ref, ks_ref, vs_ref, pa_ref, pb_ref):
    _pool_body(dp_ref, pw_ref, psc_ref, pic_ref, yd_ref, pa_ref, pb_ref)
    scale = ATT_HD ** -0.5
    nb = DEC_SEQ // CHUNK
    qs_ref[...] = _rope(q_ref[...], cq_ref[...], sq_ref[...]).astype(BF16)
    ks_ref[...] = _rope(k_ref[...], ckk_ref[...], skk_ref[...]).astype(BF16)
    vs_ref[...] = v_ref[...].astype(BF16)
    ckb = ck_ref[...].astype(BF16)
    cvb = cv_ref[...].astype(BF16)
    for n in range(nb):
        lo = max(n - 1, 0) * CHUNK
        hi = min(n + 2, nb) * CHUNK
        nloc = hi - lo
        ncol = nloc + PAST_LEN
        col = lax.broadcasted_iota(jnp.int32, (CHUNK, ncol), 1)
        row = lax.broadcasted_iota(jnp.int32, (CHUNK, ncol), 0)
        dist = jnp.abs((n * CHUNK + row) - (lo + col))
        valid = (col >= nloc) | (dist <= WINDOW)
        valid2 = jnp.concatenate([valid] * ATT_GROUP, axis=0)
        first = lax.broadcasted_iota(jnp.int32, (ATT_GROUP * CHUNK, 1), 0) < CHUNK
        outs, ss, vs_, sks = [], [], [], []
        for kvh in range(ATT_KV_HEADS):
            ks = slice(kvh * ATT_HD, (kvh + 1) * ATT_HD)
            q2 = jnp.concatenate([qs_ref[n * CHUNK:(n + 1) * CHUNK, hd * ATT_HD:(hd + 1) * ATT_HD]
                                  for hd in (ATT_GROUP * kvh, ATT_GROUP * kvh + 1)], axis=0)
            kall = jnp.concatenate([ks_ref[lo:hi, ks], ckb[:, ks]], axis=0)
            vs_.append(jnp.concatenate([vs_ref[lo:hi, ks], cvb[:, ks]], axis=0))
            s = lax.dot_general(q2, kall, (((1,), (1,)), ((), ())), preferred_element_type=F32) * scale
            ss.append(jnp.where(valid2, s, NEG_INF))
            sks.append(jnp.where(first, sink_ref[ATT_GROUP * kvh], sink_ref[ATT_GROUP * kvh + 1]))
        for o2 in _sink_softmax_pv(ss, sks, vs_):
            outs += [o2[:CHUNK], o2[CHUNK:]]
        o_ref[n * CHUNK:(n + 1) * CHUNK, :] = jnp.concatenate(outs, axis=1)


def _att_lat(sink, cq, ck, cv, cache_k, cache_v, layer, tables, dp, pw, pscale):
    kvw = ATT_KV_HEADS * ATT_HD
    pool_in, pool_scratch = _pool_specs(DEC_SEQ)
    cosq, sinq, cosk, sink_k = tables
    seq = lambda w: pl.BlockSpec((DEC_SEQ, w), lambda b: (b, 0))
    cache = pl.BlockSpec((None, None, PAST_LEN, kvw), lambda b: (b, layer, 0, 0))
    tab = lambda w: pl.BlockSpec((DEC_SEQ, w), lambda b: (0, 0))
    return pl.pallas_call(
        _att_lat_kernel,
        grid=(DEC_BATCH,),
        in_specs=[pl.BlockSpec(memory_space=pltpu.SMEM), seq(BR_W), seq(kvw), seq(kvw), cache, cache,
                  tab(BR_W), tab(BR_W), tab(kvw), tab(kvw)] + pool_in,
        out_specs=[pl.BlockSpec((DEC_SEQ, BR_W), lambda b: (b, 0))] * 2,
        out_shape=[jax.ShapeDtypeStruct((N_LAT, BR_W), F32)] * 2,
        scratch_shapes=[pltpu.VMEM((DEC_SEQ, BR_W), BF16), pltpu.VMEM((DEC_SEQ, kvw), BF16),
                        pltpu.VMEM((DEC_SEQ, kvw), BF16)] + pool_scratch,
        compiler_params=_cparams(("arbitrary",)),
        name="att_lat",
    )(sink, cq, ck, cv, cache_k, cache_v, cosq, sinq, cosk, sink_k, dp, pw, pscale.reshape(1, BR_W),
      _pool_inv_count(DEC_SEQ))


def _rope_tables():
    pos = np.arange(DEC_SEQ)
    lane = np.arange(ATT_HD)
    p = np.where(lane[None, :] < 32, (pos // GRID_W)[:, None], (pos % GRID_W)[:, None]).astype(np.float32)
    inv = (ROPE_BASE ** (-jnp.arange(0, 32, 2, dtype=F32) / 32))[lane % 16]
    ang = jnp.asarray(p) * inv[None, :]
    sign = jnp.asarray(np.where((lane % 32) < 16, -1.0, 1.0).astype(np.float32))
    cos, sin = jnp.cos(ang), jnp.sin(ang) * sign[None, :]
    return (jnp.tile(cos, (1, ATT_HEADS)), jnp.tile(sin, (1, ATT_HEADS)),
            jnp.tile(cos, (1, ATT_KV_HEADS)), jnp.tile(sin, (1, ATT_KV_HEADS)))


POOL_PAD = 8


def _pool_body(x_ref, w_ref, sc_ref, ic_ref, o_ref, a_ref, b_ref):
    n = x_ref.shape[0]
    ext = n + POOL_PAD
    x = x_ref[...]
    zeros = jnp.zeros((POOL_PAD, BR_W), F32)
    a_ref[0:POOL_PAD, :] = zeros
    b_ref[0:POOL_PAD, :] = zeros
    a_ref[POOL_PAD:POOL_PAD + n, :] = x
    a_ref[POOL_PAD + n:POOL_PAD + ext, :] = zeros
    grp = lax.broadcasted_iota(jnp.int32, (1, BR_W), 1) // POOL_GD
    acc = jnp.zeros((n, BR_W), F32)
    src_ref, dst_ref = a_ref, b_ref
    for m in range(len(POOL_SIZES)):
        back = 1 << m
        dst_ref[POOL_PAD:POOL_PAD + ext, :] = (src_ref[POOL_PAD:POOL_PAD + ext, :]
                                               + src_ref[POOL_PAD - back:POOL_PAD - back + ext, :])
        off = POOL_PAD + back - 1
        acc = jnp.where(grp == m, dst_ref[off:off + n, :], acc)
        src_ref, dst_ref = dst_ref, src_ref
    d = acc * ic_ref[...] - x
    o_ref[...] = jnp.dot(d.astype(BF16), w_ref[...], preferred_element_type=F32) * sc_ref[...]


def _pool_inv_count(seq_len):
    t = np.arange(seq_len)[:, None]
    half = (np.asarray(POOL_SIZES) // 2)[np.arange(BR_W) // POOL_GD][None, :]
    cnt = np.minimum(t + half, seq_len) - np.maximum(t - half, 0)
    return jnp.asarray(1.0 / cnt, F32)


def _pool_specs(seq_len):
    return ([pl.BlockSpec((seq_len, BR_W), lambda b: (b, 0)), pl.BlockSpec((BR_W, BR_W), lambda b: (0, 0)),
             pl.BlockSpec((1, BR_W), lambda b: (0, 0)), pl.BlockSpec((seq_len, BR_W), lambda b: (0, 0))],
            [pltpu.VMEM((seq_len + 2 * POOL_PAD, BR_W), F32)] * 2)


def _merge_kernel(x_ref, mod_ref, g_ref, a_ref, bonus_ref, of_ref, ob_ref, yc_ref, yd_ref,
                  wz_ref, wmg_ref, ws_ref, bs_ref, lng_ref, lnb_ref, ones_ref, wup_ref, wo_ref, gf_ref,
                  o_ref, *, lat, final):
    i = pl.program_id(0)
    m = mod_ref[pl.ds(_mod_row(i, TM_MERGE, lat), 1), :]
    x = x_ref[...]
    hb = _norm_mod(x, g_ref[...], m).astype(BF16)

    a = a_ref[...]
    a_u, a_v = a[:, :BR_W], a[:, BR_W:]
    grp = lax.broadcasted_iota(jnp.int32, (1, BR_W), 1) // A_GD
    svs = []
    for c in range(TM_MERGE // CHUNK):
        vc = a_v[c * CHUNK:(c + 1) * CHUNK, :]
        sv = bs_ref[...]
        for g in range(A_GROUPS):
            sv = sv + jnp.dot(ws_ref[g], jnp.where(grp == g, vc, 0.0).astype(BF16), preferred_element_type=F32)
        svs.append(sv)
    y_a = a_u * jnp.concatenate(svs, axis=0)

    ones = ones_ref[...]
    osum = of_ref[...] + ob_ref[...]
    mu = _seg_sum(osum, ones) * (1.0 / RW_HD)
    dev = osum - mu
    var = _seg_sum(dev * dev, ones) * (1.0 / RW_HD)
    y_b = dev * lax.rsqrt(var + GN_EPS) * lng_ref[...] + lnb_ref[...] + bonus_ref[...]

    merged = jnp.zeros((TM_MERGE, D_MODEL), F32)
    z_all = jnp.dot(hb, wz_ref[0], preferred_element_type=F32)
    for n, y in enumerate((y_a, y_b, yc_ref[...], yd_ref[...])):
        z = z_all[:, n * BR_W:(n + 1) * BR_W]
        ys = y * (z * _sigmoid(z))
        up = jnp.dot(ys.astype(BF16), wup_ref[n], preferred_element_type=F32)
        mg = jnp.dot(hb, wmg_ref[0, :, n * D_MODEL:(n + 1) * D_MODEL], preferred_element_type=F32)
        merged = merged + _sigmoid(mg) * up
    gate = m[:, 2 * D_MODEL:]
    out = x + gate * jnp.dot(merged.astype(BF16), wo_ref[...], preferred_element_type=F32)
    if final:
        ms = jnp.mean(out * out, axis=-1, keepdims=True)
        out = out * lax.rsqrt(ms + NORM_EPS) * gf_ref[...]
    o_ref[...] = out


def _merge(x, mod_l, g, a, bonus, o_f, o_b, y_c, y_d, w_in_b, layer, ws, bs_tile, ln_g, ln_b, ones_bd, wup, wo,
           g_final, lat, final):
    row = lambda w: pl.BlockSpec((TM_MERGE, w), lambda i: (i, 0))
    full2 = lambda s: pl.BlockSpec(s, lambda i: (0, 0))
    z0, z1 = IN_SMALL, IN_SMALL + N_BRANCH * BR_W
    cols = lambda c0, w: pl.BlockSpec((pl.Element(1), pl.Element(D_MODEL), pl.Element(w)), lambda i: (layer, 0, c0))
    return pl.pallas_call(
        functools.partial(_merge_kernel, lat=lat, final=final),
        grid=(N_GRP // TM_MERGE,),
        in_specs=[row(D_MODEL), full2((8, 3 * D_MODEL)), full2((1, D_MODEL)), row(2 * BR_W),
                  row(BR_W), row(BR_W), row(BR_W), row(BR_W), row(BR_W),
                  cols(z0, N_BRANCH * BR_W), cols(z1, N_BRANCH * D_MODEL),
                  pl.BlockSpec((None, A_GROUPS, CHUNK, CHUNK), lambda i: (layer, 0, 0, 0)), full2((CHUNK, BR_W)),
                  full2((1, BR_W)), full2((1, BR_W)), full2((BR_W, BR_W)),
                  pl.BlockSpec((None, N_BRANCH, BR_W, D_MODEL), lambda i: (layer, 0, 0, 0)),
                  pl.BlockSpec((None, D_MODEL, D_MODEL), lambda i: (layer, 0, 0)), full2((1, D_MODEL))],
        out_specs=row(D_MODEL),
        out_shape=jax.ShapeDtypeStruct((N_GRP, D_MODEL), F32),
        compiler_params=_cparams(("arbitrary",)),
        name=("merge_lat" if lat else "merge_ctx") + ("_final" if final else ""),
    )(x, mod_l, g.reshape(1, D_MODEL), a, bonus, o_f, o_b, y_c, y_d, w_in_b, w_in_b, ws, bs_tile,
      ln_g.reshape(1, BR_W), ln_b.reshape(1, BR_W), ones_bd, wup, wo, g_final.reshape(1, D_MODEL))


def _block_diag(blocks):
    n, r, c = blocks.shape
    eye = jnp.eye(n, dtype=blocks.dtype)
    return (eye[:, None, :, None] * blocks[:, :, None, :]).reshape(n * r, n * c)


def kernel(x_prompt, x_sample, cache_k, cache_v, state_rwkv, c, c_ctx, w_mod, b_mod, g_norm, w_in, w_s, b_s,
           rw_w0, rw_w_up, rw_a0, rw_a_up, rw_k_k, rw_k_a, rw_r_k, rw_ln_g, rw_ln_b, att_sink, pool_w,
           pool_scale, w_up, w_o, g_final):
    xs = [x_prompt.reshape(N_CTX, D_MODEL), x_sample.reshape(N_LAT, D_MODEL)]
    cond = jnp.concatenate([c_ctx[None, :], c, jnp.zeros((8 - 1 - DEC_BATCH, D_MODEL), F32)], axis=0)
    mod = _modulation(cond, w_mod, b_mod)

    ones_bd = _block_diag(jnp.ones((RW_HEADS, RW_HD, RW_HD), BF16))
    tables = _rope_tables()
    kvw = ATT_KV_HEADS * ATT_HD
    cache_k4 = cache_k.reshape(DEC_BATCH, DEPTH, PAST_LEN, kvw)
    cache_v4 = cache_v.reshape(DEC_BATCH, DEPTH, PAST_LEN, kvw)
    w_in_b, w_s_b, w_up_b, w_o_b = (w.astype(BF16) for w in (w_in, w_s, w_up, w_o))

    new_k, new_v, new_s = [], [], []
    for l in range(DEPTH):
        pw = _block_diag(pool_w[l]).astype(BF16)
        bs_tile = jnp.repeat(b_s[l].T, A_GD, axis=1)
        final = l == DEPTH - 1
        for lat in (False, True):
            x = xs[lat]
            a, cq, ck, cv, dp, *pre, bonus = _inproj(
                x, mod[l], g_norm[l], w_in_b, l, rw_w0[l], rw_w_up[l], rw_a0[l], rw_a_up[l], rw_k_k[l], rw_k_a[l],
                rw_r_k[l], ones_bd, lat)
            r, v = pre[0], pre[1]
            pre = pre[2:] + [r, v]
            if lat:
                o_f, o_b, _ = _scan(pre, _lat_state(state_rwkv[:, l]), nb=DEC_BATCH, t_len=DEC_SEQ, name="scan_lat")
                y_c, y_d = _att_lat(att_sink[l], cq, ck, cv, cache_k4, cache_v4, l, tables, dp, pw, pool_scale[l])
            else:
                o_f, o_b, s_fin = _scan(pre, None, nb=BATCH, t_len=SEQ, name="scan_ctx")
                y_c, y_d = _att_ctx(att_sink[l], cq, ck, cv, dp, pw, pool_scale[l])
                new_k.append(ck.reshape(BATCH, SEQ, ATT_KV_HEADS, ATT_HD))
                new_v.append(cv.reshape(BATCH, SEQ, ATT_KV_HEADS, ATT_HD))
                new_s.append(s_fin.reshape(RW_HD, RW_HD, RW_HEADS, 2, BATCH).transpose(4, 3, 2, 1, 0))

            xs[lat] = _merge(x, mod[l], g_norm[l], a, bonus, o_f.reshape(N_GRP, BR_W), o_b.reshape(N_GRP, BR_W),
                             y_c, y_d, w_in_b, l, w_s_b, bs_tile, rw_ln_g[l], rw_ln_b[l], ones_bd, w_up_b, w_o_b,
                             g_final, lat, final)

    y_prompt = xs[0].reshape(BATCH, SEQ, D_MODEL)
    y_sample = xs[1].reshape(DEC_BATCH, DEC_SEQ, D_MODEL)
    return (y_prompt, y_sample, jnp.stack(new_k, axis=1), jnp.stack(new_v, axis=1), jnp.stack(new_s, axis=1))
```

```python
import functools

import numpy as np
import jax
import jax.numpy as jnp
from jax import lax
from jax.experimental import pallas as pl
from jax.experimental.pallas import tpu as pltpu

D_MODEL = 1024
BATCH = 16
SEQ = 256
DEPTH = 2
DEC_BATCH = 4
DEC_SEQ = 1024
PAST_LEN = 512
GRID_W = 64
N_BRANCH = 4
BR_W = D_MODEL // 4
CHUNK = 128
A_GROUPS = 4
A_GD = BR_W // A_GROUPS
RW_HD = 64
RW_HEADS = BR_W // RW_HD
DECAY_RANK = 64
ICL_RANK = 64
ATT_HD = 64
ATT_HEADS = BR_W // ATT_HD
ATT_KV_HEADS = 2
ATT_GROUP = ATT_HEADS // ATT_KV_HEADS
WINDOW = 128
ROPE_BASE = 10000.0
POOL_SIZES = (2, 4, 8, 16)
POOL_GD = BR_W // len(POOL_SIZES)
NORM_EPS = 1e-6
GN_EPS = 64e-5
NEG_INF = -1e30

N_CTX = BATCH * SEQ
N_LAT = DEC_BATCH * DEC_SEQ
N_GRP = N_CTX
assert N_LAT == N_GRP
IN_SMALL = 2176

F32 = jnp.float32
BF16 = jnp.bfloat16

V7X_VMEM_LIMIT = 56 * 1024 * 1024
LANES = 128

TM_IN = 512
TM_MERGE = 512
SCAN_TC = 32


def _cparams(sem):
    return pltpu.CompilerParams(dimension_semantics=sem, vmem_limit_bytes=V7X_VMEM_LIMIT)


def _mod_row(i, tm, lat):
    return 1 + i // (DEC_SEQ // tm) if lat else 0


def _norm_mod(x, g, m):
    ms = jnp.mean(x * x, axis=-1, keepdims=True)
    y = x * lax.rsqrt(ms + NORM_EPS) * g
    shift = m[:, :D_MODEL]
    scale = m[:, D_MODEL:2 * D_MODEL]
    return y * (1.0 + scale) + shift


def _sigmoid(x):
    return 1.0 / (1.0 + jnp.exp(-x))


def _split(x):
    hi = x.astype(BF16)
    return hi, (x - hi.astype(F32)).astype(BF16)


def _seg_sum(x, ones_bd):
    hi, lo = _split(x)
    return (jnp.dot(hi, ones_bd, preferred_element_type=F32)
            + jnp.dot(lo, ones_bd, preferred_element_type=F32))


def _dot_split(x, w):
    xh, xl = _split(x)
    wh, wl = _split(w)
    lhs = jnp.concatenate([xh, xl, xh], axis=1)
    rhs = jnp.concatenate([wh, wh, wl], axis=0)
    return jnp.dot(lhs, rhs, preferred_element_type=F32)


def _mod_kernel(c_ref, w_ref, b_ref, o_ref):
    cnd = c_ref[...]
    s = cnd * _sigmoid(cnd)
    o_ref[...] = jnp.dot(s.astype(BF16), w_ref[...].astype(BF16), preferred_element_type=F32) + b_ref[...]


def _modulation(cond, w_mod, b_mod):
    nt = 3 * D_MODEL // 1024
    return pl.pallas_call(
        _mod_kernel,
        grid=(DEPTH, nt),
        in_specs=[
            pl.BlockSpec((8, D_MODEL), lambda l, j: (0, 0)),
            pl.BlockSpec((None, D_MODEL, 1024), lambda l, j: (l, 0, j)),
            pl.BlockSpec((None, 1, 1024), lambda l, j: (l, 0, j)),
        ],
        out_specs=pl.BlockSpec((None, 8, 1024), lambda l, j: (l, 0, j)),
        out_shape=jax.ShapeDtypeStruct((DEPTH, 8, 3 * D_MODEL), F32),
        compiler_params=_cparams(("arbitrary", "arbitrary")),
        name="modulation",
    )(cond, w_mod, b_mod.reshape(DEPTH, 1, 3 * D_MODEL))


def _softplus(x):
    return jnp.maximum(x, 0.0) + jnp.log(1.0 + jnp.exp(-jnp.abs(x)))


def _inproj_kernel(x_ref, mod_ref, g_ref, w_ref, w0_ref, wup_ref, a0_ref, aup_ref, kkw_ref, ka_ref, rk_ref, ones_ref,
                   a_ref, q_ref, ck_ref, cv_ref, d_ref, r_ref, v_ref,
                   w0o, w1o, kka0o, kka1o, kd0o, kd1o, kko, bonuso, *, lat):
    i = pl.program_id(0)
    m = mod_ref[pl.ds(_mod_row(i, TM_IN, lat), 1), :]
    h = _norm_mod(x_ref[...], g_ref[...], m)
    p = jnp.dot(h.astype(BF16), w_ref[0], preferred_element_type=F32)
    a_ref[...] = p[:, 0:512]
    q_ref[...] = p[:, 1408:1664]
    ck_ref[...] = p[:, 1664:1792]
    cv_ref[...] = p[:, 1792:1920]
    d_ref[...] = p[:, 1920:2176]

    r = p[:, 512:768]
    k = p[:, 768:1024]
    v = p[:, 1024:1280]
    wd_t = jnp.tanh(p[:, 1280:1344])
    ad = p[:, 1344:1408]
    r_ref[...] = r
    v_ref[...] = v
    ones = ones_ref[...]
    kk = k * kkw_ref[...]
    ss = _seg_sum(kk * kk, ones)
    kkn = kk * lax.rsqrt(ss + 1e-12)
    kko[...] = kkn
    bonus = jnp.zeros_like(v)
    for d, (wo, kkao, kdo) in enumerate(((w0o, kka0o, kd0o), (w1o, kka1o, kd1o))):
        pre = w0_ref[pl.ds(d, 1), :] + _dot_split(wd_t, wup_ref[d])
        w_log = -_softplus(-pre) - 0.5
        wo[...] = jnp.exp(-jnp.exp(w_log))
        a = _sigmoid(a0_ref[pl.ds(d, 1), :] + _dot_split(ad, aup_ref[d]))
        k_d = k * (1.0 + (a - 1.0) * ka_ref[...])
        kdo[...] = k_d
        kkao[...] = kkn * a
        bonus = bonus + _seg_sum(r * k_d * rk_ref[...], ones) * v
    bonuso[...] = bonus


def _inproj(x, mod_l, g, w_in_b, layer, w0, wup, a0, aup, k_k, k_a, r_k, ones_bd, lat):
    kvw = ATT_KV_HEADS * ATT_HD
    widths = (2 * BR_W, BR_W, kvw, kvw, BR_W) + (BR_W,) * 10
    full2 = lambda s: pl.BlockSpec(s, lambda i: (0, 0))
    return pl.pallas_call(
        functools.partial(_inproj_kernel, lat=lat),
        grid=(N_GRP // TM_IN,),
        in_specs=[
            pl.BlockSpec((TM_IN, D_MODEL), lambda i: (i, 0)),
            full2((8, 3 * D_MODEL)),
            full2((1, D_MODEL)),
            pl.BlockSpec((pl.Element(1), pl.Element(D_MODEL), pl.Element(IN_SMALL)), lambda i: (layer, 0, 0)),
            full2((2, BR_W)),
            pl.BlockSpec((2, DECAY_RANK, BR_W), lambda i: (0, 0, 0)),
            full2((2, BR_W)),
            pl.BlockSpec((2, ICL_RANK, BR_W), lambda i: (0, 0, 0)),
            full2((1, BR_W)), full2((1, BR_W)), full2((1, BR_W)),
            full2((BR_W, BR_W)),
        ],
        out_specs=[pl.BlockSpec((TM_IN, w), lambda i: (i, 0)) for w in widths],
        out_shape=[jax.ShapeDtypeStruct((N_GRP, w), F32) for w in widths],
        compiler_params=_cparams(("arbitrary",)),
        name="in_proj_lat" if lat else "in_proj_ctx",
    )(x, mod_l, g.reshape(1, D_MODEL), w_in_b, w0, wup, a0, aup,
      k_k.reshape(1, BR_W), k_a.reshape(1, BR_W), r_k.reshape(1, BR_W), ones_bd)


def _to_lanes(z0, z1, rep):
    lo = lax.broadcasted_iota(jnp.int32, (1, LANES), 1) < RW_HD
    parts = []
    for hp in range(RW_HEADS // 2):
        a = z0[:, hp * LANES:(hp + 1) * LANES]
        b = z1[:, hp * LANES:(hp + 1) * LANES]
        parts.append(jnp.where(lo, a, pltpu.roll(b, RW_HD, 1)))
        parts.append(jnp.where(lo, pltpu.roll(a, RW_HD, 1), b))
    return jnp.concatenate(parts * rep, axis=0).T


def _scan_kernel(*refs, nb, has_init, lanes_in, n_side):
    if has_init:
        s0_ref, refs = refs[0], refs[1:]
    side_in, refs = refs[:2 * n_side], refs[2 * n_side:]
    n_in = 6 if lanes_in else 12
    data_in, refs = refs[:n_in], refs[n_in:]
    of_ref, ob_ref, st_ref = refs[:3]
    side_out = refs[3:3 + n_side]
    w_s, kk_s, kka_s, kd_s, r_s, v_s, o_s = refs[3 + n_side:]
    nk, vh, _ = st_ref.shape
    tc = w_s.shape[0]
    rep = RW_HD // vh
    nrec = LANES // rep
    lane = lax.broadcasted_iota(jnp.int32, (1, LANES), 1)
    grp = lane // nrec
    lo = lane < RW_HD

    def rows(f_ref, b_ref, j, n_steps):
        return jnp.concatenate([f_ref[:, j, :], b_ref[:, n_steps - 1 - j, :]], axis=0)

    if lanes_in:
        w_in, kk_in, kka_in, kd_in, r_in, v_in = data_in
    else:
        w_in, kk_in, kka_in, kd_in, r_in, v_in = w_s, kk_s, kka_s, kd_s, r_s, v_s
        for j in range(0, tc, 2):
            for si, dst in enumerate((w_s, kk_s, kka_s, kd_s, r_s, v_s)):
                f_ref, b_ref = data_in[2 * si], data_in[2 * si + 1]
                gt = _to_lanes(rows(f_ref, b_ref, j, tc), rows(f_ref, b_ref, j + 1, tc), rep)
                for s in range(2):
                    blk = gt[s * RW_HD:(s + 1) * RW_HD]
                    if dst is v_s and rep > 1:
                        blk = sum(jnp.where(grp == q, blk[q * vh:(q + 1) * vh], 0.0) for q in range(rep))
                    dst[j + s] = blk

    p = jnp.ones((nk, LANES), F32)
    for t in range(tc):
        kk_s[t] = kk_in[t] * p
        p = p * w_in[t]
        inv = 1.0 / p
        kka_s[t] = kka_in[t] * inv
        kd_s[t] = kd_in[t] * inv
        r_s[t] = r_in[t] * p
    w_s[0] = p

    @pl.when(pl.program_id(0) == 0)
    def _():
        if has_init:
            st_ref[...] = s0_ref[...]
        else:
            st_ref[...] = jnp.zeros(st_ref.shape, F32)

    sa0 = jnp.zeros((vh, LANES), F32)
    for k in range(nk):
        sa0 = sa0 + st_ref[k] * kk_s[0, pl.ds(k, 1), :]

    def step(t, sa):
        tn = jnp.minimum(t + 1, tc - 1)
        vt = v_in[t]
        o = jnp.zeros((vh, LANES), F32)
        san = jnp.zeros((vh, LANES), F32)
        for k in range(nk):
            new = st_ref[k] - kka_s[t, pl.ds(k, 1), :] * sa + kd_s[t, pl.ds(k, 1), :] * vt
            st_ref[k] = new
            o = o + new * r_s[t, pl.ds(k, 1), :]
            san = san + new * kk_s[tn, pl.ds(k, 1), :]
        o_s[t] = o
        return san

    if n_side:
        side_tc = side_in[0].shape[1]
        per = tc // (side_tc // 2)

        def steps_and_side_pair(q, sa):
            for u in range(per):
                sa = step(per * q + u, sa)
            j = 2 * q
            words = None
            for si in range(n_side):
                f_ref, b_ref = side_in[2 * si], side_in[2 * si + 1]
                gt = _to_lanes(rows(f_ref, b_ref, j, side_tc), rows(f_ref, b_ref, j + 1, side_tc), 1)
                side_out[si][j] = gt[0:RW_HD]
                side_out[si][j + 1] = gt[RW_HD:2 * RW_HD]
                bits = lax.bitcast_convert_type(gt[0:8], jnp.uint32)
                words = bits if words is None else words | bits
            zero = lax.shift_right_logical(lax.shift_right_logical(words, jnp.uint32(16)), jnp.uint32(16))
            return sa + lax.bitcast_convert_type(zero, F32)[0:1, :]

        lax.fori_loop(0, side_tc // 2, steps_and_side_pair, sa0)
    else:
        lax.fori_loop(0, tc, step, sa0)
    for k in range(nk):
        st_ref[k] = st_ref[k] * w_s[0, pl.ds(k, 1), :]

    for j in range(0, tc, 2):
        tiles = []
        for s in range(2):
            o = o_s[j + s]
            if rep == 1:
                tiles.append(o)
            else:
                tiles.extend(jnp.where(grp == q, o, 0.0) for q in range(rep))
        mt = jnp.concatenate(tiles, axis=0).T
        sm = mt[0:nrec]
        for q in range(1, rep):
            sm = sm + mt[q * nrec:(q + 1) * nrec]
        rs = pltpu.roll(sm, RW_HD, 1)
        for hp in range(RW_HEADS // 2):
            e0, e1 = (2 * hp) * 2 * nb, (2 * hp + 1) * 2 * nb
            cur = jnp.where(lo, sm[e0:e0 + 2 * nb], rs[e1:e1 + 2 * nb])
            nxt = jnp.where(lo, rs[e0:e0 + 2 * nb], sm[e1:e1 + 2 * nb])
            cs = slice(hp * LANES, (hp + 1) * LANES)
            of_ref[:, j, cs] = cur[0:nb]
            of_ref[:, j + 1, cs] = nxt[0:nb]
            ob_ref[:, tc - 1 - j, cs] = cur[nb:2 * nb]
            ob_ref[:, tc - 2 - j, cs] = nxt[nb:2 * nb]


def _fwd_bwd_args(pre, nb, t_len, tc, n_t):
    w0, w1, kka0, kka1, kd0, kd1, kk, r, v = (p.reshape(nb, t_len, BR_W) for p in pre)
    fwd = pl.BlockSpec((nb, tc, BR_W), lambda i: (0, i, 0))
    bwd = pl.BlockSpec((nb, tc, BR_W), lambda i: (0, n_t - 1 - i, 0))
    return (w0, w1, kk, kk, kka0, kka1, kd0, kd1, r, r, v, v), [fwd, bwd] * 6


def _scan(pre, s0, *, nb, t_len, name, lanes=None, side_pre=None, side_nb=None, side_len=None):
    n_t = t_len // SCAN_TC
    vh = RW_HD * nb * 2 * RW_HEADS // LANES
    sspec = pl.BlockSpec((RW_HD, vh, LANES), lambda i: (0, 0, 0))
    has_init = s0 is not None
    args, in_specs = ((s0,), [sspec]) if has_init else ((), [])
    n_side, side_out_specs, side_out_shape = 0, [], []
    if side_pre is not None:
        n_side = 6
        side_tc = side_len // n_t
        side_args, side_specs = _fwd_bwd_args(side_pre, side_nb, side_len, side_tc, n_t)
        args, in_specs = args + side_args, in_specs + side_specs
        side_out_specs = [pl.BlockSpec((side_tc, RW_HD, LANES), lambda i: (i, 0, 0))] * n_side
        side_out_shape = [jax.ShapeDtypeStruct((side_len, RW_HD, LANES), F32)] * n_side
    if lanes is None:
        own_args, own_specs = _fwd_bwd_args(pre, nb, t_len, SCAN_TC, n_t)
    else:
        own_args = tuple(lanes)
        own_specs = ([pl.BlockSpec((SCAN_TC, RW_HD, LANES), lambda i: (i, 0, 0))] * 5
                     + [pl.BlockSpec((SCAN_TC, vh, LANES), lambda i: (i, 0, 0))])
    kbuf = pltpu.VMEM((SCAN_TC, RW_HD, LANES), F32)
    vbuf = pltpu.VMEM((SCAN_TC, vh, LANES), F32)
    return pl.pallas_call(
        functools.partial(_scan_kernel, nb=nb, has_init=has_init, lanes_in=lanes is not None, n_side=n_side),
        grid=(n_t,),
        in_specs=in_specs + own_specs,
        out_specs=[pl.BlockSpec((nb, SCAN_TC, BR_W), lambda i: (0, i, 0)),
                   pl.BlockSpec((nb, SCAN_TC, BR_W), lambda i: (0, n_t - 1 - i, 0)),
                   sspec] + side_out_specs,
        out_shape=[jax.ShapeDtypeStruct((nb, t_len, BR_W), F32),
                   jax.ShapeDtypeStruct((nb, t_len, BR_W), F32),
                   jax.ShapeDtypeStruct((RW_HD, vh, LANES), F32)] + side_out_shape,
        scratch_shapes=[kbuf] * 5 + [vbuf] * 2,
        compiler_params=_cparams(("arbitrary",)),
        name=name,
    )(*args, *own_args)


LAT_VS = 4
LAT_VH = RW_HD // LAT_VS


def _lat_state(s):
    s6 = s.reshape(DEC_BATCH, 2, RW_HEADS, LAT_VS, LAT_VH, RW_HD)
    return s6.transpose(5, 4, 3, 2, 1, 0).reshape(RW_HD, LAT_VH, LANES)


def _sink_softmax_pv(ss, sks, vbs):
    ms = [jnp.maximum(jnp.max(s, axis=-1, keepdims=True), sk) for s, sk in zip(ss, sks)]
    ps = [jnp.exp(s - m) for s, m in zip(ss, ms)]
    dens = [jnp.sum(p, axis=-1, keepdims=True) + jnp.exp(sk - m) for p, sk, m in zip(ps, sks, ms)]
    return [jnp.dot(p.astype(BF16), vb, preferred_element_type=F32) / den for p, vb, den in zip(ps, vbs, dens)]


def _att_ctx_kernel(sink_ref, q_ref, k_ref, v_ref, dp_ref, pw_ref, psc_ref, pic_ref, o_ref, yd_ref, pa_ref, pb_ref):
    _pool_body(dp_ref, pw_ref, psc_ref, pic_ref, yd_ref, pa_ref, pb_ref)
    scale = ATT_HD ** -0.5
    q = q_ref[...]
    kb = k_ref[...].astype(BF16)
    vb = v_ref[...].astype(BF16)
    first = lax.broadcasted_iota(jnp.int32, (ATT_GROUP * SEQ, 1), 0) < SEQ
    ss, sks = [], []
    for kvh in range(ATT_KV_HEADS):
        ks = slice(kvh * ATT_HD, (kvh + 1) * ATT_HD)
        q2 = jnp.concatenate([q[:, hd * ATT_HD:(hd + 1) * ATT_HD]
                              for hd in (ATT_GROUP * kvh, ATT_GROUP * kvh + 1)], axis=0).astype(BF16)
        ss.append(lax.dot_general(q2, kb[:, ks], (((1,), (1,)), ((), ())), preferred_element_type=F32) * scale)
        sks.append(jnp.where(first, sink_ref[ATT_GROUP * kvh], sink_ref[ATT_GROUP * kvh + 1]))
    outs = []
    for o2 in _sink_softmax_pv(ss, sks, [vb[:, kvh * ATT_HD:(kvh + 1) * ATT_HD] for kvh in range(ATT_KV_HEADS)]):
        outs += [o2[:SEQ], o2[SEQ:]]
    o_ref[...] = jnp.concatenate(outs, axis=1)


def _att_ctx(sink, cq, ck, cv, dp, pw, pscale):
    kvw = ATT_KV_HEADS * ATT_HD
    pool_in, pool_scratch = _pool_specs(SEQ)
    return pl.pallas_call(
        _att_ctx_kernel,
        grid=(BATCH,),
        in_specs=[
            pl.BlockSpec(memory_space=pltpu.SMEM),
            pl.BlockSpec((SEQ, BR_W), lambda b: (b, 0)),
            pl.BlockSpec((SEQ, kvw), lambda b: (b, 0)),
            pl.BlockSpec((SEQ, kvw), lambda b: (b, 0)),
        ] + pool_in,
        out_specs=[pl.BlockSpec((SEQ, BR_W), lambda b: (b, 0))] * 2,
        out_shape=[jax.ShapeDtypeStruct((N_CTX, BR_W), F32)] * 2,
        scratch_shapes=pool_scratch,
        compiler_params=_cparams(("arbitrary",)),
        name="att_ctx",
    )(sink, cq, ck, cv, dp, pw, pscale.reshape(1, BR_W), _pool_inv_count(SEQ))


def _rope(x, cos, sin_signed):
    w = x.shape[1]
    lane = lax.broadcasted_iota(jnp.int32, (1, w), 1)
    first = (lane % 32) < 16
    swapped = jnp.where(first, pltpu.roll(x, w - 16, 1), pltpu.roll(x, 16, 1))
    return x * cos + swapped * sin_signed


def _att_lat_kernel(sink_ref, q_ref, k_ref, v_ref, ck_ref, cv_ref, cq_ref, sq_ref, ckk_ref, skk_ref,
                    dp_ref, pw_ref, psc_ref, pic_ref, o_ref, yd_ref, qs_ref, ks_ref, vs_ref, pa_ref, pb_ref):
    _pool_body(dp_ref, pw_ref, psc_ref, pic_ref, yd_ref, pa_ref, pb_ref)
    scale = ATT_HD ** -0.5
    nb = DEC_SEQ // CHUNK
    qs_ref[...] = _rope(q_ref[...], cq_ref[...], sq_ref[...]).astype(BF16)
    ks_ref[...] = _rope(k_ref[...], ckk_ref[...], skk_ref[...]).astype(BF16)
    vs_ref[...] = v_ref[...].astype(BF16)
    ckb = ck_ref[...].astype(BF16)
    cvb = cv_ref[...].astype(BF16)
    for n in range(nb):
        lo = max(n - 1, 0) * CHUNK
        hi = min(n + 2, nb) * CHUNK
        nloc = hi - lo
        ncol = nloc + PAST_LEN
        col = lax.broadcasted_iota(jnp.int32, (CHUNK, ncol), 1)
        row = lax.broadcasted_iota(jnp.int32, (CHUNK, ncol), 0)
        dist = jnp.abs((n * CHUNK + row) - (lo + col))
        valid = (col >= nloc) | (dist <= WINDOW)
        valid2 = jnp.concatenate([valid] * ATT_GROUP, axis=0)
        first = lax.broadcasted_iota(jnp.int32, (ATT_GROUP * CHUNK, 1), 0) < CHUNK
        outs, ss, vs_, sks = [], [], [], []
        for kvh in range(ATT_KV_HEADS):
            ks = slice(kvh * ATT_HD, (kvh + 1) * ATT_HD)
            q2 = jnp.concatenate([qs_ref[n * CHUNK:(n + 1) * CHUNK, hd * ATT_HD:(hd + 1) * ATT_HD]
                                  for hd in (ATT_GROUP * kvh, ATT_GROUP * kvh + 1)], axis=0)
            kall = jnp.concatenate([ks_ref[lo:hi, ks], ckb[:, ks]], axis=0)
            vs_.append(jnp.concatenate([vs_ref[lo:hi, ks], cvb[:, ks]], axis=0))
            s = lax.dot_general(q2, kall, (((1,), (1,)), ((), ())), preferred_element_type=F32) * scale
            ss.append(jnp.where(valid2, s, NEG_INF))
            sks.append(jnp.where(first, sink_ref[ATT_GROUP * kvh], sink_ref[ATT_GROUP * kvh + 1]))
        for o2 in _sink_softmax_pv(ss, sks, vs_):
            outs += [o2[:CHUNK], o2[CHUNK:]]
        o_ref[n * CHUNK:(n + 1) * CHUNK, :] = jnp.concatenate(outs, axis=1)


def _att_lat(sink, cq, ck, cv, cache_k, cache_v, layer, tables, dp, pw, pscale):
    kvw = ATT_KV_HEADS * ATT_HD
    pool_in, pool_scratch = _pool_specs(DEC_SEQ)
    cosq, sinq, cosk, sink_k = tables
    seq = lambda w: pl.BlockSpec((DEC_SEQ, w), lambda b: (b, 0))
    cache = pl.BlockSpec((None, None, PAST_LEN, kvw), lambda b: (b, layer, 0, 0))
    tab = lambda w: pl.BlockSpec((DEC_SEQ, w), lambda b: (0, 0))
    return pl.pallas_call(
        _att_lat_kernel,
        grid=(DEC_BATCH,),
        in_specs=[pl.BlockSpec(memory_space=pltpu.SMEM), seq(BR_W), seq(kvw), seq(kvw), cache, cache,
                  tab(BR_W), tab(BR_W), tab(kvw), tab(kvw)] + pool_in,
        out_specs=[pl.BlockSpec((DEC_SEQ, BR_W), lambda b: (b, 0))] * 2,
        out_shape=[jax.ShapeDtypeStruct((N_LAT, BR_W), F32)] * 2,
        scratch_shapes=[pltpu.VMEM((DEC_SEQ, BR_W), BF16), pltpu.VMEM((DEC_SEQ, kvw), BF16),
                        pltpu.VMEM((DEC_SEQ, kvw), BF16)] + pool_scratch,
        compiler_params=_cparams(("arbitrary",)),
        name="att_lat",
    )(sink, cq, ck, cv, cache_k, cache_v, cosq, sinq, cosk, sink_k, dp, pw, pscale.reshape(1, BR_W),
      _pool_inv_count(DEC_SEQ))


def _rope_tables():
    pos = np.arange(DEC_SEQ)
    lane = np.arange(ATT_HD)
    p = np.where(lane[None, :] < 32, (pos // GRID_W)[:, None], (pos % GRID_W)[:, None]).astype(np.float32)
    inv = (ROPE_BASE ** (-jnp.arange(0, 32, 2, dtype=F32) / 32))[lane % 16]
    ang = jnp.asarray(p) * inv[None, :]
    sign = jnp.asarray(np.where((lane % 32) < 16, -1.0, 1.0).astype(np.float32))
    cos, sin = jnp.cos(ang), jnp.sin(ang) * sign[None, :]
    return (jnp.tile(cos, (1, ATT_HEADS)), jnp.tile(sin, (1, ATT_HEADS)),
            jnp.tile(cos, (1, ATT_KV_HEADS)), jnp.tile(sin, (1, ATT_KV_HEADS)))


POOL_PAD = 8


def _pool_body(x_ref, w_ref, sc_ref, ic_ref, o_ref, a_ref, b_ref):
    n = x_ref.shape[0]
    ext = n + POOL_PAD
    x = x_ref[...]
    zeros = jnp.zeros((POOL_PAD, BR_W), F32)
    a_ref[0:POOL_PAD, :] = zeros
    b_ref[0:POOL_PAD, :] = zeros
    a_ref[POOL_PAD:POOL_PAD + n, :] = x
    a_ref[POOL_PAD + n:POOL_PAD + ext, :] = zeros
    grp = lax.broadcasted_iota(jnp.int32, (1, BR_W), 1) // POOL_GD
    acc = jnp.zeros((n, BR_W), F32)
    src_ref, dst_ref = a_ref, b_ref
    for m in range(len(POOL_SIZES)):
        back = 1 << m
        dst_ref[POOL_PAD:POOL_PAD + ext, :] = (src_ref[POOL_PAD:POOL_PAD + ext, :]
                                               + src_ref[POOL_PAD - back:POOL_PAD - back + ext, :])
        off = POOL_PAD + back - 1
        acc = jnp.where(grp == m, dst_ref[off:off + n, :], acc)
        src_ref, dst_ref = dst_ref, src_ref
    d = acc * ic_ref[...] - x
    o_ref[...] = jnp.dot(d.astype(BF16), w_ref[...], preferred_element_type=F32) * sc_ref[...]


def _pool_inv_count(seq_len):
    t = np.arange(seq_len)[:, None]
    half = (np.asarray(POOL_SIZES) // 2)[np.arange(BR_W) // POOL_GD][None, :]
    cnt = np.minimum(t + half, seq_len) - np.maximum(t - half, 0)
    return jnp.asarray(1.0 / cnt, F32)


def _pool_specs(seq_len):
    return ([pl.BlockSpec((seq_len, BR_W), lambda b: (b, 0)), pl.BlockSpec((BR_W, BR_W), lambda b: (0, 0)),
             pl.BlockSpec((1, BR_W), lambda b: (0, 0)), pl.BlockSpec((seq_len, BR_W), lambda b: (0, 0))],
            [pltpu.VMEM((seq_len + 2 * POOL_PAD, BR_W), F32)] * 2)


def _merge_kernel(x_ref, mod_ref, g_ref, a_ref, bonus_ref, of_ref, ob_ref, yc_ref, yd_ref,
                  wz_ref, wmg_ref, ws_ref, bs_ref, lng_ref, lnb_ref, ones_ref, wup_ref, wo_ref, gf_ref,
                  o_ref, *, lat, final):
    i = pl.program_id(0)
    m = mod_ref[pl.ds(_mod_row(i, TM_MERGE, lat), 1), :]
    x = x_ref[...]
    hb = _norm_mod(x, g_ref[...], m).astype(BF16)

    a = a_ref[...]
    a_u, a_v = a[:, :BR_W], a[:, BR_W:]
    grp = lax.broadcasted_iota(jnp.int32, (1, BR_W), 1) // A_GD
    svs = []
    for c in range(TM_MERGE // CHUNK):
        vc = a_v[c * CHUNK:(c + 1) * CHUNK, :]
        sv = bs_ref[...]
        for g in range(A_GROUPS):
            sv = sv + jnp.dot(ws_ref[g], jnp.where(grp == g, vc, 0.0).astype(BF16), preferred_element_type=F32)
        svs.append(sv)
    y_a = a_u * jnp.concatenate(svs, axis=0)

    ones = ones_ref[...]
    osum = of_ref[...] + ob_ref[...]
    mu = _seg_sum(osum, ones) * (1.0 / RW_HD)
    dev = osum - mu
    var = _seg_sum(dev * dev, ones) * (1.0 / RW_HD)
    y_b = dev * lax.rsqrt(var + GN_EPS) * lng_ref[...] + lnb_ref[...] + bonus_ref[...]

    merged = jnp.zeros((TM_MERGE, D_MODEL), F32)
    z_all = jnp.dot(hb, wz_ref[0], preferred_element_type=F32)
    for n, y in enumerate((y_a, y_b, yc_ref[...], yd_ref[...])):
        z = z_all[:, n * BR_W:(n + 1) * BR_W]
        ys = y * (z * _sigmoid(z))
        up = jnp.dot(ys.astype(BF16), wup_ref[n], preferred_element_type=F32)
        mg = jnp.dot(hb, wmg_ref[0, :, n * D_MODEL:(n + 1) * D_MODEL], preferred_element_type=F32)
        merged = merged + _sigmoid(mg) * up
    gate = m[:, 2 * D_MODEL:]
    out = x + gate * jnp.dot(merged.astype(BF16), wo_ref[...], preferred_element_type=F32)
    if final:
        ms = jnp.mean(out * out, axis=-1, keepdims=True)
        out = out * lax.rsqrt(ms + NORM_EPS) * gf_ref[...]
    o_ref[...] = out


def _merge(x, mod_l, g, a, bonus, o_f, o_b, y_c, y_d, w_in_b, layer, ws, bs_tile, ln_g, ln_b, ones_bd, wup, wo,
           g_final, lat, final):
    row = lambda w: pl.BlockSpec((TM_MERGE, w), lambda i: (i, 0))
    full2 = lambda s: pl.BlockSpec(s, lambda i: (0, 0))
    z0, z1 = IN_SMALL, IN_SMALL + N_BRANCH * BR_W
    cols = lambda c0, w: pl.BlockSpec((pl.Element(1), pl.Element(D_MODEL), pl.Element(w)), lambda i: (layer, 0, c0))
    return pl.pallas_call(
        functools.partial(_merge_kernel, lat=lat, final=final),
        grid=(N_GRP // TM_MERGE,),
        in_specs=[row(D_MODEL), full2((8, 3 * D_MODEL)), full2((1, D_MODEL)), row(2 * BR_W),
                  row(BR_W), row(BR_W), row(BR_W), row(BR_W), row(BR_W),
                  cols(z0, N_BRANCH * BR_W), cols(z1, N_BRANCH * D_MODEL),
                  pl.BlockSpec((None, A_GROUPS, CHUNK, CHUNK), lambda i: (layer, 0, 0, 0)), full2((CHUNK, BR_W)),
                  full2((1, BR_W)), full2((1, BR_W)), full2((BR_W, BR_W)),
                  pl.BlockSpec((None, N_BRANCH, BR_W, D_MODEL), lambda i: (layer, 0, 0, 0)),
                  pl.BlockSpec((None, D_MODEL, D_MODEL), lambda i: (layer, 0, 0)), full2((1, D_MODEL))],
        out_specs=row(D_MODEL),
        out_shape=jax.ShapeDtypeStruct((N_GRP, D_MODEL), F32),
        compiler_params=_cparams(("arbitrary",)),
        name=("merge_lat" if lat else "merge_ctx") + ("_final" if final else ""),
    )(x, mod_l, g.reshape(1, D_MODEL), a, bonus, o_f, o_b, y_c, y_d, w_in_b, w_in_b, ws, bs_tile,
      ln_g.reshape(1, BR_W), ln_b.reshape(1, BR_W), ones_bd, wup, wo, g_final.reshape(1, D_MODEL))


def _block_diag(blocks):
    n, r, c = blocks.shape
    eye = jnp.eye(n, dtype=blocks.dtype)
    return (eye[:, None, :, None] * blocks[:, :, None, :]).reshape(n * r, n * c)


def kernel(x_prompt, x_sample, cache_k, cache_v, state_rwkv, c, c_ctx, w_mod, b_mod, g_norm, w_in, w_s, b_s,
           rw_w0, rw_w_up, rw_a0, rw_a_up, rw_k_k, rw_k_a, rw_r_k, rw_ln_g, rw_ln_b, att_sink, pool_w,
           pool_scale, w_up, w_o, g_final):
    xs = [x_prompt.reshape(N_CTX, D_MODEL), x_sample.reshape(N_LAT, D_MODEL)]
    cond = jnp.concatenate([c_ctx[None, :], c, jnp.zeros((8 - 1 - DEC_BATCH, D_MODEL), F32)], axis=0)
    mod = _modulation(cond, w_mod, b_mod)

    ones_bd = _block_diag(jnp.ones((RW_HEADS, RW_HD, RW_HD), BF16))
    tables = _rope_tables()
    kvw = ATT_KV_HEADS * ATT_HD
    cache_k4 = cache_k.reshape(DEC_BATCH, DEPTH, PAST_LEN, kvw)
    cache_v4 = cache_v.reshape(DEC_BATCH, DEPTH, PAST_LEN, kvw)
    w_in_b, w_s_b, w_up_b, w_o_b = (w.astype(BF16) for w in (w_in, w_s, w_up, w_o))

    new_k, new_v, new_s = [], [], []
    for l in range(DEPTH):
        pw = _block_diag(pool_w[l]).astype(BF16)
        bs_tile = jnp.repeat(b_s[l].T, A_GD, axis=1)
        final = l == DEPTH - 1
        proj = []
        for lat in (False, True):
            a, cq, ck, cv, dp, r, v, *pre, bonus = _inproj(
                xs[lat], mod[l], g_norm[l], w_in_b, l, rw_w0[l], rw_w_up[l], rw_a0[l], rw_a_up[l], rw_k_k[l],
                rw_k_a[l], rw_r_k[l], ones_bd, lat)
            proj.append((a, cq, ck, cv, dp, pre + [r, v], bonus))

        of_l, ob_l, _, *ctx_lanes = _scan(proj[1][5], _lat_state(state_rwkv[:, l]), nb=DEC_BATCH, t_len=DEC_SEQ,
                                          name="scan_lat", side_pre=proj[0][5], side_nb=BATCH, side_len=SEQ)
        of_c, ob_c, s_fin = _scan(None, None, nb=BATCH, t_len=SEQ, name="scan_ctx", lanes=ctx_lanes)
        new_s.append(s_fin.reshape(RW_HD, RW_HD, RW_HEADS, 2, BATCH).transpose(4, 3, 2, 1, 0))
        scans = ((of_c, ob_c), (of_l, ob_l))

        for lat in (False, True):
            a, cq, ck, cv, dp, _, bonus = proj[lat]
            o_f, o_b = scans[lat]
            if lat:
                y_c, y_d = _att_lat(att_sink[l], cq, ck, cv, cache_k4, cache_v4, l, tables, dp, pw, pool_scale[l])
            else:
                y_c, y_d = _att_ctx(att_sink[l], cq, ck, cv, dp, pw, pool_scale[l])
                new_k.append(ck.reshape(BATCH, SEQ, ATT_KV_HEADS, ATT_HD))
                new_v.append(cv.reshape(BATCH, SEQ, ATT_KV_HEADS, ATT_HD))
            xs[lat] = _merge(xs[lat], mod[l], g_norm[l], a, bonus, o_f.reshape(N_GRP, BR_W),
                             o_b.reshape(N_GRP, BR_W), y_c, y_d, w_in_b, l, w_s_b, bs_tile, rw_ln_g[l], rw_ln_b[l],
                             ones_bd, w_up_b, w_o_b, g_final, lat, final)

    y_prompt = xs[0].reshape(BATCH, SEQ, D_MODEL)
    y_sample = xs[1].reshape(DEC_BATCH, DEC_SEQ, D_MODEL)
    return (y_prompt, y_sample, jnp.stack(new_k, axis=1), jnp.stack(new_v, axis=1), jnp.stack(new_s, axis=1))
```

```python
import functools

import numpy as np
import jax
import jax.numpy as jnp
from jax import lax
from jax.experimental import pallas as pl
from jax.experimental.pallas import tpu as pltpu

D_MODEL = 1024
BATCH = 16
SEQ = 256
DEPTH = 2
DEC_BATCH = 4
DEC_SEQ = 1024
PAST_LEN = 512
GRID_W = 64
N_BRANCH = 4
BR_W = D_MODEL // 4
CHUNK = 128
A_GROUPS = 4
A_GD = BR_W // A_GROUPS
RW_HD = 64
RW_HEADS = BR_W // RW_HD
DECAY_RANK = 64
ICL_RANK = 64
ATT_HD = 64
ATT_HEADS = BR_W // ATT_HD
ATT_KV_HEADS = 2
ATT_GROUP = ATT_HEADS // ATT_KV_HEADS
WINDOW = 128
ROPE_BASE = 10000.0
POOL_SIZES = (2, 4, 8, 16)
POOL_GD = BR_W // len(POOL_SIZES)
NORM_EPS = 1e-6
GN_EPS = 64e-5
NEG_INF = -1e30
DECAY_SCALE = float(np.exp(-0.5))

N_CTX = BATCH * SEQ
N_LAT = DEC_BATCH * DEC_SEQ
N_GRP = N_CTX
assert N_LAT == N_GRP
IN_SMALL = 2176

F32 = jnp.float32
BF16 = jnp.bfloat16

V7X_VMEM_LIMIT = 56 * 1024 * 1024
LANES = 128

TM_IN = 512
TM_MERGE = 512
SCAN_TC = 32


def _cparams(sem):
    return pltpu.CompilerParams(dimension_semantics=sem, vmem_limit_bytes=V7X_VMEM_LIMIT)


def _mod_row(i, tm, lat):
    return 1 + i // (DEC_SEQ // tm) if lat else 0


def _norm_mod(x, g, m):
    ms = jnp.mean(x * x, axis=-1, keepdims=True)
    y = x * lax.rsqrt(ms + NORM_EPS) * g
    shift = m[:, :D_MODEL]
    scale = m[:, D_MODEL:2 * D_MODEL]
    return y * (1.0 + scale) + shift


def _sigmoid(x):
    return 1.0 / (1.0 + jnp.exp(-x))


def _split(x):
    hi = x.astype(BF16)
    return hi, (x - hi.astype(F32)).astype(BF16)


def _seg_sum(x, ones_bd):
    hi, lo = _split(x)
    return (jnp.dot(hi, ones_bd, preferred_element_type=F32)
            + jnp.dot(lo, ones_bd, preferred_element_type=F32))


def _dot_split(x, w):
    xh, xl = _split(x)
    wh, wl = _split(w)
    lhs = jnp.concatenate([xh, xl, xh], axis=1)
    rhs = jnp.concatenate([wh, wh, wl], axis=0)
    return jnp.dot(lhs, rhs, preferred_element_type=F32)


def _mod_kernel(c_ref, w_ref, b_ref, o_ref):
    cnd = c_ref[...]
    s = cnd * _sigmoid(cnd)
    o_ref[...] = jnp.dot(s.astype(BF16), w_ref[...].astype(BF16), preferred_element_type=F32) + b_ref[...]


def _modulation(cond, w_mod, b_mod):
    nt = 3 * D_MODEL // 1024
    return pl.pallas_call(
        _mod_kernel,
        grid=(DEPTH, nt),
        in_specs=[
            pl.BlockSpec((8, D_MODEL), lambda l, j: (0, 0)),
            pl.BlockSpec((None, D_MODEL, 1024), lambda l, j: (l, 0, j)),
            pl.BlockSpec((None, 1, 1024), lambda l, j: (l, 0, j)),
        ],
        out_specs=pl.BlockSpec((None, 8, 1024), lambda l, j: (l, 0, j)),
        out_shape=jax.ShapeDtypeStruct((DEPTH, 8, 3 * D_MODEL), F32),
        compiler_params=_cparams(("arbitrary", "arbitrary")),
        name="modulation",
    )(cond, w_mod, b_mod.reshape(DEPTH, 1, 3 * D_MODEL))


def _inproj_kernel(x_ref, mod_ref, g_ref, w_ref, w0_ref, wup_ref, a0_ref, aup_ref, kkw_ref, ka_ref, rk_ref, ones_ref,
                   a_ref, q_ref, ck_ref, cv_ref, d_ref, r_ref, v_ref,
                   w0o, w1o, kka0o, kka1o, kd0o, kd1o, kko, bonuso, *, lat):
    i = pl.program_id(0)
    m = mod_ref[pl.ds(_mod_row(i, TM_IN, lat), 1), :]
    h = _norm_mod(x_ref[...], g_ref[...], m)
    p = jnp.dot(h.astype(BF16), w_ref[0], preferred_element_type=F32)
    a_ref[...] = p[:, 0:512]
    q_ref[...] = p[:, 1408:1664]
    ck_ref[...] = p[:, 1664:1792]
    cv_ref[...] = p[:, 1792:1920]
    d_ref[...] = p[:, 1920:2176]

    r = p[:, 512:768]
    k = p[:, 768:1024]
    v = p[:, 1024:1280]
    wd_t = jnp.tanh(p[:, 1280:1344])
    ad = p[:, 1344:1408]
    r_ref[...] = r
    v_ref[...] = v
    ones = ones_ref[...]
    kk = k * kkw_ref[...]
    ss = _seg_sum(kk * kk, ones)
    kkn = kk * lax.rsqrt(ss + 1e-12)
    kko[...] = kkn
    kd_sum = jnp.zeros_like(v)
    for d, (wo, kkao, kdo) in enumerate(((w0o, kka0o, kd0o), (w1o, kka1o, kd1o))):
        pre = w0_ref[pl.ds(d, 1), :] + _dot_split(wd_t, wup_ref[d])
        wo[...] = jnp.exp(-DECAY_SCALE * _sigmoid(pre))
        a = _sigmoid(a0_ref[pl.ds(d, 1), :] + _dot_split(ad, aup_ref[d]))
        k_d = k * (1.0 + (a - 1.0) * ka_ref[...])
        kdo[...] = k_d
        kkao[...] = kkn * a
        kd_sum = kd_sum + k_d
    bonuso[...] = _seg_sum(r * kd_sum * rk_ref[...], ones) * v


def _inproj(x, mod_l, g, w_in_b, layer, w0, wup, a0, aup, k_k, k_a, r_k, ones_bd, lat):
    kvw = ATT_KV_HEADS * ATT_HD
    widths = (2 * BR_W, BR_W, kvw, kvw, BR_W) + (BR_W,) * 10
    full2 = lambda s: pl.BlockSpec(s, lambda i: (0, 0))
    return pl.pallas_call(
        functools.partial(_inproj_kernel, lat=lat),
        grid=(N_GRP // TM_IN,),
        in_specs=[
            pl.BlockSpec((TM_IN, D_MODEL), lambda i: (i, 0)),
            full2((8, 3 * D_MODEL)),
            full2((1, D_MODEL)),
            pl.BlockSpec((pl.Element(1), pl.Element(D_MODEL), pl.Element(IN_SMALL)), lambda i: (layer, 0, 0)),
            full2((2, BR_W)),
            pl.BlockSpec((2, DECAY_RANK, BR_W), lambda i: (0, 0, 0)),
            full2((2, BR_W)),
            pl.BlockSpec((2, ICL_RANK, BR_W), lambda i: (0, 0, 0)),
            full2((1, BR_W)), full2((1, BR_W)), full2((1, BR_W)),
            full2((BR_W, BR_W)),
        ],
        out_specs=[pl.BlockSpec((TM_IN, w), lambda i: (i, 0)) for w in widths],
        out_shape=[jax.ShapeDtypeStruct((N_GRP, w), F32) for w in widths],
        compiler_params=_cparams(("arbitrary",)),
        name="in_proj_lat" if lat else "in_proj_ctx",
    )(x, mod_l, g.reshape(1, D_MODEL), w_in_b, w0, wup, a0, aup,
      k_k.reshape(1, BR_W), k_a.reshape(1, BR_W), r_k.reshape(1, BR_W), ones_bd)


def _to_lanes(z0, z1, rep):
    lo = lax.broadcasted_iota(jnp.int32, (1, LANES), 1) < RW_HD
    parts = []
    for hp in range(RW_HEADS // 2):
        a = z0[:, hp * LANES:(hp + 1) * LANES]
        b = z1[:, hp * LANES:(hp + 1) * LANES]
        parts.append(jnp.where(lo, a, pltpu.roll(b, RW_HD, 1)))
        parts.append(jnp.where(lo, pltpu.roll(a, RW_HD, 1), b))
    return jnp.concatenate(parts * rep, axis=0).T


def _scan_kernel(*refs, nb, has_init, lanes_in, n_side):
    if has_init:
        s0_ref, refs = refs[0], refs[1:]
    side_in, refs = refs[:2 * n_side], refs[2 * n_side:]
    n_in = 6 if lanes_in else 12
    data_in, refs = refs[:n_in], refs[n_in:]
    of_ref, ob_ref, st_ref = refs[:3]
    side_out = refs[3:3 + n_side]
    w_s, kk_s, kka_s, kd_s, r_s, v_s, o_s = refs[3 + n_side:]
    nk, vh, _ = st_ref.shape
    tc = w_s.shape[0]
    rep = RW_HD // vh
    nrec = LANES // rep
    lane = lax.broadcasted_iota(jnp.int32, (1, LANES), 1)
    grp = lane // nrec
    lo = lane < RW_HD

    def rows(f_ref, b_ref, j, n_steps):
        return jnp.concatenate([f_ref[:, j, :], b_ref[:, n_steps - 1 - j, :]], axis=0)

    if lanes_in:
        w_in, kk_in, kka_in, kd_in, r_in, v_in = data_in
    else:
        w_in, kk_in, kka_in, kd_in, r_in, v_in = w_s, kk_s, kka_s, kd_s, r_s, v_s
        for j in range(0, tc, 2):
            for si, dst in enumerate((w_s, kk_s, kka_s, kd_s, r_s, v_s)):
                f_ref, b_ref = data_in[2 * si], data_in[2 * si + 1]
                gt = _to_lanes(rows(f_ref, b_ref, j, tc), rows(f_ref, b_ref, j + 1, tc), rep)
                for s in range(2):
                    blk = gt[s * RW_HD:(s + 1) * RW_HD]
                    if dst is v_s and rep > 1:
                        blk = sum(jnp.where(grp == q, blk[q * vh:(q + 1) * vh], 0.0) for q in range(rep))
                    dst[j + s] = blk

    p = jnp.ones((nk, LANES), F32)
    for t in range(tc):
        kk_s[t] = kk_in[t] * p
        p = p * w_in[t]
        inv = 1.0 / p
        kka_s[t] = kka_in[t] * inv
        kd_s[t] = kd_in[t] * inv
        r_s[t] = r_in[t] * p
    w_s[0] = p

    @pl.when(pl.program_id(0) == 0)
    def _():
        if has_init:
            st_ref[...] = s0_ref[...]
        else:
            st_ref[...] = jnp.zeros(st_ref.shape, F32)

    sa0 = jnp.zeros((vh, LANES), F32)
    for k in range(nk):
        sa0 = sa0 + st_ref[k] * kk_s[0, pl.ds(k, 1), :]

    def step(t, sa):
        tn = jnp.minimum(t + 1, tc - 1)
        vt = v_in[t]
        o = jnp.zeros((vh, LANES), F32)
        san = jnp.zeros((vh, LANES), F32)
        for k in range(nk):
            new = st_ref[k] - kka_s[t, pl.ds(k, 1), :] * sa + kd_s[t, pl.ds(k, 1), :] * vt
            st_ref[k] = new
            o = o + new * r_s[t, pl.ds(k, 1), :]
            san = san + new * kk_s[tn, pl.ds(k, 1), :]
        o_s[t] = o
        return san

    if n_side:
        side_tc = side_in[0].shape[1]
        per = tc // (side_tc // 2)

        def steps_and_side_pair(q, sa):
            for u in range(per):
                sa = step(per * q + u, sa)
            j = 2 * q
            words = None
            for si in range(n_side):
                f_ref, b_ref = side_in[2 * si], side_in[2 * si + 1]
                gt = _to_lanes(rows(f_ref, b_ref, j, side_tc), rows(f_ref, b_ref, j + 1, side_tc), 1)
                side_out[si][j] = gt[0:RW_HD]
                side_out[si][j + 1] = gt[RW_HD:2 * RW_HD]
                bits = lax.bitcast_convert_type(gt[0:8], jnp.uint32)
                words = bits if words is None else words | bits
            zero = lax.shift_right_logical(lax.shift_right_logical(words, jnp.uint32(16)), jnp.uint32(16))
            return sa + lax.bitcast_convert_type(zero, F32)[0:1, :]

        lax.fori_loop(0, side_tc // 2, steps_and_side_pair, sa0)
    else:
        lax.fori_loop(0, tc, step, sa0)
    for k in range(nk):
        st_ref[k] = st_ref[k] * w_s[0, pl.ds(k, 1), :]

    for j in range(0, tc, 2):
        tiles = []
        for s in range(2):
            o = o_s[j + s]
            if rep == 1:
                tiles.append(o)
            else:
                tiles.extend(jnp.where(grp == q, o, 0.0) for q in range(rep))
        mt = jnp.concatenate(tiles, axis=0).T
        sm = mt[0:nrec]
        for q in range(1, rep):
            sm = sm + mt[q * nrec:(q + 1) * nrec]
        rs = pltpu.roll(sm, RW_HD, 1)
        for hp in range(RW_HEADS // 2):
            e0, e1 = (2 * hp) * 2 * nb, (2 * hp + 1) * 2 * nb
            cur = jnp.where(lo, sm[e0:e0 + 2 * nb], rs[e1:e1 + 2 * nb])
            nxt = jnp.where(lo, rs[e0:e0 + 2 * nb], sm[e1:e1 + 2 * nb])
            cs = slice(hp * LANES, (hp + 1) * LANES)
            of_ref[:, j, cs] = cur[0:nb]
            of_ref[:, j + 1, cs] = nxt[0:nb]
            ob_ref[:, tc - 1 - j, cs] = cur[nb:2 * nb]
            ob_ref[:, tc - 2 - j, cs] = nxt[nb:2 * nb]


def _fwd_bwd_args(pre, nb, t_len, tc, n_t):
    w0, w1, kka0, kka1, kd0, kd1, kk, r, v = (p.reshape(nb, t_len, BR_W) for p in pre)
    fwd = pl.BlockSpec((nb, tc, BR_W), lambda i: (0, i, 0))
    bwd = pl.BlockSpec((nb, tc, BR_W), lambda i: (0, n_t - 1 - i, 0))
    return (w0, w1, kk, kk, kka0, kka1, kd0, kd1, r, r, v, v), [fwd, bwd] * 6


def _scan(pre, s0, *, nb, t_len, name, lanes=None, side_pre=None, side_nb=None, side_len=None):
    n_t = t_len // SCAN_TC
    vh = RW_HD * nb * 2 * RW_HEADS // LANES
    sspec = pl.BlockSpec((RW_HD, vh, LANES), lambda i: (0, 0, 0))
    has_init = s0 is not None
    args, in_specs = ((s0,), [sspec]) if has_init else ((), [])
    n_side, side_out_specs, side_out_shape = 0, [], []
    if side_pre is not None:
        n_side = 6
        side_tc = side_len // n_t
        side_args, side_specs = _fwd_bwd_args(side_pre, side_nb, side_len, side_tc, n_t)
        args, in_specs = args + side_args, in_specs + side_specs
        side_out_specs = [pl.BlockSpec((side_tc, RW_HD, LANES), lambda i: (i, 0, 0))] * n_side
        side_out_shape = [jax.ShapeDtypeStruct((side_len, RW_HD, LANES), F32)] * n_side
    if lanes is None:
        own_args, own_specs = _fwd_bwd_args(pre, nb, t_len, SCAN_TC, n_t)
    else:
        own_args = tuple(lanes)
        own_specs = ([pl.BlockSpec((SCAN_TC, RW_HD, LANES), lambda i: (i, 0, 0))] * 5
                     + [pl.BlockSpec((SCAN_TC, vh, LANES), lambda i: (i, 0, 0))])
    kbuf = pltpu.VMEM((SCAN_TC, RW_HD, LANES), F32)
    vbuf = pltpu.VMEM((SCAN_TC, vh, LANES), F32)
    return pl.pallas_call(
        functools.partial(_scan_kernel, nb=nb, has_init=has_init, lanes_in=lanes is not None, n_side=n_side),
        grid=(n_t,),
        in_specs=in_specs + own_specs,
        out_specs=[pl.BlockSpec((nb, SCAN_TC, BR_W), lambda i: (0, i, 0)),
                   pl.BlockSpec((nb, SCAN_TC, BR_W), lambda i: (0, n_t - 1 - i, 0)),
                   sspec] + side_out_specs,
        out_shape=[jax.ShapeDtypeStruct((nb, t_len, BR_W), F32),
                   jax.ShapeDtypeStruct((nb, t_len, BR_W), F32),
                   jax.ShapeDtypeStruct((RW_HD, vh, LANES), F32)] + side_out_shape,
        scratch_shapes=[kbuf] * 5 + [vbuf] * 2,
        compiler_params=_cparams(("arbitrary",)),
        name=name,
    )(*args, *own_args)


LAT_VS = 4
LAT_VH = RW_HD // LAT_VS


def _lat_state(s):
    s6 = s.reshape(DEC_BATCH, 2, RW_HEADS, LAT_VS, LAT_VH, RW_HD)
    return s6.transpose(5, 4, 3, 2, 1, 0).reshape(RW_HD, LAT_VH, LANES)


def _sink_softmax_pv(ss, sks, vbs):
    ms = [jnp.maximum(jnp.max(s, axis=-1, keepdims=True), sk) for s, sk in zip(ss, sks)]
    ps = [jnp.exp(s - m) for s, m in zip(ss, ms)]
    dens = [jnp.sum(p, axis=-1, keepdims=True) + jnp.exp(sk - m) for p, sk, m in zip(ps, sks, ms)]
    return [jnp.dot(p.astype(BF16), vb, preferred_element_type=F32) / den for p, vb, den in zip(ps, vbs, dens)]


def _att_ctx_kernel(sink_ref, q_ref, k_ref, v_ref, dp_ref, pw_ref, psc_ref, pic_ref, o_ref, yd_ref, pa_ref, pb_ref):
    _pool_body(dp_ref, pw_ref, psc_ref, pic_ref, yd_ref, pa_ref, pb_ref)
    scale = ATT_HD ** -0.5
    q = q_ref[...]
    kb = k_ref[...].astype(BF16)
    vb = v_ref[...].astype(BF16)
    first = lax.broadcasted_iota(jnp.int32, (ATT_GROUP * SEQ, 1), 0) < SEQ
    ss, sks = [], []
    for kvh in range(ATT_KV_HEADS):
        ks = slice(kvh * ATT_HD, (kvh + 1) * ATT_HD)
        q2 = jnp.concatenate([q[:, hd * ATT_HD:(hd + 1) * ATT_HD]
                              for hd in (ATT_GROUP * kvh, ATT_GROUP * kvh + 1)], axis=0).astype(BF16)
        ss.append(lax.dot_general(q2, kb[:, ks], (((1,), (1,)), ((), ())), preferred_element_type=F32) * scale)
        sks.append(jnp.where(first, sink_ref[ATT_GROUP * kvh], sink_ref[ATT_GROUP * kvh + 1]))
    outs = []
    for o2 in _sink_softmax_pv(ss, sks, [vb[:, kvh * ATT_HD:(kvh + 1) * ATT_HD] for kvh in range(ATT_KV_HEADS)]):
        outs += [o2[:SEQ], o2[SEQ:]]
    o_ref[...] = jnp.concatenate(outs, axis=1)


def _att_ctx(sink, cq, ck, cv, dp, pw, pscale):
    kvw = ATT_KV_HEADS * ATT_HD
    pool_in, pool_scratch = _pool_specs(SEQ)
    return pl.pallas_call(
        _att_ctx_kernel,
        grid=(BATCH,),
        in_specs=[
            pl.BlockSpec(memory_space=pltpu.SMEM),
            pl.BlockSpec((SEQ, BR_W), lambda b: (b, 0)),
            pl.BlockSpec((SEQ, kvw), lambda b: (b, 0)),
            pl.BlockSpec((SEQ, kvw), lambda b: (b, 0)),
        ] + pool_in,
        out_specs=[pl.BlockSpec((SEQ, BR_W), lambda b: (b, 0))] * 2,
        out_shape=[jax.ShapeDtypeStruct((N_CTX, BR_W), F32)] * 2,
        scratch_shapes=pool_scratch,
        compiler_params=_cparams(("arbitrary",)),
        name="att_ctx",
    )(sink, cq, ck, cv, dp, pw, pscale.reshape(1, BR_W), _pool_inv_count(SEQ))


def _rope(x, cos, sin_signed):
    w = x.shape[1]
    lane = lax.broadcasted_iota(jnp.int32, (1, w), 1)
    first = (lane % 32) < 16
    swapped = jnp.where(first, pltpu.roll(x, w - 16, 1), pltpu.roll(x, 16, 1))
    return x * cos + swapped * sin_signed


def _att_lat_kernel(sink_ref, q_ref, k_ref, v_ref, ck_ref, cv_ref, cq_ref, sq_ref, ckk_ref, skk_ref,
                    dp_ref, pw_ref, psc_ref, pic_ref, o_ref, yd_ref, qs_ref, ks_ref, vs_ref, pa_ref, pb_ref):
    _pool_body(dp_ref, pw_ref, psc_ref, pic_ref, yd_ref, pa_ref, pb_ref)
    scale = ATT_HD ** -0.5
    nb = DEC_SEQ // CHUNK
    qs_ref[...] = _rope(q_ref[...], cq_ref[...], sq_ref[...]).astype(BF16)
    ks_ref[...] = _rope(k_ref[...], ckk_ref[...], skk_ref[...]).astype(BF16)
    vs_ref[...] = v_ref[...].astype(BF16)
    ckb = ck_ref[...].astype(BF16)
    cvb = cv_ref[...].astype(BF16)
    for n in range(nb):
        lo = max(n - 1, 0) * CHUNK
        hi = min(n + 2, nb) * CHUNK
        nloc = hi - lo
        ncol = nloc + PAST_LEN
        col = lax.broadcasted_iota(jnp.int32, (CHUNK, ncol), 1)
        row = lax.broadcasted_iota(jnp.int32, (CHUNK, ncol), 0)
        dist = jnp.abs((n * CHUNK + row) - (lo + col))
        valid = (col >= nloc) | (dist <= WINDOW)
        valid2 = jnp.concatenate([valid] * ATT_GROUP, axis=0)
        first = lax.broadcasted_iota(jnp.int32, (ATT_GROUP * CHUNK, 1), 0) < CHUNK
        outs, ss, vs_, sks = [], [], [], []
        for kvh in range(ATT_KV_HEADS):
            ks = slice(kvh * ATT_HD, (kvh + 1) * ATT_HD)
            q2 = jnp.concatenate([qs_ref[n * CHUNK:(n + 1) * CHUNK, hd * ATT_HD:(hd + 1) * ATT_HD]
                                  for hd in (ATT_GROUP * kvh, ATT_GROUP * kvh + 1)], axis=0)
            kall = jnp.concatenate([ks_ref[lo:hi, ks], ckb[:, ks]], axis=0)
            vs_.append(jnp.concatenate([vs_ref[lo:hi, ks], cvb[:, ks]], axis=0))
            s = lax.dot_general(q2, kall, (((1,), (1,)), ((), ())), preferred_element_type=F32) * scale
            ss.append(jnp.where(valid2, s, NEG_INF))
            sks.append(jnp.where(first, sink_ref[ATT_GROUP * kvh], sink_ref[ATT_GROUP * kvh + 1]))
        for o2 in _sink_softmax_pv(ss, sks, vs_):
            outs += [o2[:CHUNK], o2[CHUNK:]]
        o_ref[n * CHUNK:(n + 1) * CHUNK, :] = jnp.concatenate(outs, axis=1)


def _att_lat(sink, cq, ck, cv, cache_k, cache_v, layer, tables, dp, pw, pscale):
    kvw = ATT_KV_HEADS * ATT_HD
    pool_in, pool_scratch = _pool_specs(DEC_SEQ)
    cosq, sinq, cosk, sink_k = tables
    seq = lambda w: pl.BlockSpec((DEC_SEQ, w), lambda b: (b, 0))
    cache = pl.BlockSpec((None, None, PAST_LEN, kvw), lambda b: (b, layer, 0, 0))
    tab = lambda w: pl.BlockSpec((DEC_SEQ, w), lambda b: (0, 0))
    return pl.pallas_call(
        _att_lat_kernel,
        grid=(DEC_BATCH,),
        in_specs=[pl.BlockSpec(memory_space=pltpu.SMEM), seq(BR_W), seq(kvw), seq(kvw), cache, cache,
                  tab(BR_W), tab(BR_W), tab(kvw), tab(kvw)] + pool_in,
        out_specs=[pl.BlockSpec((DEC_SEQ, BR_W), lambda b: (b, 0))] * 2,
        out_shape=[jax.ShapeDtypeStruct((N_LAT, BR_W), F32)] * 2,
        scratch_shapes=[pltpu.VMEM((DEC_SEQ, BR_W), BF16), pltpu.VMEM((DEC_SEQ, kvw), BF16),
                        pltpu.VMEM((DEC_SEQ, kvw), BF16)] + pool_scratch,
        compiler_params=_cparams(("arbitrary",)),
        name="att_lat",
    )(sink, cq, ck, cv, cache_k, cache_v, cosq, sinq, cosk, sink_k, dp, pw, pscale.reshape(1, BR_W),
      _pool_inv_count(DEC_SEQ))


def _rope_tables():
    pos = np.arange(DEC_SEQ)
    lane = np.arange(ATT_HD)
    p = np.where(lane[None, :] < 32, (pos // GRID_W)[:, None], (pos % GRID_W)[:, None]).astype(np.float32)
    inv = (ROPE_BASE ** (-jnp.arange(0, 32, 2, dtype=F32) / 32))[lane % 16]
    ang = jnp.asarray(p) * inv[None, :]
    sign = jnp.asarray(np.where((lane % 32) < 16, -1.0, 1.0).astype(np.float32))
    cos, sin = jnp.cos(ang), jnp.sin(ang) * sign[None, :]
    return (jnp.tile(cos, (1, ATT_HEADS)), jnp.tile(sin, (1, ATT_HEADS)),
            jnp.tile(cos, (1, ATT_KV_HEADS)), jnp.tile(sin, (1, ATT_KV_HEADS)))


POOL_PAD = 8


def _pool_body(x_ref, w_ref, sc_ref, ic_ref, o_ref, a_ref, b_ref):
    n = x_ref.shape[0]
    ext = n + POOL_PAD
    x = x_ref[...]
    zeros = jnp.zeros((POOL_PAD, BR_W), F32)
    a_ref[0:POOL_PAD, :] = zeros
    b_ref[0:POOL_PAD, :] = zeros
    a_ref[POOL_PAD:POOL_PAD + n, :] = x
    a_ref[POOL_PAD + n:POOL_PAD + ext, :] = zeros
    grp = lax.broadcasted_iota(jnp.int32, (1, BR_W), 1) // POOL_GD
    acc = jnp.zeros((n, BR_W), F32)
    src_ref, dst_ref = a_ref, b_ref
    for m in range(len(POOL_SIZES)):
        back = 1 << m
        dst_ref[POOL_PAD:POOL_PAD + ext, :] = (src_ref[POOL_PAD:POOL_PAD + ext, :]
                                               + src_ref[POOL_PAD - back:POOL_PAD - back + ext, :])
        off = POOL_PAD + back - 1
        acc = jnp.where(grp == m, dst_ref[off:off + n, :], acc)
        src_ref, dst_ref = dst_ref, src_ref
    d = acc * ic_ref[...] - x
    o_ref[...] = jnp.dot(d.astype(BF16), w_ref[...], preferred_element_type=F32) * sc_ref[...]


def _pool_inv_count(seq_len):
    t = np.arange(seq_len)[:, None]
    half = (np.asarray(POOL_SIZES) // 2)[np.arange(BR_W) // POOL_GD][None, :]
    cnt = np.minimum(t + half, seq_len) - np.maximum(t - half, 0)
    return jnp.asarray(1.0 / cnt, F32)


def _pool_specs(seq_len):
    return ([pl.BlockSpec((seq_len, BR_W), lambda b: (b, 0)), pl.BlockSpec((BR_W, BR_W), lambda b: (0, 0)),
             pl.BlockSpec((1, BR_W), lambda b: (0, 0)), pl.BlockSpec((seq_len, BR_W), lambda b: (0, 0))],
            [pltpu.VMEM((seq_len + 2 * POOL_PAD, BR_W), F32)] * 2)


def _merge_kernel(x_ref, mod_ref, g_ref, a_ref, bonus_ref, of_ref, ob_ref, yc_ref, yd_ref,
                  wz_ref, wmg_ref, ws_ref, bs_ref, lng_ref, lnb_ref, ones_ref, wup_ref, wo_ref, gf_ref,
                  o_ref, *, lat, final):
    i = pl.program_id(0)
    m = mod_ref[pl.ds(_mod_row(i, TM_MERGE, lat), 1), :]
    x = x_ref[...]
    hb = _norm_mod(x, g_ref[...], m).astype(BF16)

    a = a_ref[...]
    a_u, a_v = a[:, :BR_W], a[:, BR_W:]
    grp = lax.broadcasted_iota(jnp.int32, (1, BR_W), 1) // A_GD
    svs = []
    for c in range(TM_MERGE // CHUNK):
        vc = a_v[c * CHUNK:(c + 1) * CHUNK, :]
        sv = bs_ref[...]
        for g in range(A_GROUPS):
            sv = sv + jnp.dot(ws_ref[g], jnp.where(grp == g, vc, 0.0).astype(BF16), preferred_element_type=F32)
        svs.append(sv)
    y_a = a_u * jnp.concatenate(svs, axis=0)

    ones = ones_ref[...]
    osum = of_ref[...] + ob_ref[...]
    mu = _seg_sum(osum, ones) * (1.0 / RW_HD)
    dev = osum - mu
    var = _seg_sum(dev * dev, ones) * (1.0 / RW_HD)
    y_b = dev * lax.rsqrt(var + GN_EPS) * lng_ref[...] + lnb_ref[...] + bonus_ref[...]

    merged = jnp.zeros((TM_MERGE, D_MODEL), F32)
    z_all = jnp.dot(hb, wz_ref[0], preferred_element_type=F32)
    for n, y in enumerate((y_a, y_b, yc_ref[...], yd_ref[...])):
        z = z_all[:, n * BR_W:(n + 1) * BR_W]
        ys = y * (z * _sigmoid(z))
        up = jnp.dot(ys.astype(BF16), wup_ref[n], preferred_element_type=F32)
        mg = jnp.dot(hb, wmg_ref[0, :, n * D_MODEL:(n + 1) * D_MODEL], preferred_element_type=F32)
        merged = merged + _sigmoid(mg) * up
    gate = m[:, 2 * D_MODEL:]
    out = x + gate * jnp.dot(merged.astype(BF16), wo_ref[...], preferred_element_type=F32)
    if final:
        ms = jnp.mean(out * out, axis=-1, keepdims=True)
        out = out * lax.rsqrt(ms + NORM_EPS) * gf_ref[...]
    o_ref[...] = out


def _merge(x, mod_l, g, a, bonus, o_f, o_b, y_c, y_d, w_in_b, layer, ws, bs_tile, ln_g, ln_b, ones_bd, wup, wo,
           g_final, lat, final):
    row = lambda w: pl.BlockSpec((TM_MERGE, w), lambda i: (i, 0))
    full2 = lambda s: pl.BlockSpec(s, lambda i: (0, 0))
    z0, z1 = IN_SMALL, IN_SMALL + N_BRANCH * BR_W
    cols = lambda c0, w: pl.BlockSpec((pl.Element(1), pl.Element(D_MODEL), pl.Element(w)), lambda i: (layer, 0, c0))
    return pl.pallas_call(
        functools.partial(_merge_kernel, lat=lat, final=final),
        grid=(N_GRP // TM_MERGE,),
        in_specs=[row(D_MODEL), full2((8, 3 * D_MODEL)), full2((1, D_MODEL)), row(2 * BR_W),
                  row(BR_W), row(BR_W), row(BR_W), row(BR_W), row(BR_W),
                  cols(z0, N_BRANCH * BR_W), cols(z1, N_BRANCH * D_MODEL),
                  pl.BlockSpec((None, A_GROUPS, CHUNK, CHUNK), lambda i: (layer, 0, 0, 0)), full2((CHUNK, BR_W)),
                  full2((1, BR_W)), full2((1, BR_W)), full2((BR_W, BR_W)),
                  pl.BlockSpec((None, N_BRANCH, BR_W, D_MODEL), lambda i: (layer, 0, 0, 0)),
                  pl.BlockSpec((None, D_MODEL, D_MODEL), lambda i: (layer, 0, 0)), full2((1, D_MODEL))],
        out_specs=row(D_MODEL),
        out_shape=jax.ShapeDtypeStruct((N_GRP, D_MODEL), F32),
        compiler_params=_cparams(("arbitrary",)),
        name=("merge_lat" if lat else "merge_ctx") + ("_final" if final else ""),
    )(x, mod_l, g.reshape(1, D_MODEL), a, bonus, o_f, o_b, y_c, y_d, w_in_b, w_in_b, ws, bs_tile,
      ln_g.reshape(1, BR_W), ln_b.reshape(1, BR_W), ones_bd, wup, wo, g_final.reshape(1, D_MODEL))


def _block_diag(blocks):
    n, r, c = blocks.shape
    eye = jnp.eye(n, dtype=blocks.dtype)
    return (eye[:, None, :, None] * blocks[:, :, None, :]).reshape(n * r, n * c)


def kernel(x_prompt, x_sample, cache_k, cache_v, state_rwkv, c, c_ctx, w_mod, b_mod, g_norm, w_in, w_s, b_s,
           rw_w0, rw_w_up, rw_a0, rw_a_up, rw_k_k, rw_k_a, rw_r_k, rw_ln_g, rw_ln_b, att_sink, pool_w,
           pool_scale, w_up, w_o, g_final):
    xs = [x_prompt.reshape(N_CTX, D_MODEL), x_sample.reshape(N_LAT, D_MODEL)]
    cond = jnp.concatenate([c_ctx[None, :], c, jnp.zeros((8 - 1 - DEC_BATCH, D_MODEL), F32)], axis=0)
    mod = _modulation(cond, w_mod, b_mod)

    ones_bd = _block_diag(jnp.ones((RW_HEADS, RW_HD, RW_HD), BF16))
    tables = _rope_tables()
    kvw = ATT_KV_HEADS * ATT_HD
    cache_k4 = cache_k.reshape(DEC_BATCH, DEPTH, PAST_LEN, kvw)
    cache_v4 = cache_v.reshape(DEC_BATCH, DEPTH, PAST_LEN, kvw)
    w_in_b, w_s_b, w_up_b, w_o_b = (w.astype(BF16) for w in (w_in, w_s, w_up, w_o))

    new_k, new_v, new_s = [], [], []
    for l in range(DEPTH):
        pw = _block_diag(pool_w[l]).astype(BF16)
        bs_tile = jnp.repeat(b_s[l].T, A_GD, axis=1)
        final = l == DEPTH - 1
        proj = []
        for lat in (False, True):
            a, cq, ck, cv, dp, r, v, *pre, bonus = _inproj(
                xs[lat], mod[l], g_norm[l], w_in_b, l, rw_w0[l], rw_w_up[l], rw_a0[l], rw_a_up[l], rw_k_k[l],
                rw_k_a[l], rw_r_k[l], ones_bd, lat)
            proj.append((a, cq, ck, cv, dp, pre + [r, v], bonus))

        of_l, ob_l, _, *ctx_lanes = _scan(proj[1][5], _lat_state(state_rwkv[:, l]), nb=DEC_BATCH, t_len=DEC_SEQ,
                                          name="scan_lat", side_pre=proj[0][5], side_nb=BATCH, side_len=SEQ)
        of_c, ob_c, s_fin = _scan(None, None, nb=BATCH, t_len=SEQ, name="scan_ctx", lanes=ctx_lanes)
        new_s.append(s_fin.reshape(RW_HD, RW_HD, RW_HEADS, 2, BATCH).transpose(4, 3, 2, 1, 0))
        scans = ((of_c, ob_c), (of_l, ob_l))

        for lat in (False, True):
            a, cq, ck, cv, dp, _, bonus = proj[lat]
            o_f, o_b = scans[lat]
            if lat:
                y_c, y_d = _att_lat(att_sink[l], cq, ck, cv, cache_k4, cache_v4, l, tables, dp, pw, pool_scale[l])
            else:
                y_c, y_d = _att_ctx(att_sink[l], cq, ck, cv, dp, pw, pool_scale[l])
                new_k.append(ck.reshape(BATCH, SEQ, ATT_KV_HEADS, ATT_HD))
                new_v.append(cv.reshape(BATCH, SEQ, ATT_KV_HEADS, ATT_HD))
            xs[lat] = _merge(xs[lat], mod[l], g_norm[l], a, bonus, o_f.reshape(N_GRP, BR_W),
                             o_b.reshape(N_GRP, BR_W), y_c, y_d, w_in_b, l, w_s_b, bs_tile, rw_ln_g[l], rw_ln_b[l],
                             ones_bd, w_up_b, w_o_b, g_final, lat, final)

    y_prompt = xs[0].reshape(BATCH, SEQ, D_MODEL)
    y_sample = xs[1].reshape(DEC_BATCH, DEC_SEQ, D_MODEL)
    return (y_prompt, y_sample, jnp.stack(new_k, axis=1), jnp.stack(new_v, axis=1), jnp.stack(new_s, axis=1))
```

```python
import functools

import numpy as np
import jax
import jax.numpy as jnp
from jax import lax
from jax.experimental import pallas as pl
from jax.experimental.pallas import tpu as pltpu

D_MODEL = 1024
BATCH = 16
SEQ = 256
DEPTH = 2
DEC_BATCH = 4
DEC_SEQ = 1024
PAST_LEN = 512
GRID_W = 64
N_BRANCH = 4
BR_W = D_MODEL // 4
CHUNK = 128
A_GROUPS = 4
A_GD = BR_W // A_GROUPS
RW_HD = 64
RW_HEADS = BR_W // RW_HD
DECAY_RANK = 64
ICL_RANK = 64
ATT_HD = 64
ATT_HEADS = BR_W // ATT_HD
ATT_KV_HEADS = 2
ATT_GROUP = ATT_HEADS // ATT_KV_HEADS
WINDOW = 128
ROPE_BASE = 10000.0
POOL_SIZES = (2, 4, 8, 16)
POOL_GD = BR_W // len(POOL_SIZES)
NORM_EPS = 1e-6
GN_EPS = 64e-5
NEG_INF = -1e30
DECAY_SCALE = float(np.exp(-0.5))

N_CTX = BATCH * SEQ
N_LAT = DEC_BATCH * DEC_SEQ
N_GRP = N_CTX
assert N_LAT == N_GRP
IN_SMALL = 2176

F32 = jnp.float32
BF16 = jnp.bfloat16

V7X_VMEM_LIMIT = 56 * 1024 * 1024
LANES = 128

TM_IN = 512
TM_MERGE = 512
SCAN_TC = 32


def _cparams(sem):
    return pltpu.CompilerParams(dimension_semantics=sem, vmem_limit_bytes=V7X_VMEM_LIMIT)


def _mod_row(i, tm, lat):
    return 1 + i // (DEC_SEQ // tm) if lat else 0


def _norm_mod(x, g, m):
    ms = jnp.mean(x * x, axis=-1, keepdims=True)
    y = x * lax.rsqrt(ms + NORM_EPS) * g
    shift = m[:, :D_MODEL]
    scale = m[:, D_MODEL:2 * D_MODEL]
    return y * (1.0 + scale) + shift


def _sigmoid(x):
    return 1.0 / (1.0 + jnp.exp(-x))


def _split(x):
    hi = x.astype(BF16)
    return hi, (x - hi.astype(F32)).astype(BF16)


def _seg_sum(x, ones_bd):
    hi, lo = _split(x)
    return (jnp.dot(hi, ones_bd, preferred_element_type=F32)
            + jnp.dot(lo, ones_bd, preferred_element_type=F32))


def _dot_split(x, w):
    xh, xl = _split(x)
    wh, wl = _split(w)
    lhs = jnp.concatenate([xh, xl, xh], axis=1)
    rhs = jnp.concatenate([wh, wh, wl], axis=0)
    return jnp.dot(lhs, rhs, preferred_element_type=F32)


def _mod_kernel(c_ref, w_ref, b_ref, o_ref):
    cnd = c_ref[...]
    s = cnd * _sigmoid(cnd)
    o_ref[...] = jnp.dot(s.astype(BF16), w_ref[...].astype(BF16), preferred_element_type=F32) + b_ref[...]


def _modulation(cond, w_mod, b_mod):
    nt = 3 * D_MODEL // 1024
    return pl.pallas_call(
        _mod_kernel,
        grid=(DEPTH, nt),
        in_specs=[
            pl.BlockSpec((8, D_MODEL), lambda l, j: (0, 0)),
            pl.BlockSpec((None, D_MODEL, 1024), lambda l, j: (l, 0, j)),
            pl.BlockSpec((None, 1, 1024), lambda l, j: (l, 0, j)),
        ],
        out_specs=pl.BlockSpec((None, 8, 1024), lambda l, j: (l, 0, j)),
        out_shape=jax.ShapeDtypeStruct((DEPTH, 8, 3 * D_MODEL), F32),
        compiler_params=_cparams(("arbitrary", "arbitrary")),
        name="modulation",
    )(cond, w_mod, b_mod.reshape(DEPTH, 1, 3 * D_MODEL))


def _inproj_kernel(x_ref, mod_ref, g_ref, w_ref, w0_ref, wup_ref, a0_ref, aup_ref, kkw_ref, ka_ref, rk_ref, ones_ref,
                   a_ref, q_ref, ck_ref, cv_ref, d_ref, r_ref, v_ref,
                   w0o, w1o, kka0o, kka1o, kd0o, kd1o, kko, bonuso, *, lat):
    i = pl.program_id(0)
    m = mod_ref[pl.ds(_mod_row(i, TM_IN, lat), 1), :]
    h = _norm_mod(x_ref[...], g_ref[...], m)
    p = jnp.dot(h.astype(BF16), w_ref[0], preferred_element_type=F32)
    a_ref[...] = p[:, 0:512]
    q_ref[...] = p[:, 1408:1664]
    ck_ref[...] = p[:, 1664:1792]
    cv_ref[...] = p[:, 1792:1920]
    d_ref[...] = p[:, 1920:2176]

    r = p[:, 512:768]
    k = p[:, 768:1024]
    v = p[:, 1024:1280]
    wd_t = jnp.tanh(p[:, 1280:1344])
    ad = p[:, 1344:1408]
    r_ref[...] = r
    v_ref[...] = v
    ones = ones_ref[...]
    kk = k * kkw_ref[...]
    ss = _seg_sum(kk * kk, ones)
    kkn = kk * lax.rsqrt(ss + 1e-12)
    kko[...] = kkn
    kd_sum = jnp.zeros_like(v)
    for d, (wo, kkao, kdo) in enumerate(((w0o, kka0o, kd0o), (w1o, kka1o, kd1o))):
        pre = w0_ref[pl.ds(d, 1), :] + _dot_split(wd_t, wup_ref[d])
        wo[...] = jnp.exp(-DECAY_SCALE * _sigmoid(pre))
        a = _sigmoid(a0_ref[pl.ds(d, 1), :] + _dot_split(ad, aup_ref[d]))
        k_d = k * (1.0 + (a - 1.0) * ka_ref[...])
        kdo[...] = k_d
        kkao[...] = kkn * a
        kd_sum = kd_sum + k_d
    bonuso[...] = _seg_sum(r * kd_sum * rk_ref[...], ones) * v


def _inproj(x, mod_l, g, w_in_b, layer, w0, wup, a0, aup, k_k, k_a, r_k, ones_bd, lat):
    kvw = ATT_KV_HEADS * ATT_HD
    widths = (2 * BR_W, BR_W, kvw, kvw, BR_W) + (BR_W,) * 10
    full2 = lambda s: pl.BlockSpec(s, lambda i: (0, 0))
    return pl.pallas_call(
        functools.partial(_inproj_kernel, lat=lat),
        grid=(N_GRP // TM_IN,),
        in_specs=[
            pl.BlockSpec((TM_IN, D_MODEL), lambda i: (i, 0)),
            full2((8, 3 * D_MODEL)),
            full2((1, D_MODEL)),
            pl.BlockSpec((pl.Element(1), pl.Element(D_MODEL), pl.Element(IN_SMALL)), lambda i: (layer, 0, 0)),
            full2((2, BR_W)),
            pl.BlockSpec((2, DECAY_RANK, BR_W), lambda i: (0, 0, 0)),
            full2((2, BR_W)),
            pl.BlockSpec((2, ICL_RANK, BR_W), lambda i: (0, 0, 0)),
            full2((1, BR_W)), full2((1, BR_W)), full2((1, BR_W)),
            full2((BR_W, BR_W)),
        ],
        out_specs=[pl.BlockSpec((TM_IN, w), lambda i: (i, 0)) for w in widths],
        out_shape=[jax.ShapeDtypeStruct((N_GRP, w), F32) for w in widths],
        compiler_params=_cparams(("arbitrary",)),
        name="in_proj_lat" if lat else "in_proj_ctx",
    )(x, mod_l, g.reshape(1, D_MODEL), w_in_b, w0, wup, a0, aup,
      k_k.reshape(1, BR_W), k_a.reshape(1, BR_W), r_k.reshape(1, BR_W), ones_bd)


def _to_lanes(z0, z1, rep):
    lo = lax.broadcasted_iota(jnp.int32, (1, LANES), 1) < RW_HD
    parts = []
    for hp in range(RW_HEADS // 2):
        a = z0[:, hp * LANES:(hp + 1) * LANES]
        b = z1[:, hp * LANES:(hp + 1) * LANES]
        parts.append(jnp.where(lo, a, pltpu.roll(b, RW_HD, 1)))
        parts.append(jnp.where(lo, pltpu.roll(a, RW_HD, 1), b))
    return jnp.concatenate(parts * rep, axis=0).T


def _scan_kernel(*refs, nb, has_init, lanes_in, n_side, emit_state):
    if has_init:
        s0_ref, refs = refs[0], refs[1:]
    side_in, refs = refs[:2 * n_side], refs[2 * n_side:]
    n_in = 6 if lanes_in else 12
    data_in, refs = refs[:n_in], refs[n_in:]
    of_ref, ob_ref, st_ref = refs[:3]
    side_out = refs[3:3 + n_side]
    n_out = 3 + n_side + (1 if emit_state else 0)
    w_s, kk_s, kka_s, kd_s, r_s, v_s, o_s = refs[n_out:]
    nk, vh, _ = st_ref.shape
    tc = w_s.shape[0]
    rep = RW_HD // vh
    nrec = LANES // rep
    lane = lax.broadcasted_iota(jnp.int32, (1, LANES), 1)
    grp = lane // nrec
    lo = lane < RW_HD

    def rows(f_ref, b_ref, j, n_steps):
        return jnp.concatenate([f_ref[:, j, :], b_ref[:, n_steps - 1 - j, :]], axis=0)

    if lanes_in:
        w_in, kk_in, kka_in, kd_in, r_in, v_in = data_in
    else:
        w_in, kk_in, kka_in, kd_in, r_in, v_in = w_s, kk_s, kka_s, kd_s, r_s, v_s
        for j in range(0, tc, 2):
            for si, dst in enumerate((w_s, kk_s, kka_s, kd_s, r_s, v_s)):
                f_ref, b_ref = data_in[2 * si], data_in[2 * si + 1]
                gt = _to_lanes(rows(f_ref, b_ref, j, tc), rows(f_ref, b_ref, j + 1, tc), rep)
                for s in range(2):
                    blk = gt[s * RW_HD:(s + 1) * RW_HD]
                    if dst is v_s and rep > 1:
                        blk = sum(jnp.where(grp == q, blk[q * vh:(q + 1) * vh], 0.0) for q in range(rep))
                    dst[j + s] = blk

    p = jnp.ones((nk, LANES), F32)
    for t in range(tc):
        kk_s[t] = kk_in[t] * p
        p = p * w_in[t]
        inv = 1.0 / p
        kka_s[t] = kka_in[t] * inv
        kd_s[t] = kd_in[t] * inv
        r_s[t] = r_in[t] * p
    w_s[0] = p

    @pl.when(pl.program_id(0) == 0)
    def _():
        if has_init:
            st_ref[...] = s0_ref[...]
        else:
            st_ref[...] = jnp.zeros(st_ref.shape, F32)

    sa0 = jnp.zeros((vh, LANES), F32)
    for k in range(nk):
        sa0 = sa0 + st_ref[k] * kk_s[0, pl.ds(k, 1), :]

    def step(t, sa):
        tn = jnp.minimum(t + 1, tc - 1)
        vt = v_in[t]
        o = jnp.zeros((vh, LANES), F32)
        san = jnp.zeros((vh, LANES), F32)
        for k in range(nk):
            new = st_ref[k] - kka_s[t, pl.ds(k, 1), :] * sa + kd_s[t, pl.ds(k, 1), :] * vt
            st_ref[k] = new
            o = o + new * r_s[t, pl.ds(k, 1), :]
            san = san + new * kk_s[tn, pl.ds(k, 1), :]
        o_s[t] = o
        return san

    if n_side:
        side_tc = side_in[0].shape[1]
        per = tc // (side_tc // 2)

        def steps_and_side_pair(q, sa):
            for u in range(per):
                sa = step(per * q + u, sa)
            j = 2 * q
            words = None
            for si in range(n_side):
                f_ref, b_ref = side_in[2 * si], side_in[2 * si + 1]
                gt = _to_lanes(rows(f_ref, b_ref, j, side_tc), rows(f_ref, b_ref, j + 1, side_tc), 1)
                side_out[si][j] = gt[0:RW_HD]
                side_out[si][j + 1] = gt[RW_HD:2 * RW_HD]
                bits = lax.bitcast_convert_type(gt[0:8], jnp.uint32)
                words = bits if words is None else words | bits
            zero = lax.shift_right_logical(lax.shift_right_logical(words, jnp.uint32(16)), jnp.uint32(16))
            return sa + lax.bitcast_convert_type(zero, F32)[0:1, :]

        lax.fori_loop(0, side_tc // 2, steps_and_side_pair, sa0)
    else:
        lax.fori_loop(0, tc, step, sa0)
    for k in range(nk):
        st_ref[k] = st_ref[k] * w_s[0, pl.ds(k, 1), :]

    for j in range(0, tc, 2):
        tiles = []
        for s in range(2):
            o = o_s[j + s]
            if rep == 1:
                tiles.append(o)
            else:
                tiles.extend(jnp.where(grp == q, o, 0.0) for q in range(rep))
        mt = jnp.concatenate(tiles, axis=0).T
        sm = mt[0:nrec]
        for q in range(1, rep):
            sm = sm + mt[q * nrec:(q + 1) * nrec]
        rs = pltpu.roll(sm, RW_HD, 1)
        for hp in range(RW_HEADS // 2):
            e0, e1 = (2 * hp) * 2 * nb, (2 * hp + 1) * 2 * nb
            cur = jnp.where(lo, sm[e0:e0 + 2 * nb], rs[e1:e1 + 2 * nb])
            nxt = jnp.where(lo, rs[e0:e0 + 2 * nb], sm[e1:e1 + 2 * nb])
            cs = slice(hp * LANES, (hp + 1) * LANES)
            of_ref[:, j, cs] = cur[0:nb]
            of_ref[:, j + 1, cs] = nxt[0:nb]
            ob_ref[:, tc - 1 - j, cs] = cur[nb:2 * nb]
            ob_ref[:, tc - 2 - j, cs] = nxt[nb:2 * nb]

    if emit_state:
        fs_ref = refs[n_out - 1]

        @pl.when(pl.program_id(0) == pl.num_programs(0) - 1)
        def _():
            for vp in range(vh // 2):
                pair = jnp.concatenate([st_ref[:, 2 * vp, :], st_ref[:, 2 * vp + 1, :]], axis=0)
                fs_ref[:, vp, :] = pair.T


def _fwd_bwd_args(pre, nb, t_len, tc, n_t):
    w0, w1, kka0, kka1, kd0, kd1, kk, r, v = (p.reshape(nb, t_len, BR_W) for p in pre)
    fwd = pl.BlockSpec((nb, tc, BR_W), lambda i: (0, i, 0))
    bwd = pl.BlockSpec((nb, tc, BR_W), lambda i: (0, n_t - 1 - i, 0))
    return (w0, w1, kk, kk, kka0, kka1, kd0, kd1, r, r, v, v), [fwd, bwd] * 6


def _scan(pre, s0, *, nb, t_len, name, lanes=None, side_pre=None, side_nb=None, side_len=None, emit_state=False):
    n_t = t_len // SCAN_TC
    vh = RW_HD * nb * 2 * RW_HEADS // LANES
    sspec = pl.BlockSpec((RW_HD, vh, LANES), lambda i: (0, 0, 0))
    has_init = s0 is not None
    args, in_specs = ((s0,), [sspec]) if has_init else ((), [])
    n_side, side_out_specs, side_out_shape = 0, [], []
    if side_pre is not None:
        n_side = 6
        side_tc = side_len // n_t
        side_args, side_specs = _fwd_bwd_args(side_pre, side_nb, side_len, side_tc, n_t)
        args, in_specs = args + side_args, in_specs + side_specs
        side_out_specs = [pl.BlockSpec((side_tc, RW_HD, LANES), lambda i: (i, 0, 0))] * n_side
        side_out_shape = [jax.ShapeDtypeStruct((side_len, RW_HD, LANES), F32)] * n_side
    if lanes is None:
        own_args, own_specs = _fwd_bwd_args(pre, nb, t_len, SCAN_TC, n_t)
    else:
        own_args = tuple(lanes)
        own_specs = ([pl.BlockSpec((SCAN_TC, RW_HD, LANES), lambda i: (i, 0, 0))] * 5
                     + [pl.BlockSpec((SCAN_TC, vh, LANES), lambda i: (i, 0, 0))])
    kbuf = pltpu.VMEM((SCAN_TC, RW_HD, LANES), F32)
    vbuf = pltpu.VMEM((SCAN_TC, vh, LANES), F32)
    state_specs, state_shape = [], []
    if emit_state:
        state_specs = [pl.BlockSpec((LANES, vh // 2, 2 * RW_HD), lambda i: (0, 0, 0))]
        state_shape = [jax.ShapeDtypeStruct((LANES, vh // 2, 2 * RW_HD), F32)]
    return pl.pallas_call(
        functools.partial(_scan_kernel, nb=nb, has_init=has_init, lanes_in=lanes is not None, n_side=n_side,
                          emit_state=emit_state),
        grid=(n_t,),
        in_specs=in_specs + own_specs,
        out_specs=[pl.BlockSpec((nb, SCAN_TC, BR_W), lambda i: (0, i, 0)),
                   pl.BlockSpec((nb, SCAN_TC, BR_W), lambda i: (0, n_t - 1 - i, 0)),
                   sspec] + side_out_specs + state_specs,
        out_shape=[jax.ShapeDtypeStruct((nb, t_len, BR_W), F32),
                   jax.ShapeDtypeStruct((nb, t_len, BR_W), F32),
                   jax.ShapeDtypeStruct((RW_HD, vh, LANES), F32)] + side_out_shape + state_shape,
        scratch_shapes=[kbuf] * 5 + [vbuf] * 2,
        compiler_params=_cparams(("arbitrary",)),
        name=name,
    )(*args, *own_args)


LAT_VS = 4
LAT_VH = RW_HD // LAT_VS


def _lat_state(s):
    s6 = s.reshape(DEC_BATCH, 2, RW_HEADS, LAT_VS, LAT_VH, RW_HD)
    return s6.transpose(5, 4, 3, 2, 1, 0).reshape(RW_HD, LAT_VH, LANES)


def _sink_softmax_pv(ss, sks, vbs):
    ms = [jnp.maximum(jnp.max(s, axis=-1, keepdims=True), sk) for s, sk in zip(ss, sks)]
    ps = [jnp.exp(s - m) for s, m in zip(ss, ms)]
    dens = [jnp.sum(p, axis=-1, keepdims=True) + jnp.exp(sk - m) for p, sk, m in zip(ps, sks, ms)]
    return [jnp.dot(p.astype(BF16), vb, preferred_element_type=F32) / den for p, vb, den in zip(ps, vbs, dens)]


def _att_ctx_kernel(sink_ref, q_ref, k_ref, v_ref, dp_ref, pw_ref, psc_ref, pic_ref, o_ref, yd_ref, pa_ref, pb_ref):
    _pool_body(dp_ref, pw_ref, psc_ref, pic_ref, yd_ref, pa_ref, pb_ref)
    scale = ATT_HD ** -0.5
    q = q_ref[...]
    kb = k_ref[...].astype(BF16)
    vb = v_ref[...].astype(BF16)
    first = lax.broadcasted_iota(jnp.int32, (ATT_GROUP * SEQ, 1), 0) < SEQ
    ss, sks = [], []
    for kvh in range(ATT_KV_HEADS):
        ks = slice(kvh * ATT_HD, (kvh + 1) * ATT_HD)
        q2 = jnp.concatenate([q[:, hd * ATT_HD:(hd + 1) * ATT_HD]
                              for hd in (ATT_GROUP * kvh, ATT_GROUP * kvh + 1)], axis=0).astype(BF16)
        ss.append(lax.dot_general(q2, kb[:, ks], (((1,), (1,)), ((), ())), preferred_element_type=F32) * scale)
        sks.append(jnp.where(first, sink_ref[ATT_GROUP * kvh], sink_ref[ATT_GROUP * kvh + 1]))
    outs = []
    for o2 in _sink_softmax_pv(ss, sks, [vb[:, kvh * ATT_HD:(kvh + 1) * ATT_HD] for kvh in range(ATT_KV_HEADS)]):
        outs += [o2[:SEQ], o2[SEQ:]]
    o_ref[...] = jnp.concatenate(outs, axis=1)


def _att_ctx(sink, cq, ck, cv, dp, pw, pscale):
    kvw = ATT_KV_HEADS * ATT_HD
    pool_in, pool_scratch = _pool_specs(SEQ)
    return pl.pallas_call(
        _att_ctx_kernel,
        grid=(BATCH,),
        in_specs=[
            pl.BlockSpec(memory_space=pltpu.SMEM),
            pl.BlockSpec((SEQ, BR_W), lambda b: (b, 0)),
            pl.BlockSpec((SEQ, kvw), lambda b: (b, 0)),
            pl.BlockSpec((SEQ, kvw), lambda b: (b, 0)),
        ] + pool_in,
        out_specs=[pl.BlockSpec((SEQ, BR_W), lambda b: (b, 0))] * 2,
        out_shape=[jax.ShapeDtypeStruct((N_CTX, BR_W), F32)] * 2,
        scratch_shapes=pool_scratch,
        compiler_params=_cparams(("arbitrary",)),
        name="att_ctx",
    )(sink, cq, ck, cv, dp, pw, pscale.reshape(1, BR_W), _pool_inv_count(SEQ))


def _rope(x, cos, sin_signed):
    w = x.shape[1]
    lane = lax.broadcasted_iota(jnp.int32, (1, w), 1)
    first = (lane % 32) < 16
    swapped = jnp.where(first, pltpu.roll(x, w - 16, 1), pltpu.roll(x, 16, 1))
    return x * cos + swapped * sin_signed


def _att_lat_kernel(sink_ref, q_ref, k_ref, v_ref, ck_ref, cv_ref, cq_ref, sq_ref, ckk_ref, skk_ref,
                    dp_ref, pw_ref, psc_ref, pic_ref, o_ref, yd_ref, qs_ref, ks_ref, vs_ref, pa_ref, pb_ref):
    _pool_body(dp_ref, pw_ref, psc_ref, pic_ref, yd_ref, pa_ref, pb_ref)
    scale = ATT_HD ** -0.5
    nb = DEC_SEQ // CHUNK
    qs_ref[...] = _rope(q_ref[...], cq_ref[...], sq_ref[...]).astype(BF16)
    ks_ref[...] = _rope(k_ref[...], ckk_ref[...], skk_ref[...]).astype(BF16)
    vs_ref[...] = v_ref[...].astype(BF16)
    ckb = ck_ref[...].astype(BF16)
    cvb = cv_ref[...].astype(BF16)
    for n in range(nb):
        lo = max(n - 1, 0) * CHUNK
        hi = min(n + 2, nb) * CHUNK
        nloc = hi - lo
        ncol = nloc + PAST_LEN
        col = lax.broadcasted_iota(jnp.int32, (CHUNK, ncol), 1)
        row = lax.broadcasted_iota(jnp.int32, (CHUNK, ncol), 0)
        dist = jnp.abs((n * CHUNK + row) - (lo + col))
        valid = (col >= nloc) | (dist <= WINDOW)
        valid2 = jnp.concatenate([valid] * ATT_GROUP, axis=0)
        first = lax.broadcasted_iota(jnp.int32, (ATT_GROUP * CHUNK, 1), 0) < CHUNK
        outs, ss, vs_, sks = [], [], [], []
        for kvh in range(ATT_KV_HEADS):
            ks = slice(kvh * ATT_HD, (kvh + 1) * ATT_HD)
            q2 = jnp.concatenate([qs_ref[n * CHUNK:(n + 1) * CHUNK, hd * ATT_HD:(hd + 1) * ATT_HD]
                                  for hd in (ATT_GROUP * kvh, ATT_GROUP * kvh + 1)], axis=0)
            kall = jnp.concatenate([ks_ref[lo:hi, ks], ckb[:, ks]], axis=0)
            vs_.append(jnp.concatenate([vs_ref[lo:hi, ks], cvb[:, ks]], axis=0))
            s = lax.dot_general(q2, kall, (((1,), (1,)), ((), ())), preferred_element_type=F32) * scale
            ss.append(jnp.where(valid2, s, NEG_INF))
            sks.append(jnp.where(first, sink_ref[ATT_GROUP * kvh], sink_ref[ATT_GROUP * kvh + 1]))
        for o2 in _sink_softmax_pv(ss, sks, vs_):
            outs += [o2[:CHUNK], o2[CHUNK:]]
        o_ref[n * CHUNK:(n + 1) * CHUNK, :] = jnp.concatenate(outs, axis=1)


def _att_lat(sink, cq, ck, cv, cache_k, cache_v, layer, tables, dp, pw, pscale):
    kvw = ATT_KV_HEADS * ATT_HD
    pool_in, pool_scratch = _pool_specs(DEC_SEQ)
    cosq, sinq, cosk, sink_k = tables
    seq = lambda w: pl.BlockSpec((DEC_SEQ, w), lambda b: (b, 0))
    cache = pl.BlockSpec((None, None, PAST_LEN, kvw), lambda b: (b, layer, 0, 0))
    tab = lambda w: pl.BlockSpec((DEC_SEQ, w), lambda b: (0, 0))
    return pl.pallas_call(
        _att_lat_kernel,
        grid=(DEC_BATCH,),
        in_specs=[pl.BlockSpec(memory_space=pltpu.SMEM), seq(BR_W), seq(kvw), seq(kvw), cache, cache,
                  tab(BR_W), tab(BR_W), tab(kvw), tab(kvw)] + pool_in,
        out_specs=[pl.BlockSpec((DEC_SEQ, BR_W), lambda b: (b, 0))] * 2,
        out_shape=[jax.ShapeDtypeStruct((N_LAT, BR_W), F32)] * 2,
        scratch_shapes=[pltpu.VMEM((DEC_SEQ, BR_W), BF16), pltpu.VMEM((DEC_SEQ, kvw), BF16),
                        pltpu.VMEM((DEC_SEQ, kvw), BF16)] + pool_scratch,
        compiler_params=_cparams(("arbitrary",)),
        name="att_lat",
    )(sink, cq, ck, cv, cache_k, cache_v, cosq, sinq, cosk, sink_k, dp, pw, pscale.reshape(1, BR_W),
      _pool_inv_count(DEC_SEQ))


def _rope_tables():
    pos = np.arange(DEC_SEQ)
    lane = np.arange(ATT_HD)
    p = np.where(lane[None, :] < 32, (pos // GRID_W)[:, None], (pos % GRID_W)[:, None]).astype(np.float32)
    inv = (ROPE_BASE ** (-jnp.arange(0, 32, 2, dtype=F32) / 32))[lane % 16]
    ang = jnp.asarray(p) * inv[None, :]
    sign = jnp.asarray(np.where((lane % 32) < 16, -1.0, 1.0).astype(np.float32))
    cos, sin = jnp.cos(ang), jnp.sin(ang) * sign[None, :]
    return (jnp.tile(cos, (1, ATT_HEADS)), jnp.tile(sin, (1, ATT_HEADS)),
            jnp.tile(cos, (1, ATT_KV_HEADS)), jnp.tile(sin, (1, ATT_KV_HEADS)))


POOL_PAD = 8


def _pool_body(x_ref, w_ref, sc_ref, ic_ref, o_ref, a_ref, b_ref):
    n = x_ref.shape[0]
    ext = n + POOL_PAD
    x = x_ref[...]
    zeros = jnp.zeros((POOL_PAD, BR_W), F32)
    a_ref[0:POOL_PAD, :] = zeros
    b_ref[0:POOL_PAD, :] = zeros
    a_ref[POOL_PAD:POOL_PAD + n, :] = x
    a_ref[POOL_PAD + n:POOL_PAD + ext, :] = zeros
    grp = lax.broadcasted_iota(jnp.int32, (1, BR_W), 1) // POOL_GD
    acc = jnp.zeros((n, BR_W), F32)
    src_ref, dst_ref = a_ref, b_ref
    for m in range(len(POOL_SIZES)):
        back = 1 << m
        dst_ref[POOL_PAD:POOL_PAD + ext, :] = (src_ref[POOL_PAD:POOL_PAD + ext, :]
                                               + src_ref[POOL_PAD - back:POOL_PAD - back + ext, :])
        off = POOL_PAD + back - 1
        acc = jnp.where(grp == m, dst_ref[off:off + n, :], acc)
        src_ref, dst_ref = dst_ref, src_ref
    d = acc * ic_ref[...] - x
    o_ref[...] = jnp.dot(d.astype(BF16), w_ref[...], preferred_element_type=F32) * sc_ref[...]


def _pool_inv_count(seq_len):
    t = np.arange(seq_len)[:, None]
    half = (np.asarray(POOL_SIZES) // 2)[np.arange(BR_W) // POOL_GD][None, :]
    cnt = np.minimum(t + half, seq_len) - np.maximum(t - half, 0)
    return jnp.asarray(1.0 / cnt, F32)


def _pool_specs(seq_len):
    return ([pl.BlockSpec((seq_len, BR_W), lambda b: (b, 0)), pl.BlockSpec((BR_W, BR_W), lambda b: (0, 0)),
             pl.BlockSpec((1, BR_W), lambda b: (0, 0)), pl.BlockSpec((seq_len, BR_W), lambda b: (0, 0))],
            [pltpu.VMEM((seq_len + 2 * POOL_PAD, BR_W), F32)] * 2)


def _merge_kernel(x_ref, mod_ref, g_ref, a_ref, bonus_ref, of_ref, ob_ref, yc_ref, yd_ref,
                  wz_ref, wmg_ref, ws_ref, bs_ref, lng_ref, lnb_ref, ones_ref, wup_ref, wo_ref, gf_ref,
                  o_ref, *, lat, final):
    i = pl.program_id(0)
    m = mod_ref[pl.ds(_mod_row(i, TM_MERGE, lat), 1), :]
    x = x_ref[...]
    hb = _norm_mod(x, g_ref[...], m).astype(BF16)

    a = a_ref[...]
    a_u, a_v = a[:, :BR_W], a[:, BR_W:]
    grp = lax.broadcasted_iota(jnp.int32, (1, BR_W), 1) // A_GD
    svs = []
    for c in range(TM_MERGE // CHUNK):
        vc = a_v[c * CHUNK:(c + 1) * CHUNK, :]
        sv = bs_ref[...]
        for g in range(A_GROUPS):
            sv = sv + jnp.dot(ws_ref[g], jnp.where(grp == g, vc, 0.0).astype(BF16), preferred_element_type=F32)
        svs.append(sv)
    y_a = a_u * jnp.concatenate(svs, axis=0)

    ones = ones_ref[...]
    osum = of_ref[...] + ob_ref[...]
    mu = _seg_sum(osum, ones) * (1.0 / RW_HD)
    dev = osum - mu
    var = _seg_sum(dev * dev, ones) * (1.0 / RW_HD)
    y_b = dev * lax.rsqrt(var + GN_EPS) * lng_ref[...] + lnb_ref[...] + bonus_ref[...]

    merged = jnp.zeros((TM_MERGE, D_MODEL), F32)
    z_all = jnp.dot(hb, wz_ref[0], preferred_element_type=F32)
    for n, y in enumerate((y_a, y_b, yc_ref[...], yd_ref[...])):
        z = z_all[:, n * BR_W:(n + 1) * BR_W]
        ys = y * (z * _sigmoid(z))
        up = jnp.dot(ys.astype(BF16), wup_ref[n], preferred_element_type=F32)
        mg = jnp.dot(hb, wmg_ref[0, :, n * D_MODEL:(n + 1) * D_MODEL], preferred_element_type=F32)
        merged = merged + _sigmoid(mg) * up
    gate = m[:, 2 * D_MODEL:]
    out = x + gate * jnp.dot(merged.astype(BF16), wo_ref[...], preferred_element_type=F32)
    if final:
        ms = jnp.mean(out * out, axis=-1, keepdims=True)
        out = out * lax.rsqrt(ms + NORM_EPS) * gf_ref[...]
    o_ref[...] = out


def _merge(x, mod_l, g, a, bonus, o_f, o_b, y_c, y_d, w_in_b, layer, ws, bs_tile, ln_g, ln_b, ones_bd, wup, wo,
           g_final, lat, final):
    row = lambda w: pl.BlockSpec((TM_MERGE, w), lambda i: (i, 0))
    full2 = lambda s: pl.BlockSpec(s, lambda i: (0, 0))
    z0, z1 = IN_SMALL, IN_SMALL + N_BRANCH * BR_W
    cols = lambda c0, w: pl.BlockSpec((pl.Element(1), pl.Element(D_MODEL), pl.Element(w)), lambda i: (layer, 0, c0))
    return pl.pallas_call(
        functools.partial(_merge_kernel, lat=lat, final=final),
        grid=(N_GRP // TM_MERGE,),
        in_specs=[row(D_MODEL), full2((8, 3 * D_MODEL)), full2((1, D_MODEL)), row(2 * BR_W),
                  row(BR_W), row(BR_W), row(BR_W), row(BR_W), row(BR_W),
                  cols(z0, N_BRANCH * BR_W), cols(z1, N_BRANCH * D_MODEL),
                  pl.BlockSpec((None, A_GROUPS, CHUNK, CHUNK), lambda i: (layer, 0, 0, 0)), full2((CHUNK, BR_W)),
                  full2((1, BR_W)), full2((1, BR_W)), full2((BR_W, BR_W)),
                  pl.BlockSpec((None, N_BRANCH, BR_W, D_MODEL), lambda i: (layer, 0, 0, 0)),
                  pl.BlockSpec((None, D_MODEL, D_MODEL), lambda i: (layer, 0, 0)), full2((1, D_MODEL))],
        out_specs=row(D_MODEL),
        out_shape=jax.ShapeDtypeStruct((N_GRP, D_MODEL), F32),
        compiler_params=_cparams(("arbitrary",)),
        name=("merge_lat" if lat else "merge_ctx") + ("_final" if final else ""),
    )(x, mod_l, g.reshape(1, D_MODEL), a, bonus, o_f, o_b, y_c, y_d, w_in_b, w_in_b, ws, bs_tile,
      ln_g.reshape(1, BR_W), ln_b.reshape(1, BR_W), ones_bd, wup, wo, g_final.reshape(1, D_MODEL))


def _block_diag(blocks):
    n, r, c = blocks.shape
    eye = jnp.eye(n, dtype=blocks.dtype)
    return (eye[:, None, :, None] * blocks[:, :, None, :]).reshape(n * r, n * c)


def kernel(x_prompt, x_sample, cache_k, cache_v, state_rwkv, c, c_ctx, w_mod, b_mod, g_norm, w_in, w_s, b_s,
           rw_w0, rw_w_up, rw_a0, rw_a_up, rw_k_k, rw_k_a, rw_r_k, rw_ln_g, rw_ln_b, att_sink, pool_w,
           pool_scale, w_up, w_o, g_final):
    xs = [x_prompt.reshape(N_CTX, D_MODEL), x_sample.reshape(N_LAT, D_MODEL)]
    cond = jnp.concatenate([c_ctx[None, :], c, jnp.zeros((8 - 1 - DEC_BATCH, D_MODEL), F32)], axis=0)
    mod = _modulation(cond, w_mod, b_mod)

    ones_bd = _block_diag(jnp.ones((RW_HEADS, RW_HD, RW_HD), BF16))
    tables = _rope_tables()
    kvw = ATT_KV_HEADS * ATT_HD
    cache_k4 = cache_k.reshape(DEC_BATCH, DEPTH, PAST_LEN, kvw)
    cache_v4 = cache_v.reshape(DEC_BATCH, DEPTH, PAST_LEN, kvw)
    w_in_b, w_s_b, w_up_b, w_o_b = (w.astype(BF16) for w in (w_in, w_s, w_up, w_o))

    new_k, new_v, new_s = [], [], []
    for l in range(DEPTH):
        pw = _block_diag(pool_w[l]).astype(BF16)
        bs_tile = jnp.repeat(b_s[l].T, A_GD, axis=1)
        final = l == DEPTH - 1
        proj = []
        for lat in (False, True):
            a, cq, ck, cv, dp, r, v, *pre, bonus = _inproj(
                xs[lat], mod[l], g_norm[l], w_in_b, l, rw_w0[l], rw_w_up[l], rw_a0[l], rw_a_up[l], rw_k_k[l],
                rw_k_a[l], rw_r_k[l], ones_bd, lat)
            proj.append((a, cq, ck, cv, dp, pre + [r, v], bonus))

        of_l, ob_l, _, *ctx_lanes = _scan(proj[1][5], _lat_state(state_rwkv[:, l]), nb=DEC_BATCH, t_len=DEC_SEQ,
                                          name="scan_lat", side_pre=proj[0][5], side_nb=BATCH, side_len=SEQ)
        of_c, ob_c, _, s_fin = _scan(None, None, nb=BATCH, t_len=SEQ, name="scan_ctx", lanes=ctx_lanes,
                                     emit_state=True)
        new_s.append(s_fin.reshape(RW_HEADS, 2, BATCH, RW_HD, RW_HD).transpose(2, 1, 0, 3, 4))
        scans = ((of_c, ob_c), (of_l, ob_l))

        for lat in (False, True):
            a, cq, ck, cv, dp, _, bonus = proj[lat]
            o_f, o_b = scans[lat]
            if lat:
                y_c, y_d = _att_lat(att_sink[l], cq, ck, cv, cache_k4, cache_v4, l, tables, dp, pw, pool_scale[l])
            else:
                y_c, y_d = _att_ctx(att_sink[l], cq, ck, cv, dp, pw, pool_scale[l])
                new_k.append(ck.reshape(BATCH, SEQ, ATT_KV_HEADS, ATT_HD))
                new_v.append(cv.reshape(BATCH, SEQ, ATT_KV_HEADS, ATT_HD))
            xs[lat] = _merge(xs[lat], mod[l], g_norm[l], a, bonus, o_f.reshape(N_GRP, BR_W),
                             o_b.reshape(N_GRP, BR_W), y_c, y_d, w_in_b, l, w_s_b, bs_tile, rw_ln_g[l], rw_ln_b[l],
                             ones_bd, w_up_b, w_o_b, g_final, lat, final)

    y_prompt = xs[0].reshape(BATCH, SEQ, D_MODEL)
    y_sample = xs[1].reshape(DEC_BATCH, DEC_SEQ, D_MODEL)
    return (y_prompt, y_sample, jnp.stack(new_k, axis=1), jnp.stack(new_v, axis=1), jnp.stack(new_s, axis=1))
```

```python
import functools

import numpy as np
import jax
import jax.numpy as jnp
from jax import lax
from jax.experimental import pallas as pl
from jax.experimental.pallas import tpu as pltpu

D_MODEL = 1024
BATCH = 16
SEQ = 256
DEPTH = 2
DEC_BATCH = 4
DEC_SEQ = 1024
PAST_LEN = 512
GRID_W = 64
N_BRANCH = 4
BR_W = D_MODEL // 4
CHUNK = 128
A_GROUPS = 4
A_GD = BR_W // A_GROUPS
RW_HD = 64
RW_HEADS = BR_W // RW_HD
DECAY_RANK = 64
ICL_RANK = 64
ATT_HD = 64
ATT_HEADS = BR_W // ATT_HD
ATT_KV_HEADS = 2
ATT_GROUP = ATT_HEADS // ATT_KV_HEADS
WINDOW = 128
ROPE_BASE = 10000.0
POOL_SIZES = (2, 4, 8, 16)
POOL_GD = BR_W // len(POOL_SIZES)
NORM_EPS = 1e-6
GN_EPS = 64e-5
NEG_INF = -1e30
DECAY_SCALE = float(np.exp(-0.5))

N_CTX = BATCH * SEQ
N_LAT = DEC_BATCH * DEC_SEQ
N_GRP = N_CTX
assert N_LAT == N_GRP
IN_SMALL = 2176

F32 = jnp.float32
BF16 = jnp.bfloat16

V7X_VMEM_LIMIT = 56 * 1024 * 1024
LANES = 128

TM_IN = 512
TM_MERGE = 512
SCAN_TC = 32


def _cparams(sem):
    return pltpu.CompilerParams(dimension_semantics=sem, vmem_limit_bytes=V7X_VMEM_LIMIT)


def _mod_row(i, tm, lat):
    return 1 + i // (DEC_SEQ // tm) if lat else 0


def _norm_mod(x, g, m):
    ms = jnp.mean(x * x, axis=-1, keepdims=True)
    y = x * lax.rsqrt(ms + NORM_EPS) * g
    shift = m[:, :D_MODEL]
    scale = m[:, D_MODEL:2 * D_MODEL]
    return y * (1.0 + scale) + shift


def _sigmoid(x):
    return 1.0 / (1.0 + jnp.exp(-x))


def _split(x):
    hi = x.astype(BF16)
    return hi, (x - hi.astype(F32)).astype(BF16)


def _seg_sum(x, ones_bd):
    hi, lo = _split(x)
    return (jnp.dot(hi, ones_bd, preferred_element_type=F32)
            + jnp.dot(lo, ones_bd, preferred_element_type=F32))


def _dot_split(x, w):
    xh, xl = _split(x)
    wh, wl = _split(w)
    lhs = jnp.concatenate([xh, xl, xh], axis=1)
    rhs = jnp.concatenate([wh, wh, wl], axis=0)
    return jnp.dot(lhs, rhs, preferred_element_type=F32)


def _mod_kernel(c_ref, w_ref, b_ref, o_ref):
    cnd = c_ref[...]
    s = cnd * _sigmoid(cnd)
    o_ref[...] = jnp.dot(s.astype(BF16), w_ref[...].astype(BF16), preferred_element_type=F32) + b_ref[...]


def _modulation(cond, w_mod, b_mod):
    nt = 3 * D_MODEL // 1024
    return pl.pallas_call(
        _mod_kernel,
        grid=(DEPTH, nt),
        in_specs=[
            pl.BlockSpec((8, D_MODEL), lambda l, j: (0, 0)),
            pl.BlockSpec((None, D_MODEL, 1024), lambda l, j: (l, 0, j)),
            pl.BlockSpec((None, 1, 1024), lambda l, j: (l, 0, j)),
        ],
        out_specs=pl.BlockSpec((None, 8, 1024), lambda l, j: (l, 0, j)),
        out_shape=jax.ShapeDtypeStruct((DEPTH, 8, 3 * D_MODEL), F32),
        compiler_params=_cparams(("arbitrary", "arbitrary")),
        name="modulation",
    )(cond, w_mod, b_mod.reshape(DEPTH, 1, 3 * D_MODEL))


def _inproj_kernel(x_ref, mod_ref, g_ref, w_ref, w0_ref, wup_ref, a0_ref, aup_ref, kkw_ref, ka_ref, rk_ref, ones_ref,
                   a_ref, q_ref, ck_ref, cv_ref, d_ref, r_ref, v_ref,
                   w0o, w1o, kka0o, kka1o, kd0o, kd1o, kko, bonuso, *, lat):
    i = pl.program_id(0)
    m = mod_ref[pl.ds(_mod_row(i, TM_IN, lat), 1), :]
    h = _norm_mod(x_ref[...], g_ref[...], m)
    p = jnp.dot(h.astype(BF16), w_ref[0], preferred_element_type=F32)
    a_ref[...] = p[:, 0:512]
    q_ref[...] = p[:, 1408:1664]
    ck_ref[...] = p[:, 1664:1792]
    cv_ref[...] = p[:, 1792:1920]
    d_ref[...] = p[:, 1920:2176]

    r = p[:, 512:768]
    k = p[:, 768:1024]
    v = p[:, 1024:1280]
    wd_t = jnp.tanh(p[:, 1280:1344])
    ad = p[:, 1344:1408]
    r_ref[...] = r
    v_ref[...] = v
    ones = ones_ref[...]
    kk = k * kkw_ref[...]
    ss = _seg_sum(kk * kk, ones)
    kkn = kk * lax.rsqrt(ss + 1e-12)
    kko[...] = kkn
    kd_sum = jnp.zeros_like(v)
    for d, (wo, kkao, kdo) in enumerate(((w0o, kka0o, kd0o), (w1o, kka1o, kd1o))):
        pre = w0_ref[pl.ds(d, 1), :] + _dot_split(wd_t, wup_ref[d])
        wo[...] = jnp.exp(-DECAY_SCALE * _sigmoid(pre))
        a = _sigmoid(a0_ref[pl.ds(d, 1), :] + _dot_split(ad, aup_ref[d]))
        k_d = k * (1.0 + (a - 1.0) * ka_ref[...])
        kdo[...] = k_d
        kkao[...] = kkn * a
        kd_sum = kd_sum + k_d
    bonuso[...] = _seg_sum(r * kd_sum * rk_ref[...], ones) * v


def _inproj(x, mod_l, g, w_in_b, layer, w0, wup, a0, aup, k_k, k_a, r_k, ones_bd, lat):
    kvw = ATT_KV_HEADS * ATT_HD
    widths = (2 * BR_W, BR_W, kvw, kvw, BR_W) + (BR_W,) * 10
    full2 = lambda s: pl.BlockSpec(s, lambda i: (0, 0))
    return pl.pallas_call(
        functools.partial(_inproj_kernel, lat=lat),
        grid=(N_GRP // TM_IN,),
        in_specs=[
            pl.BlockSpec((TM_IN, D_MODEL), lambda i: (i, 0)),
            full2((8, 3 * D_MODEL)),
            full2((1, D_MODEL)),
            pl.BlockSpec((pl.Element(1), pl.Element(D_MODEL), pl.Element(IN_SMALL)), lambda i: (layer, 0, 0)),
            full2((2, BR_W)),
            pl.BlockSpec((2, DECAY_RANK, BR_W), lambda i: (0, 0, 0)),
            full2((2, BR_W)),
            pl.BlockSpec((2, ICL_RANK, BR_W), lambda i: (0, 0, 0)),
            full2((1, BR_W)), full2((1, BR_W)), full2((1, BR_W)),
            full2((BR_W, BR_W)),
        ],
        out_specs=[pl.BlockSpec((TM_IN, w), lambda i: (i, 0)) for w in widths],
        out_shape=[jax.ShapeDtypeStruct((N_GRP, w), F32) for w in widths],
        compiler_params=_cparams(("arbitrary",)),
        name="in_proj_lat" if lat else "in_proj_ctx",
    )(x, mod_l, g.reshape(1, D_MODEL), w_in_b, w0, wup, a0, aup,
      k_k.reshape(1, BR_W), k_a.reshape(1, BR_W), r_k.reshape(1, BR_W), ones_bd)


def _to_lanes(z0, z1, rep):
    lo = lax.broadcasted_iota(jnp.int32, (1, LANES), 1) < RW_HD
    parts = []
    for hp in range(RW_HEADS // 2):
        a = z0[:, hp * LANES:(hp + 1) * LANES]
        b = z1[:, hp * LANES:(hp + 1) * LANES]
        parts.append(jnp.where(lo, a, pltpu.roll(b, RW_HD, 1)))
        parts.append(jnp.where(lo, pltpu.roll(a, RW_HD, 1), b))
    return jnp.concatenate(parts * rep, axis=0).T


def _scan_kernel(*refs, nb, has_init, lanes_in, n_side, emit_state):
    if has_init:
        s0_ref, refs = refs[0], refs[1:]
    side_in, refs = refs[:2 * n_side], refs[2 * n_side:]
    n_in = 6 if lanes_in else 12
    data_in, refs = refs[:n_in], refs[n_in:]
    of_ref, ob_ref, st_ref = refs[:3]
    side_out = refs[3:3 + n_side]
    n_out = 3 + n_side + (1 if emit_state else 0)
    w_s, kk_s, kka_s, kd_s, r_s, v_s, o_s = refs[n_out:]
    nk, vh, _ = st_ref.shape
    tc = w_s.shape[0]
    rep = RW_HD // vh
    nrec = LANES // rep
    lane = lax.broadcasted_iota(jnp.int32, (1, LANES), 1)
    grp = lane // nrec
    lo = lane < RW_HD

    def rows(f_ref, b_ref, j, n_steps):
        return jnp.concatenate([f_ref[:, j, :], b_ref[:, n_steps - 1 - j, :]], axis=0)

    if lanes_in:
        w_in, kk_in, kka_in, kd_in, r_in, v_in = data_in
    else:
        w_in, kk_in, kka_in, kd_in, r_in, v_in = w_s, kk_s, kka_s, kd_s, r_s, v_s
        for j in range(0, tc, 2):
            for si, dst in enumerate((w_s, kk_s, kka_s, kd_s, r_s, v_s)):
                f_ref, b_ref = data_in[2 * si], data_in[2 * si + 1]
                gt = _to_lanes(rows(f_ref, b_ref, j, tc), rows(f_ref, b_ref, j + 1, tc), rep)
                for s in range(2):
                    blk = gt[s * RW_HD:(s + 1) * RW_HD]
                    if dst is v_s and rep > 1:
                        blk = sum(jnp.where(grp == q, blk[q * vh:(q + 1) * vh], 0.0) for q in range(rep))
                    dst[j + s] = blk

    p = jnp.ones((nk, LANES), F32)
    for t in range(tc):
        kk_s[t] = kk_in[t] * p
        p = p * w_in[t]
        inv = 1.0 / p
        kka_s[t] = kka_in[t] * inv
        kd_s[t] = kd_in[t] * inv
        r_s[t] = r_in[t] * p
    w_s[0] = p

    @pl.when(pl.program_id(0) == 0)
    def _():
        if has_init:
            st_ref[...] = s0_ref[...]
        else:
            st_ref[...] = jnp.zeros(st_ref.shape, F32)

    sa0 = jnp.zeros((vh, LANES), F32)
    for k in range(nk):
        sa0 = sa0 + st_ref[k] * kk_s[0, pl.ds(k, 1), :]

    def step(t, sa):
        tn = jnp.minimum(t + 1, tc - 1)
        vt = v_in[t]
        o = jnp.zeros((vh, LANES), F32)
        san = jnp.zeros((vh, LANES), F32)
        for k in range(nk):
            new = st_ref[k] - kka_s[t, pl.ds(k, 1), :] * sa + kd_s[t, pl.ds(k, 1), :] * vt
            st_ref[k] = new
            o = o + new * r_s[t, pl.ds(k, 1), :]
            san = san + new * kk_s[tn, pl.ds(k, 1), :]
        o_s[t] = o
        return san

    if n_side:
        side_tc = side_in[0].shape[1]
        per = tc // (side_tc // 2)

        def steps_and_side_pair(q, sa):
            for u in range(per):
                sa = step(per * q + u, sa)
            j = 2 * q
            words = None
            for si in range(n_side):
                f_ref, b_ref = side_in[2 * si], side_in[2 * si + 1]
                gt = _to_lanes(rows(f_ref, b_ref, j, side_tc), rows(f_ref, b_ref, j + 1, side_tc), 1)
                side_out[si][j] = gt[0:RW_HD]
                side_out[si][j + 1] = gt[RW_HD:2 * RW_HD]
                bits = lax.bitcast_convert_type(gt[0:8], jnp.uint32)
                words = bits if words is None else words | bits
            zero = lax.shift_right_logical(lax.shift_right_logical(words, jnp.uint32(16)), jnp.uint32(16))
            return sa + lax.bitcast_convert_type(zero, F32)[0:1, :]

        lax.fori_loop(0, side_tc // 2, steps_and_side_pair, sa0)
    else:
        lax.fori_loop(0, tc, step, sa0)
    for k in range(nk):
        st_ref[k] = st_ref[k] * w_s[0, pl.ds(k, 1), :]

    for j in range(0, tc, 2):
        tiles = []
        for s in range(2):
            o = o_s[j + s]
            if rep == 1:
                tiles.append(o)
            else:
                tiles.extend(jnp.where(grp == q, o, 0.0) for q in range(rep))
        mt = jnp.concatenate(tiles, axis=0).T
        sm = mt[0:nrec]
        for q in range(1, rep):
            sm = sm + mt[q * nrec:(q + 1) * nrec]
        rs = pltpu.roll(sm, RW_HD, 1)
        for hp in range(RW_HEADS // 2):
            e0, e1 = (2 * hp) * 2 * nb, (2 * hp + 1) * 2 * nb
            cur = jnp.where(lo, sm[e0:e0 + 2 * nb], rs[e1:e1 + 2 * nb])
            nxt = jnp.where(lo, rs[e0:e0 + 2 * nb], sm[e1:e1 + 2 * nb])
            cs = slice(hp * LANES, (hp + 1) * LANES)
            of_ref[:, j, cs] = cur[0:nb]
            of_ref[:, j + 1, cs] = nxt[0:nb]
            ob_ref[:, tc - 1 - j, cs] = cur[nb:2 * nb]
            ob_ref[:, tc - 2 - j, cs] = nxt[nb:2 * nb]

    if emit_state:
        fs_ref = refs[n_out - 1]

        @pl.when(pl.program_id(0) == pl.num_programs(0) - 1)
        def _():
            for vp in range(vh // 2):
                pair = jnp.concatenate([st_ref[:, 2 * vp, :], st_ref[:, 2 * vp + 1, :]], axis=0)
                fs_ref[:, vp, :] = pair.T


def _fwd_bwd_args(pre, nb, t_len, tc, n_t):
    w0, w1, kka0, kka1, kd0, kd1, kk, r, v = (p.reshape(nb, t_len, BR_W) for p in pre)
    fwd = pl.BlockSpec((nb, tc, BR_W), lambda i: (0, i, 0))
    bwd = pl.BlockSpec((nb, tc, BR_W), lambda i: (0, n_t - 1 - i, 0))
    return (w0, w1, kk, kk, kka0, kka1, kd0, kd1, r, r, v, v), [fwd, bwd] * 6


def _scan(pre, s0, *, nb, t_len, name, lanes=None, side_pre=None, side_nb=None, side_len=None, emit_state=False):
    n_t = t_len // SCAN_TC
    vh = RW_HD * nb * 2 * RW_HEADS // LANES
    sspec = pl.BlockSpec((RW_HD, vh, LANES), lambda i: (0, 0, 0))
    has_init = s0 is not None
    args, in_specs = ((s0,), [sspec]) if has_init else ((), [])
    n_side, side_out_specs, side_out_shape = 0, [], []
    if side_pre is not None:
        n_side = 6
        side_tc = side_len // n_t
        side_args, side_specs = _fwd_bwd_args(side_pre, side_nb, side_len, side_tc, n_t)
        args, in_specs = args + side_args, in_specs + side_specs
        side_out_specs = [pl.BlockSpec((side_tc, RW_HD, LANES), lambda i: (i, 0, 0))] * n_side
        side_out_shape = [jax.ShapeDtypeStruct((side_len, RW_HD, LANES), F32)] * n_side
    if lanes is None:
        own_args, own_specs = _fwd_bwd_args(pre, nb, t_len, SCAN_TC, n_t)
    else:
        own_args = tuple(lanes)
        own_specs = ([pl.BlockSpec((SCAN_TC, RW_HD, LANES), lambda i: (i, 0, 0))] * 5
                     + [pl.BlockSpec((SCAN_TC, vh, LANES), lambda i: (i, 0, 0))])
    kbuf = pltpu.VMEM((SCAN_TC, RW_HD, LANES), F32)
    vbuf = pltpu.VMEM((SCAN_TC, vh, LANES), F32)
    state_specs, state_shape = [], []
    if emit_state:
        state_specs = [pl.BlockSpec((LANES, vh // 2, 2 * RW_HD), lambda i: (0, 0, 0))]
        state_shape = [jax.ShapeDtypeStruct((LANES, vh // 2, 2 * RW_HD), F32)]
    return pl.pallas_call(
        functools.partial(_scan_kernel, nb=nb, has_init=has_init, lanes_in=lanes is not None, n_side=n_side,
                          emit_state=emit_state),
        grid=(n_t,),
        in_specs=in_specs + own_specs,
        out_specs=[pl.BlockSpec((nb, SCAN_TC, BR_W), lambda i: (0, i, 0)),
                   pl.BlockSpec((nb, SCAN_TC, BR_W), lambda i: (0, n_t - 1 - i, 0)),
                   sspec] + side_out_specs + state_specs,
        out_shape=[jax.ShapeDtypeStruct((nb, t_len, BR_W), F32),
                   jax.ShapeDtypeStruct((nb, t_len, BR_W), F32),
                   jax.ShapeDtypeStruct((RW_HD, vh, LANES), F32)] + side_out_shape + state_shape,
        scratch_shapes=[kbuf] * 5 + [vbuf] * 2,
        compiler_params=_cparams(("arbitrary",)),
        name=name,
    )(*args, *own_args)


LAT_VS = 4
LAT_VH = RW_HD // LAT_VS


def _lat_state(s):
    s6 = s.reshape(DEC_BATCH, 2, RW_HEADS, LAT_VS, LAT_VH, RW_HD)
    return s6.transpose(5, 4, 3, 2, 1, 0).reshape(RW_HD, LAT_VH, LANES)


def _sink_softmax_pv(ss, sks, vbs):
    ms = [jnp.maximum(jnp.max(s, axis=-1, keepdims=True), sk) for s, sk in zip(ss, sks)]
    ps = [jnp.exp(s - m) for s, m in zip(ss, ms)]
    dens = [jnp.sum(p, axis=-1, keepdims=True) + jnp.exp(sk - m) for p, sk, m in zip(ps, sks, ms)]
    return [jnp.dot(p.astype(BF16), vb, preferred_element_type=F32) / den for p, vb, den in zip(ps, vbs, dens)]


def _att_ctx_kernel(sink_ref, q_ref, k_ref, v_ref, dp_ref, pw_ref, psc_ref, pic_ref, o_ref, yd_ref, pa_ref, pb_ref):
    _pool_body(dp_ref, pw_ref, psc_ref, pic_ref, yd_ref, pa_ref, pb_ref)
    scale = ATT_HD ** -0.5
    q = q_ref[...] * scale
    kb = k_ref[...].astype(BF16)
    vb = v_ref[...].astype(BF16)
    first = lax.broadcasted_iota(jnp.int32, (ATT_GROUP * SEQ, 1), 0) < SEQ
    ss, sks = [], []
    for kvh in range(ATT_KV_HEADS):
        ks = slice(kvh * ATT_HD, (kvh + 1) * ATT_HD)
        q2 = jnp.concatenate([q[:, hd * ATT_HD:(hd + 1) * ATT_HD]
                              for hd in (ATT_GROUP * kvh, ATT_GROUP * kvh + 1)], axis=0).astype(BF16)
        ss.append(lax.dot_general(q2, kb[:, ks], (((1,), (1,)), ((), ())), preferred_element_type=F32))
        sks.append(jnp.where(first, sink_ref[ATT_GROUP * kvh], sink_ref[ATT_GROUP * kvh + 1]))
    outs = []
    for o2 in _sink_softmax_pv(ss, sks, [vb[:, kvh * ATT_HD:(kvh + 1) * ATT_HD] for kvh in range(ATT_KV_HEADS)]):
        outs += [o2[:SEQ], o2[SEQ:]]
    o_ref[...] = jnp.concatenate(outs, axis=1)


def _att_ctx(sink, cq, ck, cv, dp, pw, pscale):
    kvw = ATT_KV_HEADS * ATT_HD
    pool_in, pool_scratch = _pool_specs(SEQ)
    return pl.pallas_call(
        _att_ctx_kernel,
        grid=(BATCH,),
        in_specs=[
            pl.BlockSpec(memory_space=pltpu.SMEM),
            pl.BlockSpec((SEQ, BR_W), lambda b: (b, 0)),
            pl.BlockSpec((SEQ, kvw), lambda b: (b, 0)),
            pl.BlockSpec((SEQ, kvw), lambda b: (b, 0)),
        ] + pool_in,
        out_specs=[pl.BlockSpec((SEQ, BR_W), lambda b: (b, 0))] * 2,
        out_shape=[jax.ShapeDtypeStruct((N_CTX, BR_W), F32)] * 2,
        scratch_shapes=pool_scratch,
        compiler_params=_cparams(("arbitrary",)),
        name="att_ctx",
    )(sink, cq, ck, cv, dp, pw, pscale.reshape(1, BR_W), _pool_inv_count(SEQ))


def _rope(x, cos, sin_signed):
    w = x.shape[1]
    lane = lax.broadcasted_iota(jnp.int32, (1, w), 1)
    first = (lane % 32) < 16
    swapped = jnp.where(first, pltpu.roll(x, w - 16, 1), pltpu.roll(x, 16, 1))
    return x * cos + swapped * sin_signed


def _att_lat_kernel(sink_ref, q_ref, k_ref, v_ref, ck_ref, cv_ref, cq_ref, sq_ref, ckk_ref, skk_ref,
                    dp_ref, pw_ref, psc_ref, pic_ref, o_ref, yd_ref, qs_ref, ks_ref, vs_ref, pa_ref, pb_ref):
    _pool_body(dp_ref, pw_ref, psc_ref, pic_ref, yd_ref, pa_ref, pb_ref)
    scale = ATT_HD ** -0.5
    nb = DEC_SEQ // CHUNK
    qs_ref[...] = (_rope(q_ref[...], cq_ref[...], sq_ref[...]) * scale).astype(BF16)
    ks_ref[...] = _rope(k_ref[...], ckk_ref[...], skk_ref[...]).astype(BF16)
    vs_ref[...] = v_ref[...].astype(BF16)
    ckb = ck_ref[...].astype(BF16)
    cvb = cv_ref[...].astype(BF16)
    for n in range(nb):
        lo = max(n - 1, 0) * CHUNK
        hi = min(n + 2, nb) * CHUNK
        nloc = hi - lo
        ncol = nloc + PAST_LEN
        col = lax.broadcasted_iota(jnp.int32, (CHUNK, ncol), 1)
        row = lax.broadcasted_iota(jnp.int32, (CHUNK, ncol), 0)
        dist = jnp.abs((n * CHUNK + row) - (lo + col))
        valid = (col >= nloc) | (dist <= WINDOW)
        valid2 = jnp.concatenate([valid] * ATT_GROUP, axis=0)
        first = lax.broadcasted_iota(jnp.int32, (ATT_GROUP * CHUNK, 1), 0) < CHUNK
        outs, ss, vs_, sks = [], [], [], []
        for kvh in range(ATT_KV_HEADS):
            ks = slice(kvh * ATT_HD, (kvh + 1) * ATT_HD)
            q2 = jnp.concatenate([qs_ref[n * CHUNK:(n + 1) * CHUNK, hd * ATT_HD:(hd + 1) * ATT_HD]
                                  for hd in (ATT_GROUP * kvh, ATT_GROUP * kvh + 1)], axis=0)
            kall = jnp.concatenate([ks_ref[lo:hi, ks], ckb[:, ks]], axis=0)
            vs_.append(jnp.concatenate([vs_ref[lo:hi, ks], cvb[:, ks]], axis=0))
            s = lax.dot_general(q2, kall, (((1,), (1,)), ((), ())), preferred_element_type=F32)
            ss.append(jnp.where(valid2, s, NEG_INF))
            sks.append(jnp.where(first, sink_ref[ATT_GROUP * kvh], sink_ref[ATT_GROUP * kvh + 1]))
        for o2 in _sink_softmax_pv(ss, sks, vs_):
            outs += [o2[:CHUNK], o2[CHUNK:]]
        o_ref[n * CHUNK:(n + 1) * CHUNK, :] = jnp.concatenate(outs, axis=1)


def _att_lat(sink, cq, ck, cv, cache_k, cache_v, layer, tables, dp, pw, pscale):
    kvw = ATT_KV_HEADS * ATT_HD
    pool_in, pool_scratch = _pool_specs(DEC_SEQ)
    cosq, sinq, cosk, sink_k = tables
    seq = lambda w: pl.BlockSpec((DEC_SEQ, w), lambda b: (b, 0))
    cache = pl.BlockSpec((None, None, PAST_LEN, kvw), lambda b: (b, layer, 0, 0))
    tab = lambda w: pl.BlockSpec((DEC_SEQ, w), lambda b: (0, 0))
    return pl.pallas_call(
        _att_lat_kernel,
        grid=(DEC_BATCH,),
        in_specs=[pl.BlockSpec(memory_space=pltpu.SMEM), seq(BR_W), seq(kvw), seq(kvw), cache, cache,
                  tab(BR_W), tab(BR_W), tab(kvw), tab(kvw)] + pool_in,
        out_specs=[pl.BlockSpec((DEC_SEQ, BR_W), lambda b: (b, 0))] * 2,
        out_shape=[jax.ShapeDtypeStruct((N_LAT, BR_W), F32)] * 2,
        scratch_shapes=[pltpu.VMEM((DEC_SEQ, BR_W), BF16), pltpu.VMEM((DEC_SEQ, kvw), BF16),
                        pltpu.VMEM((DEC_SEQ, kvw), BF16)] + pool_scratch,
        compiler_params=_cparams(("arbitrary",)),
        name="att_lat",
    )(sink, cq, ck, cv, cache_k, cache_v, cosq, sinq, cosk, sink_k, dp, pw, pscale.reshape(1, BR_W),
      _pool_inv_count(DEC_SEQ))


def _rope_tables():
    pos = np.arange(DEC_SEQ)
    lane = np.arange(ATT_HD)
    p = np.where(lane[None, :] < 32, (pos // GRID_W)[:, None], (pos % GRID_W)[:, None]).astype(np.float32)
    inv = (ROPE_BASE ** (-jnp.arange(0, 32, 2, dtype=F32) / 32))[lane % 16]
    ang = jnp.asarray(p) * inv[None, :]
    sign = jnp.asarray(np.where((lane % 32) < 16, -1.0, 1.0).astype(np.float32))
    cos, sin = jnp.cos(ang), jnp.sin(ang) * sign[None, :]
    return (jnp.tile(cos, (1, ATT_HEADS)), jnp.tile(sin, (1, ATT_HEADS)),
            jnp.tile(cos, (1, ATT_KV_HEADS)), jnp.tile(sin, (1, ATT_KV_HEADS)))


POOL_PAD = 8


def _pool_body(x_ref, w_ref, sc_ref, ic_ref, o_ref, a_ref, b_ref):
    n = x_ref.shape[0]
    ext = n + POOL_PAD
    x = x_ref[...]
    zeros = jnp.zeros((POOL_PAD, BR_W), F32)
    a_ref[0:POOL_PAD, :] = zeros
    b_ref[0:POOL_PAD, :] = zeros
    a_ref[POOL_PAD:POOL_PAD + n, :] = x
    a_ref[POOL_PAD + n:POOL_PAD + ext, :] = zeros
    grp = lax.broadcasted_iota(jnp.int32, (1, BR_W), 1) // POOL_GD
    acc = jnp.zeros((n, BR_W), F32)
    src_ref, dst_ref = a_ref, b_ref
    for m in range(len(POOL_SIZES)):
        back = 1 << m
        dst_ref[POOL_PAD:POOL_PAD + ext, :] = (src_ref[POOL_PAD:POOL_PAD + ext, :]
                                               + src_ref[POOL_PAD - back:POOL_PAD - back + ext, :])
        off = POOL_PAD + back - 1
        acc = jnp.where(grp == m, dst_ref[off:off + n, :], acc)
        src_ref, dst_ref = dst_ref, src_ref
    d = acc * ic_ref[...] - x
    o_ref[...] = jnp.dot(d.astype(BF16), w_ref[...], preferred_element_type=F32) * sc_ref[...]


def _pool_inv_count(seq_len):
    t = np.arange(seq_len)[:, None]
    half = (np.asarray(POOL_SIZES) // 2)[np.arange(BR_W) // POOL_GD][None, :]
    cnt = np.minimum(t + half, seq_len) - np.maximum(t - half, 0)
    return jnp.asarray(1.0 / cnt, F32)


def _pool_specs(seq_len):
    return ([pl.BlockSpec((seq_len, BR_W), lambda b: (b, 0)), pl.BlockSpec((BR_W, BR_W), lambda b: (0, 0)),
             pl.BlockSpec((1, BR_W), lambda b: (0, 0)), pl.BlockSpec((seq_len, BR_W), lambda b: (0, 0))],
            [pltpu.VMEM((seq_len + 2 * POOL_PAD, BR_W), F32)] * 2)


def _merge_kernel(x_ref, mod_ref, g_ref, a_ref, bonus_ref, of_ref, ob_ref, yc_ref, yd_ref,
                  wz_ref, wmg_ref, ws_ref, bs_ref, lng_ref, lnb_ref, ones_ref, wup_ref, wo_ref, gf_ref,
                  o_ref, *, lat, final):
    i = pl.program_id(0)
    m = mod_ref[pl.ds(_mod_row(i, TM_MERGE, lat), 1), :]
    x = x_ref[...]
    hb = _norm_mod(x, g_ref[...], m).astype(BF16)

    a = a_ref[...]
    a_u, a_v = a[:, :BR_W], a[:, BR_W:]
    grp = lax.broadcasted_iota(jnp.int32, (1, BR_W), 1) // A_GD
    svs = []
    for c in range(TM_MERGE // CHUNK):
        vc = a_v[c * CHUNK:(c + 1) * CHUNK, :]
        sv = bs_ref[...]
        for g in range(A_GROUPS):
            sv = sv + jnp.dot(ws_ref[g], jnp.where(grp == g, vc, 0.0).astype(BF16), preferred_element_type=F32)
        svs.append(sv)
    y_a = a_u * jnp.concatenate(svs, axis=0)

    ones = ones_ref[...]
    osum = of_ref[...] + ob_ref[...]
    mu = _seg_sum(osum, ones) * (1.0 / RW_HD)
    dev = osum - mu
    var = _seg_sum(dev * dev, ones) * (1.0 / RW_HD)
    y_b = dev * lax.rsqrt(var + GN_EPS) * lng_ref[...] + lnb_ref[...] + bonus_ref[...]

    merged = jnp.zeros((TM_MERGE, D_MODEL), F32)
    z_all = jnp.dot(hb, wz_ref[0], preferred_element_type=F32)
    for n, y in enumerate((y_a, y_b, yc_ref[...], yd_ref[...])):
        z = z_all[:, n * BR_W:(n + 1) * BR_W]
        ys = y * (z * _sigmoid(z))
        up = jnp.dot(ys.astype(BF16), wup_ref[n], preferred_element_type=F32)
        mg = jnp.dot(hb, wmg_ref[0, :, n * D_MODEL:(n + 1) * D_MODEL], preferred_element_type=F32)
        merged = merged + _sigmoid(mg) * up
    gate = m[:, 2 * D_MODEL:]
    out = x + gate * jnp.dot(merged.astype(BF16), wo_ref[...], preferred_element_type=F32)
    if final:
        ms = jnp.mean(out * out, axis=-1, keepdims=True)
        out = out * lax.rsqrt(ms + NORM_EPS) * gf_ref[...]
    o_ref[...] = out


def _merge(x, mod_l, g, a, bonus, o_f, o_b, y_c, y_d, w_in_b, layer, ws, bs_tile, ln_g, ln_b, ones_bd, wup, wo,
           g_final, lat, final):
    row = lambda w: pl.BlockSpec((TM_MERGE, w), lambda i: (i, 0))
    full2 = lambda s: pl.BlockSpec(s, lambda i: (0, 0))
    z0, z1 = IN_SMALL, IN_SMALL + N_BRANCH * BR_W
    cols = lambda c0, w: pl.BlockSpec((pl.Element(1), pl.Element(D_MODEL), pl.Element(w)), lambda i: (layer, 0, c0))
    return pl.pallas_call(
        functools.partial(_merge_kernel, lat=lat, final=final),
        grid=(N_GRP // TM_MERGE,),
        in_specs=[row(D_MODEL), full2((8, 3 * D_MODEL)), full2((1, D_MODEL)), row(2 * BR_W),
                  row(BR_W), row(BR_W), row(BR_W), row(BR_W), row(BR_W),
                  cols(z0, N_BRANCH * BR_W), cols(z1, N_BRANCH * D_MODEL),
                  pl.BlockSpec((None, A_GROUPS, CHUNK, CHUNK), lambda i: (layer, 0, 0, 0)), full2((CHUNK, BR_W)),
                  full2((1, BR_W)), full2((1, BR_W)), full2((BR_W, BR_W)),
                  pl.BlockSpec((None, N_BRANCH, BR_W, D_MODEL), lambda i: (layer, 0, 0, 0)),
                  pl.BlockSpec((None, D_MODEL, D_MODEL), lambda i: (layer, 0, 0)), full2((1, D_MODEL))],
        out_specs=row(D_MODEL),
        out_shape=jax.ShapeDtypeStruct((N_GRP, D_MODEL), F32),
        compiler_params=_cparams(("arbitrary",)),
        name=("merge_lat" if lat else "merge_ctx") + ("_final" if final else ""),
    )(x, mod_l, g.reshape(1, D_MODEL), a, bonus, o_f, o_b, y_c, y_d, w_in_b, w_in_b, ws, bs_tile,
      ln_g.reshape(1, BR_W), ln_b.reshape(1, BR_W), ones_bd, wup, wo, g_final.reshape(1, D_MODEL))


def _block_diag(blocks):
    n, r, c = blocks.shape
    eye = jnp.eye(n, dtype=blocks.dtype)
    return (eye[:, None, :, None] * blocks[:, :, None, :]).reshape(n * r, n * c)


def kernel(x_prompt, x_sample, cache_k, cache_v, state_rwkv, c, c_ctx, w_mod, b_mod, g_norm, w_in, w_s, b_s,
           rw_w0, rw_w_up, rw_a0, rw_a_up, rw_k_k, rw_k_a, rw_r_k, rw_ln_g, rw_ln_b, att_sink, pool_w,
           pool_scale, w_up, w_o, g_final):
    xs = [x_prompt.reshape(N_CTX, D_MODEL), x_sample.reshape(N_LAT, D_MODEL)]
    cond = jnp.concatenate([c_ctx[None, :], c, jnp.zeros((8 - 1 - DEC_BATCH, D_MODEL), F32)], axis=0)
    mod = _modulation(cond, w_mod, b_mod)

    ones_bd = _block_diag(jnp.ones((RW_HEADS, RW_HD, RW_HD), BF16))
    tables = _rope_tables()
    kvw = ATT_KV_HEADS * ATT_HD
    cache_k4 = cache_k.reshape(DEC_BATCH, DEPTH, PAST_LEN, kvw)
    cache_v4 = cache_v.reshape(DEC_BATCH, DEPTH, PAST_LEN, kvw)
    w_in_b, w_s_b, w_up_b, w_o_b = (w.astype(BF16) for w in (w_in, w_s, w_up, w_o))

    new_k, new_v, new_s = [], [], []
    for l in range(DEPTH):
        pw = _block_diag(pool_w[l]).astype(BF16)
        bs_tile = jnp.repeat(b_s[l].T, A_GD, axis=1)
        final = l == DEPTH - 1
        proj = []
        for lat in (False, True):
            a, cq, ck, cv, dp, r, v, *pre, bonus = _inproj(
                xs[lat], mod[l], g_norm[l], w_in_b, l, rw_w0[l], rw_w_up[l], rw_a0[l], rw_a_up[l], rw_k_k[l],
                rw_k_a[l], rw_r_k[l], ones_bd, lat)
            proj.append((a, cq, ck, cv, dp, pre + [r, v], bonus))

        of_l, ob_l, _, *ctx_lanes = _scan(proj[1][5], _lat_state(state_rwkv[:, l]), nb=DEC_BATCH, t_len=DEC_SEQ,
                                          name="scan_lat", side_pre=proj[0][5], side_nb=BATCH, side_len=SEQ)
        of_c, ob_c, _, s_fin = _scan(None, None, nb=BATCH, t_len=SEQ, name="scan_ctx", lanes=ctx_lanes,
                                     emit_state=True)
        new_s.append(s_fin.reshape(RW_HEADS, 2, BATCH, RW_HD, RW_HD).transpose(2, 1, 0, 3, 4))
        scans = ((of_c, ob_c), (of_l, ob_l))

        for lat in (False, True):
            a, cq, ck, cv, dp, _, bonus = proj[lat]
            o_f, o_b = scans[lat]
            if lat:
                y_c, y_d = _att_lat(att_sink[l], cq, ck, cv, cache_k4, cache_v4, l, tables, dp, pw, pool_scale[l])
            else:
                y_c, y_d = _att_ctx(att_sink[l], cq, ck, cv, dp, pw, pool_scale[l])
                new_k.append(ck.reshape(BATCH, SEQ, ATT_KV_HEADS, ATT_HD))
                new_v.append(cv.reshape(BATCH, SEQ, ATT_KV_HEADS, ATT_HD))
            xs[lat] = _merge(xs[lat], mod[l], g_norm[l], a, bonus, o_f.reshape(N_GRP, BR_W),
                             o_b.reshape(N_GRP, BR_W), y_c, y_d, w_in_b, l, w_s_b, bs_tile, rw_ln_g[l], rw_ln_b[l],
                             ones_bd, w_up_b, w_o_b, g_final, lat, final)

    y_prompt = xs[0].reshape(BATCH, SEQ, D_MODEL)
    y_sample = xs[1].reshape(DEC_BATCH, DEC_SEQ, D_MODEL)
    return (y_prompt, y_sample, jnp.stack(new_k, axis=1), jnp.stack(new_v, axis=1), jnp.stack(new_s, axis=1))
```

```python
import functools

import numpy as np
import jax
import jax.numpy as jnp
from jax import lax
from jax.experimental import pallas as pl
from jax.experimental.pallas import tpu as pltpu

D_MODEL = 1024
BATCH = 16
SEQ = 256
DEPTH = 2
DEC_BATCH = 4
DEC_SEQ = 1024
PAST_LEN = 512
GRID_W = 64
N_BRANCH = 4
BR_W = D_MODEL // 4
CHUNK = 128
A_GROUPS = 4
A_GD = BR_W // A_GROUPS
RW_HD = 64
RW_HEADS = BR_W // RW_HD
DECAY_RANK = 64
ICL_RANK = 64
ATT_HD = 64
ATT_HEADS = BR_W // ATT_HD
ATT_KV_HEADS = 2
ATT_GROUP = ATT_HEADS // ATT_KV_HEADS
WINDOW = 128
ROPE_BASE = 10000.0
POOL_SIZES = (2, 4, 8, 16)
POOL_GD = BR_W // len(POOL_SIZES)
NORM_EPS = 1e-6
GN_EPS = 64e-5
NEG_INF = -1e30
DECAY_SCALE = float(np.exp(-0.5))

N_CTX = BATCH * SEQ
N_LAT = DEC_BATCH * DEC_SEQ
N_GRP = N_CTX
assert N_LAT == N_GRP
IN_SMALL = 2176

F32 = jnp.float32
BF16 = jnp.bfloat16

V7X_VMEM_LIMIT = 56 * 1024 * 1024
LANES = 128

TM_IN = 512
TM_MERGE = 1024
SCAN_TC = 32


def _cparams(sem):
    return pltpu.CompilerParams(dimension_semantics=sem, vmem_limit_bytes=V7X_VMEM_LIMIT)


def _mod_row(i, tm, lat):
    return 1 + i // (DEC_SEQ // tm) if lat else 0


def _norm_mod(x, g, m):
    ms = jnp.mean(x * x, axis=-1, keepdims=True)
    y = x * lax.rsqrt(ms + NORM_EPS) * g
    shift = m[:, :D_MODEL]
    scale = m[:, D_MODEL:2 * D_MODEL]
    return y * (1.0 + scale) + shift


def _sigmoid(x):
    return 1.0 / (1.0 + jnp.exp(-x))


def _split(x):
    hi = x.astype(BF16)
    return hi, (x - hi.astype(F32)).astype(BF16)


def _seg_sum(x, ones_bd):
    hi, lo = _split(x)
    return (jnp.dot(hi, ones_bd, preferred_element_type=F32)
            + jnp.dot(lo, ones_bd, preferred_element_type=F32))


def _dot_split(x, w):
    xh, xl = _split(x)
    wh, wl = _split(w)
    lhs = jnp.concatenate([xh, xl, xh], axis=1)
    rhs = jnp.concatenate([wh, wh, wl], axis=0)
    return jnp.dot(lhs, rhs, preferred_element_type=F32)


def _mod_kernel(c_ref, w_ref, b_ref, o_ref):
    cnd = c_ref[...]
    s = cnd * _sigmoid(cnd)
    o_ref[...] = jnp.dot(s.astype(BF16), w_ref[...].astype(BF16), preferred_element_type=F32) + b_ref[...]


def _modulation(cond, w_mod, b_mod):
    nt = 3 * D_MODEL // 1024
    return pl.pallas_call(
        _mod_kernel,
        grid=(DEPTH, nt),
        in_specs=[
            pl.BlockSpec((8, D_MODEL), lambda l, j: (0, 0)),
            pl.BlockSpec((None, D_MODEL, 1024), lambda l, j: (l, 0, j)),
            pl.BlockSpec((None, 1, 1024), lambda l, j: (l, 0, j)),
        ],
        out_specs=pl.BlockSpec((None, 8, 1024), lambda l, j: (l, 0, j)),
        out_shape=jax.ShapeDtypeStruct((DEPTH, 8, 3 * D_MODEL), F32),
        compiler_params=_cparams(("arbitrary", "arbitrary")),
        name="modulation",
    )(cond, w_mod, b_mod.reshape(DEPTH, 1, 3 * D_MODEL))


def _inproj_kernel(x_ref, mod_ref, g_ref, w_ref, w0_ref, wup_ref, a0_ref, aup_ref, kkw_ref, ka_ref, rk_ref, ones_ref,
                   a_ref, q_ref, ck_ref, cv_ref, d_ref, r_ref, v_ref,
                   w0o, w1o, kka0o, kka1o, kd0o, kd1o, kko, bonuso, *, lat):
    i = pl.program_id(0)
    m = mod_ref[pl.ds(_mod_row(i, TM_IN, lat), 1), :]
    h = _norm_mod(x_ref[...], g_ref[...], m)
    p = jnp.dot(h.astype(BF16), w_ref[0], preferred_element_type=F32)
    a_ref[...] = p[:, 0:512]
    q_ref[...] = p[:, 1408:1664]
    ck_ref[...] = p[:, 1664:1792]
    cv_ref[...] = p[:, 1792:1920]
    d_ref[...] = p[:, 1920:2176]

    r = p[:, 512:768]
    k = p[:, 768:1024]
    v = p[:, 1024:1280]
    wd_t = jnp.tanh(p[:, 1280:1344])
    ad = p[:, 1344:1408]
    r_ref[...] = r
    v_ref[...] = v
    ones = ones_ref[...]
    kk = k * kkw_ref[...]
    ss = _seg_sum(kk * kk, ones)
    kkn = kk * lax.rsqrt(ss + 1e-12)
    kko[...] = kkn
    kd_sum = jnp.zeros_like(v)
    for d, (wo, kkao, kdo) in enumerate(((w0o, kka0o, kd0o), (w1o, kka1o, kd1o))):
        pre = w0_ref[pl.ds(d, 1), :] + _dot_split(wd_t, wup_ref[d])
        wo[...] = jnp.exp(-DECAY_SCALE * _sigmoid(pre))
        a = _sigmoid(a0_ref[pl.ds(d, 1), :] + _dot_split(ad, aup_ref[d]))
        k_d = k * (1.0 + (a - 1.0) * ka_ref[...])
        kdo[...] = k_d
        kkao[...] = kkn * a
        kd_sum = kd_sum + k_d
    bonuso[...] = _seg_sum(r * kd_sum * rk_ref[...], ones) * v


def _inproj(x, mod_l, g, w_in_b, layer, w0, wup, a0, aup, k_k, k_a, r_k, ones_bd, lat):
    kvw = ATT_KV_HEADS * ATT_HD
    widths = (2 * BR_W, BR_W, kvw, kvw, BR_W) + (BR_W,) * 10
    full2 = lambda s: pl.BlockSpec(s, lambda i: (0, 0))
    return pl.pallas_call(
        functools.partial(_inproj_kernel, lat=lat),
        grid=(N_GRP // TM_IN,),
        in_specs=[
            pl.BlockSpec((TM_IN, D_MODEL), lambda i: (i, 0)),
            full2((8, 3 * D_MODEL)),
            full2((1, D_MODEL)),
            pl.BlockSpec((pl.Element(1), pl.Element(D_MODEL), pl.Element(IN_SMALL)), lambda i: (layer, 0, 0)),
            full2((2, BR_W)),
            pl.BlockSpec((2, DECAY_RANK, BR_W), lambda i: (0, 0, 0)),
            full2((2, BR_W)),
            pl.BlockSpec((2, ICL_RANK, BR_W), lambda i: (0, 0, 0)),
            full2((1, BR_W)), full2((1, BR_W)), full2((1, BR_W)),
            full2((BR_W, BR_W)),
        ],
        out_specs=[pl.BlockSpec((TM_IN, w), lambda i: (i, 0)) for w in widths],
        out_shape=[jax.ShapeDtypeStruct((N_GRP, w), F32) for w in widths],
        compiler_params=_cparams(("arbitrary",)),
        name="in_proj_lat" if lat else "in_proj_ctx",
    )(x, mod_l, g.reshape(1, D_MODEL), w_in_b, w0, wup, a0, aup,
      k_k.reshape(1, BR_W), k_a.reshape(1, BR_W), r_k.reshape(1, BR_W), ones_bd)


def _to_lanes(z0, z1, rep):
    lo = lax.broadcasted_iota(jnp.int32, (1, LANES), 1) < RW_HD
    parts = []
    for hp in range(RW_HEADS // 2):
        a = z0[:, hp * LANES:(hp + 1) * LANES]
        b = z1[:, hp * LANES:(hp + 1) * LANES]
        parts.append(jnp.where(lo, a, pltpu.roll(b, RW_HD, 1)))
        parts.append(jnp.where(lo, pltpu.roll(a, RW_HD, 1), b))
    return jnp.concatenate(parts * rep, axis=0).T


def _scan_kernel(*refs, nb, has_init, lanes_in, n_side, emit_state):
    if has_init:
        s0_ref, refs = refs[0], refs[1:]
    side_in, refs = refs[:2 * n_side], refs[2 * n_side:]
    n_in = 6 if lanes_in else 12
    data_in, refs = refs[:n_in], refs[n_in:]
    of_ref, ob_ref, st_ref = refs[:3]
    side_out = refs[3:3 + n_side]
    n_out = 3 + n_side + (1 if emit_state else 0)
    w_s, kk_s, kka_s, kd_s, r_s, v_s, o_s = refs[n_out:]
    nk, vh, _ = st_ref.shape
    tc = w_s.shape[0]
    rep = RW_HD // vh
    nrec = LANES // rep
    lane = lax.broadcasted_iota(jnp.int32, (1, LANES), 1)
    grp = lane // nrec
    lo = lane < RW_HD

    def rows(f_ref, b_ref, j, n_steps):
        return jnp.concatenate([f_ref[:, j, :], b_ref[:, n_steps - 1 - j, :]], axis=0)

    if lanes_in:
        w_in, kk_in, kka_in, kd_in, r_in, v_in = data_in
    else:
        w_in, kk_in, kka_in, kd_in, r_in, v_in = w_s, kk_s, kka_s, kd_s, r_s, v_s
        for j in range(0, tc, 2):
            for si, dst in enumerate((w_s, kk_s, kka_s, kd_s, r_s, v_s)):
                f_ref, b_ref = data_in[2 * si], data_in[2 * si + 1]
                gt = _to_lanes(rows(f_ref, b_ref, j, tc), rows(f_ref, b_ref, j + 1, tc), rep)
                for s in range(2):
                    blk = gt[s * RW_HD:(s + 1) * RW_HD]
                    if dst is v_s and rep > 1:
                        blk = sum(jnp.where(grp == q, blk[q * vh:(q + 1) * vh], 0.0) for q in range(rep))
                    dst[j + s] = blk

    p = jnp.ones((nk, LANES), F32)
    for t in range(tc):
        kk_s[t] = kk_in[t] * p
        p = p * w_in[t]
        inv = 1.0 / p
        kka_s[t] = kka_in[t] * inv
        kd_s[t] = kd_in[t] * inv
        r_s[t] = r_in[t] * p
    w_s[0] = p

    @pl.when(pl.program_id(0) == 0)
    def _():
        if has_init:
            st_ref[...] = s0_ref[...]
        else:
            st_ref[...] = jnp.zeros(st_ref.shape, F32)

    sa0 = jnp.zeros((vh, LANES), F32)
    for k in range(nk):
        sa0 = sa0 + st_ref[k] * kk_s[0, pl.ds(k, 1), :]

    def step(t, sa):
        tn = jnp.minimum(t + 1, tc - 1)
        vt = v_in[t]
        o = jnp.zeros((vh, LANES), F32)
        san = jnp.zeros((vh, LANES), F32)
        for k in range(nk):
            new = st_ref[k] - kka_s[t, pl.ds(k, 1), :] * sa + kd_s[t, pl.ds(k, 1), :] * vt
            st_ref[k] = new
            o = o + new * r_s[t, pl.ds(k, 1), :]
            san = san + new * kk_s[tn, pl.ds(k, 1), :]
        o_s[t] = o
        return san

    if n_side:
        side_tc = side_in[0].shape[1]
        per = tc // (side_tc // 2)

        def steps_and_side_pair(q, sa):
            for u in range(per):
                sa = step(per * q + u, sa)
            j = 2 * q
            words = None
            for si in range(n_side):
                f_ref, b_ref = side_in[2 * si], side_in[2 * si + 1]
                gt = _to_lanes(rows(f_ref, b_ref, j, side_tc), rows(f_ref, b_ref, j + 1, side_tc), 1)
                side_out[si][j] = gt[0:RW_HD]
                side_out[si][j + 1] = gt[RW_HD:2 * RW_HD]
                bits = lax.bitcast_convert_type(gt[0:8], jnp.uint32)
                words = bits if words is None else words | bits
            zero = lax.shift_right_logical(lax.shift_right_logical(words, jnp.uint32(16)), jnp.uint32(16))
            return sa + lax.bitcast_convert_type(zero, F32)[0:1, :]

        lax.fori_loop(0, side_tc // 2, steps_and_side_pair, sa0)
    else:
        lax.fori_loop(0, tc, step, sa0)
    for k in range(nk):
        st_ref[k] = st_ref[k] * w_s[0, pl.ds(k, 1), :]

    for j in range(0, tc, 2):
        tiles = []
        for s in range(2):
            o = o_s[j + s]
            if rep == 1:
                tiles.append(o)
            else:
                tiles.extend(jnp.where(grp == q, o, 0.0) for q in range(rep))
        mt = jnp.concatenate(tiles, axis=0).T
        sm = mt[0:nrec]
        for q in range(1, rep):
            sm = sm + mt[q * nrec:(q + 1) * nrec]
        rs = pltpu.roll(sm, RW_HD, 1)
        for hp in range(RW_HEADS // 2):
            e0, e1 = (2 * hp) * 2 * nb, (2 * hp + 1) * 2 * nb
            cur = jnp.where(lo, sm[e0:e0 + 2 * nb], rs[e1:e1 + 2 * nb])
            nxt = jnp.where(lo, rs[e0:e0 + 2 * nb], sm[e1:e1 + 2 * nb])
            cs = slice(hp * LANES, (hp + 1) * LANES)
            of_ref[:, j, cs] = cur[0:nb]
            of_ref[:, j + 1, cs] = nxt[0:nb]
            ob_ref[:, tc - 1 - j, cs] = cur[nb:2 * nb]
            ob_ref[:, tc - 2 - j, cs] = nxt[nb:2 * nb]

    if emit_state:
        fs_ref = refs[n_out - 1]

        @pl.when(pl.program_id(0) == pl.num_programs(0) - 1)
        def _():
            for vp in range(vh // 2):
                pair = jnp.concatenate([st_ref[:, 2 * vp, :], st_ref[:, 2 * vp + 1, :]], axis=0)
                fs_ref[:, vp, :] = pair.T


def _fwd_bwd_args(pre, nb, t_len, tc, n_t):
    w0, w1, kka0, kka1, kd0, kd1, kk, r, v = (p.reshape(nb, t_len, BR_W) for p in pre)
    fwd = pl.BlockSpec((nb, tc, BR_W), lambda i: (0, i, 0))
    bwd = pl.BlockSpec((nb, tc, BR_W), lambda i: (0, n_t - 1 - i, 0))
    return (w0, w1, kk, kk, kka0, kka1, kd0, kd1, r, r, v, v), [fwd, bwd] * 6


def _scan(pre, s0, *, nb, t_len, name, lanes=None, side_pre=None, side_nb=None, side_len=None, emit_state=False):
    n_t = t_len // SCAN_TC
    vh = RW_HD * nb * 2 * RW_HEADS // LANES
    sspec = pl.BlockSpec((RW_HD, vh, LANES), lambda i: (0, 0, 0))
    has_init = s0 is not None
    args, in_specs = ((s0,), [sspec]) if has_init else ((), [])
    n_side, side_out_specs, side_out_shape = 0, [], []
    if side_pre is not None:
        n_side = 6
        side_tc = side_len // n_t
        side_args, side_specs = _fwd_bwd_args(side_pre, side_nb, side_len, side_tc, n_t)
        args, in_specs = args + side_args, in_specs + side_specs
        side_out_specs = [pl.BlockSpec((side_tc, RW_HD, LANES), lambda i: (i, 0, 0))] * n_side
        side_out_shape = [jax.ShapeDtypeStruct((side_len, RW_HD, LANES), F32)] * n_side
    if lanes is None:
        own_args, own_specs = _fwd_bwd_args(pre, nb, t_len, SCAN_TC, n_t)
    else:
        own_args = tuple(lanes)
        own_specs = ([pl.BlockSpec((SCAN_TC, RW_HD, LANES), lambda i: (i, 0, 0))] * 5
                     + [pl.BlockSpec((SCAN_TC, vh, LANES), lambda i: (i, 0, 0))])
    kbuf = pltpu.VMEM((SCAN_TC, RW_HD, LANES), F32)
    vbuf = pltpu.VMEM((SCAN_TC, vh, LANES), F32)
    state_specs, state_shape = [], []
    if emit_state:
        state_specs = [pl.BlockSpec((LANES, vh // 2, 2 * RW_HD), lambda i: (0, 0, 0))]
        state_shape = [jax.ShapeDtypeStruct((LANES, vh // 2, 2 * RW_HD), F32)]
    return pl.pallas_call(
        functools.partial(_scan_kernel, nb=nb, has_init=has_init, lanes_in=lanes is not None, n_side=n_side,
                          emit_state=emit_state),
        grid=(n_t,),
        in_specs=in_specs + own_specs,
        out_specs=[pl.BlockSpec((nb, SCAN_TC, BR_W), lambda i: (0, i, 0)),
                   pl.BlockSpec((nb, SCAN_TC, BR_W), lambda i: (0, n_t - 1 - i, 0)),
                   sspec] + side_out_specs + state_specs,
        out_shape=[jax.ShapeDtypeStruct((nb, t_len, BR_W), F32),
                   jax.ShapeDtypeStruct((nb, t_len, BR_W), F32),
                   jax.ShapeDtypeStruct((RW_HD, vh, LANES), F32)] + side_out_shape + state_shape,
        scratch_shapes=[kbuf] * 5 + [vbuf] * 2,
        compiler_params=_cparams(("arbitrary",)),
        name=name,
    )(*args, *own_args)


LAT_VS = 4
LAT_VH = RW_HD // LAT_VS


def _lat_state(s):
    s6 = s.reshape(DEC_BATCH, 2, RW_HEADS, LAT_VS, LAT_VH, RW_HD)
    return s6.transpose(5, 4, 3, 2, 1, 0).reshape(RW_HD, LAT_VH, LANES)


def _sink_softmax_pv(ss, sks, vbs):
    ms = [jnp.maximum(jnp.max(s, axis=-1, keepdims=True), sk) for s, sk in zip(ss, sks)]
    ps = [jnp.exp(s - m) for s, m in zip(ss, ms)]
    dens = [jnp.sum(p, axis=-1, keepdims=True) + jnp.exp(sk - m) for p, sk, m in zip(ps, sks, ms)]
    return [jnp.dot(p.astype(BF16), vb, preferred_element_type=F32) / den for p, vb, den in zip(ps, vbs, dens)]


def _att_ctx_kernel(sink_ref, q_ref, k_ref, v_ref, dp_ref, pw_ref, psc_ref, pic_ref, o_ref, yd_ref, pa_ref, pb_ref):
    _pool_body(dp_ref, pw_ref, psc_ref, pic_ref, yd_ref, pa_ref, pb_ref)
    scale = ATT_HD ** -0.5
    q = q_ref[...] * scale
    kb = k_ref[...].astype(BF16)
    vb = v_ref[...].astype(BF16)
    first = lax.broadcasted_iota(jnp.int32, (ATT_GROUP * SEQ, 1), 0) < SEQ
    ss, sks = [], []
    for kvh in range(ATT_KV_HEADS):
        ks = slice(kvh * ATT_HD, (kvh + 1) * ATT_HD)
        q2 = jnp.concatenate([q[:, hd * ATT_HD:(hd + 1) * ATT_HD]
                              for hd in (ATT_GROUP * kvh, ATT_GROUP * kvh + 1)], axis=0).astype(BF16)
        ss.append(lax.dot_general(q2, kb[:, ks], (((1,), (1,)), ((), ())), preferred_element_type=F32))
        sks.append(jnp.where(first, sink_ref[ATT_GROUP * kvh], sink_ref[ATT_GROUP * kvh + 1]))
    outs = []
    for o2 in _sink_softmax_pv(ss, sks, [vb[:, kvh * ATT_HD:(kvh + 1) * ATT_HD] for kvh in range(ATT_KV_HEADS)]):
        outs += [o2[:SEQ], o2[SEQ:]]
    o_ref[...] = jnp.concatenate(outs, axis=1)


def _att_ctx(sink, cq, ck, cv, dp, pw, pscale):
    kvw = ATT_KV_HEADS * ATT_HD
    pool_in, pool_scratch = _pool_specs(SEQ)
    return pl.pallas_call(
        _att_ctx_kernel,
        grid=(BATCH,),
        in_specs=[
            pl.BlockSpec(memory_space=pltpu.SMEM),
            pl.BlockSpec((SEQ, BR_W), lambda b: (b, 0)),
            pl.BlockSpec((SEQ, kvw), lambda b: (b, 0)),
            pl.BlockSpec((SEQ, kvw), lambda b: (b, 0)),
        ] + pool_in,
        out_specs=[pl.BlockSpec((SEQ, BR_W), lambda b: (b, 0))] * 2,
        out_shape=[jax.ShapeDtypeStruct((N_CTX, BR_W), F32)] * 2,
        scratch_shapes=pool_scratch,
        compiler_params=_cparams(("arbitrary",)),
        name="att_ctx",
    )(sink, cq, ck, cv, dp, pw, pscale.reshape(1, BR_W), _pool_inv_count(SEQ))


def _rope(x, cos, sin_signed):
    w = x.shape[1]
    lane = lax.broadcasted_iota(jnp.int32, (1, w), 1)
    first = (lane % 32) < 16
    swapped = jnp.where(first, pltpu.roll(x, w - 16, 1), pltpu.roll(x, 16, 1))
    return x * cos + swapped * sin_signed


def _att_lat_kernel(sink_ref, q_ref, k_ref, v_ref, ck_ref, cv_ref, cq_ref, sq_ref, ckk_ref, skk_ref,
                    dp_ref, pw_ref, psc_ref, pic_ref, o_ref, yd_ref, qs_ref, ks_ref, vs_ref, pa_ref, pb_ref):
    _pool_body(dp_ref, pw_ref, psc_ref, pic_ref, yd_ref, pa_ref, pb_ref)
    scale = ATT_HD ** -0.5
    nb = DEC_SEQ // CHUNK
    qs_ref[...] = (_rope(q_ref[...], cq_ref[...], sq_ref[...]) * scale).astype(BF16)
    ks_ref[...] = _rope(k_ref[...], ckk_ref[...], skk_ref[...]).astype(BF16)
    vs_ref[...] = v_ref[...].astype(BF16)
    ckb = ck_ref[...].astype(BF16)
    cvb = cv_ref[...].astype(BF16)
    for n in range(nb):
        lo = max(n - 1, 0) * CHUNK
        hi = min(n + 2, nb) * CHUNK
        nloc = hi - lo
        ncol = nloc + PAST_LEN
        col = lax.broadcasted_iota(jnp.int32, (CHUNK, ncol), 1)
        row = lax.broadcasted_iota(jnp.int32, (CHUNK, ncol), 0)
        dist = jnp.abs((n * CHUNK + row) - (lo + col))
        valid = (col >= nloc) | (dist <= WINDOW)
        valid2 = jnp.concatenate([valid] * ATT_GROUP, axis=0)
        first = lax.broadcasted_iota(jnp.int32, (ATT_GROUP * CHUNK, 1), 0) < CHUNK
        outs, ss, vs_, sks = [], [], [], []
        for kvh in range(ATT_KV_HEADS):
            ks = slice(kvh * ATT_HD, (kvh + 1) * ATT_HD)
            q2 = jnp.concatenate([qs_ref[n * CHUNK:(n + 1) * CHUNK, hd * ATT_HD:(hd + 1) * ATT_HD]
                                  for hd in (ATT_GROUP * kvh, ATT_GROUP * kvh + 1)], axis=0)
            kall = jnp.concatenate([ks_ref[lo:hi, ks], ckb[:, ks]], axis=0)
            vs_.append(jnp.concatenate([vs_ref[lo:hi, ks], cvb[:, ks]], axis=0))
            s = lax.dot_general(q2, kall, (((1,), (1,)), ((), ())), preferred_element_type=F32)
            ss.append(jnp.where(valid2, s, NEG_INF))
            sks.append(jnp.where(first, sink_ref[ATT_GROUP * kvh], sink_ref[ATT_GROUP * kvh + 1]))
        for o2 in _sink_softmax_pv(ss, sks, vs_):
            outs += [o2[:CHUNK], o2[CHUNK:]]
        o_ref[n * CHUNK:(n + 1) * CHUNK, :] = jnp.concatenate(outs, axis=1)


def _att_lat(sink, cq, ck, cv, cache_k, cache_v, layer, tables, dp, pw, pscale):
    kvw = ATT_KV_HEADS * ATT_HD
    pool_in, pool_scratch = _pool_specs(DEC_SEQ)
    cosq, sinq, cosk, sink_k = tables
    seq = lambda w: pl.BlockSpec((DEC_SEQ, w), lambda b: (b, 0))
    cache = pl.BlockSpec((None, None, PAST_LEN, kvw), lambda b: (b, layer, 0, 0))
    tab = lambda w: pl.BlockSpec((DEC_SEQ, w), lambda b: (0, 0))
    return pl.pallas_call(
        _att_lat_kernel,
        grid=(DEC_BATCH,),
        in_specs=[pl.BlockSpec(memory_space=pltpu.SMEM), seq(BR_W), seq(kvw), seq(kvw), cache, cache,
                  tab(BR_W), tab(BR_W), tab(kvw), tab(kvw)] + pool_in,
        out_specs=[pl.BlockSpec((DEC_SEQ, BR_W), lambda b: (b, 0))] * 2,
        out_shape=[jax.ShapeDtypeStruct((N_LAT, BR_W), F32)] * 2,
        scratch_shapes=[pltpu.VMEM((DEC_SEQ, BR_W), BF16), pltpu.VMEM((DEC_SEQ, kvw), BF16),
                        pltpu.VMEM((DEC_SEQ, kvw), BF16)] + pool_scratch,
        compiler_params=_cparams(("arbitrary",)),
        name="att_lat",
    )(sink, cq, ck, cv, cache_k, cache_v, cosq, sinq, cosk, sink_k, dp, pw, pscale.reshape(1, BR_W),
      _pool_inv_count(DEC_SEQ))


def _rope_tables():
    pos = np.arange(DEC_SEQ)
    lane = np.arange(ATT_HD)
    p = np.where(lane[None, :] < 32, (pos // GRID_W)[:, None], (pos % GRID_W)[:, None]).astype(np.float32)
    inv = (ROPE_BASE ** (-jnp.arange(0, 32, 2, dtype=F32) / 32))[lane % 16]
    ang = jnp.asarray(p) * inv[None, :]
    sign = jnp.asarray(np.where((lane % 32) < 16, -1.0, 1.0).astype(np.float32))
    cos, sin = jnp.cos(ang), jnp.sin(ang) * sign[None, :]
    return (jnp.tile(cos, (1, ATT_HEADS)), jnp.tile(sin, (1, ATT_HEADS)),
            jnp.tile(cos, (1, ATT_KV_HEADS)), jnp.tile(sin, (1, ATT_KV_HEADS)))


POOL_PAD = 8


def _pool_body(x_ref, w_ref, sc_ref, ic_ref, o_ref, a_ref, b_ref):
    n = x_ref.shape[0]
    ext = n + POOL_PAD
    x = x_ref[...]
    zeros = jnp.zeros((POOL_PAD, BR_W), F32)
    a_ref[0:POOL_PAD, :] = zeros
    b_ref[0:POOL_PAD, :] = zeros
    a_ref[POOL_PAD:POOL_PAD + n, :] = x
    a_ref[POOL_PAD + n:POOL_PAD + ext, :] = zeros
    grp = lax.broadcasted_iota(jnp.int32, (1, BR_W), 1) // POOL_GD
    acc = jnp.zeros((n, BR_W), F32)
    src_ref, dst_ref = a_ref, b_ref
    for m in range(len(POOL_SIZES)):
        back = 1 << m
        dst_ref[POOL_PAD:POOL_PAD + ext, :] = (src_ref[POOL_PAD:POOL_PAD + ext, :]
                                               + src_ref[POOL_PAD - back:POOL_PAD - back + ext, :])
        off = POOL_PAD + back - 1
        acc = jnp.where(grp == m, dst_ref[off:off + n, :], acc)
        src_ref, dst_ref = dst_ref, src_ref
    d = acc * ic_ref[...] - x
    o_ref[...] = jnp.dot(d.astype(BF16), w_ref[...], preferred_element_type=F32) * sc_ref[...]


def _pool_inv_count(seq_len):
    t = np.arange(seq_len)[:, None]
    half = (np.asarray(POOL_SIZES) // 2)[np.arange(BR_W) // POOL_GD][None, :]
    cnt = np.minimum(t + half, seq_len) - np.maximum(t - half, 0)
    return jnp.asarray(1.0 / cnt, F32)


def _pool_specs(seq_len):
    return ([pl.BlockSpec((seq_len, BR_W), lambda b: (b, 0)), pl.BlockSpec((BR_W, BR_W), lambda b: (0, 0)),
             pl.BlockSpec((1, BR_W), lambda b: (0, 0)), pl.BlockSpec((seq_len, BR_W), lambda b: (0, 0))],
            [pltpu.VMEM((seq_len + 2 * POOL_PAD, BR_W), F32)] * 2)


def _merge_kernel(x_ref, mod_ref, g_ref, a_ref, bonus_ref, of_ref, ob_ref, yc_ref, yd_ref,
                  wz_ref, wmg_ref, ws_ref, bs_ref, lng_ref, lnb_ref, ones_ref, wup_ref, wo_ref, gf_ref,
                  o_ref, *, lat, final):
    i = pl.program_id(0)
    m = mod_ref[pl.ds(_mod_row(i, TM_MERGE, lat), 1), :]
    x = x_ref[...]
    hb = _norm_mod(x, g_ref[...], m).astype(BF16)

    a = a_ref[...]
    a_u, a_v = a[:, :BR_W], a[:, BR_W:]
    grp = lax.broadcasted_iota(jnp.int32, (1, BR_W), 1) // A_GD
    svs = []
    for c in range(TM_MERGE // CHUNK):
        vc = a_v[c * CHUNK:(c + 1) * CHUNK, :]
        sv = bs_ref[...]
        for g in range(A_GROUPS):
            sv = sv + jnp.dot(ws_ref[g], jnp.where(grp == g, vc, 0.0).astype(BF16), preferred_element_type=F32)
        svs.append(sv)
    y_a = a_u * jnp.concatenate(svs, axis=0)

    ones = ones_ref[...]
    osum = of_ref[...] + ob_ref[...]
    mu = _seg_sum(osum, ones) * (1.0 / RW_HD)
    dev = osum - mu
    var = _seg_sum(dev * dev, ones) * (1.0 / RW_HD)
    y_b = dev * lax.rsqrt(var + GN_EPS) * lng_ref[...] + lnb_ref[...] + bonus_ref[...]

    merged = jnp.zeros((TM_MERGE, D_MODEL), F32)
    z_all = jnp.dot(hb, wz_ref[0], preferred_element_type=F32)
    for n, y in enumerate((y_a, y_b, yc_ref[...], yd_ref[...])):
        z = z_all[:, n * BR_W:(n + 1) * BR_W]
        ys = y * (z * _sigmoid(z))
        up = jnp.dot(ys.astype(BF16), wup_ref[n], preferred_element_type=F32)
        mg = jnp.dot(hb, wmg_ref[0, :, n * D_MODEL:(n + 1) * D_MODEL], preferred_element_type=F32)
        merged = merged + _sigmoid(mg) * up
    gate = m[:, 2 * D_MODEL:]
    out = x + gate * jnp.dot(merged.astype(BF16), wo_ref[...], preferred_element_type=F32)
    if final:
        ms = jnp.mean(out * out, axis=-1, keepdims=True)
        out = out * lax.rsqrt(ms + NORM_EPS) * gf_ref[...]
    o_ref[...] = out


def _merge(x, mod_l, g, a, bonus, o_f, o_b, y_c, y_d, w_in_b, layer, ws, bs_tile, ln_g, ln_b, ones_bd, wup, wo,
           g_final, lat, final):
    row = lambda w: pl.BlockSpec((TM_MERGE, w), lambda i: (i, 0))
    full2 = lambda s: pl.BlockSpec(s, lambda i: (0, 0))
    z0, z1 = IN_SMALL, IN_SMALL + N_BRANCH * BR_W
    cols = lambda c0, w: pl.BlockSpec((pl.Element(1), pl.Element(D_MODEL), pl.Element(w)), lambda i: (layer, 0, c0))
    return pl.pallas_call(
        functools.partial(_merge_kernel, lat=lat, final=final),
        grid=(N_GRP // TM_MERGE,),
        in_specs=[row(D_MODEL), full2((8, 3 * D_MODEL)), full2((1, D_MODEL)), row(2 * BR_W),
                  row(BR_W), row(BR_W), row(BR_W), row(BR_W), row(BR_W),
                  cols(z0, N_BRANCH * BR_W), cols(z1, N_BRANCH * D_MODEL),
                  pl.BlockSpec((None, A_GROUPS, CHUNK, CHUNK), lambda i: (layer, 0, 0, 0)), full2((CHUNK, BR_W)),
                  full2((1, BR_W)), full2((1, BR_W)), full2((BR_W, BR_W)),
                  pl.BlockSpec((None, N_BRANCH, BR_W, D_MODEL), lambda i: (layer, 0, 0, 0)),
                  pl.BlockSpec((None, D_MODEL, D_MODEL), lambda i: (layer, 0, 0)), full2((1, D_MODEL))],
        out_specs=row(D_MODEL),
        out_shape=jax.ShapeDtypeStruct((N_GRP, D_MODEL), F32),
        compiler_params=_cparams(("arbitrary",)),
        name=("merge_lat" if lat else "merge_ctx") + ("_final" if final else ""),
    )(x, mod_l, g.reshape(1, D_MODEL), a, bonus, o_f, o_b, y_c, y_d, w_in_b, w_in_b, ws, bs_tile,
      ln_g.reshape(1, BR_W), ln_b.reshape(1, BR_W), ones_bd, wup, wo, g_final.reshape(1, D_MODEL))


def _block_diag(blocks):
    n, r, c = blocks.shape
    eye = jnp.eye(n, dtype=blocks.dtype)
    return (eye[:, None, :, None] * blocks[:, :, None, :]).reshape(n * r, n * c)


def kernel(x_prompt, x_sample, cache_k, cache_v, state_rwkv, c, c_ctx, w_mod, b_mod, g_norm, w_in, w_s, b_s,
           rw_w0, rw_w_up, rw_a0, rw_a_up, rw_k_k, rw_k_a, rw_r_k, rw_ln_g, rw_ln_b, att_sink, pool_w,
           pool_scale, w_up, w_o, g_final):
    xs = [x_prompt.reshape(N_CTX, D_MODEL), x_sample.reshape(N_LAT, D_MODEL)]
    cond = jnp.concatenate([c_ctx[None, :], c, jnp.zeros((8 - 1 - DEC_BATCH, D_MODEL), F32)], axis=0)
    mod = _modulation(cond, w_mod, b_mod)

    ones_bd = _block_diag(jnp.ones((RW_HEADS, RW_HD, RW_HD), BF16))
    tables = _rope_tables()
    kvw = ATT_KV_HEADS * ATT_HD
    cache_k4 = cache_k.reshape(DEC_BATCH, DEPTH, PAST_LEN, kvw)
    cache_v4 = cache_v.reshape(DEC_BATCH, DEPTH, PAST_LEN, kvw)
    w_in_b, w_s_b, w_up_b, w_o_b = (w.astype(BF16) for w in (w_in, w_s, w_up, w_o))

    new_k, new_v, new_s = [], [], []
    for l in range(DEPTH):
        pw = _block_diag(pool_w[l]).astype(BF16)
        bs_tile = jnp.repeat(b_s[l].T, A_GD, axis=1)
        final = l == DEPTH - 1
        proj = []
        for lat in (False, True):
            a, cq, ck, cv, dp, r, v, *pre, bonus = _inproj(
                xs[lat], mod[l], g_norm[l], w_in_b, l, rw_w0[l], rw_w_up[l], rw_a0[l], rw_a_up[l], rw_k_k[l],
                rw_k_a[l], rw_r_k[l], ones_bd, lat)
            proj.append((a, cq, ck, cv, dp, pre + [r, v], bonus))

        of_l, ob_l, _, *ctx_lanes = _scan(proj[1][5], _lat_state(state_rwkv[:, l]), nb=DEC_BATCH, t_len=DEC_SEQ,
                                          name="scan_lat", side_pre=proj[0][5], side_nb=BATCH, side_len=SEQ)
        of_c, ob_c, _, s_fin = _scan(None, None, nb=BATCH, t_len=SEQ, name="scan_ctx", lanes=ctx_lanes,
                                     emit_state=True)
        new_s.append(s_fin.reshape(RW_HEADS, 2, BATCH, RW_HD, RW_HD).transpose(2, 1, 0, 3, 4))
        scans = ((of_c, ob_c), (of_l, ob_l))

        for lat in (False, True):
            a, cq, ck, cv, dp, _, bonus = proj[lat]
            o_f, o_b = scans[lat]
            if lat:
                y_c, y_d = _att_lat(att_sink[l], cq, ck, cv, cache_k4, cache_v4, l, tables, dp, pw, pool_scale[l])
            else:
                y_c, y_d = _att_ctx(att_sink[l], cq, ck, cv, dp, pw, pool_scale[l])
                new_k.append(ck.reshape(BATCH, SEQ, ATT_KV_HEADS, ATT_HD))
                new_v.append(cv.reshape(BATCH, SEQ, ATT_KV_HEADS, ATT_HD))
            xs[lat] = _merge(xs[lat], mod[l], g_norm[l], a, bonus, o_f.reshape(N_GRP, BR_W),
                             o_b.reshape(N_GRP, BR_W), y_c, y_d, w_in_b, l, w_s_b, bs_tile, rw_ln_g[l], rw_ln_b[l],
                             ones_bd, w_up_b, w_o_b, g_final, lat, final)

    y_prompt = xs[0].reshape(BATCH, SEQ, D_MODEL)
    y_sample = xs[1].reshape(DEC_BATCH, DEC_SEQ, D_MODEL)
    return (y_prompt, y_sample, jnp.stack(new_k, axis=1), jnp.stack(new_v, axis=1), jnp.stack(new_s, axis=1))
```

```python
import functools

import numpy as np
import jax
import jax.numpy as jnp
from jax import lax
from jax.experimental import pallas as pl
from jax.experimental.pallas import tpu as pltpu

D_MODEL = 1024
BATCH = 16
SEQ = 256
DEPTH = 2
DEC_BATCH = 4
DEC_SEQ = 1024
PAST_LEN = 512
GRID_W = 64
N_BRANCH = 4
BR_W = D_MODEL // 4
CHUNK = 128
A_GROUPS = 4
A_GD = BR_W // A_GROUPS
RW_HD = 64
RW_HEADS = BR_W // RW_HD
DECAY_RANK = 64
ICL_RANK = 64
ATT_HD = 64
ATT_HEADS = BR_W // ATT_HD
ATT_KV_HEADS = 2
ATT_GROUP = ATT_HEADS // ATT_KV_HEADS
WINDOW = 128
ROPE_BASE = 10000.0
POOL_SIZES = (2, 4, 8, 16)
POOL_GD = BR_W // len(POOL_SIZES)
NORM_EPS = 1e-6
GN_EPS = 64e-5
NEG_INF = -1e30
DECAY_SCALE = float(np.exp(-0.5))

N_CTX = BATCH * SEQ
N_LAT = DEC_BATCH * DEC_SEQ
N_GRP = N_CTX
assert N_LAT == N_GRP
IN_SMALL = 2176

F32 = jnp.float32
BF16 = jnp.bfloat16

V7X_VMEM_LIMIT = 56 * 1024 * 1024
LANES = 128

TM_IN = 512
TM_MERGE = 512
CTX_PER_STEP = 2
SCAN_TC = 32


def _cparams(sem):
    return pltpu.CompilerParams(dimension_semantics=sem, vmem_limit_bytes=V7X_VMEM_LIMIT)


def _mod_row(i, tm, lat):
    return 1 + i // (DEC_SEQ // tm) if lat else 0


def _norm_mod(x, g, m):
    ms = jnp.mean(x * x, axis=-1, keepdims=True)
    y = x * lax.rsqrt(ms + NORM_EPS) * g
    shift = m[:, :D_MODEL]
    scale = m[:, D_MODEL:2 * D_MODEL]
    return y * (1.0 + scale) + shift


def _sigmoid(x):
    return 1.0 / (1.0 + jnp.exp(-x))


def _split(x):
    hi = x.astype(BF16)
    return hi, (x - hi.astype(F32)).astype(BF16)


def _seg_sum(x, ones_bd):
    hi, lo = _split(x)
    return (jnp.dot(hi, ones_bd, preferred_element_type=F32)
            + jnp.dot(lo, ones_bd, preferred_element_type=F32))


def _dot_split(x, w):
    xh, xl = _split(x)
    wh, wl = _split(w)
    lhs = jnp.concatenate([xh, xl, xh], axis=1)
    rhs = jnp.concatenate([wh, wh, wl], axis=0)
    return jnp.dot(lhs, rhs, preferred_element_type=F32)


def _mod_kernel(c_ref, w_ref, b_ref, o_ref):
    cnd = c_ref[...]
    s = cnd * _sigmoid(cnd)
    o_ref[...] = jnp.dot(s.astype(BF16), w_ref[...].astype(BF16), preferred_element_type=F32) + b_ref[...]


def _modulation(cond, w_mod, b_mod):
    nt = 3 * D_MODEL // 1024
    return pl.pallas_call(
        _mod_kernel,
        grid=(DEPTH, nt),
        in_specs=[
            pl.BlockSpec((8, D_MODEL), lambda l, j: (0, 0)),
            pl.BlockSpec((None, D_MODEL, 1024), lambda l, j: (l, 0, j)),
            pl.BlockSpec((None, 1, 1024), lambda l, j: (l, 0, j)),
        ],
        out_specs=pl.BlockSpec((None, 8, 1024), lambda l, j: (l, 0, j)),
        out_shape=jax.ShapeDtypeStruct((DEPTH, 8, 3 * D_MODEL), F32),
        compiler_params=_cparams(("arbitrary", "arbitrary")),
        name="modulation",
    )(cond, w_mod, b_mod.reshape(DEPTH, 1, 3 * D_MODEL))


def _inproj_kernel(x_ref, mod_ref, g_ref, w_ref, w0_ref, wup_ref, a0_ref, aup_ref, kkw_ref, ka_ref, rk_ref, ones_ref,
                   a_ref, q_ref, ck_ref, cv_ref, d_ref, r_ref, v_ref,
                   w0o, w1o, kka0o, kka1o, kd0o, kd1o, kko, bonuso, *, lat):
    i = pl.program_id(0)
    m = mod_ref[pl.ds(_mod_row(i, TM_IN, lat), 1), :]
    h = _norm_mod(x_ref[...], g_ref[...], m)
    p = jnp.dot(h.astype(BF16), w_ref[0], preferred_element_type=F32)
    a_ref[...] = p[:, 0:512]
    q_ref[...] = p[:, 1408:1664]
    ck_ref[...] = p[:, 1664:1792]
    cv_ref[...] = p[:, 1792:1920]
    d_ref[...] = p[:, 1920:2176]

    r = p[:, 512:768]
    k = p[:, 768:1024]
    v = p[:, 1024:1280]
    wd_t = jnp.tanh(p[:, 1280:1344])
    ad = p[:, 1344:1408]
    r_ref[...] = r
    v_ref[...] = v
    ones = ones_ref[...]
    kk = k * kkw_ref[...]
    ss = _seg_sum(kk * kk, ones)
    kkn = kk * lax.rsqrt(ss + 1e-12)
    kko[...] = kkn
    kd_sum = jnp.zeros_like(v)
    for d, (wo, kkao, kdo) in enumerate(((w0o, kka0o, kd0o), (w1o, kka1o, kd1o))):
        pre = w0_ref[pl.ds(d, 1), :] + _dot_split(wd_t, wup_ref[d])
        wo[...] = jnp.exp(-DECAY_SCALE * _sigmoid(pre))
        a = _sigmoid(a0_ref[pl.ds(d, 1), :] + _dot_split(ad, aup_ref[d]))
        k_d = k * (1.0 + (a - 1.0) * ka_ref[...])
        kdo[...] = k_d
        kkao[...] = kkn * a
        kd_sum = kd_sum + k_d
    bonuso[...] = _seg_sum(r * kd_sum * rk_ref[...], ones) * v


def _inproj(x, mod_l, g, w_in_b, layer, w0, wup, a0, aup, k_k, k_a, r_k, ones_bd, lat):
    kvw = ATT_KV_HEADS * ATT_HD
    widths = (2 * BR_W, BR_W, kvw, kvw, BR_W) + (BR_W,) * 10
    full2 = lambda s: pl.BlockSpec(s, lambda i: (0, 0))
    return pl.pallas_call(
        functools.partial(_inproj_kernel, lat=lat),
        grid=(N_GRP // TM_IN,),
        in_specs=[
            pl.BlockSpec((TM_IN, D_MODEL), lambda i: (i, 0)),
            full2((8, 3 * D_MODEL)),
            full2((1, D_MODEL)),
            pl.BlockSpec((pl.Element(1), pl.Element(D_MODEL), pl.Element(IN_SMALL)), lambda i: (layer, 0, 0)),
            full2((2, BR_W)),
            pl.BlockSpec((2, DECAY_RANK, BR_W), lambda i: (0, 0, 0)),
            full2((2, BR_W)),
            pl.BlockSpec((2, ICL_RANK, BR_W), lambda i: (0, 0, 0)),
            full2((1, BR_W)), full2((1, BR_W)), full2((1, BR_W)),
            full2((BR_W, BR_W)),
        ],
        out_specs=[pl.BlockSpec((TM_IN, w), lambda i: (i, 0)) for w in widths],
        out_shape=[jax.ShapeDtypeStruct((N_GRP, w), F32) for w in widths],
        compiler_params=_cparams(("arbitrary",)),
        name="in_proj_lat" if lat else "in_proj_ctx",
    )(x, mod_l, g.reshape(1, D_MODEL), w_in_b, w0, wup, a0, aup,
      k_k.reshape(1, BR_W), k_a.reshape(1, BR_W), r_k.reshape(1, BR_W), ones_bd)


def _to_lanes(z0, z1, rep):
    lo = lax.broadcasted_iota(jnp.int32, (1, LANES), 1) < RW_HD
    parts = []
    for hp in range(RW_HEADS // 2):
        a = z0[:, hp * LANES:(hp + 1) * LANES]
        b = z1[:, hp * LANES:(hp + 1) * LANES]
        parts.append(jnp.where(lo, a, pltpu.roll(b, RW_HD, 1)))
        parts.append(jnp.where(lo, pltpu.roll(a, RW_HD, 1), b))
    return jnp.concatenate(parts * rep, axis=0).T


def _scan_kernel(*refs, nb, has_init, lanes_in, n_side, emit_state):
    if has_init:
        s0_ref, refs = refs[0], refs[1:]
    side_in, refs = refs[:2 * n_side], refs[2 * n_side:]
    n_in = 6 if lanes_in else 12
    data_in, refs = refs[:n_in], refs[n_in:]
    of_ref, ob_ref, st_ref = refs[:3]
    side_out = refs[3:3 + n_side]
    n_out = 3 + n_side + (1 if emit_state else 0)
    w_s, kk_s, kka_s, kd_s, r_s, v_s, o_s = refs[n_out:]
    nk, vh, _ = st_ref.shape
    tc = w_s.shape[0]
    rep = RW_HD // vh
    nrec = LANES // rep
    lane = lax.broadcasted_iota(jnp.int32, (1, LANES), 1)
    grp = lane // nrec
    lo = lane < RW_HD

    def rows(f_ref, b_ref, j, n_steps):
        return jnp.concatenate([f_ref[:, j, :], b_ref[:, n_steps - 1 - j, :]], axis=0)

    if lanes_in:
        w_in, kk_in, kka_in, kd_in, r_in, v_in = data_in
    else:
        w_in, kk_in, kka_in, kd_in, r_in, v_in = w_s, kk_s, kka_s, kd_s, r_s, v_s
        for j in range(0, tc, 2):
            for si, dst in enumerate((w_s, kk_s, kka_s, kd_s, r_s, v_s)):
                f_ref, b_ref = data_in[2 * si], data_in[2 * si + 1]
                gt = _to_lanes(rows(f_ref, b_ref, j, tc), rows(f_ref, b_ref, j + 1, tc), rep)
                for s in range(2):
                    blk = gt[s * RW_HD:(s + 1) * RW_HD]
                    if dst is v_s and rep > 1:
                        blk = sum(jnp.where(grp == q, blk[q * vh:(q + 1) * vh], 0.0) for q in range(rep))
                    dst[j + s] = blk

    p = jnp.ones((nk, LANES), F32)
    for t in range(tc):
        kk_s[t] = kk_in[t] * p
        p = p * w_in[t]
        inv = 1.0 / p
        kka_s[t] = kka_in[t] * inv
        kd_s[t] = kd_in[t] * inv
        r_s[t] = r_in[t] * p
    w_s[0] = p

    @pl.when(pl.program_id(0) == 0)
    def _():
        if has_init:
            st_ref[...] = s0_ref[...]
        else:
            st_ref[...] = jnp.zeros(st_ref.shape, F32)

    sa0 = jnp.zeros((vh, LANES), F32)
    for k in range(nk):
        sa0 = sa0 + st_ref[k] * kk_s[0, pl.ds(k, 1), :]

    def step(t, sa):
        tn = jnp.minimum(t + 1, tc - 1)
        vt = v_in[t]
        o = jnp.zeros((vh, LANES), F32)
        san = jnp.zeros((vh, LANES), F32)
        for k in range(nk):
            new = st_ref[k] - kka_s[t, pl.ds(k, 1), :] * sa + kd_s[t, pl.ds(k, 1), :] * vt
            st_ref[k] = new
            o = o + new * r_s[t, pl.ds(k, 1), :]
            san = san + new * kk_s[tn, pl.ds(k, 1), :]
        o_s[t] = o
        return san

    if n_side:
        side_tc = side_in[0].shape[1]
        per = tc // (side_tc // 2)

        def steps_and_side_pair(q, sa):
            for u in range(per):
                sa = step(per * q + u, sa)
            j = 2 * q
            words = None
            for si in range(n_side):
                f_ref, b_ref = side_in[2 * si], side_in[2 * si + 1]
                gt = _to_lanes(rows(f_ref, b_ref, j, side_tc), rows(f_ref, b_ref, j + 1, side_tc), 1)
                side_out[si][j] = gt[0:RW_HD]
                side_out[si][j + 1] = gt[RW_HD:2 * RW_HD]
                bits = lax.bitcast_convert_type(gt[0:8], jnp.uint32)
                words = bits if words is None else words | bits
            zero = lax.shift_right_logical(lax.shift_right_logical(words, jnp.uint32(16)), jnp.uint32(16))
            return sa + lax.bitcast_convert_type(zero, F32)[0:1, :]

        lax.fori_loop(0, side_tc // 2, steps_and_side_pair, sa0)
    else:
        lax.fori_loop(0, tc, step, sa0)
    for k in range(nk):
        st_ref[k] = st_ref[k] * w_s[0, pl.ds(k, 1), :]

    for j in range(0, tc, 2):
        tiles = []
        for s in range(2):
            o = o_s[j + s]
            if rep == 1:
                tiles.append(o)
            else:
                tiles.extend(jnp.where(grp == q, o, 0.0) for q in range(rep))
        mt = jnp.concatenate(tiles, axis=0).T
        sm = mt[0:nrec]
        for q in range(1, rep):
            sm = sm + mt[q * nrec:(q + 1) * nrec]
        rs = pltpu.roll(sm, RW_HD, 1)
        for hp in range(RW_HEADS // 2):
            e0, e1 = (2 * hp) * 2 * nb, (2 * hp + 1) * 2 * nb
            cur = jnp.where(lo, sm[e0:e0 + 2 * nb], rs[e1:e1 + 2 * nb])
            nxt = jnp.where(lo, rs[e0:e0 + 2 * nb], sm[e1:e1 + 2 * nb])
            cs = slice(hp * LANES, (hp + 1) * LANES)
            of_ref[:, j, cs] = cur[0:nb]
            of_ref[:, j + 1, cs] = nxt[0:nb]
            ob_ref[:, tc - 1 - j, cs] = cur[nb:2 * nb]
            ob_ref[:, tc - 2 - j, cs] = nxt[nb:2 * nb]

    if emit_state:
        fs_ref = refs[n_out - 1]

        @pl.when(pl.program_id(0) == pl.num_programs(0) - 1)
        def _():
            for vp in range(vh // 2):
                pair = jnp.concatenate([st_ref[:, 2 * vp, :], st_ref[:, 2 * vp + 1, :]], axis=0)
                fs_ref[:, vp, :] = pair.T


def _fwd_bwd_args(pre, nb, t_len, tc, n_t):
    w0, w1, kka0, kka1, kd0, kd1, kk, r, v = (p.reshape(nb, t_len, BR_W) for p in pre)
    fwd = pl.BlockSpec((nb, tc, BR_W), lambda i: (0, i, 0))
    bwd = pl.BlockSpec((nb, tc, BR_W), lambda i: (0, n_t - 1 - i, 0))
    return (w0, w1, kk, kk, kka0, kka1, kd0, kd1, r, r, v, v), [fwd, bwd] * 6


def _scan(pre, s0, *, nb, t_len, name, lanes=None, side_pre=None, side_nb=None, side_len=None, emit_state=False):
    n_t = t_len // SCAN_TC
    vh = RW_HD * nb * 2 * RW_HEADS // LANES
    sspec = pl.BlockSpec((RW_HD, vh, LANES), lambda i: (0, 0, 0))
    has_init = s0 is not None
    args, in_specs = ((s0,), [sspec]) if has_init else ((), [])
    n_side, side_out_specs, side_out_shape = 0, [], []
    if side_pre is not None:
        n_side = 6
        side_tc = side_len // n_t
        side_args, side_specs = _fwd_bwd_args(side_pre, side_nb, side_len, side_tc, n_t)
        args, in_specs = args + side_args, in_specs + side_specs
        side_out_specs = [pl.BlockSpec((side_tc, RW_HD, LANES), lambda i: (i, 0, 0))] * n_side
        side_out_shape = [jax.ShapeDtypeStruct((side_len, RW_HD, LANES), F32)] * n_side
    if lanes is None:
        own_args, own_specs = _fwd_bwd_args(pre, nb, t_len, SCAN_TC, n_t)
    else:
        own_args = tuple(lanes)
        own_specs = ([pl.BlockSpec((SCAN_TC, RW_HD, LANES), lambda i: (i, 0, 0))] * 5
                     + [pl.BlockSpec((SCAN_TC, vh, LANES), lambda i: (i, 0, 0))])
    kbuf = pltpu.VMEM((SCAN_TC, RW_HD, LANES), F32)
    vbuf = pltpu.VMEM((SCAN_TC, vh, LANES), F32)
    state_specs, state_shape = [], []
    if emit_state:
        state_specs = [pl.BlockSpec((LANES, vh // 2, 2 * RW_HD), lambda i: (0, 0, 0))]
        state_shape = [jax.ShapeDtypeStruct((LANES, vh // 2, 2 * RW_HD), F32)]
    return pl.pallas_call(
        functools.partial(_scan_kernel, nb=nb, has_init=has_init, lanes_in=lanes is not None, n_side=n_side,
                          emit_state=emit_state),
        grid=(n_t,),
        in_specs=in_specs + own_specs,
        out_specs=[pl.BlockSpec((nb, SCAN_TC, BR_W), lambda i: (0, i, 0)),
                   pl.BlockSpec((nb, SCAN_TC, BR_W), lambda i: (0, n_t - 1 - i, 0)),
                   sspec] + side_out_specs + state_specs,
        out_shape=[jax.ShapeDtypeStruct((nb, t_len, BR_W), F32),
                   jax.ShapeDtypeStruct((nb, t_len, BR_W), F32),
                   jax.ShapeDtypeStruct((RW_HD, vh, LANES), F32)] + side_out_shape + state_shape,
        scratch_shapes=[kbuf] * 5 + [vbuf] * 2,
        compiler_params=_cparams(("arbitrary",)),
        name=name,
    )(*args, *own_args)


LAT_VS = 4
LAT_VH = RW_HD // LAT_VS


def _lat_state(s):
    s6 = s.reshape(DEC_BATCH, 2, RW_HEADS, LAT_VS, LAT_VH, RW_HD)
    return s6.transpose(5, 4, 3, 2, 1, 0).reshape(RW_HD, LAT_VH, LANES)


def _sink_softmax_pv(ss, sks, vbs):
    ms = [jnp.maximum(jnp.max(s, axis=-1, keepdims=True), sk) for s, sk in zip(ss, sks)]
    ps = [jnp.exp(s - m) for s, m in zip(ss, ms)]
    dens = [jnp.sum(p, axis=-1, keepdims=True) + jnp.exp(sk - m) for p, sk, m in zip(ps, sks, ms)]
    return [jnp.dot(p.astype(BF16), vb, preferred_element_type=F32) / den for p, vb, den in zip(ps, vbs, dens)]


def _att_ctx_kernel(sink_ref, q_ref, k_ref, v_ref, dp_ref, pw_ref, psc_ref, pic_ref, o_ref, yd_ref, pa_ref, pb_ref):
    scale = ATT_HD ** -0.5
    first = lax.broadcasted_iota(jnp.int32, (ATT_GROUP * SEQ, 1), 0) < SEQ
    for sq in range(CTX_PER_STEP):
        rows = pl.ds(sq * SEQ, SEQ)
        _pool_body(dp_ref.at[rows], pw_ref, psc_ref, pic_ref, yd_ref.at[rows], pa_ref, pb_ref)
        q = q_ref[rows, :] * scale
        kb = k_ref[rows, :].astype(BF16)
        vb = v_ref[rows, :].astype(BF16)
        ss, sks = [], []
        for kvh in range(ATT_KV_HEADS):
            ks = slice(kvh * ATT_HD, (kvh + 1) * ATT_HD)
            q2 = jnp.concatenate([q[:, hd * ATT_HD:(hd + 1) * ATT_HD]
                                  for hd in (ATT_GROUP * kvh, ATT_GROUP * kvh + 1)], axis=0).astype(BF16)
            ss.append(lax.dot_general(q2, kb[:, ks], (((1,), (1,)), ((), ())), preferred_element_type=F32))
            sks.append(jnp.where(first, sink_ref[ATT_GROUP * kvh], sink_ref[ATT_GROUP * kvh + 1]))
        outs = []
        for o2 in _sink_softmax_pv(ss, sks, [vb[:, kvh * ATT_HD:(kvh + 1) * ATT_HD] for kvh in range(ATT_KV_HEADS)]):
            outs += [o2[:SEQ], o2[SEQ:]]
        o_ref[rows, :] = jnp.concatenate(outs, axis=1)


def _att_ctx(sink, cq, ck, cv, dp, pw, pscale):
    kvw = ATT_KV_HEADS * ATT_HD
    rows = CTX_PER_STEP * SEQ
    pool_in, pool_scratch = _pool_specs(SEQ)
    pool_in[0] = pl.BlockSpec((rows, BR_W), lambda b: (b, 0))
    return pl.pallas_call(
        _att_ctx_kernel,
        grid=(BATCH // CTX_PER_STEP,),
        in_specs=[
            pl.BlockSpec(memory_space=pltpu.SMEM),
            pl.BlockSpec((rows, BR_W), lambda b: (b, 0)),
            pl.BlockSpec((rows, kvw), lambda b: (b, 0)),
            pl.BlockSpec((rows, kvw), lambda b: (b, 0)),
        ] + pool_in,
        out_specs=[pl.BlockSpec((rows, BR_W), lambda b: (b, 0))] * 2,
        out_shape=[jax.ShapeDtypeStruct((N_CTX, BR_W), F32)] * 2,
        scratch_shapes=pool_scratch,
        compiler_params=_cparams(("arbitrary",)),
        name="att_ctx",
    )(sink, cq, ck, cv, dp, pw, pscale.reshape(1, BR_W), _pool_inv_count(SEQ))


def _rope(x, cos, sin_signed):
    w = x.shape[1]
    lane = lax.broadcasted_iota(jnp.int32, (1, w), 1)
    first = (lane % 32) < 16
    swapped = jnp.where(first, pltpu.roll(x, w - 16, 1), pltpu.roll(x, 16, 1))
    return x * cos + swapped * sin_signed


def _att_lat_kernel(sink_ref, q_ref, k_ref, v_ref, ck_ref, cv_ref, cq_ref, sq_ref, ckk_ref, skk_ref,
                    dp_ref, pw_ref, psc_ref, pic_ref, o_ref, yd_ref, qs_ref, ks_ref, vs_ref, pa_ref, pb_ref):
    _pool_body(dp_ref, pw_ref, psc_ref, pic_ref, yd_ref, pa_ref, pb_ref)
    scale = ATT_HD ** -0.5
    nb = DEC_SEQ // CHUNK
    qs_ref[...] = (_rope(q_ref[...], cq_ref[...], sq_ref[...]) * scale).astype(BF16)
    ks_ref[...] = _rope(k_ref[...], ckk_ref[...], skk_ref[...]).astype(BF16)
    vs_ref[...] = v_ref[...].astype(BF16)
    ckb = ck_ref[...].astype(BF16)
    cvb = cv_ref[...].astype(BF16)
    for n in range(nb):
        lo = max(n - 1, 0) * CHUNK
        hi = min(n + 2, nb) * CHUNK
        nloc = hi - lo
        ncol = nloc + PAST_LEN
        col = lax.broadcasted_iota(jnp.int32, (CHUNK, ncol), 1)
        row = lax.broadcasted_iota(jnp.int32, (CHUNK, ncol), 0)
        dist = jnp.abs((n * CHUNK + row) - (lo + col))
        valid = (col >= nloc) | (dist <= WINDOW)
        valid2 = jnp.concatenate([valid] * ATT_GROUP, axis=0)
        first = lax.broadcasted_iota(jnp.int32, (ATT_GROUP * CHUNK, 1), 0) < CHUNK
        outs, ss, vs_, sks = [], [], [], []
        for kvh in range(ATT_KV_HEADS):
            ks = slice(kvh * ATT_HD, (kvh + 1) * ATT_HD)
            q2 = jnp.concatenate([qs_ref[n * CHUNK:(n + 1) * CHUNK, hd * ATT_HD:(hd + 1) * ATT_HD]
                                  for hd in (ATT_GROUP * kvh, ATT_GROUP * kvh + 1)], axis=0)
            kall = jnp.concatenate([ks_ref[lo:hi, ks], ckb[:, ks]], axis=0)
            vs_.append(jnp.concatenate([vs_ref[lo:hi, ks], cvb[:, ks]], axis=0))
            s = lax.dot_general(q2, kall, (((1,), (1,)), ((), ())), preferred_element_type=F32)
            ss.append(jnp.where(valid2, s, NEG_INF))
            sks.append(jnp.where(first, sink_ref[ATT_GROUP * kvh], sink_ref[ATT_GROUP * kvh + 1]))
        for o2 in _sink_softmax_pv(ss, sks, vs_):
            outs += [o2[:CHUNK], o2[CHUNK:]]
        o_ref[n * CHUNK:(n + 1) * CHUNK, :] = jnp.concatenate(outs, axis=1)


def _att_lat(sink, cq, ck, cv, cache_k, cache_v, layer, tables, dp, pw, pscale):
    kvw = ATT_KV_HEADS * ATT_HD
    pool_in, pool_scratch = _pool_specs(DEC_SEQ)
    cosq, sinq, cosk, sink_k = tables
    seq = lambda w: pl.BlockSpec((DEC_SEQ, w), lambda b: (b, 0))
    cache = pl.BlockSpec((None, None, PAST_LEN, kvw), lambda b: (b, layer, 0, 0))
    tab = lambda w: pl.BlockSpec((DEC_SEQ, w), lambda b: (0, 0))
    return pl.pallas_call(
        _att_lat_kernel,
        grid=(DEC_BATCH,),
        in_specs=[pl.BlockSpec(memory_space=pltpu.SMEM), seq(BR_W), seq(kvw), seq(kvw), cache, cache,
                  tab(BR_W), tab(BR_W), tab(kvw), tab(kvw)] + pool_in,
        out_specs=[pl.BlockSpec((DEC_SEQ, BR_W), lambda b: (b, 0))] * 2,
        out_shape=[jax.ShapeDtypeStruct((N_LAT, BR_W), F32)] * 2,
        scratch_shapes=[pltpu.VMEM((DEC_SEQ, BR_W), BF16), pltpu.VMEM((DEC_SEQ, kvw), BF16),
                        pltpu.VMEM((DEC_SEQ, kvw), BF16)] + pool_scratch,
        compiler_params=_cparams(("arbitrary",)),
        name="att_lat",
    )(sink, cq, ck, cv, cache_k, cache_v, cosq, sinq, cosk, sink_k, dp, pw, pscale.reshape(1, BR_W),
      _pool_inv_count(DEC_SEQ))


def _rope_tables():
    pos = np.arange(DEC_SEQ)
    lane = np.arange(ATT_HD)
    p = np.where(lane[None, :] < 32, (pos // GRID_W)[:, None], (pos % GRID_W)[:, None]).astype(np.float32)
    inv = (ROPE_BASE ** (-jnp.arange(0, 32, 2, dtype=F32) / 32))[lane % 16]
    ang = jnp.asarray(p) * inv[None, :]
    sign = jnp.asarray(np.where((lane % 32) < 16, -1.0, 1.0).astype(np.float32))
    cos, sin = jnp.cos(ang), jnp.sin(ang) * sign[None, :]
    return (jnp.tile(cos, (1, ATT_HEADS)), jnp.tile(sin, (1, ATT_HEADS)),
            jnp.tile(cos, (1, ATT_KV_HEADS)), jnp.tile(sin, (1, ATT_KV_HEADS)))


POOL_PAD = 8


def _pool_body(x_ref, w_ref, sc_ref, ic_ref, o_ref, a_ref, b_ref):
    n = x_ref.shape[0]
    ext = n + POOL_PAD
    x = x_ref[...]
    zeros = jnp.zeros((POOL_PAD, BR_W), F32)
    a_ref[0:POOL_PAD, :] = zeros
    b_ref[0:POOL_PAD, :] = zeros
    a_ref[POOL_PAD:POOL_PAD + n, :] = x
    a_ref[POOL_PAD + n:POOL_PAD + ext, :] = zeros
    grp = lax.broadcasted_iota(jnp.int32, (1, BR_W), 1) // POOL_GD
    acc = jnp.zeros((n, BR_W), F32)
    src_ref, dst_ref = a_ref, b_ref
    for m in range(len(POOL_SIZES)):
        back = 1 << m
        dst_ref[POOL_PAD:POOL_PAD + ext, :] = (src_ref[POOL_PAD:POOL_PAD + ext, :]
                                               + src_ref[POOL_PAD - back:POOL_PAD - back + ext, :])
        off = POOL_PAD + back - 1
        acc = jnp.where(grp == m, dst_ref[off:off + n, :], acc)
        src_ref, dst_ref = dst_ref, src_ref
    d = acc * ic_ref[...] - x
    o_ref[...] = jnp.dot(d.astype(BF16), w_ref[...], preferred_element_type=F32) * sc_ref[...]


def _pool_inv_count(seq_len):
    t = np.arange(seq_len)[:, None]
    half = (np.asarray(POOL_SIZES) // 2)[np.arange(BR_W) // POOL_GD][None, :]
    cnt = np.minimum(t + half, seq_len) - np.maximum(t - half, 0)
    return jnp.asarray(1.0 / cnt, F32)


def _pool_specs(seq_len):
    return ([pl.BlockSpec((seq_len, BR_W), lambda b: (b, 0)), pl.BlockSpec((BR_W, BR_W), lambda b: (0, 0)),
             pl.BlockSpec((1, BR_W), lambda b: (0, 0)), pl.BlockSpec((seq_len, BR_W), lambda b: (0, 0))],
            [pltpu.VMEM((seq_len + 2 * POOL_PAD, BR_W), F32)] * 2)


def _merge_kernel(x_ref, mod_ref, g_ref, a_ref, bonus_ref, of_ref, ob_ref, yc_ref, yd_ref,
                  wz_ref, wmg_ref, ws_ref, bs_ref, lng_ref, lnb_ref, ones_ref, wup_ref, wo_ref, gf_ref,
                  o_ref, *, lat, final):
    i = pl.program_id(0)
    m = mod_ref[pl.ds(_mod_row(i, TM_MERGE, lat), 1), :]
    x = x_ref[...]
    hb = _norm_mod(x, g_ref[...], m).astype(BF16)

    a = a_ref[...]
    a_u, a_v = a[:, :BR_W], a[:, BR_W:]
    grp = lax.broadcasted_iota(jnp.int32, (1, BR_W), 1) // A_GD
    svs = []
    for c in range(TM_MERGE // CHUNK):
        vc = a_v[c * CHUNK:(c + 1) * CHUNK, :]
        sv = bs_ref[...]
        for g in range(A_GROUPS):
            sv = sv + jnp.dot(ws_ref[g], jnp.where(grp == g, vc, 0.0).astype(BF16), preferred_element_type=F32)
        svs.append(sv)
    y_a = a_u * jnp.concatenate(svs, axis=0)

    ones = ones_ref[...]
    osum = of_ref[...] + ob_ref[...]
    mu = _seg_sum(osum, ones) * (1.0 / RW_HD)
    dev = osum - mu
    var = _seg_sum(dev * dev, ones) * (1.0 / RW_HD)
    y_b = dev * lax.rsqrt(var + GN_EPS) * lng_ref[...] + lnb_ref[...] + bonus_ref[...]

    merged = jnp.zeros((TM_MERGE, D_MODEL), F32)
    z_all = jnp.dot(hb, wz_ref[0], preferred_element_type=F32)
    for n, y in enumerate((y_a, y_b, yc_ref[...], yd_ref[...])):
        z = z_all[:, n * BR_W:(n + 1) * BR_W]
        ys = y * (z * _sigmoid(z))
        up = jnp.dot(ys.astype(BF16), wup_ref[n], preferred_element_type=F32)
        mg = jnp.dot(hb, wmg_ref[0, :, n * D_MODEL:(n + 1) * D_MODEL], preferred_element_type=F32)
        merged = merged + _sigmoid(mg) * up
    gate = m[:, 2 * D_MODEL:]
    out = x + gate * jnp.dot(merged.astype(BF16), wo_ref[...], preferred_element_type=F32)
    if final:
        ms = jnp.mean(out * out, axis=-1, keepdims=True)
        out = out * lax.rsqrt(ms + NORM_EPS) * gf_ref[...]
    o_ref[...] = out


def _merge(x, mod_l, g, a, bonus, o_f, o_b, y_c, y_d, w_in_b, layer, ws, bs_tile, ln_g, ln_b, ones_bd, wup, wo,
           g_final, lat, final):
    row = lambda w: pl.BlockSpec((TM_MERGE, w), lambda i: (i, 0))
    full2 = lambda s: pl.BlockSpec(s, lambda i: (0, 0))
    z0, z1 = IN_SMALL, IN_SMALL + N_BRANCH * BR_W
    cols = lambda c0, w: pl.BlockSpec((pl.Element(1), pl.Element(D_MODEL), pl.Element(w)), lambda i: (layer, 0, c0))
    return pl.pallas_call(
        functools.partial(_merge_kernel, lat=lat, final=final),
        grid=(N_GRP // TM_MERGE,),
        in_specs=[row(D_MODEL), full2((8, 3 * D_MODEL)), full2((1, D_MODEL)), row(2 * BR_W),
                  row(BR_W), row(BR_W), row(BR_W), row(BR_W), row(BR_W),
                  cols(z0, N_BRANCH * BR_W), cols(z1, N_BRANCH * D_MODEL),
                  pl.BlockSpec((None, A_GROUPS, CHUNK, CHUNK), lambda i: (layer, 0, 0, 0)), full2((CHUNK, BR_W)),
                  full2((1, BR_W)), full2((1, BR_W)), full2((BR_W, BR_W)),
                  pl.BlockSpec((None, N_BRANCH, BR_W, D_MODEL), lambda i: (layer, 0, 0, 0)),
                  pl.BlockSpec((None, D_MODEL, D_MODEL), lambda i: (layer, 0, 0)), full2((1, D_MODEL))],
        out_specs=row(D_MODEL),
        out_shape=jax.ShapeDtypeStruct((N_GRP, D_MODEL), F32),
        compiler_params=_cparams(("arbitrary",)),
        name=("merge_lat" if lat else "merge_ctx") + ("_final" if final else ""),
    )(x, mod_l, g.reshape(1, D_MODEL), a, bonus, o_f, o_b, y_c, y_d, w_in_b, w_in_b, ws, bs_tile,
      ln_g.reshape(1, BR_W), ln_b.reshape(1, BR_W), ones_bd, wup, wo, g_final.reshape(1, D_MODEL))


def _block_diag(blocks):
    n, r, c = blocks.shape
    eye = jnp.eye(n, dtype=blocks.dtype)
    return (eye[:, None, :, None] * blocks[:, :, None, :]).reshape(n * r, n * c)


def kernel(x_prompt, x_sample, cache_k, cache_v, state_rwkv, c, c_ctx, w_mod, b_mod, g_norm, w_in, w_s, b_s,
           rw_w0, rw_w_up, rw_a0, rw_a_up, rw_k_k, rw_k_a, rw_r_k, rw_ln_g, rw_ln_b, att_sink, pool_w,
           pool_scale, w_up, w_o, g_final):
    xs = [x_prompt.reshape(N_CTX, D_MODEL), x_sample.reshape(N_LAT, D_MODEL)]
    cond = jnp.concatenate([c_ctx[None, :], c, jnp.zeros((8 - 1 - DEC_BATCH, D_MODEL), F32)], axis=0)
    mod = _modulation(cond, w_mod, b_mod)

    ones_bd = _block_diag(jnp.ones((RW_HEADS, RW_HD, RW_HD), BF16))
    tables = _rope_tables()
    kvw = ATT_KV_HEADS * ATT_HD
    cache_k4 = cache_k.reshape(DEC_BATCH, DEPTH, PAST_LEN, kvw)
    cache_v4 = cache_v.reshape(DEC_BATCH, DEPTH, PAST_LEN, kvw)
    w_in_b, w_s_b, w_up_b, w_o_b = (w.astype(BF16) for w in (w_in, w_s, w_up, w_o))

    new_k, new_v, new_s = [], [], []
    for l in range(DEPTH):
        pw = _block_diag(pool_w[l]).astype(BF16)
        bs_tile = jnp.repeat(b_s[l].T, A_GD, axis=1)
        final = l == DEPTH - 1
        proj = []
        for lat in (False, True):
            a, cq, ck, cv, dp, r, v, *pre, bonus = _inproj(
                xs[lat], mod[l], g_norm[l], w_in_b, l, rw_w0[l], rw_w_up[l], rw_a0[l], rw_a_up[l], rw_k_k[l],
                rw_k_a[l], rw_r_k[l], ones_bd, lat)
            proj.append((a, cq, ck, cv, dp, pre + [r, v], bonus))

        of_l, ob_l, _, *ctx_lanes = _scan(proj[1][5], _lat_state(state_rwkv[:, l]), nb=DEC_BATCH, t_len=DEC_SEQ,
                                          name="scan_lat", side_pre=proj[0][5], side_nb=BATCH, side_len=SEQ)
        of_c, ob_c, _, s_fin = _scan(None, None, nb=BATCH, t_len=SEQ, name="scan_ctx", lanes=ctx_lanes,
                                     emit_state=True)
        new_s.append(s_fin.reshape(RW_HEADS, 2, BATCH, RW_HD, RW_HD).transpose(2, 1, 0, 3, 4))
        scans = ((of_c, ob_c), (of_l, ob_l))

        for lat in (False, True):
            a, cq, ck, cv, dp, _, bonus = proj[lat]
            o_f, o_b = scans[lat]
            if lat:
                y_c, y_d = _att_lat(att_sink[l], cq, ck, cv, cache_k4, cache_v4, l, tables, dp, pw, pool_scale[l])
            else:
                y_c, y_d = _att_ctx(att_sink[l], cq, ck, cv, dp, pw, pool_scale[l])
                new_k.append(ck.reshape(BATCH, SEQ, ATT_KV_HEADS, ATT_HD))
                new_v.append(cv.reshape(BATCH, SEQ, ATT_KV_HEADS, ATT_HD))
            xs[lat] = _merge(xs[lat], mod[l], g_norm[l], a, bonus, o_f.reshape(N_GRP, BR_W),
                             o_b.reshape(N_GRP, BR_W), y_c, y_d, w_in_b, l, w_s_b, bs_tile, rw_ln_g[l], rw_ln_b[l],
                             ones_bd, w_up_b, w_o_b, g_final, lat, final)

    y_prompt = xs[0].reshape(BATCH, SEQ, D_MODEL)
    y_sample = xs[1].reshape(DEC_BATCH, DEC_SEQ, D_MODEL)
    return (y_prompt, y_sample, jnp.stack(new_k, axis=1), jnp.stack(new_v, axis=1), jnp.stack(new_s, axis=1))
```

```python
import functools

import numpy as np
import jax
import jax.numpy as jnp
from jax import lax
from jax.experimental import pallas as pl
from jax.experimental.pallas import tpu as pltpu

D_MODEL = 1024
BATCH = 16
SEQ = 256
DEPTH = 2
DEC_BATCH = 4
DEC_SEQ = 1024
PAST_LEN = 512
GRID_W = 64
N_BRANCH = 4
BR_W = D_MODEL // 4
CHUNK = 128
A_GROUPS = 4
A_GD = BR_W // A_GROUPS
RW_HD = 64
RW_HEADS = BR_W // RW_HD
DECAY_RANK = 64
ICL_RANK = 64
ATT_HD = 64
ATT_HEADS = BR_W // ATT_HD
ATT_KV_HEADS = 2
ATT_GROUP = ATT_HEADS // ATT_KV_HEADS
WINDOW = 128
ROPE_BASE = 10000.0
POOL_SIZES = (2, 4, 8, 16)
POOL_GD = BR_W // len(POOL_SIZES)
NORM_EPS = 1e-6
GN_EPS = 64e-5
NEG_INF = -1e30
DECAY_SCALE = float(np.exp(-0.5))

N_CTX = BATCH * SEQ
N_LAT = DEC_BATCH * DEC_SEQ
N_GRP = N_CTX
assert N_LAT == N_GRP
IN_SMALL = 2176

F32 = jnp.float32
BF16 = jnp.bfloat16

V7X_VMEM_LIMIT = 56 * 1024 * 1024
LANES = 128

TM_IN = 512
X_SLOTS = 3
TM_MERGE = 512
SCAN_TC = 32


def _cparams(sem):
    return pltpu.CompilerParams(dimension_semantics=sem, vmem_limit_bytes=V7X_VMEM_LIMIT)


def _mod_row(i, tm, lat):
    return 1 + i // (DEC_SEQ // tm) if lat else 0


def _norm_mod(x, g, m):
    ms = jnp.mean(x * x, axis=-1, keepdims=True)
    y = x * lax.rsqrt(ms + NORM_EPS) * g
    shift = m[:, :D_MODEL]
    scale = m[:, D_MODEL:2 * D_MODEL]
    return y * (1.0 + scale) + shift


def _sigmoid(x):
    return 1.0 / (1.0 + jnp.exp(-x))


def _split(x):
    hi = x.astype(BF16)
    return hi, (x - hi.astype(F32)).astype(BF16)


def _seg_sum(x, ones_bd):
    hi, lo = _split(x)
    return (jnp.dot(hi, ones_bd, preferred_element_type=F32)
            + jnp.dot(lo, ones_bd, preferred_element_type=F32))


def _dot_split(x, w):
    xh, xl = _split(x)
    wh, wl = _split(w)
    lhs = jnp.concatenate([xh, xl, xh], axis=1)
    rhs = jnp.concatenate([wh, wh, wl], axis=0)
    return jnp.dot(lhs, rhs, preferred_element_type=F32)


def _mod_kernel(c_ref, w_ref, b_ref, o_ref):
    cnd = c_ref[...]
    s = cnd * _sigmoid(cnd)
    o_ref[...] = jnp.dot(s.astype(BF16), w_ref[...].astype(BF16), preferred_element_type=F32) + b_ref[...]


def _modulation(cond, w_mod, b_mod):
    nt = 3 * D_MODEL // 1024
    return pl.pallas_call(
        _mod_kernel,
        grid=(DEPTH, nt),
        in_specs=[
            pl.BlockSpec((8, D_MODEL), lambda l, j: (0, 0)),
            pl.BlockSpec((None, D_MODEL, 1024), lambda l, j: (l, 0, j)),
            pl.BlockSpec((None, 1, 1024), lambda l, j: (l, 0, j)),
        ],
        out_specs=pl.BlockSpec((None, 8, 1024), lambda l, j: (l, 0, j)),
        out_shape=jax.ShapeDtypeStruct((DEPTH, 8, 3 * D_MODEL), F32),
        compiler_params=_cparams(("arbitrary", "arbitrary")),
        name="modulation",
    )(cond, w_mod, b_mod.reshape(DEPTH, 1, 3 * D_MODEL))


def _inproj_kernel(x_ref, mod_ref, g_ref, w_ref, w0_ref, wup_ref, a0_ref, aup_ref, kkw_ref, ka_ref, rk_ref, ones_ref,
                   a_ref, q_ref, ck_ref, cv_ref, d_ref, r_ref, v_ref,
                   w0o, w1o, kka0o, kka1o, kd0o, kd1o, kko, bonuso, xbuf, xsem, *, lat):
    i = pl.program_id(0)
    n_tiles = pl.num_programs(0)

    def x_copy(t):
        slot = t % X_SLOTS
        return pltpu.make_async_copy(x_ref.at[pl.ds(t * TM_IN, TM_IN), :], xbuf.at[slot], xsem.at[slot])

    @pl.when(i == 0)
    def _():
        x_copy(0).start()
        x_copy(1).start()

    @pl.when(i + 2 < n_tiles)
    def _():
        x_copy(i + 2).start()

    x_copy(i).wait()
    m = mod_ref[pl.ds(_mod_row(i, TM_IN, lat), 1), :]
    h = _norm_mod(xbuf[i % X_SLOTS], g_ref[...], m)
    p = jnp.dot(h.astype(BF16), w_ref[0], preferred_element_type=F32)
    a_ref[...] = p[:, 0:512]
    q_ref[...] = p[:, 1408:1664]
    ck_ref[...] = p[:, 1664:1792]
    cv_ref[...] = p[:, 1792:1920]
    d_ref[...] = p[:, 1920:2176]

    r = p[:, 512:768]
    k = p[:, 768:1024]
    v = p[:, 1024:1280]
    wd_t = jnp.tanh(p[:, 1280:1344])
    ad = p[:, 1344:1408]
    r_ref[...] = r
    v_ref[...] = v
    ones = ones_ref[...]
    kk = k * kkw_ref[...]
    ss = _seg_sum(kk * kk, ones)
    kkn = kk * lax.rsqrt(ss + 1e-12)
    kko[...] = kkn
    kd_sum = jnp.zeros_like(v)
    for d, (wo, kkao, kdo) in enumerate(((w0o, kka0o, kd0o), (w1o, kka1o, kd1o))):
        pre = w0_ref[pl.ds(d, 1), :] + _dot_split(wd_t, wup_ref[d])
        wo[...] = jnp.exp(-DECAY_SCALE * _sigmoid(pre))
        a = _sigmoid(a0_ref[pl.ds(d, 1), :] + _dot_split(ad, aup_ref[d]))
        k_d = k * (1.0 + (a - 1.0) * ka_ref[...])
        kdo[...] = k_d
        kkao[...] = kkn * a
        kd_sum = kd_sum + k_d
    bonuso[...] = _seg_sum(r * kd_sum * rk_ref[...], ones) * v


def _inproj(x, mod_l, g, w_in_b, layer, w0, wup, a0, aup, k_k, k_a, r_k, ones_bd, lat):
    kvw = ATT_KV_HEADS * ATT_HD
    widths = (2 * BR_W, BR_W, kvw, kvw, BR_W) + (BR_W,) * 10
    full2 = lambda s: pl.BlockSpec(s, lambda i: (0, 0))
    return pl.pallas_call(
        functools.partial(_inproj_kernel, lat=lat),
        grid=(N_GRP // TM_IN,),
        in_specs=[
            pl.BlockSpec(memory_space=pl.ANY),
            full2((8, 3 * D_MODEL)),
            full2((1, D_MODEL)),
            pl.BlockSpec((pl.Element(1), pl.Element(D_MODEL), pl.Element(IN_SMALL)), lambda i: (layer, 0, 0)),
            full2((2, BR_W)),
            pl.BlockSpec((2, DECAY_RANK, BR_W), lambda i: (0, 0, 0)),
            full2((2, BR_W)),
            pl.BlockSpec((2, ICL_RANK, BR_W), lambda i: (0, 0, 0)),
            full2((1, BR_W)), full2((1, BR_W)), full2((1, BR_W)),
            full2((BR_W, BR_W)),
        ],
        out_specs=[pl.BlockSpec((TM_IN, w), lambda i: (i, 0)) for w in widths],
        out_shape=[jax.ShapeDtypeStruct((N_GRP, w), F32) for w in widths],
        scratch_shapes=[pltpu.VMEM((X_SLOTS, TM_IN, D_MODEL), F32), pltpu.SemaphoreType.DMA((X_SLOTS,))],
        compiler_params=_cparams(("arbitrary",)),
        name="in_proj_lat" if lat else "in_proj_ctx",
    )(x, mod_l, g.reshape(1, D_MODEL), w_in_b, w0, wup, a0, aup,
      k_k.reshape(1, BR_W), k_a.reshape(1, BR_W), r_k.reshape(1, BR_W), ones_bd)


def _to_lanes(z0, z1, rep):
    lo = lax.broadcasted_iota(jnp.int32, (1, LANES), 1) < RW_HD
    parts = []
    for hp in range(RW_HEADS // 2):
        a = z0[:, hp * LANES:(hp + 1) * LANES]
        b = z1[:, hp * LANES:(hp + 1) * LANES]
        parts.append(jnp.where(lo, a, pltpu.roll(b, RW_HD, 1)))
        parts.append(jnp.where(lo, pltpu.roll(a, RW_HD, 1), b))
    return jnp.concatenate(parts * rep, axis=0).T


def _scan_kernel(*refs, nb, has_init, lanes_in, n_side, emit_state):
    if has_init:
        s0_ref, refs = refs[0], refs[1:]
    side_in, refs = refs[:2 * n_side], refs[2 * n_side:]
    n_in = 6 if lanes_in else 12
    data_in, refs = refs[:n_in], refs[n_in:]
    of_ref, ob_ref, st_ref = refs[:3]
    side_out = refs[3:3 + n_side]
    n_out = 3 + n_side + (1 if emit_state else 0)
    w_s, kk_s, kka_s, kd_s, r_s, v_s, o_s = refs[n_out:]
    nk, vh, _ = st_ref.shape
    tc = w_s.shape[0]
    rep = RW_HD // vh
    nrec = LANES // rep
    lane = lax.broadcasted_iota(jnp.int32, (1, LANES), 1)
    grp = lane // nrec
    lo = lane < RW_HD

    def rows(f_ref, b_ref, j, n_steps):
        return jnp.concatenate([f_ref[:, j, :], b_ref[:, n_steps - 1 - j, :]], axis=0)

    if lanes_in:
        w_in, kk_in, kka_in, kd_in, r_in, v_in = data_in
    else:
        w_in, kk_in, kka_in, kd_in, r_in, v_in = w_s, kk_s, kka_s, kd_s, r_s, v_s
        for j in range(0, tc, 2):
            for si, dst in enumerate((w_s, kk_s, kka_s, kd_s, r_s, v_s)):
                f_ref, b_ref = data_in[2 * si], data_in[2 * si + 1]
                gt = _to_lanes(rows(f_ref, b_ref, j, tc), rows(f_ref, b_ref, j + 1, tc), rep)
                for s in range(2):
                    blk = gt[s * RW_HD:(s + 1) * RW_HD]
                    if dst is v_s and rep > 1:
                        blk = sum(jnp.where(grp == q, blk[q * vh:(q + 1) * vh], 0.0) for q in range(rep))
                    dst[j + s] = blk

    p = jnp.ones((nk, LANES), F32)
    for t in range(tc):
        kk_s[t] = kk_in[t] * p
        p = p * w_in[t]
        inv = 1.0 / p
        kka_s[t] = kka_in[t] * inv
        kd_s[t] = kd_in[t] * inv
        r_s[t] = r_in[t] * p
    w_s[0] = p

    @pl.when(pl.program_id(0) == 0)
    def _():
        if has_init:
            st_ref[...] = s0_ref[...]
        else:
            st_ref[...] = jnp.zeros(st_ref.shape, F32)

    sa0 = jnp.zeros((vh, LANES), F32)
    for k in range(nk):
        sa0 = sa0 + st_ref[k] * kk_s[0, pl.ds(k, 1), :]

    def step(t, sa):
        tn = jnp.minimum(t + 1, tc - 1)
        vt = v_in[t]
        o = jnp.zeros((vh, LANES), F32)
        san = jnp.zeros((vh, LANES), F32)
        for k in range(nk):
            new = st_ref[k] - kka_s[t, pl.ds(k, 1), :] * sa + kd_s[t, pl.ds(k, 1), :] * vt
            st_ref[k] = new
            o = o + new * r_s[t, pl.ds(k, 1), :]
            san = san + new * kk_s[tn, pl.ds(k, 1), :]
        o_s[t] = o
        return san

    if n_side:
        side_tc = side_in[0].shape[1]
        per = tc // (side_tc // 2)

        def steps_and_side_pair(q, sa):
            for u in range(per):
                sa = step(per * q + u, sa)
            j = 2 * q
            words = None
            for si in range(n_side):
                f_ref, b_ref = side_in[2 * si], side_in[2 * si + 1]
                gt = _to_lanes(rows(f_ref, b_ref, j, side_tc), rows(f_ref, b_ref, j + 1, side_tc), 1)
                side_out[si][j] = gt[0:RW_HD]
                side_out[si][j + 1] = gt[RW_HD:2 * RW_HD]
                bits = lax.bitcast_convert_type(gt[0:8], jnp.uint32)
                words = bits if words is None else words | bits
            zero = lax.shift_right_logical(lax.shift_right_logical(words, jnp.uint32(16)), jnp.uint32(16))
            return sa + lax.bitcast_convert_type(zero, F32)[0:1, :]

        lax.fori_loop(0, side_tc // 2, steps_and_side_pair, sa0)
    else:
        lax.fori_loop(0, tc, step, sa0)
    for k in range(nk):
        st_ref[k] = st_ref[k] * w_s[0, pl.ds(k, 1), :]

    for j in range(0, tc, 2):
        tiles = []
        for s in range(2):
            o = o_s[j + s]
            if rep == 1:
                tiles.append(o)
            else:
                tiles.extend(jnp.where(grp == q, o, 0.0) for q in range(rep))
        mt = jnp.concatenate(tiles, axis=0).T
        sm = mt[0:nrec]
        for q in range(1, rep):
            sm = sm + mt[q * nrec:(q + 1) * nrec]
        rs = pltpu.roll(sm, RW_HD, 1)
        for hp in range(RW_HEADS // 2):
            e0, e1 = (2 * hp) * 2 * nb, (2 * hp + 1) * 2 * nb
            cur = jnp.where(lo, sm[e0:e0 + 2 * nb], rs[e1:e1 + 2 * nb])
            nxt = jnp.where(lo, rs[e0:e0 + 2 * nb], sm[e1:e1 + 2 * nb])
            cs = slice(hp * LANES, (hp + 1) * LANES)
            of_ref[:, j, cs] = cur[0:nb]
            of_ref[:, j + 1, cs] = nxt[0:nb]
            ob_ref[:, tc - 1 - j, cs] = cur[nb:2 * nb]
            ob_ref[:, tc - 2 - j, cs] = nxt[nb:2 * nb]

    if emit_state:
        fs_ref = refs[n_out - 1]

        @pl.when(pl.program_id(0) == pl.num_programs(0) - 1)
        def _():
            for vp in range(vh // 2):
                pair = jnp.concatenate([st_ref[:, 2 * vp, :], st_ref[:, 2 * vp + 1, :]], axis=0)
                fs_ref[:, vp, :] = pair.T


def _fwd_bwd_args(pre, nb, t_len, tc, n_t):
    w0, w1, kka0, kka1, kd0, kd1, kk, r, v = (p.reshape(nb, t_len, BR_W) for p in pre)
    fwd = pl.BlockSpec((nb, tc, BR_W), lambda i: (0, i, 0))
    bwd = pl.BlockSpec((nb, tc, BR_W), lambda i: (0, n_t - 1 - i, 0))
    return (w0, w1, kk, kk, kka0, kka1, kd0, kd1, r, r, v, v), [fwd, bwd] * 6


def _scan(pre, s0, *, nb, t_len, name, lanes=None, side_pre=None, side_nb=None, side_len=None, emit_state=False):
    n_t = t_len // SCAN_TC
    vh = RW_HD * nb * 2 * RW_HEADS // LANES
    sspec = pl.BlockSpec((RW_HD, vh, LANES), lambda i: (0, 0, 0))
    has_init = s0 is not None
    args, in_specs = ((s0,), [sspec]) if has_init else ((), [])
    n_side, side_out_specs, side_out_shape = 0, [], []
    if side_pre is not None:
        n_side = 6
        side_tc = side_len // n_t
        side_args, side_specs = _fwd_bwd_args(side_pre, side_nb, side_len, side_tc, n_t)
        args, in_specs = args + side_args, in_specs + side_specs
        side_out_specs = [pl.BlockSpec((side_tc, RW_HD, LANES), lambda i: (i, 0, 0))] * n_side
        side_out_shape = [jax.ShapeDtypeStruct((side_len, RW_HD, LANES), F32)] * n_side
    if lanes is None:
        own_args, own_specs = _fwd_bwd_args(pre, nb, t_len, SCAN_TC, n_t)
    else:
        own_args = tuple(lanes)
        own_specs = ([pl.BlockSpec((SCAN_TC, RW_HD, LANES), lambda i: (i, 0, 0))] * 5
                     + [pl.BlockSpec((SCAN_TC, vh, LANES), lambda i: (i, 0, 0))])
    kbuf = pltpu.VMEM((SCAN_TC, RW_HD, LANES), F32)
    vbuf = pltpu.VMEM((SCAN_TC, vh, LANES), F32)
    state_specs, state_shape = [], []
    if emit_state:
        state_specs = [pl.BlockSpec((LANES, vh // 2, 2 * RW_HD), lambda i: (0, 0, 0))]
        state_shape = [jax.ShapeDtypeStruct((LANES, vh // 2, 2 * RW_HD), F32)]
    return pl.pallas_call(
        functools.partial(_scan_kernel, nb=nb, has_init=has_init, lanes_in=lanes is not None, n_side=n_side,
                          emit_state=emit_state),
        grid=(n_t,),
        in_specs=in_specs + own_specs,
        out_specs=[pl.BlockSpec((nb, SCAN_TC, BR_W), lambda i: (0, i, 0)),
                   pl.BlockSpec((nb, SCAN_TC, BR_W), lambda i: (0, n_t - 1 - i, 0)),
                   sspec] + side_out_specs + state_specs,
        out_shape=[jax.ShapeDtypeStruct((nb, t_len, BR_W), F32),
                   jax.ShapeDtypeStruct((nb, t_len, BR_W), F32),
                   jax.ShapeDtypeStruct((RW_HD, vh, LANES), F32)] + side_out_shape + state_shape,
        scratch_shapes=[kbuf] * 5 + [vbuf] * 2,
        compiler_params=_cparams(("arbitrary",)),
        name=name,
    )(*args, *own_args)


LAT_VS = 4
LAT_VH = RW_HD // LAT_VS


def _lat_state(s):
    s6 = s.reshape(DEC_BATCH, 2, RW_HEADS, LAT_VS, LAT_VH, RW_HD)
    return s6.transpose(5, 4, 3, 2, 1, 0).reshape(RW_HD, LAT_VH, LANES)


def _sink_softmax_pv(ss, sks, vbs):
    ms = [jnp.maximum(jnp.max(s, axis=-1, keepdims=True), sk) for s, sk in zip(ss, sks)]
    ps = [jnp.exp(s - m) for s, m in zip(ss, ms)]
    dens = [jnp.sum(p, axis=-1, keepdims=True) + jnp.exp(sk - m) for p, sk, m in zip(ps, sks, ms)]
    return [jnp.dot(p.astype(BF16), vb, preferred_element_type=F32) / den for p, vb, den in zip(ps, vbs, dens)]


def _att_ctx_kernel(sink_ref, q_ref, k_ref, v_ref, dp_ref, pw_ref, psc_ref, pic_ref, o_ref, yd_ref, pa_ref, pb_ref):
    _pool_body(dp_ref, pw_ref, psc_ref, pic_ref, yd_ref, pa_ref, pb_ref)
    scale = ATT_HD ** -0.5
    q = q_ref[...] * scale
    kb = k_ref[...].astype(BF16)
    vb = v_ref[...].astype(BF16)
    first = lax.broadcasted_iota(jnp.int32, (ATT_GROUP * SEQ, 1), 0) < SEQ
    ss, sks = [], []
    for kvh in range(ATT_KV_HEADS):
        ks = slice(kvh * ATT_HD, (kvh + 1) * ATT_HD)
        q2 = jnp.concatenate([q[:, hd * ATT_HD:(hd + 1) * ATT_HD]
                              for hd in (ATT_GROUP * kvh, ATT_GROUP * kvh + 1)], axis=0).astype(BF16)
        ss.append(lax.dot_general(q2, kb[:, ks], (((1,), (1,)), ((), ())), preferred_element_type=F32))
        sks.append(jnp.where(first, sink_ref[ATT_GROUP * kvh], sink_ref[ATT_GROUP * kvh + 1]))
    outs = []
    for o2 in _sink_softmax_pv(ss, sks, [vb[:, kvh * ATT_HD:(kvh + 1) * ATT_HD] for kvh in range(ATT_KV_HEADS)]):
        outs += [o2[:SEQ], o2[SEQ:]]
    o_ref[...] = jnp.concatenate(outs, axis=1)


def _att_ctx(sink, cq, ck, cv, dp, pw, pscale):
    kvw = ATT_KV_HEADS * ATT_HD
    pool_in, pool_scratch = _pool_specs(SEQ)
    return pl.pallas_call(
        _att_ctx_kernel,
        grid=(BATCH,),
        in_specs=[
            pl.BlockSpec(memory_space=pltpu.SMEM),
            pl.BlockSpec((SEQ, BR_W), lambda b: (b, 0)),
            pl.BlockSpec((SEQ, kvw), lambda b: (b, 0)),
            pl.BlockSpec((SEQ, kvw), lambda b: (b, 0)),
        ] + pool_in,
        out_specs=[pl.BlockSpec((SEQ, BR_W), lambda b: (b, 0))] * 2,
        out_shape=[jax.ShapeDtypeStruct((N_CTX, BR_W), F32)] * 2,
        scratch_shapes=pool_scratch,
        compiler_params=_cparams(("arbitrary",)),
        name="att_ctx",
    )(sink, cq, ck, cv, dp, pw, pscale.reshape(1, BR_W), _pool_inv_count(SEQ))


def _rope(x, cos, sin_signed):
    w = x.shape[1]
    lane = lax.broadcasted_iota(jnp.int32, (1, w), 1)
    first = (lane % 32) < 16
    swapped = jnp.where(first, pltpu.roll(x, w - 16, 1), pltpu.roll(x, 16, 1))
    return x * cos + swapped * sin_signed


def _att_lat_kernel(sink_ref, q_ref, k_ref, v_ref, ck_ref, cv_ref, cq_ref, sq_ref, ckk_ref, skk_ref,
                    dp_ref, pw_ref, psc_ref, pic_ref, o_ref, yd_ref, qs_ref, ks_ref, vs_ref, pa_ref, pb_ref):
    _pool_body(dp_ref, pw_ref, psc_ref, pic_ref, yd_ref, pa_ref, pb_ref)
    scale = ATT_HD ** -0.5
    nb = DEC_SEQ // CHUNK
    qs_ref[...] = (_rope(q_ref[...], cq_ref[...], sq_ref[...]) * scale).astype(BF16)
    ks_ref[...] = _rope(k_ref[...], ckk_ref[...], skk_ref[...]).astype(BF16)
    vs_ref[...] = v_ref[...].astype(BF16)
    ckb = ck_ref[...].astype(BF16)
    cvb = cv_ref[...].astype(BF16)
    for n in range(nb):
        lo = max(n - 1, 0) * CHUNK
        hi = min(n + 2, nb) * CHUNK
        nloc = hi - lo
        ncol = nloc + PAST_LEN
        col = lax.broadcasted_iota(jnp.int32, (CHUNK, ncol), 1)
        row = lax.broadcasted_iota(jnp.int32, (CHUNK, ncol), 0)
        dist = jnp.abs((n * CHUNK + row) - (lo + col))
        valid = (col >= nloc) | (dist <= WINDOW)
        valid2 = jnp.concatenate([valid] * ATT_GROUP, axis=0)
        first = lax.broadcasted_iota(jnp.int32, (ATT_GROUP * CHUNK, 1), 0) < CHUNK
        outs, ss, vs_, sks = [], [], [], []
        for kvh in range(ATT_KV_HEADS):
            ks = slice(kvh * ATT_HD, (kvh + 1) * ATT_HD)
            q2 = jnp.concatenate([qs_ref[n * CHUNK:(n + 1) * CHUNK, hd * ATT_HD:(hd + 1) * ATT_HD]
                                  for hd in (ATT_GROUP * kvh, ATT_GROUP * kvh + 1)], axis=0)
            kall = jnp.concatenate([ks_ref[lo:hi, ks], ckb[:, ks]], axis=0)
            vs_.append(jnp.concatenate([vs_ref[lo:hi, ks], cvb[:, ks]], axis=0))
            s = lax.dot_general(q2, kall, (((1,), (1,)), ((), ())), preferred_element_type=F32)
            ss.append(jnp.where(valid2, s, NEG_INF))
            sks.append(jnp.where(first, sink_ref[ATT_GROUP * kvh], sink_ref[ATT_GROUP * kvh + 1]))
        for o2 in _sink_softmax_pv(ss, sks, vs_):
            outs += [o2[:CHUNK], o2[CHUNK:]]
        o_ref[n * CHUNK:(n + 1) * CHUNK, :] = jnp.concatenate(outs, axis=1)


def _att_lat(sink, cq, ck, cv, cache_k, cache_v, layer, tables, dp, pw, pscale):
    kvw = ATT_KV_HEADS * ATT_HD
    pool_in, pool_scratch = _pool_specs(DEC_SEQ)
    cosq, sinq, cosk, sink_k = tables
    seq = lambda w: pl.BlockSpec((DEC_SEQ, w), lambda b: (b, 0))
    cache = pl.BlockSpec((None, None, PAST_LEN, kvw), lambda b: (b, layer, 0, 0))
    tab = lambda w: pl.BlockSpec((DEC_SEQ, w), lambda b: (0, 0))
    return pl.pallas_call(
        _att_lat_kernel,
        grid=(DEC_BATCH,),
        in_specs=[pl.BlockSpec(memory_space=pltpu.SMEM), seq(BR_W), seq(kvw), seq(kvw), cache, cache,
                  tab(BR_W), tab(BR_W), tab(kvw), tab(kvw)] + pool_in,
        out_specs=[pl.BlockSpec((DEC_SEQ, BR_W), lambda b: (b, 0))] * 2,
        out_shape=[jax.ShapeDtypeStruct((N_LAT, BR_W), F32)] * 2,
        scratch_shapes=[pltpu.VMEM((DEC_SEQ, BR_W), BF16), pltpu.VMEM((DEC_SEQ, kvw), BF16),
                        pltpu.VMEM((DEC_SEQ, kvw), BF16)] + pool_scratch,
        compiler_params=_cparams(("arbitrary",)),
        name="att_lat",
    )(sink, cq, ck, cv, cache_k, cache_v, cosq, sinq, cosk, sink_k, dp, pw, pscale.reshape(1, BR_W),
      _pool_inv_count(DEC_SEQ))


def _rope_tables():
    pos = np.arange(DEC_SEQ)
    lane = np.arange(ATT_HD)
    p = np.where(lane[None, :] < 32, (pos // GRID_W)[:, None], (pos % GRID_W)[:, None]).astype(np.float32)
    inv = (ROPE_BASE ** (-jnp.arange(0, 32, 2, dtype=F32) / 32))[lane % 16]
    ang = jnp.asarray(p) * inv[None, :]
    sign = jnp.asarray(np.where((lane % 32) < 16, -1.0, 1.0).astype(np.float32))
    cos, sin = jnp.cos(ang), jnp.sin(ang) * sign[None, :]
    return (jnp.tile(cos, (1, ATT_HEADS)), jnp.tile(sin, (1, ATT_HEADS)),
            jnp.tile(cos, (1, ATT_KV_HEADS)), jnp.tile(sin, (1, ATT_KV_HEADS)))


POOL_PAD = 8


def _pool_body(x_ref, w_ref, sc_ref, ic_ref, o_ref, a_ref, b_ref):
    n = x_ref.shape[0]
    ext = n + POOL_PAD
    x = x_ref[...]
    zeros = jnp.zeros((POOL_PAD, BR_W), F32)
    a_ref[0:POOL_PAD, :] = zeros
    b_ref[0:POOL_PAD, :] = zeros
    a_ref[POOL_PAD:POOL_PAD + n, :] = x
    a_ref[POOL_PAD + n:POOL_PAD + ext, :] = zeros
    grp = lax.broadcasted_iota(jnp.int32, (1, BR_W), 1) // POOL_GD
    acc = jnp.zeros((n, BR_W), F32)
    src_ref, dst_ref = a_ref, b_ref
    for m in range(len(POOL_SIZES)):
        back = 1 << m
        dst_ref[POOL_PAD:POOL_PAD + ext, :] = (src_ref[POOL_PAD:POOL_PAD + ext, :]
                                               + src_ref[POOL_PAD - back:POOL_PAD - back + ext, :])
        off = POOL_PAD + back - 1
        acc = jnp.where(grp == m, dst_ref[off:off + n, :], acc)
        src_ref, dst_ref = dst_ref, src_ref
    d = acc * ic_ref[...] - x
    o_ref[...] = jnp.dot(d.astype(BF16), w_ref[...], preferred_element_type=F32) * sc_ref[...]


def _pool_inv_count(seq_len):
    t = np.arange(seq_len)[:, None]
    half = (np.asarray(POOL_SIZES) // 2)[np.arange(BR_W) // POOL_GD][None, :]
    cnt = np.minimum(t + half, seq_len) - np.maximum(t - half, 0)
    return jnp.asarray(1.0 / cnt, F32)


def _pool_specs(seq_len):
    return ([pl.BlockSpec((seq_len, BR_W), lambda b: (b, 0)), pl.BlockSpec((BR_W, BR_W), lambda b: (0, 0)),
             pl.BlockSpec((1, BR_W), lambda b: (0, 0)), pl.BlockSpec((seq_len, BR_W), lambda b: (0, 0))],
            [pltpu.VMEM((seq_len + 2 * POOL_PAD, BR_W), F32)] * 2)


def _merge_kernel(x_ref, mod_ref, g_ref, a_ref, bonus_ref, of_ref, ob_ref, yc_ref, yd_ref,
                  wz_ref, wmg_ref, ws_ref, bs_ref, lng_ref, lnb_ref, ones_ref, wup_ref, wo_ref, gf_ref,
                  o_ref, *, lat, final):
    i = pl.program_id(0)
    m = mod_ref[pl.ds(_mod_row(i, TM_MERGE, lat), 1), :]
    x = x_ref[...]
    hb = _norm_mod(x, g_ref[...], m).astype(BF16)

    a = a_ref[...]
    a_u, a_v = a[:, :BR_W], a[:, BR_W:]
    grp = lax.broadcasted_iota(jnp.int32, (1, BR_W), 1) // A_GD
    svs = []
    for c in range(TM_MERGE // CHUNK):
        vc = a_v[c * CHUNK:(c + 1) * CHUNK, :]
        sv = bs_ref[...]
        for g in range(A_GROUPS):
            sv = sv + jnp.dot(ws_ref[g], jnp.where(grp == g, vc, 0.0).astype(BF16), preferred_element_type=F32)
        svs.append(sv)
    y_a = a_u * jnp.concatenate(svs, axis=0)

    ones = ones_ref[...]
    osum = of_ref[...] + ob_ref[...]
    mu = _seg_sum(osum, ones) * (1.0 / RW_HD)
    dev = osum - mu
    var = _seg_sum(dev * dev, ones) * (1.0 / RW_HD)
    y_b = dev * lax.rsqrt(var + GN_EPS) * lng_ref[...] + lnb_ref[...] + bonus_ref[...]

    merged = jnp.zeros((TM_MERGE, D_MODEL), F32)
    z_all = jnp.dot(hb, wz_ref[0], preferred_element_type=F32)
    for n, y in enumerate((y_a, y_b, yc_ref[...], yd_ref[...])):
        z = z_all[:, n * BR_W:(n + 1) * BR_W]
        ys = y * (z * _sigmoid(z))
        up = jnp.dot(ys.astype(BF16), wup_ref[n], preferred_element_type=F32)
        mg = jnp.dot(hb, wmg_ref[0, :, n * D_MODEL:(n + 1) * D_MODEL], preferred_element_type=F32)
        merged = merged + _sigmoid(mg) * up
    gate = m[:, 2 * D_MODEL:]
    out = x + gate * jnp.dot(merged.astype(BF16), wo_ref[...], preferred_element_type=F32)
    if final:
        ms = jnp.mean(out * out, axis=-1, keepdims=True)
        out = out * lax.rsqrt(ms + NORM_EPS) * gf_ref[...]
    o_ref[...] = out


def _merge(x, mod_l, g, a, bonus, o_f, o_b, y_c, y_d, w_in_b, layer, ws, bs_tile, ln_g, ln_b, ones_bd, wup, wo,
           g_final, lat, final):
    row = lambda w: pl.BlockSpec((TM_MERGE, w), lambda i: (i, 0))
    full2 = lambda s: pl.BlockSpec(s, lambda i: (0, 0))
    z0, z1 = IN_SMALL, IN_SMALL + N_BRANCH * BR_W
    cols = lambda c0, w: pl.BlockSpec((pl.Element(1), pl.Element(D_MODEL), pl.Element(w)), lambda i: (layer, 0, c0))
    return pl.pallas_call(
        functools.partial(_merge_kernel, lat=lat, final=final),
        grid=(N_GRP // TM_MERGE,),
        in_specs=[row(D_MODEL), full2((8, 3 * D_MODEL)), full2((1, D_MODEL)), row(2 * BR_W),
                  row(BR_W), row(BR_W), row(BR_W), row(BR_W), row(BR_W),
                  cols(z0, N_BRANCH * BR_W), cols(z1, N_BRANCH * D_MODEL),
                  pl.BlockSpec((None, A_GROUPS, CHUNK, CHUNK), lambda i: (layer, 0, 0, 0)), full2((CHUNK, BR_W)),
                  full2((1, BR_W)), full2((1, BR_W)), full2((BR_W, BR_W)),
                  pl.BlockSpec((None, N_BRANCH, BR_W, D_MODEL), lambda i: (layer, 0, 0, 0)),
                  pl.BlockSpec((None, D_MODEL, D_MODEL), lambda i: (layer, 0, 0)), full2((1, D_MODEL))],
        out_specs=row(D_MODEL),
        out_shape=jax.ShapeDtypeStruct((N_GRP, D_MODEL), F32),
        compiler_params=_cparams(("arbitrary",)),
        name=("merge_lat" if lat else "merge_ctx") + ("_final" if final else ""),
    )(x, mod_l, g.reshape(1, D_MODEL), a, bonus, o_f, o_b, y_c, y_d, w_in_b, w_in_b, ws, bs_tile,
      ln_g.reshape(1, BR_W), ln_b.reshape(1, BR_W), ones_bd, wup, wo, g_final.reshape(1, D_MODEL))


def _block_diag(blocks):
    n, r, c = blocks.shape
    eye = jnp.eye(n, dtype=blocks.dtype)
    return (eye[:, None, :, None] * blocks[:, :, None, :]).reshape(n * r, n * c)


def kernel(x_prompt, x_sample, cache_k, cache_v, state_rwkv, c, c_ctx, w_mod, b_mod, g_norm, w_in, w_s, b_s,
           rw_w0, rw_w_up, rw_a0, rw_a_up, rw_k_k, rw_k_a, rw_r_k, rw_ln_g, rw_ln_b, att_sink, pool_w,
           pool_scale, w_up, w_o, g_final):
    xs = [x_prompt.reshape(N_CTX, D_MODEL), x_sample.reshape(N_LAT, D_MODEL)]
    cond = jnp.concatenate([c_ctx[None, :], c, jnp.zeros((8 - 1 - DEC_BATCH, D_MODEL), F32)], axis=0)
    mod = _modulation(cond, w_mod, b_mod)

    ones_bd = _block_diag(jnp.ones((RW_HEADS, RW_HD, RW_HD), BF16))
    tables = _rope_tables()
    kvw = ATT_KV_HEADS * ATT_HD
    cache_k4 = cache_k.reshape(DEC_BATCH, DEPTH, PAST_LEN, kvw)
    cache_v4 = cache_v.reshape(DEC_BATCH, DEPTH, PAST_LEN, kvw)
    w_in_b, w_s_b, w_up_b, w_o_b = (w.astype(BF16) for w in (w_in, w_s, w_up, w_o))

    new_k, new_v, new_s = [], [], []
    for l in range(DEPTH):
        pw = _block_diag(pool_w[l]).astype(BF16)
        bs_tile = jnp.repeat(b_s[l].T, A_GD, axis=1)
        final = l == DEPTH - 1
        proj = []
        for lat in (False, True):
            a, cq, ck, cv, dp, r, v, *pre, bonus = _inproj(
                xs[lat], mod[l], g_norm[l], w_in_b, l, rw_w0[l], rw_w_up[l], rw_a0[l], rw_a_up[l], rw_k_k[l],
                rw_k_a[l], rw_r_k[l], ones_bd, lat)
            proj.append((a, cq, ck, cv, dp, pre + [r, v], bonus))

        of_l, ob_l, _, *ctx_lanes = _scan(proj[1][5], _lat_state(state_rwkv[:, l]), nb=DEC_BATCH, t_len=DEC_SEQ,
                                          name="scan_lat", side_pre=proj[0][5], side_nb=BATCH, side_len=SEQ)
        of_c, ob_c, _, s_fin = _scan(None, None, nb=BATCH, t_len=SEQ, name="scan_ctx", lanes=ctx_lanes,
                                     emit_state=True)
        new_s.append(s_fin.reshape(RW_HEADS, 2, BATCH, RW_HD, RW_HD).transpose(2, 1, 0, 3, 4))
        scans = ((of_c, ob_c), (of_l, ob_l))

        for lat in (False, True):
            a, cq, ck, cv, dp, _, bonus = proj[lat]
            o_f, o_b = scans[lat]
            if lat:
                y_c, y_d = _att_lat(att_sink[l], cq, ck, cv, cache_k4, cache_v4, l, tables, dp, pw, pool_scale[l])
            else:
                y_c, y_d = _att_ctx(att_sink[l], cq, ck, cv, dp, pw, pool_scale[l])
                new_k.append(ck.reshape(BATCH, SEQ, ATT_KV_HEADS, ATT_HD))
                new_v.append(cv.reshape(BATCH, SEQ, ATT_KV_HEADS, ATT_HD))
            xs[lat] = _merge(xs[lat], mod[l], g_norm[l], a, bonus, o_f.reshape(N_GRP, BR_W),
                             o_b.reshape(N_GRP, BR_W), y_c, y_d, w_in_b, l, w_s_b, bs_tile, rw_ln_g[l], rw_ln_b[l],
                             ones_bd, w_up_b, w_o_b, g_final, lat, final)

    y_prompt = xs[0].reshape(BATCH, SEQ, D_MODEL)
    y_sample = xs[1].reshape(DEC_BATCH, DEC_SEQ, D_MODEL)
    return (y_prompt, y_sample, jnp.stack(new_k, axis=1), jnp.stack(new_v, axis=1), jnp.stack(new_s, axis=1))
```

```python
import functools

import numpy as np
import jax
import jax.numpy as jnp
from jax import lax
from jax.experimental import pallas as pl
from jax.experimental.pallas import tpu as pltpu

D_MODEL = 1024
BATCH = 16
SEQ = 256
DEPTH = 2
DEC_BATCH = 4
DEC_SEQ = 1024
PAST_LEN = 512
GRID_W = 64
N_BRANCH = 4
BR_W = D_MODEL // 4
CHUNK = 128
A_GROUPS = 4
A_GD = BR_W // A_GROUPS
RW_HD = 64
RW_HEADS = BR_W // RW_HD
DECAY_RANK = 64
ICL_RANK = 64
ATT_HD = 64
ATT_HEADS = BR_W // ATT_HD
ATT_KV_HEADS = 2
ATT_GROUP = ATT_HEADS // ATT_KV_HEADS
WINDOW = 128
ROPE_BASE = 10000.0
POOL_SIZES = (2, 4, 8, 16)
POOL_GD = BR_W // len(POOL_SIZES)
NORM_EPS = 1e-6
GN_EPS = 64e-5
NEG_INF = -1e30
DECAY_SCALE = float(np.exp(-0.5))

N_CTX = BATCH * SEQ
N_LAT = DEC_BATCH * DEC_SEQ
N_GRP = N_CTX
assert N_LAT == N_GRP
IN_SMALL = 2176

F32 = jnp.float32
BF16 = jnp.bfloat16

V7X_VMEM_LIMIT = 56 * 1024 * 1024
LANES = 128

TM_IN = 512
N_WIDE = 10
TM_MERGE = 512
SCAN_TC = 32


def _cparams(sem):
    return pltpu.CompilerParams(dimension_semantics=sem, vmem_limit_bytes=V7X_VMEM_LIMIT)


def _mod_row(i, tm, lat):
    return 1 + i // (DEC_SEQ // tm) if lat else 0


def _norm_mod(x, g, m):
    ms = jnp.mean(x * x, axis=-1, keepdims=True)
    y = x * lax.rsqrt(ms + NORM_EPS) * g
    shift = m[:, :D_MODEL]
    scale = m[:, D_MODEL:2 * D_MODEL]
    return y * (1.0 + scale) + shift


def _sigmoid(x):
    return 1.0 / (1.0 + jnp.exp(-x))


def _split(x):
    hi = x.astype(BF16)
    return hi, (x - hi.astype(F32)).astype(BF16)


def _seg_sum(x, ones_bd):
    hi, lo = _split(x)
    return (jnp.dot(hi, ones_bd, preferred_element_type=F32)
            + jnp.dot(lo, ones_bd, preferred_element_type=F32))


def _dot_split(x, w):
    xh, xl = _split(x)
    wh, wl = _split(w)
    lhs = jnp.concatenate([xh, xl, xh], axis=1)
    rhs = jnp.concatenate([wh, wh, wl], axis=0)
    return jnp.dot(lhs, rhs, preferred_element_type=F32)


def _mod_kernel(c_ref, w_ref, b_ref, o_ref):
    cnd = c_ref[...]
    s = cnd * _sigmoid(cnd)
    o_ref[...] = jnp.dot(s.astype(BF16), w_ref[...].astype(BF16), preferred_element_type=F32) + b_ref[...]


def _modulation(cond, w_mod, b_mod):
    nt = 3 * D_MODEL // 1024
    return pl.pallas_call(
        _mod_kernel,
        grid=(DEPTH, nt),
        in_specs=[
            pl.BlockSpec((8, D_MODEL), lambda l, j: (0, 0)),
            pl.BlockSpec((None, D_MODEL, 1024), lambda l, j: (l, 0, j)),
            pl.BlockSpec((None, 1, 1024), lambda l, j: (l, 0, j)),
        ],
        out_specs=pl.BlockSpec((None, 8, 1024), lambda l, j: (l, 0, j)),
        out_shape=jax.ShapeDtypeStruct((DEPTH, 8, 3 * D_MODEL), F32),
        compiler_params=_cparams(("arbitrary", "arbitrary")),
        name="modulation",
    )(cond, w_mod, b_mod.reshape(DEPTH, 1, 3 * D_MODEL))


def _inproj_kernel(x_ref, mod_ref, g_ref, w_ref, w0_ref, wup_ref, a0_ref, aup_ref, kkw_ref, ka_ref, rk_ref, ones_ref,
                   a_ref, q_ref, ck_ref, cv_ref, d_ref, wide_ref, *, lat):
    r_ref, v_ref, w0o, w1o, kka0o, kka1o, kd0o, kd1o, kko, bonuso = (
        wide_ref.at[:, pl.ds(c * BR_W, BR_W)] for c in range(N_WIDE))
    i = pl.program_id(0)
    m = mod_ref[pl.ds(_mod_row(i, TM_IN, lat), 1), :]
    h = _norm_mod(x_ref[...], g_ref[...], m)
    p = jnp.dot(h.astype(BF16), w_ref[0], preferred_element_type=F32)
    a_ref[...] = p[:, 0:512]
    q_ref[...] = p[:, 1408:1664]
    ck_ref[...] = p[:, 1664:1792]
    cv_ref[...] = p[:, 1792:1920]
    d_ref[...] = p[:, 1920:2176]

    r = p[:, 512:768]
    k = p[:, 768:1024]
    v = p[:, 1024:1280]
    wd_t = jnp.tanh(p[:, 1280:1344])
    ad = p[:, 1344:1408]
    r_ref[...] = r
    v_ref[...] = v
    ones = ones_ref[...]
    kk = k * kkw_ref[...]
    ss = _seg_sum(kk * kk, ones)
    kkn = kk * lax.rsqrt(ss + 1e-12)
    kko[...] = kkn
    kd_sum = jnp.zeros_like(v)
    for d, (wo, kkao, kdo) in enumerate(((w0o, kka0o, kd0o), (w1o, kka1o, kd1o))):
        pre = w0_ref[pl.ds(d, 1), :] + _dot_split(wd_t, wup_ref[d])
        wo[...] = jnp.exp(-DECAY_SCALE * _sigmoid(pre))
        a = _sigmoid(a0_ref[pl.ds(d, 1), :] + _dot_split(ad, aup_ref[d]))
        k_d = k * (1.0 + (a - 1.0) * ka_ref[...])
        kdo[...] = k_d
        kkao[...] = kkn * a
        kd_sum = kd_sum + k_d
    bonuso[...] = _seg_sum(r * kd_sum * rk_ref[...], ones) * v


def _inproj(x, mod_l, g, w_in_b, layer, w0, wup, a0, aup, k_k, k_a, r_k, ones_bd, lat):
    kvw = ATT_KV_HEADS * ATT_HD
    widths = (2 * BR_W, BR_W, kvw, kvw, BR_W, N_WIDE * BR_W)
    full2 = lambda s: pl.BlockSpec(s, lambda i: (0, 0))
    return pl.pallas_call(
        functools.partial(_inproj_kernel, lat=lat),
        grid=(N_GRP // TM_IN,),
        in_specs=[
            pl.BlockSpec((TM_IN, D_MODEL), lambda i: (i, 0)),
            full2((8, 3 * D_MODEL)),
            full2((1, D_MODEL)),
            pl.BlockSpec((pl.Element(1), pl.Element(D_MODEL), pl.Element(IN_SMALL)), lambda i: (layer, 0, 0)),
            full2((2, BR_W)),
            pl.BlockSpec((2, DECAY_RANK, BR_W), lambda i: (0, 0, 0)),
            full2((2, BR_W)),
            pl.BlockSpec((2, ICL_RANK, BR_W), lambda i: (0, 0, 0)),
            full2((1, BR_W)), full2((1, BR_W)), full2((1, BR_W)),
            full2((BR_W, BR_W)),
        ],
        out_specs=[pl.BlockSpec((TM_IN, w), lambda i: (i, 0)) for w in widths],
        out_shape=[jax.ShapeDtypeStruct((N_GRP, w), F32) for w in widths],
        compiler_params=_cparams(("arbitrary",)),
        name="in_proj_lat" if lat else "in_proj_ctx",
    )(x, mod_l, g.reshape(1, D_MODEL), w_in_b, w0, wup, a0, aup,
      k_k.reshape(1, BR_W), k_a.reshape(1, BR_W), r_k.reshape(1, BR_W), ones_bd)


def _to_lanes(z0, z1, rep):
    lo = lax.broadcasted_iota(jnp.int32, (1, LANES), 1) < RW_HD
    parts = []
    for hp in range(RW_HEADS // 2):
        a = z0[:, hp * LANES:(hp + 1) * LANES]
        b = z1[:, hp * LANES:(hp + 1) * LANES]
        parts.append(jnp.where(lo, a, pltpu.roll(b, RW_HD, 1)))
        parts.append(jnp.where(lo, pltpu.roll(a, RW_HD, 1), b))
    return jnp.concatenate(parts * rep, axis=0).T


def _scan_kernel(*refs, nb, has_init, lanes_in, n_side, emit_state):
    if has_init:
        s0_ref, refs = refs[0], refs[1:]
    side_in, refs = refs[:2 * n_side], refs[2 * n_side:]
    n_in = 6 if lanes_in else 12
    data_in, refs = refs[:n_in], refs[n_in:]
    of_ref, ob_ref, st_ref = refs[:3]
    side_out = refs[3:3 + n_side]
    n_out = 3 + n_side + (1 if emit_state else 0)
    w_s, kk_s, kka_s, kd_s, r_s, v_s, o_s = refs[n_out:]
    nk, vh, _ = st_ref.shape
    tc = w_s.shape[0]
    rep = RW_HD // vh
    nrec = LANES // rep
    lane = lax.broadcasted_iota(jnp.int32, (1, LANES), 1)
    grp = lane // nrec
    lo = lane < RW_HD

    def rows(f_ref, b_ref, j, n_steps):
        return jnp.concatenate([f_ref[:, j, :], b_ref[:, n_steps - 1 - j, :]], axis=0)

    if lanes_in:
        w_in, kk_in, kka_in, kd_in, r_in, v_in = data_in
    else:
        w_in, kk_in, kka_in, kd_in, r_in, v_in = w_s, kk_s, kka_s, kd_s, r_s, v_s
        for j in range(0, tc, 2):
            for si, dst in enumerate((w_s, kk_s, kka_s, kd_s, r_s, v_s)):
                f_ref, b_ref = data_in[2 * si], data_in[2 * si + 1]
                gt = _to_lanes(rows(f_ref, b_ref, j, tc), rows(f_ref, b_ref, j + 1, tc), rep)
                for s in range(2):
                    blk = gt[s * RW_HD:(s + 1) * RW_HD]
                    if dst is v_s and rep > 1:
                        blk = sum(jnp.where(grp == q, blk[q * vh:(q + 1) * vh], 0.0) for q in range(rep))
                    dst[j + s] = blk

    p = jnp.ones((nk, LANES), F32)
    for t in range(tc):
        kk_s[t] = kk_in[t] * p
        p = p * w_in[t]
        inv = 1.0 / p
        kka_s[t] = kka_in[t] * inv
        kd_s[t] = kd_in[t] * inv
        r_s[t] = r_in[t] * p
    w_s[0] = p

    @pl.when(pl.program_id(0) == 0)
    def _():
        if has_init:
            st_ref[...] = s0_ref[...]
        else:
            st_ref[...] = jnp.zeros(st_ref.shape, F32)

    sa0 = jnp.zeros((vh, LANES), F32)
    for k in range(nk):
        sa0 = sa0 + st_ref[k] * kk_s[0, pl.ds(k, 1), :]

    def step(t, sa):
        tn = jnp.minimum(t + 1, tc - 1)
        vt = v_in[t]
        o = jnp.zeros((vh, LANES), F32)
        san = jnp.zeros((vh, LANES), F32)
        for k in range(nk):
            new = st_ref[k] - kka_s[t, pl.ds(k, 1), :] * sa + kd_s[t, pl.ds(k, 1), :] * vt
            st_ref[k] = new
            o = o + new * r_s[t, pl.ds(k, 1), :]
            san = san + new * kk_s[tn, pl.ds(k, 1), :]
        o_s[t] = o
        return san

    if n_side:
        side_tc = side_in[0].shape[1]
        per = tc // (side_tc // 2)

        def steps_and_side_pair(q, sa):
            for u in range(per):
                sa = step(per * q + u, sa)
            j = 2 * q
            words = None
            for si in range(n_side):
                f_ref, b_ref = side_in[2 * si], side_in[2 * si + 1]
                gt = _to_lanes(rows(f_ref, b_ref, j, side_tc), rows(f_ref, b_ref, j + 1, side_tc), 1)
                side_out[si][j] = gt[0:RW_HD]
                side_out[si][j + 1] = gt[RW_HD:2 * RW_HD]
                bits = lax.bitcast_convert_type(gt[0:8], jnp.uint32)
                words = bits if words is None else words | bits
            zero = lax.shift_right_logical(lax.shift_right_logical(words, jnp.uint32(16)), jnp.uint32(16))
            return sa + lax.bitcast_convert_type(zero, F32)[0:1, :]

        lax.fori_loop(0, side_tc // 2, steps_and_side_pair, sa0)
    else:
        lax.fori_loop(0, tc, step, sa0)
    for k in range(nk):
        st_ref[k] = st_ref[k] * w_s[0, pl.ds(k, 1), :]

    for j in range(0, tc, 2):
        tiles = []
        for s in range(2):
            o = o_s[j + s]
            if rep == 1:
                tiles.append(o)
            else:
                tiles.extend(jnp.where(grp == q, o, 0.0) for q in range(rep))
        mt = jnp.concatenate(tiles, axis=0).T
        sm = mt[0:nrec]
        for q in range(1, rep):
            sm = sm + mt[q * nrec:(q + 1) * nrec]
        rs = pltpu.roll(sm, RW_HD, 1)
        for hp in range(RW_HEADS // 2):
            e0, e1 = (2 * hp) * 2 * nb, (2 * hp + 1) * 2 * nb
            cur = jnp.where(lo, sm[e0:e0 + 2 * nb], rs[e1:e1 + 2 * nb])
            nxt = jnp.where(lo, rs[e0:e0 + 2 * nb], sm[e1:e1 + 2 * nb])
            cs = slice(hp * LANES, (hp + 1) * LANES)
            of_ref[:, j, cs] = cur[0:nb]
            of_ref[:, j + 1, cs] = nxt[0:nb]
            ob_ref[:, tc - 1 - j, cs] = cur[nb:2 * nb]
            ob_ref[:, tc - 2 - j, cs] = nxt[nb:2 * nb]

    if emit_state:
        fs_ref = refs[n_out - 1]

        @pl.when(pl.program_id(0) == pl.num_programs(0) - 1)
        def _():
            for vp in range(vh // 2):
                pair = jnp.concatenate([st_ref[:, 2 * vp, :], st_ref[:, 2 * vp + 1, :]], axis=0)
                fs_ref[:, vp, :] = pair.T


def _fwd_bwd_args(pre, nb, t_len, tc, n_t):
    wide = pre.reshape(nb, t_len, N_WIDE * BR_W)
    fwd = lambda c: pl.BlockSpec((nb, tc, BR_W), lambda i: (0, i, c))
    bwd = lambda c: pl.BlockSpec((nb, tc, BR_W), lambda i: (0, n_t - 1 - i, c))
    cols = ((2, 3), (8, 8), (4, 5), (6, 7), (0, 0), (1, 1))
    specs = [s for cf, cb in cols for s in (fwd(cf), bwd(cb))]
    return (wide,) * 12, specs


def _scan(pre, s0, *, nb, t_len, name, lanes=None, side_pre=None, side_nb=None, side_len=None, emit_state=False):
    n_t = t_len // SCAN_TC
    vh = RW_HD * nb * 2 * RW_HEADS // LANES
    sspec = pl.BlockSpec((RW_HD, vh, LANES), lambda i: (0, 0, 0))
    has_init = s0 is not None
    args, in_specs = ((s0,), [sspec]) if has_init else ((), [])
    n_side, side_out_specs, side_out_shape = 0, [], []
    if side_pre is not None:
        n_side = 6
        side_tc = side_len // n_t
        side_args, side_specs = _fwd_bwd_args(side_pre, side_nb, side_len, side_tc, n_t)
        args, in_specs = args + side_args, in_specs + side_specs
        side_out_specs = [pl.BlockSpec((side_tc, RW_HD, LANES), lambda i: (i, 0, 0))] * n_side
        side_out_shape = [jax.ShapeDtypeStruct((side_len, RW_HD, LANES), F32)] * n_side
    if lanes is None:
        own_args, own_specs = _fwd_bwd_args(pre, nb, t_len, SCAN_TC, n_t)
    else:
        own_args = tuple(lanes)
        own_specs = ([pl.BlockSpec((SCAN_TC, RW_HD, LANES), lambda i: (i, 0, 0))] * 5
                     + [pl.BlockSpec((SCAN_TC, vh, LANES), lambda i: (i, 0, 0))])
    kbuf = pltpu.VMEM((SCAN_TC, RW_HD, LANES), F32)
    vbuf = pltpu.VMEM((SCAN_TC, vh, LANES), F32)
    state_specs, state_shape = [], []
    if emit_state:
        state_specs = [pl.BlockSpec((LANES, vh // 2, 2 * RW_HD), lambda i: (0, 0, 0))]
        state_shape = [jax.ShapeDtypeStruct((LANES, vh // 2, 2 * RW_HD), F32)]
    return pl.pallas_call(
        functools.partial(_scan_kernel, nb=nb, has_init=has_init, lanes_in=lanes is not None, n_side=n_side,
                          emit_state=emit_state),
        grid=(n_t,),
        in_specs=in_specs + own_specs,
        out_specs=[pl.BlockSpec((nb, SCAN_TC, BR_W), lambda i: (0, i, 0)),
                   pl.BlockSpec((nb, SCAN_TC, BR_W), lambda i: (0, n_t - 1 - i, 0)),
                   sspec] + side_out_specs + state_specs,
        out_shape=[jax.ShapeDtypeStruct((nb, t_len, BR_W), F32),
                   jax.ShapeDtypeStruct((nb, t_len, BR_W), F32),
                   jax.ShapeDtypeStruct((RW_HD, vh, LANES), F32)] + side_out_shape + state_shape,
        scratch_shapes=[kbuf] * 5 + [vbuf] * 2,
        compiler_params=_cparams(("arbitrary",)),
        name=name,
    )(*args, *own_args)


LAT_VS = 4
LAT_VH = RW_HD // LAT_VS


def _lat_state(s):
    s6 = s.reshape(DEC_BATCH, 2, RW_HEADS, LAT_VS, LAT_VH, RW_HD)
    return s6.transpose(5, 4, 3, 2, 1, 0).reshape(RW_HD, LAT_VH, LANES)


def _sink_softmax_pv(ss, sks, vbs):
    ms = [jnp.maximum(jnp.max(s, axis=-1, keepdims=True), sk) for s, sk in zip(ss, sks)]
    ps = [jnp.exp(s - m) for s, m in zip(ss, ms)]
    dens = [jnp.sum(p, axis=-1, keepdims=True) + jnp.exp(sk - m) for p, sk, m in zip(ps, sks, ms)]
    return [jnp.dot(p.astype(BF16), vb, preferred_element_type=F32) / den for p, vb, den in zip(ps, vbs, dens)]


def _att_ctx_kernel(sink_ref, q_ref, k_ref, v_ref, dp_ref, pw_ref, psc_ref, pic_ref, o_ref, yd_ref, pa_ref, pb_ref):
    _pool_body(dp_ref, pw_ref, psc_ref, pic_ref, yd_ref, pa_ref, pb_ref)
    scale = ATT_HD ** -0.5
    q = q_ref[...] * scale
    kb = k_ref[...].astype(BF16)
    vb = v_ref[...].astype(BF16)
    first = lax.broadcasted_iota(jnp.int32, (ATT_GROUP * SEQ, 1), 0) < SEQ
    ss, sks = [], []
    for kvh in range(ATT_KV_HEADS):
        ks = slice(kvh * ATT_HD, (kvh + 1) * ATT_HD)
        q2 = jnp.concatenate([q[:, hd * ATT_HD:(hd + 1) * ATT_HD]
                              for hd in (ATT_GROUP * kvh, ATT_GROUP * kvh + 1)], axis=0).astype(BF16)
        ss.append(lax.dot_general(q2, kb[:, ks], (((1,), (1,)), ((), ())), preferred_element_type=F32))
        sks.append(jnp.where(first, sink_ref[ATT_GROUP * kvh], sink_ref[ATT_GROUP * kvh + 1]))
    outs = []
    for o2 in _sink_softmax_pv(ss, sks, [vb[:, kvh * ATT_HD:(kvh + 1) * ATT_HD] for kvh in range(ATT_KV_HEADS)]):
        outs += [o2[:SEQ], o2[SEQ:]]
    o_ref[...] = jnp.concatenate(outs, axis=1)


def _att_ctx(sink, cq, ck, cv, dp, pw, pscale):
    kvw = ATT_KV_HEADS * ATT_HD
    pool_in, pool_scratch = _pool_specs(SEQ)
    return pl.pallas_call(
        _att_ctx_kernel,
        grid=(BATCH,),
        in_specs=[
            pl.BlockSpec(memory_space=pltpu.SMEM),
            pl.BlockSpec((SEQ, BR_W), lambda b: (b, 0)),
            pl.BlockSpec((SEQ, kvw), lambda b: (b, 0)),
            pl.BlockSpec((SEQ, kvw), lambda b: (b, 0)),
        ] + pool_in,
        out_specs=[pl.BlockSpec((SEQ, BR_W), lambda b: (b, 0))] * 2,
        out_shape=[jax.ShapeDtypeStruct((N_CTX, BR_W), F32)] * 2,
        scratch_shapes=pool_scratch,
        compiler_params=_cparams(("arbitrary",)),
        name="att_ctx",
    )(sink, cq, ck, cv, dp, pw, pscale.reshape(1, BR_W), _pool_inv_count(SEQ))


def _rope(x, cos, sin_signed):
    w = x.shape[1]
    lane = lax.broadcasted_iota(jnp.int32, (1, w), 1)
    first = (lane % 32) < 16
    swapped = jnp.where(first, pltpu.roll(x, w - 16, 1), pltpu.roll(x, 16, 1))
    return x * cos + swapped * sin_signed


def _att_lat_kernel(sink_ref, q_ref, k_ref, v_ref, ck_ref, cv_ref, cq_ref, sq_ref, ckk_ref, skk_ref,
                    dp_ref, pw_ref, psc_ref, pic_ref, o_ref, yd_ref, qs_ref, ks_ref, vs_ref, pa_ref, pb_ref):
    _pool_body(dp_ref, pw_ref, psc_ref, pic_ref, yd_ref, pa_ref, pb_ref)
    scale = ATT_HD ** -0.5
    nb = DEC_SEQ // CHUNK
    qs_ref[...] = (_rope(q_ref[...], cq_ref[...], sq_ref[...]) * scale).astype(BF16)
    ks_ref[...] = _rope(k_ref[...], ckk_ref[...], skk_ref[...]).astype(BF16)
    vs_ref[...] = v_ref[...].astype(BF16)
    ckb = ck_ref[...].astype(BF16)
    cvb = cv_ref[...].astype(BF16)
    for n in range(nb):
        lo = max(n - 1, 0) * CHUNK
        hi = min(n + 2, nb) * CHUNK
        nloc = hi - lo
        ncol = nloc + PAST_LEN
        col = lax.broadcasted_iota(jnp.int32, (CHUNK, ncol), 1)
        row = lax.broadcasted_iota(jnp.int32, (CHUNK, ncol), 0)
        dist = jnp.abs((n * CHUNK + row) - (lo + col))
        valid = (col >= nloc) | (dist <= WINDOW)
        valid2 = jnp.concatenate([valid] * ATT_GROUP, axis=0)
        first = lax.broadcasted_iota(jnp.int32, (ATT_GROUP * CHUNK, 1), 0) < CHUNK
        outs, ss, vs_, sks = [], [], [], []
        for kvh in range(ATT_KV_HEADS):
            ks = slice(kvh * ATT_HD, (kvh + 1) * ATT_HD)
            q2 = jnp.concatenate([qs_ref[n * CHUNK:(n + 1) * CHUNK, hd * ATT_HD:(hd + 1) * ATT_HD]
                                  for hd in (ATT_GROUP * kvh, ATT_GROUP * kvh + 1)], axis=0)
            kall = jnp.concatenate([ks_ref[lo:hi, ks], ckb[:, ks]], axis=0)
            vs_.append(jnp.concatenate([vs_ref[lo:hi, ks], cvb[:, ks]], axis=0))
            s = lax.dot_general(q2, kall, (((1,), (1,)), ((), ())), preferred_element_type=F32)
            ss.append(jnp.where(valid2, s, NEG_INF))
            sks.append(jnp.where(first, sink_ref[ATT_GROUP * kvh], sink_ref[ATT_GROUP * kvh + 1]))
        for o2 in _sink_softmax_pv(ss, sks, vs_):
            outs += [o2[:CHUNK], o2[CHUNK:]]
        o_ref[n * CHUNK:(n + 1) * CHUNK, :] = jnp.concatenate(outs, axis=1)


def _att_lat(sink, cq, ck, cv, cache_k, cache_v, layer, tables, dp, pw, pscale):
    kvw = ATT_KV_HEADS * ATT_HD
    pool_in, pool_scratch = _pool_specs(DEC_SEQ)
    cosq, sinq, cosk, sink_k = tables
    seq = lambda w: pl.BlockSpec((DEC_SEQ, w), lambda b: (b, 0))
    cache = pl.BlockSpec((None, None, PAST_LEN, kvw), lambda b: (b, layer, 0, 0))
    tab = lambda w: pl.BlockSpec((DEC_SEQ, w), lambda b: (0, 0))
    return pl.pallas_call(
        _att_lat_kernel,
        grid=(DEC_BATCH,),
        in_specs=[pl.BlockSpec(memory_space=pltpu.SMEM), seq(BR_W), seq(kvw), seq(kvw), cache, cache,
                  tab(BR_W), tab(BR_W), tab(kvw), tab(kvw)] + pool_in,
        out_specs=[pl.BlockSpec((DEC_SEQ, BR_W), lambda b: (b, 0))] * 2,
        out_shape=[jax.ShapeDtypeStruct((N_LAT, BR_W), F32)] * 2,
        scratch_shapes=[pltpu.VMEM((DEC_SEQ, BR_W), BF16), pltpu.VMEM((DEC_SEQ, kvw), BF16),
                        pltpu.VMEM((DEC_SEQ, kvw), BF16)] + pool_scratch,
        compiler_params=_cparams(("arbitrary",)),
        name="att_lat",
    )(sink, cq, ck, cv, cache_k, cache_v, cosq, sinq, cosk, sink_k, dp, pw, pscale.reshape(1, BR_W),
      _pool_inv_count(DEC_SEQ))


def _rope_tables():
    pos = np.arange(DEC_SEQ)
    lane = np.arange(ATT_HD)
    p = np.where(lane[None, :] < 32, (pos // GRID_W)[:, None], (pos % GRID_W)[:, None]).astype(np.float32)
    inv = (ROPE_BASE ** (-jnp.arange(0, 32, 2, dtype=F32) / 32))[lane % 16]
    ang = jnp.asarray(p) * inv[None, :]
    sign = jnp.asarray(np.where((lane % 32) < 16, -1.0, 1.0).astype(np.float32))
    cos, sin = jnp.cos(ang), jnp.sin(ang) * sign[None, :]
    return (jnp.tile(cos, (1, ATT_HEADS)), jnp.tile(sin, (1, ATT_HEADS)),
            jnp.tile(cos, (1, ATT_KV_HEADS)), jnp.tile(sin, (1, ATT_KV_HEADS)))


POOL_PAD = 8


def _pool_body(x_ref, w_ref, sc_ref, ic_ref, o_ref, a_ref, b_ref):
    n = x_ref.shape[0]
    ext = n + POOL_PAD
    x = x_ref[...]
    zeros = jnp.zeros((POOL_PAD, BR_W), F32)
    a_ref[0:POOL_PAD, :] = zeros
    b_ref[0:POOL_PAD, :] = zeros
    a_ref[POOL_PAD:POOL_PAD + n, :] = x
    a_ref[POOL_PAD + n:POOL_PAD + ext, :] = zeros
    grp = lax.broadcasted_iota(jnp.int32, (1, BR_W), 1) // POOL_GD
    acc = jnp.zeros((n, BR_W), F32)
    src_ref, dst_ref = a_ref, b_ref
    for m in range(len(POOL_SIZES)):
        back = 1 << m
        dst_ref[POOL_PAD:POOL_PAD + ext, :] = (src_ref[POOL_PAD:POOL_PAD + ext, :]
                                               + src_ref[POOL_PAD - back:POOL_PAD - back + ext, :])
        off = POOL_PAD + back - 1
        acc = jnp.where(grp == m, dst_ref[off:off + n, :], acc)
        src_ref, dst_ref = dst_ref, src_ref
    d = acc * ic_ref[...] - x
    o_ref[...] = jnp.dot(d.astype(BF16), w_ref[...], preferred_element_type=F32) * sc_ref[...]


def _pool_inv_count(seq_len):
    t = np.arange(seq_len)[:, None]
    half = (np.asarray(POOL_SIZES) // 2)[np.arange(BR_W) // POOL_GD][None, :]
    cnt = np.minimum(t + half, seq_len) - np.maximum(t - half, 0)
    return jnp.asarray(1.0 / cnt, F32)


def _pool_specs(seq_len):
    return ([pl.BlockSpec((seq_len, BR_W), lambda b: (b, 0)), pl.BlockSpec((BR_W, BR_W), lambda b: (0, 0)),
             pl.BlockSpec((1, BR_W), lambda b: (0, 0)), pl.BlockSpec((seq_len, BR_W), lambda b: (0, 0))],
            [pltpu.VMEM((seq_len + 2 * POOL_PAD, BR_W), F32)] * 2)


def _merge_kernel(x_ref, mod_ref, g_ref, a_ref, bonus_ref, of_ref, ob_ref, yc_ref, yd_ref,
                  wz_ref, wmg_ref, ws_ref, bs_ref, lng_ref, lnb_ref, ones_ref, wup_ref, wo_ref, gf_ref,
                  o_ref, *, lat, final):
    i = pl.program_id(0)
    m = mod_ref[pl.ds(_mod_row(i, TM_MERGE, lat), 1), :]
    x = x_ref[...]
    hb = _norm_mod(x, g_ref[...], m).astype(BF16)

    a = a_ref[...]
    a_u, a_v = a[:, :BR_W], a[:, BR_W:]
    grp = lax.broadcasted_iota(jnp.int32, (1, BR_W), 1) // A_GD
    svs = []
    for c in range(TM_MERGE // CHUNK):
        vc = a_v[c * CHUNK:(c + 1) * CHUNK, :]
        sv = bs_ref[...]
        for g in range(A_GROUPS):
            sv = sv + jnp.dot(ws_ref[g], jnp.where(grp == g, vc, 0.0).astype(BF16), preferred_element_type=F32)
        svs.append(sv)
    y_a = a_u * jnp.concatenate(svs, axis=0)

    ones = ones_ref[...]
    osum = of_ref[...] + ob_ref[...]
    mu = _seg_sum(osum, ones) * (1.0 / RW_HD)
    dev = osum - mu
    var = _seg_sum(dev * dev, ones) * (1.0 / RW_HD)
    y_b = dev * lax.rsqrt(var + GN_EPS) * lng_ref[...] + lnb_ref[...] + bonus_ref[...]

    merged = jnp.zeros((TM_MERGE, D_MODEL), F32)
    z_all = jnp.dot(hb, wz_ref[0], preferred_element_type=F32)
    for n, y in enumerate((y_a, y_b, yc_ref[...], yd_ref[...])):
        z = z_all[:, n * BR_W:(n + 1) * BR_W]
        ys = y * (z * _sigmoid(z))
        up = jnp.dot(ys.astype(BF16), wup_ref[n], preferred_element_type=F32)
        mg = jnp.dot(hb, wmg_ref[0, :, n * D_MODEL:(n + 1) * D_MODEL], preferred_element_type=F32)
        merged = merged + _sigmoid(mg) * up
    gate = m[:, 2 * D_MODEL:]
    out = x + gate * jnp.dot(merged.astype(BF16), wo_ref[...], preferred_element_type=F32)
    if final:
        ms = jnp.mean(out * out, axis=-1, keepdims=True)
        out = out * lax.rsqrt(ms + NORM_EPS) * gf_ref[...]
    o_ref[...] = out


def _merge(x, mod_l, g, a, bonus, o_f, o_b, y_c, y_d, w_in_b, layer, ws, bs_tile, ln_g, ln_b, ones_bd, wup, wo,
           g_final, lat, final):
    row = lambda w: pl.BlockSpec((TM_MERGE, w), lambda i: (i, 0))
    full2 = lambda s: pl.BlockSpec(s, lambda i: (0, 0))
    z0, z1 = IN_SMALL, IN_SMALL + N_BRANCH * BR_W
    cols = lambda c0, w: pl.BlockSpec((pl.Element(1), pl.Element(D_MODEL), pl.Element(w)), lambda i: (layer, 0, c0))
    return pl.pallas_call(
        functools.partial(_merge_kernel, lat=lat, final=final),
        grid=(N_GRP // TM_MERGE,),
        in_specs=[row(D_MODEL), full2((8, 3 * D_MODEL)), full2((1, D_MODEL)), row(2 * BR_W),
                  pl.BlockSpec((TM_MERGE, BR_W), lambda i: (i, N_WIDE - 1)), row(BR_W), row(BR_W), row(BR_W), row(BR_W),
                  cols(z0, N_BRANCH * BR_W), cols(z1, N_BRANCH * D_MODEL),
                  pl.BlockSpec((None, A_GROUPS, CHUNK, CHUNK), lambda i: (layer, 0, 0, 0)), full2((CHUNK, BR_W)),
                  full2((1, BR_W)), full2((1, BR_W)), full2((BR_W, BR_W)),
                  pl.BlockSpec((None, N_BRANCH, BR_W, D_MODEL), lambda i: (layer, 0, 0, 0)),
                  pl.BlockSpec((None, D_MODEL, D_MODEL), lambda i: (layer, 0, 0)), full2((1, D_MODEL))],
        out_specs=row(D_MODEL),
        out_shape=jax.ShapeDtypeStruct((N_GRP, D_MODEL), F32),
        compiler_params=_cparams(("arbitrary",)),
        name=("merge_lat" if lat else "merge_ctx") + ("_final" if final else ""),
    )(x, mod_l, g.reshape(1, D_MODEL), a, bonus, o_f, o_b, y_c, y_d, w_in_b, w_in_b, ws, bs_tile,
      ln_g.reshape(1, BR_W), ln_b.reshape(1, BR_W), ones_bd, wup, wo, g_final.reshape(1, D_MODEL))


def _block_diag(blocks):
    n, r, c = blocks.shape
    eye = jnp.eye(n, dtype=blocks.dtype)
    return (eye[:, None, :, None] * blocks[:, :, None, :]).reshape(n * r, n * c)


def kernel(x_prompt, x_sample, cache_k, cache_v, state_rwkv, c, c_ctx, w_mod, b_mod, g_norm, w_in, w_s, b_s,
           rw_w0, rw_w_up, rw_a0, rw_a_up, rw_k_k, rw_k_a, rw_r_k, rw_ln_g, rw_ln_b, att_sink, pool_w,
           pool_scale, w_up, w_o, g_final):
    xs = [x_prompt.reshape(N_CTX, D_MODEL), x_sample.reshape(N_LAT, D_MODEL)]
    cond = jnp.concatenate([c_ctx[None, :], c, jnp.zeros((8 - 1 - DEC_BATCH, D_MODEL), F32)], axis=0)
    mod = _modulation(cond, w_mod, b_mod)

    ones_bd = _block_diag(jnp.ones((RW_HEADS, RW_HD, RW_HD), BF16))
    tables = _rope_tables()
    kvw = ATT_KV_HEADS * ATT_HD
    cache_k4 = cache_k.reshape(DEC_BATCH, DEPTH, PAST_LEN, kvw)
    cache_v4 = cache_v.reshape(DEC_BATCH, DEPTH, PAST_LEN, kvw)
    w_in_b, w_s_b, w_up_b, w_o_b = (w.astype(BF16) for w in (w_in, w_s, w_up, w_o))

    new_k, new_v, new_s = [], [], []
    for l in range(DEPTH):
        pw = _block_diag(pool_w[l]).astype(BF16)
        bs_tile = jnp.repeat(b_s[l].T, A_GD, axis=1)
        final = l == DEPTH - 1
        proj = []
        for lat in (False, True):
            a, cq, ck, cv, dp, wide = _inproj(
                xs[lat], mod[l], g_norm[l], w_in_b, l, rw_w0[l], rw_w_up[l], rw_a0[l], rw_a_up[l], rw_k_k[l],
                rw_k_a[l], rw_r_k[l], ones_bd, lat)
            proj.append((a, cq, ck, cv, dp, wide, wide))

        of_l, ob_l, _, *ctx_lanes = _scan(proj[1][5], _lat_state(state_rwkv[:, l]), nb=DEC_BATCH, t_len=DEC_SEQ,
                                          name="scan_lat", side_pre=proj[0][5], side_nb=BATCH, side_len=SEQ)
        of_c, ob_c, _, s_fin = _scan(None, None, nb=BATCH, t_len=SEQ, name="scan_ctx", lanes=ctx_lanes,
                                     emit_state=True)
        new_s.append(s_fin.reshape(RW_HEADS, 2, BATCH, RW_HD, RW_HD).transpose(2, 1, 0, 3, 4))
        scans = ((of_c, ob_c), (of_l, ob_l))

        for lat in (False, True):
            a, cq, ck, cv, dp, _, bonus = proj[lat]
            o_f, o_b = scans[lat]
            if lat:
                y_c, y_d = _att_lat(att_sink[l], cq, ck, cv, cache_k4, cache_v4, l, tables, dp, pw, pool_scale[l])
            else:
                y_c, y_d = _att_ctx(att_sink[l], cq, ck, cv, dp, pw, pool_scale[l])
                new_k.append(ck.reshape(BATCH, SEQ, ATT_KV_HEADS, ATT_HD))
                new_v.append(cv.reshape(BATCH, SEQ, ATT_KV_HEADS, ATT_HD))
            xs[lat] = _merge(xs[lat], mod[l], g_norm[l], a, bonus, o_f.reshape(N_GRP, BR_W),
                             o_b.reshape(N_GRP, BR_W), y_c, y_d, w_in_b, l, w_s_b, bs_tile, rw_ln_g[l], rw_ln_b[l],
                             ones_bd, w_up_b, w_o_b, g_final, lat, final)

    y_prompt = xs[0].reshape(BATCH, SEQ, D_MODEL)
    y_sample = xs[1].reshape(DEC_BATCH, DEC_SEQ, D_MODEL)
    return (y_prompt, y_sample, jnp.stack(new_k, axis=1), jnp.stack(new_v, axis=1), jnp.stack(new_s, axis=1))
```
